```python
import math
import jax
import jax.numpy as jnp
from jax import lax
import numpy as np

D_MODEL = 1024
BATCH = 8
SEQ = 2048
DEPTH = 1
DEC_BATCH = 128
DEC_SEQ = 8
PAST_LEN = 2048
PAGE_SIZE = 128

HG_HEADS = 4
HG_DIM = 128
HG_WIDTH = HG_HEADS * HG_DIM
HG_CHUNK = 16
MB_HEADS = 4
MB_DIM = 128
MB_WIDTH = MB_HEADS * MB_DIM
MB_BLOCK = 256
MB_TOPK = 3
MB_GATHER_ROWS = 128
REL_BUCKETS = 32
REL_MAX_DIST = 128
MEM_LEN = 256
XA_HEADS = 4
XA_DIM = 128
XA_WIDTH = XA_HEADS * XA_DIM
N_GROUPS = 4
EXPERTS_PER_GROUP = 8
EXPERT_TOPK = 2
EXPERT_HIDDEN = 256
NORM_EPS = 1e-6
IN_SPLITS = (HG_WIDTH, 2 * HG_WIDTH, 3 * HG_WIDTH, 4 * HG_WIDTH,
             4 * HG_WIDTH + MB_WIDTH, 4 * HG_WIDTH + 2 * MB_WIDTH, 4 * HG_WIDTH + 3 * MB_WIDTH,
             4 * HG_WIDTH + 3 * MB_WIDTH + D_MODEL)
IN_COLS = 4 * HG_WIDTH + 3 * MB_WIDTH + 2 * D_MODEL

kernel_name = 'hgrn2_moba_gated_hmoe_decode_step'


def rmsnorm(x, g):
    xf = x.astype(jnp.float32)
    y = xf * lax.rsqrt(jnp.mean(xf * xf, axis=-1, keepdims=True) + NORM_EPS)
    return (y * g.astype(jnp.float32)).astype(x.dtype)


def hgrn2_scan(q, logf, k, v, s0):
    B, T, H, K = q.shape
    C = math.gcd(T, HG_CHUNK)
    n = T // C

    def chunks(a):
        return a.reshape(B, n, C, H, a.shape[-1]).transpose(1, 0, 3, 2, 4)

    causal = jnp.tril(jnp.ones((C, C), bool))[:, :, None]

    def step(S, inp):
        qb, fb, kb, vb = inp
        b = jnp.cumsum(fb, axis=2)
        o_inter = jnp.einsum('bhck,bhkv->bhcv', qb * jnp.exp(b), S)
        diff = b[:, :, :, None, :] - b[:, :, None, :, :]
        decay = jnp.where(causal, jnp.exp(jnp.where(causal, diff, 0.0)), 0.0)
        scores = jnp.einsum('bhtk,bhtsk,bhsk->bhts', qb, decay, kb)
        o = o_inter + jnp.einsum('bhts,bhsv->bhtv', scores, vb)
        b_last = b[:, :, -1:, :]
        S = S * jnp.exp(b_last[:, :, 0, :])[..., None] + jnp.einsum('bhck,bhcv->bhkv', kb * jnp.exp(b_last - b), vb)
        return S, o

    S, o = lax.scan(step, s0, (chunks(q), chunks(logf), chunks(k), chunks(v)))
    o = o.transpose(1, 0, 3, 2, 4).reshape(B, T, H, v.shape[-1])
    return o, S


def hgrn2_branch(hq, hf, hi, hg, lb, hg_norm, s0):
    B, T, _ = hq.shape
    f = lb + (1.0 - lb) * jax.nn.sigmoid(hf.astype(jnp.float32))
    q = jax.nn.silu(hq.astype(jnp.float32))
    heads = lambda a: a.reshape(B, T, HG_HEADS, HG_DIM)
    o, s_new = hgrn2_scan(heads(q), heads(jnp.log(f)), heads(1.0 - f),
                          heads(hi.astype(jnp.float32)), s0.astype(jnp.float32))
    o = rmsnorm(o, hg_norm.reshape(HG_HEADS, HG_DIM)).reshape(B, T, HG_WIDTH)
    return (o * jax.nn.silu(hg.astype(jnp.float32))).astype(hq.dtype), s_new


def rel_bucket(rel):
    max_exact = REL_BUCKETS // 2
    d = jnp.maximum(rel, 0)
    df = jnp.maximum(d, 1).astype(jnp.float32)
    large = max_exact + (jnp.log(df / max_exact) / math.log(REL_MAX_DIST / max_exact)
                         * (REL_BUCKETS - max_exact)).astype(jnp.int32)
    large = jnp.minimum(large, REL_BUCKETS - 1)
    return jnp.where(d < max_exact, d, large)


def query_group_size(batch, t):
    cap = max(1, MB_GATHER_ROWS // batch)
    return max(d for d in range(1, min(cap, t) + 1) if t % d == 0)


def moba_attention(q, k, v, q_pos, rel_table):
    B, T, H, D = q.shape
    L = k.shape[1]
    n_blk = -(-L // MB_BLOCK)
    pad = n_blk * MB_BLOCK - L
    kb = jnp.pad(k, ((0, 0), (0, pad), (0, 0), (0, 0))).reshape(B, n_blk, MB_BLOCK, H, D)
    vb = jnp.pad(v, ((0, 0), (0, pad), (0, 0), (0, 0))).reshape(B, n_blk, MB_BLOCK, H, D)
    k_mean = jnp.mean(kb.astype(jnp.float32), axis=2)
    own = q_pos // MB_BLOCK
    gate = jnp.einsum('bthd,bnhd->bthn', q.astype(jnp.float32), k_mean)
    past = jnp.arange(n_blk)[None, :] < own[:, None]
    gate = jnp.where(past[None, :, None, :], gate, -jnp.inf)
    _, top_idx = lax.top_k(gate, min(MB_TOPK, n_blk))
    top_idx = top_idx.astype(jnp.int32)
    sel_ok = top_idx < own[None, :, None, None]
    blk_idx = jnp.concatenate([top_idx, jnp.broadcast_to(own[None, :, None, None], (B, T, H, 1)).astype(jnp.int32)], axis=-1)
    blk_ok = jnp.concatenate([sel_ok, jnp.ones((B, T, H, 1), bool)], axis=-1)
    n_sel = blk_idx.shape[-1]
    kt = kb.transpose(0, 3, 1, 2, 4)
    vt = vb.transpose(0, 3, 1, 2, 4)
    g = query_group_size(B, T)
    n_grp = T // g

    def groups(a):
        return a.reshape((B, n_grp, g) + a.shape[2:]).swapaxes(0, 1)

    b_ix = jnp.arange(B)[:, None, None, None]
    h_ix = jnp.arange(H)[None, None, :, None]
    offs = jnp.arange(MB_BLOCK, dtype=jnp.int32)
    scale = D ** -0.5

    def attend(args):
        qg, ig, okg, pg = args
        kg = kt[b_ix, h_ix, ig]
        vg = vt[b_ix, h_ix, ig]
        rel = pg[None, :, None, None, None] - (ig[..., None] * MB_BLOCK + offs)
        mask = okg[..., None] & (rel >= 0)
        bias = rel_table[rel_bucket(rel), h_ix[..., None]].astype(jnp.float32)
        logits = jnp.einsum('bghd,bghrkd->bghrk', qg, kg).astype(jnp.float32) * scale + bias
        logits = jnp.where(mask, logits, -jnp.inf).reshape(B, g, H, n_sel * MB_BLOCK)
        p = jax.nn.softmax(logits, axis=-1).reshape(B, g, H, n_sel, MB_BLOCK)
        return jnp.einsum('bghrk,bghrkd->bghd', p.astype(vg.dtype), vg)

    out = lax.map(attend, (groups(q), groups(blk_idx), groups(blk_ok), q_pos.reshape(n_grp, g)))
    return out.swapaxes(0, 1).reshape(B, T, H, D)


def memory_attention(q, mem_k, mem_v):
    logits = jnp.einsum('bthd,bmhd->bhtm', q, mem_k).astype(jnp.float32) * XA_DIM ** -0.5
    p = jax.nn.softmax(logits, axis=-1)
    return jnp.einsum('bhtm,bmhd->bthd', p.astype(mem_v.dtype), mem_v)


def hier_moe(x, w_rg, b_rg, w_re, b_re, w_ug, w_u, w_d):
    N = x.shape[0]
    g_logits = (x @ w_rg + b_rg).astype(jnp.float32)
    grp = jnp.argmax(g_logits, axis=-1)
    g_prob = jnp.take_along_axis(jax.nn.softmax(g_logits, axis=-1), grp[:, None], axis=-1)
    e_logits = (x @ w_re + b_re).astype(jnp.float32).reshape(N, N_GROUPS, EXPERTS_PER_GROUP)
    e_logits = jnp.take_along_axis(e_logits, grp[:, None, None], axis=1)[:, 0]
    top_val, top_idx = lax.top_k(e_logits, EXPERT_TOPK)
    top_w = jax.nn.softmax(top_val, axis=-1) * g_prob
    e_w = jnp.sum(jax.nn.one_hot(top_idx, EXPERTS_PER_GROUP, dtype=jnp.float32) * top_w[..., None], axis=1)
    comb = e_w[:, None, :] * jax.nn.one_hot(grp, N_GROUPS, dtype=jnp.float32)[:, :, None]
    y = jnp.zeros((N, D_MODEL), jnp.float32)
    for gi in range(N_GROUPS):
        hdn = jax.nn.silu(jnp.einsum('nd,edf->nef', x, w_ug[gi])) * jnp.einsum('nd,edf->nef', x, w_u[gi])
        hdn = hdn * comb[:, gi, :, None].astype(hdn.dtype)
        y = y + jnp.einsum('nef,efd->nd', hdn, w_d[gi]).astype(jnp.float32)
    return y.astype(x.dtype)


def decoder_layer(x, k_past, v_past, s0, pos0, mem_k, mem_v, rel_table, lb,
                  norm_mix, w_in, hg_norm, w_branch_a, w_branch_b, w_mix_out,
                  norm_xattn, w_xq, w_xo, norm_ffn, w_rg, b_rg, w_re, b_re, w_ug, w_u, w_d):
    B, T, _ = x.shape
    h = rmsnorm(x, norm_mix) @ w_in
    hq, hf, hi, hg, mq, mk, mv, ga, gb = jnp.split(h, IN_SPLITS, axis=-1)
    oa, s_new = hgrn2_branch(hq, hf, hi, hg, lb, hg_norm, s0)
    heads = lambda a: a.reshape(B, T, MB_HEADS, MB_DIM)
    k_new, v_new = heads(mk), heads(mv)
    k_all = k_new if k_past is None else jnp.concatenate([k_past.astype(k_new.dtype), k_new], axis=1)
    v_all = v_new if v_past is None else jnp.concatenate([v_past.astype(v_new.dtype), v_new], axis=1)
    q_pos = pos0 + jnp.arange(T, dtype=jnp.int32)
    ob = moba_attention(heads(mq), k_all, v_all, q_pos, rel_table).reshape(B, T, MB_WIDTH)
    merged = jax.nn.sigmoid(ga) * (oa @ w_branch_a) + jax.nn.sigmoid(gb) * (ob @ w_branch_b)
    x = x + merged @ w_mix_out
    xq = (rmsnorm(x, norm_xattn) @ w_xq).reshape(B, T, XA_HEADS, XA_DIM)
    x = x + memory_attention(xq, mem_k, mem_v).reshape(B, T, XA_WIDTH).astype(x.dtype) @ w_xo
    xf = rmsnorm(x, norm_ffn).reshape(B * T, D_MODEL)
    x = x + hier_moe(xf, w_rg, b_rg, w_re, b_re, w_ug, w_u, w_d).reshape(B, T, D_MODEL)
    return x, k_new, v_new, s_new


def setup_inputs(seed: int = 0) -> dict:
    key = jax.random.key(seed)
    keys = jax.random.split(key, 40)
    ctr = [0]

    def nrm(shape, scale):
        k = keys[ctr[0]]
        ctr[0] += 1
        return jax.random.normal(k, shape, jnp.float32) * scale

    def gain(shape):
        return 1.0 + nrm(shape, 0.02)

    n_pages = PAST_LEN // PAGE_SIZE
    n_pool = (DEC_BATCH * n_pages * 5) // 4
    GE = N_GROUPS * EXPERTS_PER_GROUP
    x_prompt = nrm((BATCH, SEQ, D_MODEL), 1.0)
    x_sample = nrm((DEC_BATCH, DEC_SEQ, D_MODEL), 1.0)
    cache_k = nrm((DEPTH, n_pool, PAGE_SIZE, MB_HEADS, MB_DIM), 1.0)
    cache_v = nrm((DEPTH, n_pool, PAGE_SIZE, MB_HEADS, MB_DIM), 1.0)
    state_hgrn = nrm((DEPTH, DEC_BATCH, HG_HEADS, HG_DIM, HG_DIM), 0.3)
    cache_mem_k = nrm((DEPTH, DEC_BATCH, MEM_LEN, XA_HEADS, XA_DIM), 1.0)
    cache_mem_v = nrm((DEPTH, DEC_BATCH, MEM_LEN, XA_HEADS, XA_DIM), 1.0)
    perm = jax.random.permutation(keys[ctr[0]], n_pool)
    ctr[0] += 1
    page_table = perm[:DEC_BATCH * n_pages].reshape(DEC_BATCH, n_pages).astype(jnp.int32)
    mem_prompt = nrm((BATCH, MEM_LEN, D_MODEL), 1.0)
    return {
        'x_prompt': x_prompt,
        'x_sample': x_sample,
        'cache_k': cache_k,
        'cache_v': cache_v,
        'state_hgrn': state_hgrn,
        'cache_mem_k': cache_mem_k,
        'cache_mem_v': cache_mem_v,
        'page_table': page_table,
        'mem_prompt': mem_prompt,
        'norm_mix': gain((DEPTH, D_MODEL)),
        'w_in': nrm((DEPTH, D_MODEL, IN_COLS), D_MODEL ** -0.5),
        'hg_lb_logits': nrm((DEPTH + 1, HG_WIDTH), 0.5),
        'hg_norm': gain((DEPTH, HG_WIDTH)),
        'w_branch_a': nrm((DEPTH, HG_WIDTH, D_MODEL), HG_WIDTH ** -0.5),
        'w_branch_b': nrm((DEPTH, MB_WIDTH, D_MODEL), MB_WIDTH ** -0.5),
        'w_mix_out': nrm((DEPTH, D_MODEL, D_MODEL), D_MODEL ** -0.5),
        'rel_table': nrm((REL_BUCKETS, MB_HEADS), 0.5),
        'norm_xattn': gain((DEPTH, D_MODEL)),
        'norm_mem': gain((DEPTH, D_MODEL)),
        'w_xq': nrm((DEPTH, D_MODEL, XA_WIDTH), D_MODEL ** -0.5),
        'w_xk': nrm((DEPTH, D_MODEL, XA_WIDTH), D_MODEL ** -0.5),
        'w_xv': nrm((DEPTH, D_MODEL, XA_WIDTH), D_MODEL ** -0.5),
        'w_xo': nrm((DEPTH, XA_WIDTH, D_MODEL), XA_WIDTH ** -0.5),
        'norm_ffn': gain((DEPTH, D_MODEL)),
        'w_group_router': nrm((DEPTH, D_MODEL, N_GROUPS), D_MODEL ** -0.5),
        'b_group_router': nrm((DEPTH, N_GROUPS), 0.01),
        'w_expert_router': nrm((DEPTH, D_MODEL, GE), D_MODEL ** -0.5),
        'b_expert_router': nrm((DEPTH, GE), 0.01),
        'w_expert_gate': nrm((DEPTH, N_GROUPS, EXPERTS_PER_GROUP, D_MODEL, EXPERT_HIDDEN), D_MODEL ** -0.5),
        'w_expert_up': nrm((DEPTH, N_GROUPS, EXPERTS_PER_GROUP, D_MODEL, EXPERT_HIDDEN), D_MODEL ** -0.5),
        'w_expert_down': nrm((DEPTH, N_GROUPS, EXPERTS_PER_GROUP, EXPERT_HIDDEN, D_MODEL), EXPERT_HIDDEN ** -0.5),
        'norm_final': gain((D_MODEL,)),
    }


def reference(x_prompt, x_sample, cache_k, cache_v, state_hgrn, cache_mem_k, cache_mem_v, page_table,
              mem_prompt, norm_mix, w_in, hg_lb_logits, hg_norm, w_branch_a, w_branch_b, w_mix_out,
              rel_table, norm_xattn, norm_mem, w_xq, w_xk, w_xv, w_xo, norm_ffn,
              w_group_router, b_group_router, w_expert_router, b_expert_router,
              w_expert_gate, w_expert_up, w_expert_down, norm_final):
    B = x_prompt.shape[0]
    DB = x_sample.shape[0]
    past_len = page_table.shape[1] * PAGE_SIZE
    lower_bounds = jnp.cumsum(jax.nn.softmax(hg_lb_logits.astype(jnp.float32), axis=0), axis=0)[:DEPTH]
    hp, hs = x_prompt, x_sample
    kp_l, vp_l, sp_l, mkp_l, mvp_l, ks_l, vs_l, ss_l = ([] for _ in range(8))
    for l in range(DEPTH):
        layer_w = (rel_table, lower_bounds[l], norm_mix[l], w_in[l], hg_norm[l], w_branch_a[l],
                   w_branch_b[l], w_mix_out[l], norm_xattn[l], w_xq[l], w_xo[l], norm_ffn[l],
                   w_group_router[l], b_group_router[l], w_expert_router[l], b_expert_router[l],
                   w_expert_gate[l], w_expert_up[l], w_expert_down[l])
        mem_n = rmsnorm(mem_prompt, norm_mem[l])
        mk_p = (mem_n @ w_xk[l]).reshape(B, MEM_LEN, XA_HEADS, XA_DIM)
        mv_p = (mem_n @ w_xv[l]).reshape(B, MEM_LEN, XA_HEADS, XA_DIM)
        s0 = jnp.zeros((B, HG_HEADS, HG_DIM, HG_DIM), jnp.float32)
        hp, kp, vp, sp = decoder_layer(hp, None, None, s0, 0, mk_p, mv_p, *layer_w)
        k_past = cache_k[l][page_table].reshape(DB, past_len, MB_HEADS, MB_DIM)
        v_past = cache_v[l][page_table].reshape(DB, past_len, MB_HEADS, MB_DIM)
        hs, ks, vs, ss = decoder_layer(hs, k_past, v_past, state_hgrn[l], past_len,
                                       cache_mem_k[l], cache_mem_v[l], *layer_w)
        kp_l.append(kp)
        vp_l.append(vp)
        sp_l.append(sp)
        mkp_l.append(mk_p)
        mvp_l.append(mv_p)
        ks_l.append(ks)
        vs_l.append(vs)
        ss_l.append(ss)
    y_prompt = rmsnorm(hp, norm_final)
    y_sample = rmsnorm(hs, norm_final)
    return (y_prompt, y_sample, jnp.stack(kp_l), jnp.stack(vp_l), jnp.stack(sp_l),
            jnp.stack(mkp_l), jnp.stack(mvp_l), jnp.stack(ks_l), jnp.stack(vs_l), jnp.stack(ss_l))
```

```python
import functools
import math

import numpy as np
import jax
import jax.numpy as jnp
from jax import lax
from jax.experimental import pallas as pl
from jax.experimental.pallas import tpu as pltpu

F32 = jnp.float32
BF16 = jnp.bfloat16
I32 = jnp.int32

D_MODEL = 1024
DEPTH = 1
PAGE_SIZE = 128
HG_HEADS = 4
HG_DIM = 128
HG_WIDTH = HG_HEADS * HG_DIM
MB_HEADS = 4
MB_DIM = 128
MB_WIDTH = MB_HEADS * MB_DIM
MB_BLOCK = 256
MB_TOPK = 3
REL_BUCKETS = 32
REL_MAX_DIST = 128
REL_MAX_EXACT = REL_BUCKETS // 2
MEM_LEN = 256
XA_HEADS = 4
XA_DIM = 128
XA_WIDTH = XA_HEADS * XA_DIM
N_GROUPS = 4
EXPERTS_PER_GROUP = 8
N_EXPERTS = N_GROUPS * EXPERTS_PER_GROUP
EXPERT_TOPK = 2
EXPERT_HIDDEN = 256
NORM_EPS = 1e-6
IN_COLS = 4 * HG_WIDTH + 3 * MB_WIDTH + 2 * D_MODEL
COL_MQ = 4 * HG_WIDTH
COL_MK = COL_MQ + MB_WIDTH
COL_MV = COL_MK + MB_WIDTH
COL_GA = COL_MV + MB_WIDTH
COL_GB = COL_GA + D_MODEL

V7X_LANES = 128
V7X_VMEM_BYTES = 64 * 1024 * 1024
MIB = 1024 * 1024

NEG_INF = float("-inf")
GATE_PAD = 16
ROUTER_LANES = 128
HGRN_PROMPT_CHUNK = 64
HGRN_MIN_CHUNK = 16


def _params(semantics, vmem_mib):
    return pltpu.CompilerParams(dimension_semantics=semantics,
                                vmem_limit_bytes=min(vmem_mib * MIB, V7X_VMEM_BYTES - 8 * MIB))


def _dot(a, b):
    return jnp.dot(a, b, preferred_element_type=F32)


def _dot_nt(a, b):
    return lax.dot_general(a, b, (((1,), (1,)), ((), ())), preferred_element_type=F32)


def _dot_tn(a, b):
    return lax.dot_general(a, b, (((0,), (0,)), ((), ())), preferred_element_type=F32)


def _split2(a):
    hi = a.astype(BF16)
    lo = (a - hi.astype(F32)).astype(BF16)
    return hi, lo


def _dot_nt_f32acc(a, b):
    ah, al = _split2(a)
    bh, bl = _split2(b)
    return _dot_nt(ah, bh) + (_dot_nt(ah, bl) + _dot_nt(al, bh))


def _dot_f32acc(a, b):
    ah, al = _split2(a)
    bh, bl = _split2(b)
    return _dot(ah, bh) + (_dot(ah, bl) + _dot(al, bh))


def _rmsnorm(x, g):
    return x * lax.rsqrt(jnp.mean(x * x, axis=-1, keepdims=True) + NORM_EPS) * g


def _sigmoid(x):
    return 1.0 / (1.0 + jnp.exp(-x))


def _silu(x):
    return x * _sigmoid(x)


def _norm_matmul_kernel(x_ref, g_ref, w_ref, o_ref, xn_ref):
    @pl.when(pl.program_id(1) == 0)
    def _():
        xn_ref[...] = _rmsnorm(x_ref[...], g_ref[...]).astype(BF16)

    o_ref[...] = _dot(xn_ref[...], w_ref[...])


def _norm_matmul(x, g, w_bf16, *, tm, tn):
    m, d = x.shape
    n = w_bf16.shape[1]
    assert m % tm == 0 and n % tn == 0
    vmem = (2 * tm * d * 4 + tm * d * 2 + 2 * d * tn * 2 + 2 * tm * tn * 4) // MIB + 12
    return pl.pallas_call(
        _norm_matmul_kernel,
        grid=(m // tm, n // tn),
        in_specs=[pl.BlockSpec((tm, d), lambda i, j: (i, 0)),
                  pl.BlockSpec((1, d), lambda i, j: (0, 0)),
                  pl.BlockSpec((d, tn), lambda i, j: (0, j))],
        out_specs=pl.BlockSpec((tm, tn), lambda i, j: (i, j)),
        out_shape=jax.ShapeDtypeStruct((m, n), F32),
        scratch_shapes=[pltpu.VMEM((tm, d), BF16)],
        compiler_params=_params(("parallel", "arbitrary"), vmem),
        name="norm_matmul",
    )(x, g.reshape(1, d), w_bf16)


def _hgrn_levels(c):
    return int(round(math.log2(c)))


def _hgrn_sum_masks(c):
    t = np.arange(c)[:, None]
    j = np.arange(c)[None, :]
    rows = [j <= t, j > t]
    for lv in range(_hgrn_levels(c)):
        half = c >> (lv + 1)
        blk = 2 * half
        mid = (t // blk) * blk + half - 1
        upper = (t % blk) >= half
        rows.append(upper & (j > mid) & (j <= t))
        rows.append((~upper) & (j > t) & (j <= mid))
    return np.concatenate(rows, axis=0).astype(np.float32)


def _hgrn_kernel(hq_ref, hf_ref, hi_ref, hg_ref, lbl_ref, gn_ref, mask_ref, s0_ref,
                 o_ref, s_ref, st_ref, *, c, t_blk):
    j = pl.program_id(1)

    @pl.when(j == 0)
    def _():
        for hd in range(HG_HEADS):
            st_ref[hd] = s0_ref[0, hd].T

    lbl = lbl_ref[...]
    lmax = jnp.max(lbl, axis=0, keepdims=True)
    lexp = jnp.exp(lbl - lmax)
    lb_all = lexp[0:1, :] / jnp.sum(lexp, axis=0, keepdims=True)

    masks = mask_ref[...]
    pad = c - t_blk
    row = lax.broadcasted_iota(I32, (c, HG_DIM), 0)
    rr = lax.broadcasted_iota(I32, (c, c), 0)
    cc = lax.broadcasted_iota(I32, (c, c), 1)

    def padded(a):
        if pad == 0:
            return a
        return jnp.concatenate([a, jnp.zeros((pad, HG_DIM), F32)], axis=0)

    for hd in range(HG_HEADS):
        sl = slice(hd * HG_DIM, (hd + 1) * HG_DIM)
        xq = hq_ref[0][:, sl]
        lb = lb_all[:, sl]
        f = lb + (1.0 - lb) * _sigmoid(hf_ref[0][:, sl])
        q = padded(_silu(xq))
        logf = padded(jnp.log(f))
        kk = padded(1.0 - f)
        v = padded(hi_ref[0][:, sl])
        l1 = logf.astype(BF16)
        r1 = logf - l1.astype(F32)
        l2 = r1.astype(BF16)
        l3 = (r1 - l2.astype(F32)).astype(BF16)
        ex = jnp.exp(_dot(masks, l1) + (_dot(masks, l2) + _dot(masks, l3)))
        eb = ex[0:c]
        eb_rev = ex[c:2 * c]
        scores = jnp.zeros((c, c), F32)
        for lv in range(_hgrn_levels(c)):
            half = c >> (lv + 1)
            upper = (row & half) != 0
            base = (2 + 2 * lv) * c
            a = jnp.where(upper, q * ex[base:base + c], 0.0).astype(BF16)
            bm = jnp.where(upper, 0.0, kk * ex[base + c:base + 2 * c]).astype(BF16)
            scores = scores + jnp.where((rr ^ cc) < 2 * half, _dot_nt(a, bm), 0.0)
        diag = jnp.sum(q * kk, axis=-1, keepdims=True)
        st = st_ref[hd]
        vb = v.astype(BF16)
        o = (_dot_nt((q * eb).astype(BF16), st.astype(BF16))
             + _dot(scores.astype(BF16), vb) + diag * v)
        st_ref[hd] = st * eb[c - 1:c, :] + _dot_tn(vb, (kk * eb_rev).astype(BF16))
        o = o[0:t_blk]
        xg = hg_ref[0][:, sl]
        o_ref[0, :, sl] = _rmsnorm(o, gn_ref[:, sl]) * _silu(xg)

    @pl.when(j == pl.num_programs(1) - 1)
    def _():
        for hd in range(HG_HEADS):
            s_ref[0, hd] = st_ref[hd].T


def _hgrn(h3, lb_logits, hg_norm, s0):
    b, t, _ = h3.shape
    t_blk = math.gcd(t, HGRN_PROMPT_CHUNK)
    c = max(t_blk, HGRN_MIN_CHUNK)
    masks = jnp.asarray(_hgrn_sum_masks(c), BF16)
    col = lambda k: pl.BlockSpec((1, t_blk, HG_WIDTH), lambda i, j: (i, j, k))
    state_spec = pl.BlockSpec((1, HG_HEADS, HG_DIM, HG_DIM), lambda i, j: (i, 0, 0, 0))
    kern = functools.partial(_hgrn_kernel, c=c, t_blk=t_blk)
    return pl.pallas_call(
        kern,
        grid=(b, t // t_blk),
        in_specs=[col(0), col(1), col(2), col(3),
                  pl.BlockSpec(lb_logits.shape, lambda i, j: (0, 0)),
                  pl.BlockSpec((1, HG_WIDTH), lambda i, j: (0, 0)),
                  pl.BlockSpec(masks.shape, lambda i, j: (0, 0)),
                  state_spec],
        out_specs=[pl.BlockSpec((1, t_blk, HG_WIDTH), lambda i, j: (i, j, 0)), state_spec],
        out_shape=[jax.ShapeDtypeStruct((b, t, HG_WIDTH), F32),
                   jax.ShapeDtypeStruct(s0.shape, F32)],
        scratch_shapes=[pltpu.VMEM((HG_HEADS, HG_DIM, HG_DIM), F32)],
        compiler_params=_params(("parallel", "arbitrary"), 32),
        name="hgrn2",
    )(h3, h3, h3, h3, lb_logits, hg_norm.reshape(1, HG_WIDTH), masks, s0)


def _rel_bias(rel_ref, head, dist):
    d = jnp.maximum(dist, 0)
    df = jnp.maximum(d, 1).astype(F32)
    large = REL_MAX_EXACT + (jnp.log(df / REL_MAX_EXACT) / math.log(REL_MAX_DIST / REL_MAX_EXACT)
                             * (REL_BUCKETS - REL_MAX_EXACT)).astype(I32)
    large = jnp.minimum(large, REL_BUCKETS - 1)
    bucket = jnp.where(d < REL_MAX_EXACT, d, large)
    out = jnp.zeros(dist.shape, F32)
    for kb in range(REL_BUCKETS):
        out = jnp.where(bucket == kb, rel_ref[kb, head], out)
    return out


def _stack_rows(rows):
    ridx = lax.broadcasted_iota(I32, (GATE_PAD, MB_DIM), 0)
    out = jnp.zeros((GATE_PAD, MB_DIM), F32)
    for n, r in enumerate(rows):
        out = jnp.where(ridx == n, r, out)
    return out


def _topk_blocks(gate, n_cand, n_past):
    lane = lax.broadcasted_iota(I32, gate.shape, 1)
    past = lane < n_past
    g = jnp.where(past, gate, NEG_INF)
    rank = jnp.zeros(gate.shape, I32)
    for m in range(n_cand):
        gm = g[:, m:m + 1]
        ahead = jnp.where(gm > g, 1, jnp.where((gm == g) & (lane > m), 1, 0))
        rank = rank + ahead
    return jnp.where(past & (rank < MB_TOPK), 1.0, 0.0)


def _moba_prompt_kernel(rel_ref, q_ref, k_ref, v_ref, o_ref,
                        kb_ref, vb_ref, km_ref, bias_ref, m_ref, l_ref, acc_ref, *, n_blk):
    head = pl.program_id(0)
    b = pl.program_id(1)
    i = pl.program_id(2)
    blk = MB_BLOCK
    scale = MB_DIM ** -0.5

    @pl.when((b == 0) & (i == 0))
    def _():
        r = lax.broadcasted_iota(I32, (blk, blk), 0)
        c = lax.broadcasted_iota(I32, (blk, blk), 1)
        bias_ref[0] = _rel_bias(rel_ref, head, r - c)
        bias_ref[1] = _rel_bias(rel_ref, head, r - c + blk)

    @pl.when(i == 0)
    def _():
        kb_ref[...] = k_ref[0].astype(BF16)
        vb_ref[...] = v_ref[0].astype(BF16)
        means = [jnp.sum(k_ref[0, n * blk:(n + 1) * blk, :], axis=0, keepdims=True) * (1.0 / blk)
                 for n in range(n_blk)]
        km_ref[...] = _stack_rows(means)

    q = q_ref[0]
    qb = q.astype(BF16)
    sel = _topk_blocks(_dot_nt_f32acc(q, km_ref[...]), n_blk, i)

    own0 = pl.multiple_of(i * blk, blk)
    r = lax.broadcasted_iota(I32, (blk, blk), 0)
    c = lax.broadcasted_iota(I32, (blk, blk), 1)
    s = _dot_nt(qb, kb_ref[pl.ds(own0, blk), :]) * scale + bias_ref[0]
    s = jnp.where(c <= r, s, NEG_INF)
    m0 = jnp.max(s, axis=-1, keepdims=True)
    p = jnp.exp(s - m0)
    m_ref[...] = m0
    l_ref[...] = jnp.sum(p, axis=-1, keepdims=True)
    acc_ref[...] = _dot(p.astype(BF16), vb_ref[pl.ds(own0, blk), :])

    far_bias = rel_ref[REL_BUCKETS - 1, head]
    for n in range(n_blk - 1):
        @pl.when(n < i)
        def _(n=n):
            s = _dot_nt(qb, kb_ref[n * blk:(n + 1) * blk, :]) * scale
            s = s + jnp.where(n == i - 1, bias_ref[1], far_bias)
            s = jnp.where(sel[:, n:n + 1] > 0.0, s, NEG_INF)
            m_old = m_ref[...]
            m_new = jnp.maximum(m_old, jnp.max(s, axis=-1, keepdims=True))
            alpha = jnp.exp(m_old - m_new)
            p = jnp.exp(s - m_new)
            l_ref[...] = alpha * l_ref[...] + jnp.sum(p, axis=-1, keepdims=True)
            acc_ref[...] = alpha * acc_ref[...] + _dot(p.astype(BF16), vb_ref[n * blk:(n + 1) * blk, :])
            m_ref[...] = m_new

    o_ref[0] = acc_ref[...] / l_ref[...]


def _moba_prompt(h3, rel_table):
    b, t, _ = h3.shape
    assert t % MB_BLOCK == 0
    n_blk = t // MB_BLOCK
    assert n_blk <= GATE_PAD
    qcol, kcol, vcol = COL_MQ // MB_DIM, COL_MK // MB_DIM, COL_MV // MB_DIM
    kern = functools.partial(_moba_prompt_kernel, n_blk=n_blk)
    return pl.pallas_call(
        kern,
        grid=(MB_HEADS, b, n_blk),
        in_specs=[pl.BlockSpec(memory_space=pltpu.SMEM),
                  pl.BlockSpec((1, MB_BLOCK, MB_DIM), lambda h, bb, i: (bb, i, qcol + h)),
                  pl.BlockSpec((1, t, MB_DIM), lambda h, bb, i: (bb, 0, kcol + h)),
                  pl.BlockSpec((1, t, MB_DIM), lambda h, bb, i: (bb, 0, vcol + h))],
        out_specs=pl.BlockSpec((1, MB_BLOCK, MB_DIM), lambda h, bb, i: (bb, i, h)),
        out_shape=jax.ShapeDtypeStruct((b, t, MB_WIDTH), F32),
        scratch_shapes=[pltpu.VMEM((t, MB_DIM), BF16), pltpu.VMEM((t, MB_DIM), BF16),
                        pltpu.VMEM((GATE_PAD, MB_DIM), F32),
                        pltpu.VMEM((2, MB_BLOCK, MB_BLOCK), F32),
                        pltpu.VMEM((MB_BLOCK, 1), F32), pltpu.VMEM((MB_BLOCK, 1), F32),
                        pltpu.VMEM((MB_BLOCK, MB_DIM), F32)],
        compiler_params=_params(("arbitrary", "arbitrary", "arbitrary"), 32),
        name="moba_prompt",
    )(rel_table, h3, h3, h3)


def _moba_sample_kernel(pt_ref, rel_ref, q_ref, kn_ref, vn_ref, *refs, n_pages, past_len):
    del pt_ref
    kp = refs[:n_pages]
    vp = refs[n_pages:2 * n_pages]
    o_ref, bias_ref, bias_own_ref = refs[2 * n_pages:]
    t = q_ref.shape[1]
    pages_per_blk = MB_BLOCK // PAGE_SIZE
    n_past = past_len // MB_BLOCK
    scale = MB_DIM ** -0.5

    @pl.when(pl.program_id(0) == 0)
    def _():
        r = lax.broadcasted_iota(I32, (t, past_len), 0)
        c = lax.broadcasted_iota(I32, (t, past_len), 1)
        ro = lax.broadcasted_iota(I32, (t, t), 0)
        co = lax.broadcasted_iota(I32, (t, t), 1)
        for hd in range(MB_HEADS):
            bias_ref[hd] = _rel_bias(rel_ref, hd, past_len + r - c)
            bias_own_ref[hd] = _rel_bias(rel_ref, hd, ro - co)

    ro = lax.broadcasted_iota(I32, (t, t), 0)
    co = lax.broadcasted_iota(I32, (t, t), 1)
    for hd in range(MB_HEADS):
        sl = slice(hd * MB_DIM, (hd + 1) * MB_DIM)
        q = q_ref[0][:, sl]
        qb = q.astype(BF16)
        sums = [jnp.sum(kp[p][0, :, sl], axis=0, keepdims=True) for p in range(n_pages)]
        means = []
        for n in range(n_past):
            acc = sums[n * pages_per_blk]
            for pp in range(1, pages_per_blk):
                acc = acc + sums[n * pages_per_blk + pp]
            means.append(acc * (1.0 / MB_BLOCK))
        sel = _topk_blocks(_dot_nt_f32acc(q, _stack_rows(means)), n_past, n_past)

        s_own = _dot_nt(q, kn_ref[0][:, sl]) * scale + bias_own_ref[hd]
        s_own = jnp.where(co <= ro, s_own, NEG_INF)
        m = jnp.max(s_own, axis=-1, keepdims=True)
        s_past = []
        for p in range(n_pages):
            n = p // pages_per_blk
            s = _dot_nt(qb, kp[p][0, :, sl].astype(BF16)) * scale
            s = s + bias_ref[hd, :, p * PAGE_SIZE:(p + 1) * PAGE_SIZE]
            s = jnp.where(sel[:, n:n + 1] > 0.0, s, NEG_INF)
            s_past.append(s)
            m = jnp.maximum(m, jnp.max(s, axis=-1, keepdims=True))
        p_own = jnp.exp(s_own - m)
        l = jnp.sum(p_own, axis=-1, keepdims=True)
        out = _dot(p_own, vn_ref[0][:, sl])
        for p in range(n_pages):
            pr = jnp.exp(s_past[p] - m)
            l = l + jnp.sum(pr, axis=-1, keepdims=True)
            out = out + _dot(pr.astype(BF16), vp[p][0, :, sl].astype(BF16))
        o_ref[0, :, sl] = out / l


def _moba_sample(h3, cache_k, cache_v, page_table, rel_table):
    db, t, _ = h3.shape
    n_pages = page_table.shape[1]
    past_len = n_pages * PAGE_SIZE
    assert past_len % MB_BLOCK == 0 and t <= MB_BLOCK and past_len // MB_BLOCK < GATE_PAD
    qcol, kcol, vcol = COL_MQ // MB_WIDTH, COL_MK // MB_WIDTH, COL_MV // MB_WIDTH
    new = lambda k: pl.BlockSpec((1, t, MB_WIDTH), lambda i, pt: (i, 0, k))
    page = lambda p: pl.BlockSpec((1, PAGE_SIZE, MB_WIDTH), lambda i, pt: (pt[i, p], 0, 0))
    kern = functools.partial(_moba_sample_kernel, n_pages=n_pages, past_len=past_len)
    grid_spec = pltpu.PrefetchScalarGridSpec(
        num_scalar_prefetch=1,
        grid=(db,),
        in_specs=([pl.BlockSpec(memory_space=pltpu.SMEM), new(qcol), new(kcol), new(vcol)]
                  + [page(p) for p in range(n_pages)] * 2),
        out_specs=pl.BlockSpec((1, t, MB_WIDTH), lambda i, pt: (i, 0, 0)),
        scratch_shapes=[pltpu.VMEM((MB_HEADS, t, past_len), F32),
                        pltpu.VMEM((MB_HEADS, t, t), F32)],
    )
    return pl.pallas_call(
        kern,
        grid_spec=grid_spec,
        out_shape=jax.ShapeDtypeStruct((db, t, MB_WIDTH), F32),
        compiler_params=_params(("arbitrary",), 40),
        name="moba_sample",
    )(page_table, rel_table, h3, h3, h3, *([cache_k] * n_pages), *([cache_v] * n_pages))


def _merge_kernel(oa_ref, ob_ref, ga0_ref, ga1_ref, gb0_ref, gb1_ref, x_ref,
                  wa_ref, wb_ref, wo_ref, gx_ref, wq_ref, x1_ref, xq_ref):
    ga = jnp.concatenate([ga0_ref[...], ga1_ref[...]], axis=-1)
    gb = jnp.concatenate([gb0_ref[...], gb1_ref[...]], axis=-1)
    pa = _dot(oa_ref[...].astype(BF16), wa_ref[...])
    pb = _dot(ob_ref[...].astype(BF16), wb_ref[...])
    merged = _sigmoid(ga) * pa + _sigmoid(gb) * pb
    x1 = x_ref[...] + _dot(merged.astype(BF16), wo_ref[...])
    x1_ref[...] = x1
    xq_ref[...] = _dot(_rmsnorm(x1, gx_ref[...]).astype(BF16), wq_ref[...])


def _merge(oa, ob, h2, x, wa, wb, wo, gx, wq, *, tm):
    n = x.shape[0]
    assert n % tm == 0
    half = D_MODEL // 2
    tok = lambda w, k=0: pl.BlockSpec((tm, w), lambda i: (i, k))
    full = lambda a: pl.BlockSpec(a.shape, lambda i: (0, 0))
    gx2 = gx.reshape(1, D_MODEL)
    return pl.pallas_call(
        _merge_kernel,
        grid=(n // tm,),
        in_specs=[tok(HG_WIDTH), tok(MB_WIDTH),
                  tok(half, COL_GA // half), tok(half, COL_GA // half + 1),
                  tok(half, COL_GB // half), tok(half, COL_GB // half + 1),
                  tok(D_MODEL), full(wa), full(wb), full(wo), full(gx2), full(wq)],
        out_specs=[tok(D_MODEL), tok(XA_WIDTH)],
        out_shape=[jax.ShapeDtypeStruct((n, D_MODEL), F32),
                   jax.ShapeDtypeStruct((n, XA_WIDTH), F32)],
        compiler_params=_params(("parallel",), 48),
        name="merge_mix",
    )(oa, ob, h2, h2, h2, h2, x, wa, wb, wo, gx2, wq)


def _mem_attn_kernel(q_ref, k_ref, v_ref, o_ref):
    scale = XA_DIM ** -0.5
    for hd in range(XA_HEADS):
        sl = slice(hd * XA_DIM, (hd + 1) * XA_DIM)
        s = _dot_nt(q_ref[0][:, sl].astype(BF16), k_ref[0][:, sl].astype(BF16)) * scale
        m = jnp.max(s, axis=-1, keepdims=True)
        p = jnp.exp(s - m)
        l = jnp.sum(p, axis=-1, keepdims=True)
        o_ref[0, :, sl] = _dot(p.astype(BF16), v_ref[0][:, sl].astype(BF16)) / l


def _mem_attn(xq3, mem_k, mem_v, *, tq):
    b, t, _ = xq3.shape
    assert t % tq == 0
    mem = pl.BlockSpec((1, mem_k.shape[1], XA_WIDTH), lambda i, j: (i, 0, 0))
    qs = pl.BlockSpec((1, tq, XA_WIDTH), lambda i, j: (i, j, 0))
    return pl.pallas_call(
        _mem_attn_kernel,
        grid=(b, t // tq),
        in_specs=[qs, mem, mem],
        out_specs=qs,
        out_shape=jax.ShapeDtypeStruct((b, t, XA_WIDTH), F32),
        compiler_params=_params(("parallel", "parallel"), 32),
        name="mem_attn",
    )(xq3, mem_k, mem_v)


def _route(logits):
    lane = lax.broadcasted_iota(I32, logits.shape, 1).astype(F32)
    first = lambda hit: jnp.min(jnp.where(hit, lane, float(ROUTER_LANES)), axis=-1, keepdims=True)
    gl = jnp.where(lane < N_GROUPS, logits, NEG_INF)
    gmax = jnp.max(gl, axis=-1, keepdims=True)
    grp = first(gl == gmax)
    g_prob = 1.0 / jnp.sum(jnp.exp(gl - gmax), axis=-1, keepdims=True)
    e_lo = N_GROUPS + grp * EXPERTS_PER_GROUP
    in_grp = (lane >= e_lo) & (lane < e_lo + EXPERTS_PER_GROUP)
    el = jnp.where(in_grp, logits, NEG_INF)
    top1 = jnp.max(el, axis=-1, keepdims=True)
    idx1 = first(el == top1)
    el2 = jnp.where(lane == idx1, NEG_INF, el)
    top2 = jnp.max(el2, axis=-1, keepdims=True)
    idx2 = first(el2 == top2)
    e2 = jnp.exp(top2 - top1)
    w1 = g_prob / (1.0 + e2)
    w2 = w1 * e2
    return jnp.where(lane == idx1, w1, 0.0) + jnp.where(lane == idx2, w2, 0.0)


def _moe_kernel(x1_ref, at_ref, wxo_ref, gf_ref, wr_ref, br_ref, wg_ref, wu_ref, wd_ref, gn_ref,
                y_ref, x2_ref, xf_ref, comb_ref, acc_ref):
    e = pl.program_id(1)

    @pl.when(e == 0)
    def _():
        x2 = x1_ref[...] + _dot(at_ref[...].astype(BF16), wxo_ref[...])
        x2_ref[...] = x2
        xf = _rmsnorm(x2, gf_ref[...])
        xf_ref[...] = xf.astype(BF16)
        comb_ref[...] = _route(_dot_f32acc(xf, wr_ref[...]) + br_ref[...])
        acc_ref[...] = jnp.zeros(acc_ref.shape, F32)

    comb = comb_ref[...]
    lane = lax.broadcasted_iota(I32, comb.shape, 1)
    w_e = jnp.sum(jnp.where(lane == e + N_GROUPS, comb, 0.0), axis=-1, keepdims=True)
    xf = xf_ref[...]
    hdn = _silu(_dot(xf, wg_ref[0])) * _dot(xf, wu_ref[0])
    acc_ref[...] += _dot((hdn * w_e).astype(BF16), wd_ref[0])

    @pl.when(e == pl.num_programs(1) - 1)
    def _():
        y_ref[...] = _rmsnorm(x2_ref[...] + acc_ref[...], gn_ref[...])


def _moe(x1, attn, wxo, g_ffn, w_router, b_router, wg, wu, wd, g_final, *, tm):
    n = x1.shape[0]
    assert n % tm == 0
    tok = lambda w: pl.BlockSpec((tm, w), lambda i, e: (i, 0))
    full = lambda a: pl.BlockSpec(a.shape, lambda i, e: (0, 0))
    gf2 = g_ffn.reshape(1, D_MODEL)
    gn2 = g_final.reshape(1, D_MODEL)
    return pl.pallas_call(
        _moe_kernel,
        grid=(n // tm, N_EXPERTS),
        in_specs=[tok(D_MODEL), tok(XA_WIDTH), full(wxo), full(gf2), full(w_router), full(b_router),
                  pl.BlockSpec((1, D_MODEL, EXPERT_HIDDEN), lambda i, e: (e, 0, 0)),
                  pl.BlockSpec((1, D_MODEL, EXPERT_HIDDEN), lambda i, e: (e, 0, 0)),
                  pl.BlockSpec((1, EXPERT_HIDDEN, D_MODEL), lambda i, e: (e, 0, 0)),
                  full(gn2)],
        out_specs=tok(D_MODEL),
        out_shape=jax.ShapeDtypeStruct((n, D_MODEL), F32),
        scratch_shapes=[pltpu.VMEM((tm, D_MODEL), F32), pltpu.VMEM((tm, D_MODEL), BF16),
                        pltpu.VMEM((tm, ROUTER_LANES), F32), pltpu.VMEM((tm, D_MODEL), F32)],
        compiler_params=_params(("parallel", "arbitrary"), 48),
        name="xo_moe_norm",
    )(x1, attn, wxo, gf2, w_router, b_router, wg, wu, wd, gn2)


def _token_tile(n):
    return 512 if n % 512 == 0 else n


def _layer(x3, w, s0, mem_k, mem_v, moba_fn):
    b, t, _ = x3.shape
    n = b * t
    x2d = x3.reshape(n, D_MODEL)
    tm = _token_tile(n)
    h2 = _norm_matmul(x2d, w["norm_mix"], w["w_in"], tm=tm, tn=512)
    h3 = h2.reshape(b, t, IN_COLS)
    oa, s_new = _hgrn(h3, w["hg_lb_logits"], w["hg_norm"], s0)
    ob = moba_fn(h3)
    x1, xq = _merge(oa.reshape(n, HG_WIDTH), ob.reshape(n, MB_WIDTH), h2, x2d,
                    w["w_branch_a"], w["w_branch_b"], w["w_mix_out"], w["norm_xattn"], w["w_xq"], tm=tm)
    attn = _mem_attn(xq.reshape(b, t, XA_WIDTH), mem_k, mem_v, tq=math.gcd(t, 512))
    y = _moe(x1, attn.reshape(n, XA_WIDTH), w["w_xo"], w["norm_ffn"], w["w_router"], w["b_router"],
             w["w_expert_gate"], w["w_expert_up"], w["w_expert_down"], w["norm_final"], tm=tm)
    k_new = h3[:, :, COL_MK:COL_MV].reshape(b, t, MB_HEADS, MB_DIM)
    v_new = h3[:, :, COL_MV:COL_GA].reshape(b, t, MB_HEADS, MB_DIM)
    return y.reshape(b, t, D_MODEL), k_new, v_new, s_new


def kernel(x_prompt, x_sample, cache_k, cache_v, state_hgrn, cache_mem_k, cache_mem_v, page_table, mem_prompt, norm_mix, w_in, hg_lb_logits, hg_norm, w_branch_a, w_branch_b, w_mix_out, rel_table, norm_xattn, norm_mem, w_xq, w_xk, w_xv, w_xo, norm_ffn, w_group_router, b_group_router, w_expert_router, b_expert_router, w_expert_gate, w_expert_up, w_expert_down, norm_final):
    assert w_in.shape[0] == DEPTH == 1 and hg_lb_logits.shape[0] == DEPTH + 1
    b = x_prompt.shape[0]
    db = x_sample.shape[0]
    n_pool = cache_k.shape[1]
    pad_lanes = ROUTER_LANES - N_GROUPS - N_EXPERTS
    w = {
        "norm_mix": norm_mix[0], "w_in": w_in[0].astype(BF16),
        "hg_lb_logits": hg_lb_logits, "hg_norm": hg_norm[0],
        "w_branch_a": w_branch_a[0].astype(BF16), "w_branch_b": w_branch_b[0].astype(BF16),
        "w_mix_out": w_mix_out[0].astype(BF16), "norm_xattn": norm_xattn[0],
        "w_xq": w_xq[0].astype(BF16), "w_xo": w_xo[0].astype(BF16), "norm_ffn": norm_ffn[0],
        "w_router": jnp.pad(jnp.concatenate([w_group_router[0], w_expert_router[0]], axis=1),
                            ((0, 0), (0, pad_lanes))),
        "b_router": jnp.pad(jnp.concatenate([b_group_router[0], b_expert_router[0]]),
                            (0, pad_lanes)).reshape(1, ROUTER_LANES),
        "w_expert_gate": w_expert_gate[0].reshape(N_EXPERTS, D_MODEL, EXPERT_HIDDEN).astype(BF16),
        "w_expert_up": w_expert_up[0].reshape(N_EXPERTS, D_MODEL, EXPERT_HIDDEN).astype(BF16),
        "w_expert_down": w_expert_down[0].reshape(N_EXPERTS, EXPERT_HIDDEN, D_MODEL).astype(BF16),
        "norm_final": norm_final,
    }

    w_mem = jnp.concatenate([w_xk[0], w_xv[0]], axis=1).astype(BF16)
    mem_kv = _norm_matmul(mem_prompt.reshape(b * MEM_LEN, D_MODEL), norm_mem[0], w_mem,
                          tm=_token_tile(b * MEM_LEN), tn=512).reshape(b, MEM_LEN, 2 * XA_WIDTH)
    mk_p = mem_kv[:, :, :XA_WIDTH]
    mv_p = mem_kv[:, :, XA_WIDTH:]
    s0 = jnp.zeros((b, HG_HEADS, HG_DIM, HG_DIM), F32)
    y_p, k_p, v_p, s_p = _layer(x_prompt, w, s0, mk_p, mv_p,
                                functools.partial(_moba_prompt, rel_table=rel_table))

    ck = cache_k[0].reshape(n_pool, PAGE_SIZE, MB_WIDTH)
    cv = cache_v[0].reshape(n_pool, PAGE_SIZE, MB_WIDTH)
    moba_s = functools.partial(_moba_sample, cache_k=ck, cache_v=cv, page_table=page_table,
                               rel_table=rel_table)
    y_s, k_s, v_s, s_s = _layer(x_sample, w, state_hgrn[0],
                                cache_mem_k[0].reshape(db, MEM_LEN, XA_WIDTH),
                                cache_mem_v[0].reshape(db, MEM_LEN, XA_WIDTH), moba_s)

    heads = lambda a: a.reshape(b, MEM_LEN, XA_HEADS, XA_DIM)[None]
    return (y_p, y_s, k_p[None], v_p[None], s_p[None], heads(mk_p), heads(mv_p),
            k_s[None], v_s[None], s_s[None])
```

```python
import functools
import math

import numpy as np
import jax
import jax.numpy as jnp
from jax import lax
from jax.experimental import pallas as pl
from jax.experimental.pallas import tpu as pltpu

F32 = jnp.float32
BF16 = jnp.bfloat16
I32 = jnp.int32

D_MODEL = 1024
DEPTH = 1
PAGE_SIZE = 128
HG_HEADS = 4
HG_DIM = 128
HG_WIDTH = HG_HEADS * HG_DIM
MB_HEADS = 4
MB_DIM = 128
MB_WIDTH = MB_HEADS * MB_DIM
MB_BLOCK = 256
MB_TOPK = 3
REL_BUCKETS = 32
REL_MAX_DIST = 128
REL_MAX_EXACT = REL_BUCKETS // 2
MEM_LEN = 256
XA_HEADS = 4
XA_DIM = 128
XA_WIDTH = XA_HEADS * XA_DIM
N_GROUPS = 4
EXPERTS_PER_GROUP = 8
N_EXPERTS = N_GROUPS * EXPERTS_PER_GROUP
EXPERT_TOPK = 2
EXPERT_HIDDEN = 256
NORM_EPS = 1e-6
IN_COLS = 4 * HG_WIDTH + 3 * MB_WIDTH + 2 * D_MODEL
COL_MQ = 4 * HG_WIDTH
COL_MK = COL_MQ + MB_WIDTH
COL_MV = COL_MK + MB_WIDTH
COL_GA = COL_MV + MB_WIDTH
COL_GB = COL_GA + D_MODEL

V7X_LANES = 128
V7X_VMEM_BYTES = 64 * 1024 * 1024
MIB = 1024 * 1024

NEG_INF = float("-inf")
GATE_PAD = 16
ROUTER_LANES = 128
HGRN_PROMPT_CHUNK = 64
HGRN_MIN_CHUNK = 16


def _params(semantics, vmem_mib):
    return pltpu.CompilerParams(dimension_semantics=semantics,
                                vmem_limit_bytes=min(vmem_mib * MIB, V7X_VMEM_BYTES - 8 * MIB))


def _dot(a, b):
    return jnp.dot(a, b, preferred_element_type=F32)


def _dot_nt(a, b):
    return lax.dot_general(a, b, (((1,), (1,)), ((), ())), preferred_element_type=F32)


def _dot_tn(a, b):
    return lax.dot_general(a, b, (((0,), (0,)), ((), ())), preferred_element_type=F32)


def _split2(a):
    hi = a.astype(BF16)
    lo = (a - hi.astype(F32)).astype(BF16)
    return hi, lo


def _dot_nt_f32acc(a, b):
    ah, al = _split2(a)
    bh, bl = _split2(b)
    return _dot_nt(ah, bh) + (_dot_nt(ah, bl) + _dot_nt(al, bh))


def _dot_f32acc(a, b):
    ah, al = _split2(a)
    bh, bl = _split2(b)
    return _dot(ah, bh) + (_dot(ah, bl) + _dot(al, bh))


def _rmsnorm(x, g):
    return x * lax.rsqrt(jnp.mean(x * x, axis=-1, keepdims=True) + NORM_EPS) * g


def _sigmoid(x):
    return 1.0 / (1.0 + jnp.exp(-x))


def _silu(x):
    return x * _sigmoid(x)


def _norm_matmul_kernel(x_ref, g_ref, w_ref, o_ref, xn_ref):
    @pl.when(pl.program_id(1) == 0)
    def _():
        xn_ref[...] = _rmsnorm(x_ref[...], g_ref[...]).astype(BF16)

    o_ref[...] = _dot(xn_ref[...], w_ref[...])


def _norm_matmul(x, g, w_bf16, *, tm, tn):
    m, d = x.shape
    n = w_bf16.shape[1]
    assert m % tm == 0 and n % tn == 0
    vmem = (2 * tm * d * 4 + tm * d * 2 + 2 * d * tn * 2 + 2 * tm * tn * 4) // MIB + 12
    return pl.pallas_call(
        _norm_matmul_kernel,
        grid=(m // tm, n // tn),
        in_specs=[pl.BlockSpec((tm, d), lambda i, j: (i, 0)),
                  pl.BlockSpec((1, d), lambda i, j: (0, 0)),
                  pl.BlockSpec((d, tn), lambda i, j: (0, j))],
        out_specs=pl.BlockSpec((tm, tn), lambda i, j: (i, j)),
        out_shape=jax.ShapeDtypeStruct((m, n), F32),
        scratch_shapes=[pltpu.VMEM((tm, d), BF16)],
        compiler_params=_params(("parallel", "arbitrary"), vmem),
        name="norm_matmul",
    )(x, g.reshape(1, d), w_bf16)


def _hgrn_levels(c):
    return int(round(math.log2(c)))


def _hgrn_sum_masks(c):
    t = np.arange(c)[:, None]
    j = np.arange(c)[None, :]
    rows = [j <= t, j > t]
    for lv in range(_hgrn_levels(c)):
        half = c >> (lv + 1)
        blk = 2 * half
        mid = (t // blk) * blk + half - 1
        upper = (t % blk) >= half
        rows.append(upper & (j > mid) & (j <= t))
        rows.append((~upper) & (j > t) & (j <= mid))
    return np.concatenate(rows, axis=0).astype(np.float32)


def _hgrn_kernel(hq_ref, hf_ref, hi_ref, hg_ref, lbl_ref, gn_ref, mask_ref, s0_ref,
                 o_ref, s_ref, st_ref, *, c, t_blk):
    j = pl.program_id(1)

    @pl.when(j == 0)
    def _():
        for hd in range(HG_HEADS):
            st_ref[hd] = s0_ref[0, hd].T

    lbl = lbl_ref[...]
    lmax = jnp.max(lbl, axis=0, keepdims=True)
    lexp = jnp.exp(lbl - lmax)
    lb_all = lexp[0:1, :] / jnp.sum(lexp, axis=0, keepdims=True)

    masks = mask_ref[...]
    pad = c - t_blk
    row = lax.broadcasted_iota(I32, (c, HG_DIM), 0)
    rr = lax.broadcasted_iota(I32, (c, c), 0)
    cc = lax.broadcasted_iota(I32, (c, c), 1)

    def padded(a):
        if pad == 0:
            return a
        return jnp.concatenate([a, jnp.zeros((pad, HG_DIM), F32)], axis=0)

    for hd in range(HG_HEADS):
        sl = slice(hd * HG_DIM, (hd + 1) * HG_DIM)
        xq = hq_ref[0][:, sl]
        lb = lb_all[:, sl]
        f = lb + (1.0 - lb) * _sigmoid(hf_ref[0][:, sl])
        q = padded(_silu(xq))
        logf = padded(jnp.log(f))
        kk = padded(1.0 - f)
        v = padded(hi_ref[0][:, sl])
        l1 = logf.astype(BF16)
        r1 = logf - l1.astype(F32)
        l2 = r1.astype(BF16)
        l3 = (r1 - l2.astype(F32)).astype(BF16)
        ex = jnp.exp(_dot(masks, l1) + (_dot(masks, l2) + _dot(masks, l3)))
        eb = ex[0:c]
        eb_rev = ex[c:2 * c]
        scores = jnp.zeros((c, c), F32)
        for lv in range(_hgrn_levels(c)):
            half = c >> (lv + 1)
            upper = (row & half) != 0
            base = (2 + 2 * lv) * c
            a = jnp.where(upper, q * ex[base:base + c], 0.0).astype(BF16)
            bm = jnp.where(upper, 0.0, kk * ex[base + c:base + 2 * c]).astype(BF16)
            scores = scores + jnp.where((rr ^ cc) < 2 * half, _dot_nt(a, bm), 0.0)
        diag = jnp.sum(q * kk, axis=-1, keepdims=True)
        st = st_ref[hd]
        vb = v.astype(BF16)
        o = (_dot_nt((q * eb).astype(BF16), st.astype(BF16))
             + _dot(scores.astype(BF16), vb) + diag * v)
        st_ref[hd] = st * eb[c - 1:c, :] + _dot_tn(vb, (kk * eb_rev).astype(BF16))
        o = o[0:t_blk]
        xg = hg_ref[0][:, sl]
        o_ref[0, :, sl] = _rmsnorm(o, gn_ref[:, sl]) * _silu(xg)

    @pl.when(j == pl.num_programs(1) - 1)
    def _():
        for hd in range(HG_HEADS):
            s_ref[0, hd] = st_ref[hd].T


def _hgrn(h3, lb_logits, hg_norm, s0):
    b, t, _ = h3.shape
    t_blk = math.gcd(t, HGRN_PROMPT_CHUNK)
    c = max(t_blk, HGRN_MIN_CHUNK)
    masks = jnp.asarray(_hgrn_sum_masks(c), BF16)
    col = lambda k: pl.BlockSpec((1, t_blk, HG_WIDTH), lambda i, j: (i, j, k))
    state_spec = pl.BlockSpec((1, HG_HEADS, HG_DIM, HG_DIM), lambda i, j: (i, 0, 0, 0))
    kern = functools.partial(_hgrn_kernel, c=c, t_blk=t_blk)
    return pl.pallas_call(
        kern,
        grid=(b, t // t_blk),
        in_specs=[col(0), col(1), col(2), col(3),
                  pl.BlockSpec(lb_logits.shape, lambda i, j: (0, 0)),
                  pl.BlockSpec((1, HG_WIDTH), lambda i, j: (0, 0)),
                  pl.BlockSpec(masks.shape, lambda i, j: (0, 0)),
                  state_spec],
        out_specs=[pl.BlockSpec((1, t_blk, HG_WIDTH), lambda i, j: (i, j, 0)), state_spec],
        out_shape=[jax.ShapeDtypeStruct((b, t, HG_WIDTH), F32),
                   jax.ShapeDtypeStruct(s0.shape, F32)],
        scratch_shapes=[pltpu.VMEM((HG_HEADS, HG_DIM, HG_DIM), F32)],
        compiler_params=_params(("parallel", "arbitrary"), 32),
        name="hgrn2",
    )(h3, h3, h3, h3, lb_logits, hg_norm.reshape(1, HG_WIDTH), masks, s0)


def _rel_bias(rel_ref, head, dist):
    d = jnp.maximum(dist, 0)
    df = jnp.maximum(d, 1).astype(F32)
    large = REL_MAX_EXACT + (jnp.log(df / REL_MAX_EXACT) / math.log(REL_MAX_DIST / REL_MAX_EXACT)
                             * (REL_BUCKETS - REL_MAX_EXACT)).astype(I32)
    large = jnp.minimum(large, REL_BUCKETS - 1)
    bucket = jnp.where(d < REL_MAX_EXACT, d, large)
    out = jnp.zeros(dist.shape, F32)
    for kb in range(REL_BUCKETS):
        out = jnp.where(bucket == kb, rel_ref[kb, head], out)
    return out


def _stack_rows(rows):
    ridx = lax.broadcasted_iota(I32, (GATE_PAD, MB_DIM), 0)
    out = jnp.zeros((GATE_PAD, MB_DIM), F32)
    for n, r in enumerate(rows):
        out = jnp.where(ridx == n, r, out)
    return out


def _topk_blocks(gate, n_cand, n_past):
    lane = lax.broadcasted_iota(I32, gate.shape, 1)
    past = lane < n_past
    g = jnp.where(past, gate, NEG_INF)
    rank = jnp.zeros(gate.shape, I32)
    for m in range(n_cand):
        gm = g[:, m:m + 1]
        ahead = jnp.where(gm > g, 1, jnp.where((gm == g) & (lane > m), 1, 0))
        rank = rank + ahead
    return jnp.where(past & (rank < MB_TOPK), 1.0, 0.0)


def _moba_prompt_kernel(rel_ref, q_ref, k_ref, v_ref, o_ref,
                        kb_ref, vb_ref, km_ref, bias_ref, m_ref, l_ref, acc_ref, *, n_blk):
    head = pl.program_id(0)
    b = pl.program_id(1)
    i = pl.program_id(2)
    blk = MB_BLOCK
    scale = MB_DIM ** -0.5

    @pl.when((b == 0) & (i == 0))
    def _():
        r = lax.broadcasted_iota(I32, (blk, blk), 0)
        c = lax.broadcasted_iota(I32, (blk, blk), 1)
        bias_ref[0] = _rel_bias(rel_ref, head, r - c)
        bias_ref[1] = _rel_bias(rel_ref, head, r - c + blk)

    @pl.when(i == 0)
    def _():
        kb_ref[...] = k_ref[0].astype(BF16)
        vb_ref[...] = v_ref[0].astype(BF16)
        means = [jnp.sum(k_ref[0, n * blk:(n + 1) * blk, :], axis=0, keepdims=True) * (1.0 / blk)
                 for n in range(n_blk)]
        km_ref[...] = _stack_rows(means)

    q = q_ref[0]
    qb = q.astype(BF16)
    sel = _topk_blocks(_dot_nt_f32acc(q, km_ref[...]), n_blk, i)

    own0 = pl.multiple_of(i * blk, blk)
    r = lax.broadcasted_iota(I32, (blk, blk), 0)
    c = lax.broadcasted_iota(I32, (blk, blk), 1)
    s = _dot_nt(qb, kb_ref[pl.ds(own0, blk), :]) * scale + bias_ref[0]
    s = jnp.where(c <= r, s, NEG_INF)
    m0 = jnp.max(s, axis=-1, keepdims=True)
    p = jnp.exp(s - m0)
    m_ref[...] = m0
    l_ref[...] = jnp.sum(p, axis=-1, keepdims=True)
    acc_ref[...] = _dot(p.astype(BF16), vb_ref[pl.ds(own0, blk), :])

    far_bias = rel_ref[REL_BUCKETS - 1, head]
    for n in range(n_blk - 1):
        @pl.when(n < i)
        def _(n=n):
            s = _dot_nt(qb, kb_ref[n * blk:(n + 1) * blk, :]) * scale
            s = s + jnp.where(n == i - 1, bias_ref[1], far_bias)
            s = jnp.where(sel[:, n:n + 1] > 0.0, s, NEG_INF)
            m_old = m_ref[...]
            m_new = jnp.maximum(m_old, jnp.max(s, axis=-1, keepdims=True))
            alpha = jnp.exp(m_old - m_new)
            p = jnp.exp(s - m_new)
            l_ref[...] = alpha * l_ref[...] + jnp.sum(p, axis=-1, keepdims=True)
            acc_ref[...] = alpha * acc_ref[...] + _dot(p.astype(BF16), vb_ref[n * blk:(n + 1) * blk, :])
            m_ref[...] = m_new

    o_ref[0] = acc_ref[...] / l_ref[...]


def _moba_prompt(h3, rel_table):
    b, t, _ = h3.shape
    assert t % MB_BLOCK == 0
    n_blk = t // MB_BLOCK
    assert n_blk <= GATE_PAD
    qcol, kcol, vcol = COL_MQ // MB_DIM, COL_MK // MB_DIM, COL_MV // MB_DIM
    kern = functools.partial(_moba_prompt_kernel, n_blk=n_blk)
    return pl.pallas_call(
        kern,
        grid=(MB_HEADS, b, n_blk),
        in_specs=[pl.BlockSpec(memory_space=pltpu.SMEM),
                  pl.BlockSpec((1, MB_BLOCK, MB_DIM), lambda h, bb, i: (bb, i, qcol + h)),
                  pl.BlockSpec((1, t, MB_DIM), lambda h, bb, i: (bb, 0, kcol + h)),
                  pl.BlockSpec((1, t, MB_DIM), lambda h, bb, i: (bb, 0, vcol + h))],
        out_specs=pl.BlockSpec((1, MB_BLOCK, MB_DIM), lambda h, bb, i: (bb, i, h)),
        out_shape=jax.ShapeDtypeStruct((b, t, MB_WIDTH), F32),
        scratch_shapes=[pltpu.VMEM((t, MB_DIM), BF16), pltpu.VMEM((t, MB_DIM), BF16),
                        pltpu.VMEM((GATE_PAD, MB_DIM), F32),
                        pltpu.VMEM((2, MB_BLOCK, MB_BLOCK), F32),
                        pltpu.VMEM((MB_BLOCK, 1), F32), pltpu.VMEM((MB_BLOCK, 1), F32),
                        pltpu.VMEM((MB_BLOCK, MB_DIM), F32)],
        compiler_params=_params(("arbitrary", "arbitrary", "arbitrary"), 32),
        name="moba_prompt",
    )(rel_table, h3, h3, h3)


def _moba_sample_kernel(pt_ref, rel_ref, q_ref, kn_ref, vn_ref, *refs, n_pages, past_len):
    del pt_ref
    kp = refs[:n_pages]
    vp = refs[n_pages:2 * n_pages]
    o_ref, bias_ref, bias_own_ref = refs[2 * n_pages:]
    t = q_ref.shape[1]
    rows = MB_HEADS * t
    page_rows = PAGE_SIZE * MB_HEADS
    pages_per_blk = MB_BLOCK // PAGE_SIZE
    n_past = past_len // MB_BLOCK
    scale = MB_DIM ** -0.5

    def head_rows(ref):
        x = ref[0]
        return jnp.concatenate([x[:, hd * MB_DIM:(hd + 1) * MB_DIM] for hd in range(MB_HEADS)], axis=0)

    @pl.when(pl.program_id(0) == 0)
    def _():
        tq = lax.broadcasted_iota(I32, (t, page_rows), 0)
        kc = lax.broadcasted_iota(I32, (t, page_rows), 1)
        for p in range(n_pages):
            dist = past_len + tq - (p * PAGE_SIZE + (kc >> 2))
            bias_ref[p] = jnp.concatenate(
                [jnp.where((kc & (MB_HEADS - 1)) == hd, _rel_bias(rel_ref, hd, dist), NEG_INF)
                 for hd in range(MB_HEADS)], axis=0)
        ro = lax.broadcasted_iota(I32, (t, rows), 0)
        co = lax.broadcasted_iota(I32, (t, rows), 1)
        for hd in range(MB_HEADS):
            own = _rel_bias(rel_ref, hd, ro - (co - hd * t))
            keep = (co >= hd * t) & (co <= hd * t + ro)
            bias_own_ref[hd * t:(hd + 1) * t, :] = jnp.where(keep, own, NEG_INF)

    q = head_rows(q_ref)
    qb = q.astype(BF16)
    row_head = lax.broadcasted_iota(I32, (rows, 1), 0) // t

    groups = page_rows // 8
    sums = [jnp.sum(kp[p][0].reshape(groups, 8, MB_DIM), axis=0) for p in range(n_pages)]
    blk_sums = []
    for n in range(n_past):
        acc = sums[n * pages_per_blk]
        for pp in range(1, pages_per_blk):
            acc = acc + sums[n * pages_per_blk + pp]
        blk_sums.append(acc)
    g_full = _dot_nt_f32acc(q, jnp.concatenate(blk_sums, axis=0))
    gc = lax.broadcasted_iota(I32, g_full.shape, 1)
    g_full = jnp.where((gc & (MB_HEADS - 1)) == row_head, g_full, 0.0)
    lane = lax.broadcasted_iota(I32, (rows, GATE_PAD), 1)
    gate = jnp.zeros((rows, GATE_PAD), F32)
    for n in range(n_past):
        g_n = jnp.sum(g_full[:, n * 8:(n + 1) * 8], axis=-1, keepdims=True) * (1.0 / MB_BLOCK)
        gate = jnp.where(lane == n, g_n, gate)
    sel = _topk_blocks(gate, n_past, n_past)

    s_own = _dot_nt(q, head_rows(kn_ref)) * scale + bias_own_ref[...]
    m = jnp.max(s_own, axis=-1, keepdims=True)
    s_past = []
    for p in range(n_pages):
        n = p // pages_per_blk
        s = _dot_nt(qb, kp[p][0].astype(BF16)) * scale + bias_ref[p]
        s = jnp.where(sel[:, n:n + 1] > 0.0, s, NEG_INF)
        s_past.append(s)
        m = jnp.maximum(m, jnp.max(s, axis=-1, keepdims=True))
    p_own = jnp.exp(s_own - m)
    l = jnp.sum(p_own, axis=-1, keepdims=True)
    out = _dot(p_own, head_rows(vn_ref))
    for p in range(n_pages):
        pr = jnp.exp(s_past[p] - m)
        l = l + jnp.sum(pr, axis=-1, keepdims=True)
        out = out + _dot(pr.astype(BF16), vp[p][0].astype(BF16))
    out = out / l
    for hd in range(MB_HEADS):
        o_ref[0, :, hd * MB_DIM:(hd + 1) * MB_DIM] = out[hd * t:(hd + 1) * t]


def _moba_sample(h3, cache_k, cache_v, page_table, rel_table):
    db, t, _ = h3.shape
    n_pages = page_table.shape[1]
    past_len = n_pages * PAGE_SIZE
    assert past_len % MB_BLOCK == 0 and t <= MB_BLOCK and past_len // MB_BLOCK < GATE_PAD
    assert MB_HEADS == 4 and t % 8 == 0
    qcol, kcol, vcol = COL_MQ // MB_WIDTH, COL_MK // MB_WIDTH, COL_MV // MB_WIDTH
    page_rows = PAGE_SIZE * MB_HEADS
    new = lambda k: pl.BlockSpec((1, t, MB_WIDTH), lambda i, pt: (i, 0, k))
    page = lambda p: pl.BlockSpec((1, page_rows, MB_DIM), lambda i, pt: (pt[i, p], 0, 0))
    kern = functools.partial(_moba_sample_kernel, n_pages=n_pages, past_len=past_len)
    grid_spec = pltpu.PrefetchScalarGridSpec(
        num_scalar_prefetch=1,
        grid=(db,),
        in_specs=([pl.BlockSpec(memory_space=pltpu.SMEM), new(qcol), new(kcol), new(vcol)]
                  + [page(p) for p in range(n_pages)] * 2),
        out_specs=pl.BlockSpec((1, t, MB_WIDTH), lambda i, pt: (i, 0, 0)),
        scratch_shapes=[pltpu.VMEM((n_pages, MB_HEADS * t, page_rows), F32),
                        pltpu.VMEM((MB_HEADS * t, MB_HEADS * t), F32)],
    )
    return pl.pallas_call(
        kern,
        grid_spec=grid_spec,
        out_shape=jax.ShapeDtypeStruct((db, t, MB_WIDTH), F32),
        compiler_params=_params(("arbitrary",), 40),
        name="moba_sample",
    )(page_table, rel_table, h3, h3, h3, *([cache_k] * n_pages), *([cache_v] * n_pages))


def _merge_kernel(oa_ref, ob_ref, ga0_ref, ga1_ref, gb0_ref, gb1_ref, x_ref,
                  wa_ref, wb_ref, wo_ref, gx_ref, wq_ref, x1_ref, xq_ref):
    ga = jnp.concatenate([ga0_ref[...], ga1_ref[...]], axis=-1)
    gb = jnp.concatenate([gb0_ref[...], gb1_ref[...]], axis=-1)
    pa = _dot(oa_ref[...].astype(BF16), wa_ref[...])
    pb = _dot(ob_ref[...].astype(BF16), wb_ref[...])
    merged = _sigmoid(ga) * pa + _sigmoid(gb) * pb
    x1 = x_ref[...] + _dot(merged.astype(BF16), wo_ref[...])
    x1_ref[...] = x1
    xq_ref[...] = _dot(_rmsnorm(x1, gx_ref[...]).astype(BF16), wq_ref[...])


def _merge(oa, ob, h2, x, wa, wb, wo, gx, wq, *, tm):
    n = x.shape[0]
    assert n % tm == 0
    half = D_MODEL // 2
    tok = lambda w, k=0: pl.BlockSpec((tm, w), lambda i: (i, k))
    full = lambda a: pl.BlockSpec(a.shape, lambda i: (0, 0))
    gx2 = gx.reshape(1, D_MODEL)
    return pl.pallas_call(
        _merge_kernel,
        grid=(n // tm,),
        in_specs=[tok(HG_WIDTH), tok(MB_WIDTH),
                  tok(half, COL_GA // half), tok(half, COL_GA // half + 1),
                  tok(half, COL_GB // half), tok(half, COL_GB // half + 1),
                  tok(D_MODEL), full(wa), full(wb), full(wo), full(gx2), full(wq)],
        out_specs=[tok(D_MODEL), tok(XA_WIDTH)],
        out_shape=[jax.ShapeDtypeStruct((n, D_MODEL), F32),
                   jax.ShapeDtypeStruct((n, XA_WIDTH), F32)],
        compiler_params=_params(("parallel",), 48),
        name="merge_mix",
    )(oa, ob, h2, h2, h2, h2, x, wa, wb, wo, gx2, wq)


def _mem_attn_kernel(q_ref, k_ref, v_ref, o_ref):
    scale = XA_DIM ** -0.5
    for hd in range(XA_HEADS):
        sl = slice(hd * XA_DIM, (hd + 1) * XA_DIM)
        s = _dot_nt(q_ref[0][:, sl].astype(BF16), k_ref[0][:, sl].astype(BF16)) * scale
        m = jnp.max(s, axis=-1, keepdims=True)
        p = jnp.exp(s - m)
        l = jnp.sum(p, axis=-1, keepdims=True)
        o_ref[0, :, sl] = _dot(p.astype(BF16), v_ref[0][:, sl].astype(BF16)) / l


def _mem_attn(xq3, mem_kv, *, tq):
    b, t, _ = xq3.shape
    assert t % tq == 0
    mem = lambda k: pl.BlockSpec((1, mem_kv.shape[1], XA_WIDTH), lambda i, j: (i, 0, k))
    qs = pl.BlockSpec((1, tq, XA_WIDTH), lambda i, j: (i, j, 0))
    return pl.pallas_call(
        _mem_attn_kernel,
        grid=(b, t // tq),
        in_specs=[qs, mem(0), mem(1)],
        out_specs=qs,
        out_shape=jax.ShapeDtypeStruct((b, t, XA_WIDTH), F32),
        compiler_params=_params(("parallel", "parallel"), 32),
        name="mem_attn",
    )(xq3, mem_kv, mem_kv)


def _mem_attn_rows_kernel(q_ref, k_ref, v_ref, o_ref):
    bb, t, _ = q_ref.shape
    rows = XA_HEADS * t
    scale = XA_DIM ** -0.5
    row_head = lax.broadcasted_iota(I32, (rows, 1), 0) // t
    kc = lax.broadcasted_iota(I32, (rows, k_ref.shape[1]), 1)
    same_head = (kc & (XA_HEADS - 1)) == row_head
    for i in range(bb):
        x = q_ref[i]
        q = jnp.concatenate([x[:, hd * XA_DIM:(hd + 1) * XA_DIM] for hd in range(XA_HEADS)], axis=0)
        s = _dot_nt(q.astype(BF16), k_ref[i].astype(BF16)) * scale
        s = jnp.where(same_head, s, NEG_INF)
        m = jnp.max(s, axis=-1, keepdims=True)
        p = jnp.exp(s - m)
        l = jnp.sum(p, axis=-1, keepdims=True)
        out = _dot(p.astype(BF16), v_ref[i].astype(BF16)) / l
        for hd in range(XA_HEADS):
            o_ref[i, :, hd * XA_DIM:(hd + 1) * XA_DIM] = out[hd * t:(hd + 1) * t]


def _mem_attn_rows(xq3, mem_k, mem_v, *, bb):
    b, t, _ = xq3.shape
    assert b % bb == 0 and XA_HEADS == 4 and t % 8 == 0
    mem = pl.BlockSpec((bb,) + mem_k.shape[1:], lambda i: (i, 0, 0))
    qs = pl.BlockSpec((bb, t, XA_WIDTH), lambda i: (i, 0, 0))
    return pl.pallas_call(
        _mem_attn_rows_kernel,
        grid=(b // bb,),
        in_specs=[qs, mem, mem],
        out_specs=qs,
        out_shape=jax.ShapeDtypeStruct((b, t, XA_WIDTH), F32),
        compiler_params=_params(("parallel",), 32),
        name="mem_attn_rows",
    )(xq3, mem_k, mem_v)


def _route(logits):
    lane = lax.broadcasted_iota(I32, logits.shape, 1).astype(F32)
    first = lambda hit: jnp.min(jnp.where(hit, lane, float(ROUTER_LANES)), axis=-1, keepdims=True)
    gl = jnp.where(lane < N_GROUPS, logits, NEG_INF)
    gmax = jnp.max(gl, axis=-1, keepdims=True)
    grp = first(gl == gmax)
    g_prob = 1.0 / jnp.sum(jnp.exp(gl - gmax), axis=-1, keepdims=True)
    e_lo = N_GROUPS + grp * EXPERTS_PER_GROUP
    in_grp = (lane >= e_lo) & (lane < e_lo + EXPERTS_PER_GROUP)
    el = jnp.where(in_grp, logits, NEG_INF)
    top1 = jnp.max(el, axis=-1, keepdims=True)
    idx1 = first(el == top1)
    el2 = jnp.where(lane == idx1, NEG_INF, el)
    top2 = jnp.max(el2, axis=-1, keepdims=True)
    idx2 = first(el2 == top2)
    e2 = jnp.exp(top2 - top1)
    w1 = g_prob / (1.0 + e2)
    w2 = w1 * e2
    return jnp.where(lane == idx1, w1, 0.0) + jnp.where(lane == idx2, w2, 0.0)


def _moe_kernel(x1_ref, at_ref, wxo_ref, gf_ref, wr_ref, br_ref, wg_ref, wu_ref, wd_ref, gn_ref,
                y_ref, x2_ref, xf_ref, comb_ref, acc_ref):
    e = pl.program_id(1)

    @pl.when(e == 0)
    def _():
        x2 = x1_ref[...] + _dot(at_ref[...].astype(BF16), wxo_ref[...])
        x2_ref[...] = x2
        xf = _rmsnorm(x2, gf_ref[...])
        xf_ref[...] = xf.astype(BF16)
        comb_ref[...] = _route(_dot_f32acc(xf, wr_ref[...]) + br_ref[...])
        acc_ref[...] = jnp.zeros(acc_ref.shape, F32)

    comb = comb_ref[...]
    lane = lax.broadcasted_iota(I32, comb.shape, 1)
    w_e = jnp.sum(jnp.where(lane == e + N_GROUPS, comb, 0.0), axis=-1, keepdims=True)
    xf = xf_ref[...]
    hdn = _silu(_dot(xf, wg_ref[0])) * _dot(xf, wu_ref[0])
    acc_ref[...] += _dot((hdn * w_e).astype(BF16), wd_ref[0])

    @pl.when(e == pl.num_programs(1) - 1)
    def _():
        y_ref[...] = _rmsnorm(x2_ref[...] + acc_ref[...], gn_ref[...])


def _moe(x1, attn, wxo, g_ffn, w_router, b_router, wg, wu, wd, g_final, *, tm):
    n = x1.shape[0]
    assert n % tm == 0
    tok = lambda w: pl.BlockSpec((tm, w), lambda i, e: (i, 0))
    full = lambda a: pl.BlockSpec(a.shape, lambda i, e: (0, 0))
    gf2 = g_ffn.reshape(1, D_MODEL)
    gn2 = g_final.reshape(1, D_MODEL)
    return pl.pallas_call(
        _moe_kernel,
        grid=(n // tm, N_EXPERTS),
        in_specs=[tok(D_MODEL), tok(XA_WIDTH), full(wxo), full(gf2), full(w_router), full(b_router),
                  pl.BlockSpec((1, D_MODEL, EXPERT_HIDDEN), lambda i, e: (e, 0, 0)),
                  pl.BlockSpec((1, D_MODEL, EXPERT_HIDDEN), lambda i, e: (e, 0, 0)),
                  pl.BlockSpec((1, EXPERT_HIDDEN, D_MODEL), lambda i, e: (e, 0, 0)),
                  full(gn2)],
        out_specs=tok(D_MODEL),
        out_shape=jax.ShapeDtypeStruct((n, D_MODEL), F32),
        scratch_shapes=[pltpu.VMEM((tm, D_MODEL), F32), pltpu.VMEM((tm, D_MODEL), BF16),
                        pltpu.VMEM((tm, ROUTER_LANES), F32), pltpu.VMEM((tm, D_MODEL), F32)],
        compiler_params=_params(("parallel", "arbitrary"), 48),
        name="xo_moe_norm",
    )(x1, attn, wxo, gf2, w_router, b_router, wg, wu, wd, gn2)


def _token_tile(n, cap=512):
    return cap if n % cap == 0 else n


def _col_tile(n, cap=1536):
    return max(c for c in range(V7X_LANES, cap + 1, V7X_LANES) if n % c == 0)


def _layer(x3, w, s0, moba_fn, mem_fn):
    b, t, _ = x3.shape
    n = b * t
    x2d = x3.reshape(n, D_MODEL)
    tm = _token_tile(n)
    h2 = _norm_matmul(x2d, w["norm_mix"], w["w_in"], tm=_token_tile(n, 1024), tn=_col_tile(IN_COLS))
    h3 = h2.reshape(b, t, IN_COLS)
    oa, s_new = _hgrn(h3, w["hg_lb_logits"], w["hg_norm"], s0)
    ob = moba_fn(h3)
    x1, xq = _merge(oa.reshape(n, HG_WIDTH), ob.reshape(n, MB_WIDTH), h2, x2d,
                    w["w_branch_a"], w["w_branch_b"], w["w_mix_out"], w["norm_xattn"], w["w_xq"], tm=tm)
    attn = mem_fn(xq.reshape(b, t, XA_WIDTH))
    y = _moe(x1, attn.reshape(n, XA_WIDTH), w["w_xo"], w["norm_ffn"], w["w_router"], w["b_router"],
             w["w_expert_gate"], w["w_expert_up"], w["w_expert_down"], w["norm_final"], tm=tm)
    k_new = h3[:, :, COL_MK:COL_MV].reshape(b, t, MB_HEADS, MB_DIM)
    v_new = h3[:, :, COL_MV:COL_GA].reshape(b, t, MB_HEADS, MB_DIM)
    return y.reshape(b, t, D_MODEL), k_new, v_new, s_new


def kernel(x_prompt, x_sample, cache_k, cache_v, state_hgrn, cache_mem_k, cache_mem_v, page_table, mem_prompt, norm_mix, w_in, hg_lb_logits, hg_norm, w_branch_a, w_branch_b, w_mix_out, rel_table, norm_xattn, norm_mem, w_xq, w_xk, w_xv, w_xo, norm_ffn, w_group_router, b_group_router, w_expert_router, b_expert_router, w_expert_gate, w_expert_up, w_expert_down, norm_final):
    assert w_in.shape[0] == DEPTH == 1 and hg_lb_logits.shape[0] == DEPTH + 1
    b = x_prompt.shape[0]
    db = x_sample.shape[0]
    n_pool = cache_k.shape[1]
    pad_lanes = ROUTER_LANES - N_GROUPS - N_EXPERTS
    w = {
        "norm_mix": norm_mix[0], "w_in": w_in[0].astype(BF16),
        "hg_lb_logits": hg_lb_logits, "hg_norm": hg_norm[0],
        "w_branch_a": w_branch_a[0].astype(BF16), "w_branch_b": w_branch_b[0].astype(BF16),
        "w_mix_out": w_mix_out[0].astype(BF16), "norm_xattn": norm_xattn[0],
        "w_xq": w_xq[0].astype(BF16), "w_xo": w_xo[0].astype(BF16), "norm_ffn": norm_ffn[0],
        "w_router": jnp.pad(jnp.concatenate([w_group_router[0], w_expert_router[0]], axis=1),
                            ((0, 0), (0, pad_lanes))),
        "b_router": jnp.pad(jnp.concatenate([b_group_router[0], b_expert_router[0]]),
                            (0, pad_lanes)).reshape(1, ROUTER_LANES),
        "w_expert_gate": w_expert_gate[0].reshape(N_EXPERTS, D_MODEL, EXPERT_HIDDEN).astype(BF16),
        "w_expert_up": w_expert_up[0].reshape(N_EXPERTS, D_MODEL, EXPERT_HIDDEN).astype(BF16),
        "w_expert_down": w_expert_down[0].reshape(N_EXPERTS, EXPERT_HIDDEN, D_MODEL).astype(BF16),
        "norm_final": norm_final,
    }

    w_mem = jnp.concatenate([w_xk[0], w_xv[0]], axis=1).astype(BF16)
    mem_kv = _norm_matmul(mem_prompt.reshape(b * MEM_LEN, D_MODEL), norm_mem[0], w_mem,
                          tm=_token_tile(b * MEM_LEN), tn=512).reshape(b, MEM_LEN, 2 * XA_WIDTH)
    mk_p = mem_kv[:, :, :XA_WIDTH]
    mv_p = mem_kv[:, :, XA_WIDTH:]
    s0 = jnp.zeros((b, HG_HEADS, HG_DIM, HG_DIM), F32)
    y_p, k_p, v_p, s_p = _layer(x_prompt, w, s0,
                                functools.partial(_moba_prompt, rel_table=rel_table),
                                functools.partial(_mem_attn, mem_kv=mem_kv, tq=512))

    ck = cache_k[0].reshape(n_pool, PAGE_SIZE * MB_HEADS, MB_DIM)
    cv = cache_v[0].reshape(n_pool, PAGE_SIZE * MB_HEADS, MB_DIM)
    moba_s = functools.partial(_moba_sample, cache_k=ck, cache_v=cv, page_table=page_table,
                               rel_table=rel_table)
    mem_s = functools.partial(_mem_attn_rows, bb=4,
                              mem_k=cache_mem_k[0].reshape(db, MEM_LEN * XA_HEADS, XA_DIM),
                              mem_v=cache_mem_v[0].reshape(db, MEM_LEN * XA_HEADS, XA_DIM))
    y_s, k_s, v_s, s_s = _layer(x_sample, w, state_hgrn[0], moba_s, mem_s)

    heads = lambda a: a.reshape(b, MEM_LEN, XA_HEADS, XA_DIM)[None]
    return (y_p, y_s, k_p[None], v_p[None], s_p[None], heads(mk_p), heads(mv_p),
            k_s[None], v_s[None], s_s[None])
```

```python
import functools
import math

import numpy as np
import jax
import jax.numpy as jnp
from jax import lax
from jax.experimental import pallas as pl
from jax.experimental.pallas import tpu as pltpu

F32 = jnp.float32
BF16 = jnp.bfloat16
I32 = jnp.int32

D_MODEL = 1024
DEPTH = 1
PAGE_SIZE = 128
HG_HEADS = 4
HG_DIM = 128
HG_WIDTH = HG_HEADS * HG_DIM
MB_HEADS = 4
MB_DIM = 128
MB_WIDTH = MB_HEADS * MB_DIM
MB_BLOCK = 256
MB_TOPK = 3
REL_BUCKETS = 32
REL_MAX_DIST = 128
REL_MAX_EXACT = REL_BUCKETS // 2
MEM_LEN = 256
XA_HEADS = 4
XA_DIM = 128
XA_WIDTH = XA_HEADS * XA_DIM
N_GROUPS = 4
EXPERTS_PER_GROUP = 8
N_EXPERTS = N_GROUPS * EXPERTS_PER_GROUP
EXPERT_TOPK = 2
EXPERT_HIDDEN = 256
NORM_EPS = 1e-6
IN_COLS = 4 * HG_WIDTH + 3 * MB_WIDTH + 2 * D_MODEL
COL_MQ = 4 * HG_WIDTH
COL_MK = COL_MQ + MB_WIDTH
COL_MV = COL_MK + MB_WIDTH
COL_GA = COL_MV + MB_WIDTH
COL_GB = COL_GA + D_MODEL

V7X_LANES = 128
V7X_VMEM_BYTES = 64 * 1024 * 1024
MIB = 1024 * 1024

NEG_INF = float("-inf")
MASK_BIG = -1e30
GATE_PAD = 16
ROUTER_LANES = 128
HGRN_PROMPT_CHUNK = 64
HGRN_MIN_CHUNK = 16
HGRN_CHAINS = 16


def _params(semantics, vmem_mib):
    return pltpu.CompilerParams(dimension_semantics=semantics,
                                vmem_limit_bytes=min(vmem_mib * MIB, V7X_VMEM_BYTES - 8 * MIB))


def _dot(a, b):
    return jnp.dot(a, b, preferred_element_type=F32)


def _dot_nt(a, b):
    return lax.dot_general(a, b, (((1,), (1,)), ((), ())), preferred_element_type=F32)


def _dot_tn(a, b):
    return lax.dot_general(a, b, (((0,), (0,)), ((), ())), preferred_element_type=F32)


def _split2(a):
    hi = a.astype(BF16)
    lo = (a - hi.astype(F32)).astype(BF16)
    return hi, lo


def _dot_nt_f32acc(a, b):
    ah, al = _split2(a)
    bh, bl = _split2(b)
    return _dot_nt(ah, bh) + (_dot_nt(ah, bl) + _dot_nt(al, bh))


def _dot_f32acc(a, b):
    ah, al = _split2(a)
    bh, bl = _split2(b)
    return _dot(ah, bh) + (_dot(ah, bl) + _dot(al, bh))


def _rmsnorm(x, g):
    return x * lax.rsqrt(jnp.mean(x * x, axis=-1, keepdims=True) + NORM_EPS) * g


def _sigmoid(x):
    return 1.0 / (1.0 + jnp.exp(-x))


def _silu(x):
    return x * _sigmoid(x)


def _norm_matmul_kernel(x_ref, g_ref, w_ref, o_ref, xn_ref):
    @pl.when(pl.program_id(1) == 0)
    def _():
        xn_ref[...] = _rmsnorm(x_ref[...], g_ref[...]).astype(BF16)

    o_ref[...] = _dot(xn_ref[...], w_ref[...])


def _norm_matmul(x, g, w_bf16, *, tm, tn):
    m, d = x.shape
    n = w_bf16.shape[1]
    assert m % tm == 0 and n % tn == 0
    vmem = (2 * tm * d * 4 + tm * d * 2 + 2 * d * tn * 2 + 2 * tm * tn * 4) // MIB + 12
    return pl.pallas_call(
        _norm_matmul_kernel,
        grid=(m // tm, n // tn),
        in_specs=[pl.BlockSpec((tm, d), lambda i, j: (i, 0)),
                  pl.BlockSpec((1, d), lambda i, j: (0, 0)),
                  pl.BlockSpec((d, tn), lambda i, j: (0, j))],
        out_specs=pl.BlockSpec((tm, tn), lambda i, j: (i, j)),
        out_shape=jax.ShapeDtypeStruct((m, n), F32),
        scratch_shapes=[pltpu.VMEM((tm, d), BF16)],
        compiler_params=_params(("parallel", "arbitrary"), vmem),
        name="norm_matmul",
    )(x, g.reshape(1, d), w_bf16)


def _hgrn_levels(c):
    return int(round(math.log2(c)))


def _hgrn_sum_masks(c):
    t = np.arange(c)[:, None]
    j = np.arange(c)[None, :]
    rows = [j <= t, j > t]
    for lv in range(_hgrn_levels(c)):
        half = c >> (lv + 1)
        blk = 2 * half
        mid = (t // blk) * blk + half - 1
        upper = (t % blk) >= half
        rows.append((upper & (j > mid) & (j <= t)) | ((~upper) & (j > t) & (j <= mid)))
    return np.concatenate(rows, axis=0).astype(np.float32)


def _hgrn_kernel(hq_ref, hf_ref, hi_ref, hg_ref, lbl_ref, gn_ref, mask_ref, s0_ref,
                 o_ref, s_ref, st_ref, *, c, t_blk, bb):
    j = pl.program_id(1)
    chains = [(bi, hd) for bi in range(bb) for hd in range(HG_HEADS)]

    @pl.when(j == 0)
    def _():
        for bi, hd in chains:
            st_ref[bi * HG_HEADS + hd] = s0_ref[bi, hd].T

    lbl = lbl_ref[...]
    lmax = jnp.max(lbl, axis=0, keepdims=True)
    lexp = jnp.exp(lbl - lmax)
    lb_all = lexp[0:1, :] / jnp.sum(lexp, axis=0, keepdims=True)

    masks = mask_ref[...]
    pad = c - t_blk
    row = lax.broadcasted_iota(I32, (c, HG_DIM), 0)
    rr = lax.broadcasted_iota(I32, (c, c), 0)
    cc = lax.broadcasted_iota(I32, (c, c), 1)

    def padded(a):
        if pad == 0:
            return a
        return jnp.concatenate([a, jnp.zeros((pad, HG_DIM), F32)], axis=0)

    for bi, hd in chains:
        sl = slice(hd * HG_DIM, (hd + 1) * HG_DIM)
        xq = hq_ref[bi][:, sl]
        lb = lb_all[:, sl]
        f = lb + (1.0 - lb) * _sigmoid(hf_ref[bi][:, sl])
        q = padded(_silu(xq))
        logf = padded(jnp.log(f))
        kk = padded(1.0 - f)
        v = padded(hi_ref[bi][:, sl])
        l_hi, l_lo = _split2(logf)
        sums = _dot(masks, jnp.concatenate([l_hi, l_lo], axis=-1))
        ex = jnp.exp(sums[:, :HG_DIM] + sums[:, HG_DIM:])
        eb = ex[0:c]
        eb_rev = ex[c:2 * c]
        scores = jnp.zeros((c, c), F32)
        for lv in range(_hgrn_levels(c)):
            half = c >> (lv + 1)
            upper = (row & half) != 0
            ex_lv = ex[(2 + lv) * c:(3 + lv) * c]
            a = jnp.where(upper, q * ex_lv, 0.0).astype(BF16)
            bm = jnp.where(upper, 0.0, kk * ex_lv).astype(BF16)
            scores = scores + jnp.where((rr ^ cc) < 2 * half, _dot_nt(a, bm), 0.0)
        diag = jnp.sum(q * kk, axis=-1, keepdims=True)
        st = st_ref[bi * HG_HEADS + hd]
        vb = v.astype(BF16)
        o = (_dot_nt((q * eb).astype(BF16), st.astype(BF16))
             + _dot(scores.astype(BF16), vb) + diag * v)
        st_ref[bi * HG_HEADS + hd] = st * eb[c - 1:c, :] + _dot_tn(vb, (kk * eb_rev).astype(BF16))
        o = o[0:t_blk]
        xg = hg_ref[bi][:, sl]
        o_ref[bi, :, sl] = _rmsnorm(o, gn_ref[:, sl]) * _silu(xg)

    @pl.when(j == pl.num_programs(1) - 1)
    def _():
        for bi, hd in chains:
            s_ref[bi, hd] = st_ref[bi * HG_HEADS + hd].T


def _hgrn(h3, lb_logits, hg_norm, s0):
    b, t, _ = h3.shape
    t_blk = math.gcd(t, HGRN_PROMPT_CHUNK)
    c = max(t_blk, HGRN_MIN_CHUNK)
    bb = math.gcd(b, HGRN_CHAINS // HG_HEADS) if t == t_blk else math.gcd(b, 2)
    masks = jnp.asarray(_hgrn_sum_masks(c), BF16)
    col = lambda k: pl.BlockSpec((bb, t_blk, HG_WIDTH), lambda i, j: (i, j, k))
    state_spec = pl.BlockSpec((bb, HG_HEADS, HG_DIM, HG_DIM), lambda i, j: (i, 0, 0, 0))
    kern = functools.partial(_hgrn_kernel, c=c, t_blk=t_blk, bb=bb)
    return pl.pallas_call(
        kern,
        grid=(b // bb, t // t_blk),
        in_specs=[col(0), col(1), col(2), col(3),
                  pl.BlockSpec(lb_logits.shape, lambda i, j: (0, 0)),
                  pl.BlockSpec((1, HG_WIDTH), lambda i, j: (0, 0)),
                  pl.BlockSpec(masks.shape, lambda i, j: (0, 0)),
                  state_spec],
        out_specs=[pl.BlockSpec((bb, t_blk, HG_WIDTH), lambda i, j: (i, j, 0)), state_spec],
        out_shape=[jax.ShapeDtypeStruct((b, t, HG_WIDTH), F32),
                   jax.ShapeDtypeStruct(s0.shape, F32)],
        scratch_shapes=[pltpu.VMEM((bb * HG_HEADS, HG_DIM, HG_DIM), F32)],
        compiler_params=_params(("parallel", "arbitrary"), 32),
        name="hgrn2",
    )(h3, h3, h3, h3, lb_logits, hg_norm.reshape(1, HG_WIDTH), masks, s0)


def _rel_bias(rel_ref, head, dist):
    d = jnp.maximum(dist, 0)
    df = jnp.maximum(d, 1).astype(F32)
    large = REL_MAX_EXACT + (jnp.log(df / REL_MAX_EXACT) / math.log(REL_MAX_DIST / REL_MAX_EXACT)
                             * (REL_BUCKETS - REL_MAX_EXACT)).astype(I32)
    large = jnp.minimum(large, REL_BUCKETS - 1)
    bucket = jnp.where(d < REL_MAX_EXACT, d, large)
    out = jnp.zeros(dist.shape, F32)
    for kb in range(REL_BUCKETS):
        out = jnp.where(bucket == kb, rel_ref[kb, head], out)
    return out


def _stack_rows(rows):
    ridx = lax.broadcasted_iota(I32, (GATE_PAD, MB_DIM), 0)
    out = jnp.zeros((GATE_PAD, MB_DIM), F32)
    for n, r in enumerate(rows):
        out = jnp.where(ridx == n, r, out)
    return out


def _topk_blocks(gate, n_cand, n_past):
    lane = lax.broadcasted_iota(I32, gate.shape, 1)
    past = lane < n_past
    g = jnp.where(past, gate, NEG_INF)
    rank = jnp.zeros(gate.shape, I32)
    for m in range(n_cand):
        gm = g[:, m:m + 1]
        ahead = jnp.where(gm > g, 1, jnp.where((gm == g) & (lane > m), 1, 0))
        rank = rank + ahead
    return jnp.where(past & (rank < MB_TOPK), 1.0, 0.0)


def _topk_blocks_t(gate_t, n_cand, n_past):
    blk_id = lax.broadcasted_iota(I32, gate_t.shape, 0)
    past = blk_id < n_past
    g = jnp.where(past, gate_t, NEG_INF)
    rank = jnp.zeros(gate_t.shape, I32)
    for m in range(n_cand):
        gm = g[m:m + 1, :]
        rank = rank + jnp.where(gm > g, 1, jnp.where((gm == g) & (blk_id > m), 1, 0))
    return past & (rank < MB_TOPK)


def _moba_prompt_kernel(rel_ref, q_ref, k_ref, v_ref, o_ref,
                        kx_ref, vx_ref, km_ref, bias_ref, eye_ref, s_ref, mx_ref, acc_ref, *, n_blk):
    b = pl.program_id(0)
    i = pl.program_id(1)
    blk = MB_BLOCK
    scale = MB_DIM ** -0.5
    heads = range(MB_HEADS)
    hsl = lambda hd: slice(hd * MB_DIM, (hd + 1) * MB_DIM)

    t = k_ref.shape[1]
    own_slot = n_blk - 1

    @pl.when((b == 0) & (i == 0))
    def _():
        r = lax.broadcasted_iota(I32, (blk, blk), 0)
        c = lax.broadcasted_iota(I32, (blk, blk), 1)
        eye_ref[...] = jnp.where(r == c, 1.0, 0.0).astype(BF16)
        kr = lax.broadcasted_iota(I32, (t, MB_DIM), 0)
        kc = lax.broadcasted_iota(I32, (t, MB_DIM), 1)
        blk_onehot = jnp.where(kr // blk == kc, 1.0, 0.0).astype(BF16)
        for hd in heads:
            bias_ref[hd, 0] = jnp.where(c <= r, _rel_bias(rel_ref, hd, r - c), NEG_INF)
            bias_ref[hd, 1] = _rel_bias(rel_ref, hd, r - c + blk)
            kx_ref[hd, :, MB_DIM:] = blk_onehot
            vx_ref[hd, :, MB_DIM:] = jnp.ones((t, MB_DIM), BF16)

    @pl.when(i == 0)
    def _():
        means = [jnp.sum(k_ref[0, n * blk:(n + 1) * blk, :], axis=0, keepdims=True) * (1.0 / blk)
                 for n in range(n_blk)]
        for hd in heads:
            kx_ref[hd, :, :MB_DIM] = k_ref[0][:, hsl(hd)].astype(BF16)
            vx_ref[hd, :, :MB_DIM] = v_ref[0][:, hsl(hd)].astype(BF16)
            km_ref[hd] = _stack_rows([mn[:, hsl(hd)] for mn in means])

    q = q_ref[0]
    blk_id = lax.broadcasted_iota(I32, (GATE_PAD, blk), 0)
    qx = []
    for hd in heads:
        qh = q[:, hsl(hd)]
        sel_t = _topk_blocks_t(_dot_nt_f32acc(km_ref[hd], qh), n_blk, i)
        qm_t = jnp.where(sel_t | (blk_id >= i), 0.0, MASK_BIG).astype(BF16)
        qm_t = jnp.concatenate([qm_t, jnp.zeros((MB_DIM - GATE_PAD, blk), BF16)], axis=0)
        qmask = _dot_nt(eye_ref[...], qm_t).astype(BF16)
        qx.append(jnp.concatenate([qh.astype(BF16), qmask], axis=-1))

    own0 = pl.multiple_of(i * blk, blk)
    for hd in heads:
        s = _dot_nt(qx[hd], kx_ref[hd, pl.ds(own0, blk), :]) * scale + bias_ref[hd, 0]
        s_ref[hd, own_slot] = s
        mx_ref[hd] = s

    for n in range(n_blk - 1):
        @pl.when(n < i)
        def _(n=n):
            for hd in heads:
                far_bias = rel_ref[REL_BUCKETS - 1, hd]
                s = _dot_nt(qx[hd], kx_ref[hd, n * blk:(n + 1) * blk, :]) * scale
                s = s + jnp.where(n == i - 1, bias_ref[hd, 1], far_bias)
                s_ref[hd, n] = s
                mx_ref[hd] = jnp.maximum(mx_ref[hd], s)

    for hd in heads:
        m = jnp.max(mx_ref[hd], axis=-1, keepdims=True)
        mx_ref[hd] = jnp.broadcast_to(m, (blk, 2 * MB_DIM))
        p = jnp.exp(s_ref[hd, own_slot] - mx_ref[hd])
        acc_ref[hd] = _dot(p.astype(BF16), vx_ref[hd, pl.ds(own0, blk), :])

    for n in range(n_blk - 1):
        @pl.when(n < i)
        def _(n=n):
            for hd in heads:
                p = jnp.exp(s_ref[hd, n] - mx_ref[hd])
                acc_ref[hd] += _dot(p.astype(BF16), vx_ref[hd, n * blk:(n + 1) * blk, :])

    for hd in heads:
        acc = acc_ref[hd]
        o_ref[0, :, hsl(hd)] = acc[:, :MB_DIM] / acc[:, MB_DIM:]


def _moba_prompt(h3, rel_table):
    b, t, _ = h3.shape
    assert t % MB_BLOCK == 0
    n_blk = t // MB_BLOCK
    assert n_blk <= GATE_PAD
    qcol, kcol, vcol = COL_MQ // MB_WIDTH, COL_MK // MB_WIDTH, COL_MV // MB_WIDTH
    kern = functools.partial(_moba_prompt_kernel, n_blk=n_blk)
    wide = 2 * MB_DIM
    vmem = (2 * 2 * t * MB_WIDTH * 4 + 2 * MB_HEADS * t * wide * 2
            + MB_HEADS * (n_blk + 4) * MB_BLOCK * wide * 4) // MIB + 12
    return pl.pallas_call(
        kern,
        grid=(b, n_blk),
        in_specs=[pl.BlockSpec(memory_space=pltpu.SMEM),
                  pl.BlockSpec((1, MB_BLOCK, MB_WIDTH), lambda bb, i: (bb, i, qcol)),
                  pl.BlockSpec((1, t, MB_WIDTH), lambda bb, i: (bb, 0, kcol)),
                  pl.BlockSpec((1, t, MB_WIDTH), lambda bb, i: (bb, 0, vcol))],
        out_specs=pl.BlockSpec((1, MB_BLOCK, MB_WIDTH), lambda bb, i: (bb, i, 0)),
        out_shape=jax.ShapeDtypeStruct((b, t, MB_WIDTH), F32),
        scratch_shapes=[pltpu.VMEM((MB_HEADS, t, wide), BF16), pltpu.VMEM((MB_HEADS, t, wide), BF16),
                        pltpu.VMEM((MB_HEADS, GATE_PAD, MB_DIM), F32),
                        pltpu.VMEM((MB_HEADS, 2, MB_BLOCK, MB_BLOCK), F32),
                        pltpu.VMEM((MB_BLOCK, MB_BLOCK), BF16),
                        pltpu.VMEM((MB_HEADS, n_blk, MB_BLOCK, MB_BLOCK), F32),
                        pltpu.VMEM((MB_HEADS, MB_BLOCK, wide), F32),
                        pltpu.VMEM((MB_HEADS, MB_BLOCK, wide), F32)],
        compiler_params=_params(("arbitrary", "arbitrary"), vmem),
        name="moba_prompt",
    )(rel_table, h3, h3, h3)


def _moba_sample_kernel(pt_ref, rel_ref, q_ref, kn_ref, vn_ref, *refs, n_pages, past_len):
    del pt_ref
    kp = refs[:n_pages]
    vp = refs[n_pages:2 * n_pages]
    o_ref, bias_ref, bias_own_ref = refs[2 * n_pages:]
    t = q_ref.shape[1]
    rows = MB_HEADS * t
    page_rows = PAGE_SIZE * MB_HEADS
    pages_per_blk = MB_BLOCK // PAGE_SIZE
    n_past = past_len // MB_BLOCK
    scale = MB_DIM ** -0.5

    def head_rows(ref):
        x = ref[0]
        return jnp.concatenate([x[:, hd * MB_DIM:(hd + 1) * MB_DIM] for hd in range(MB_HEADS)], axis=0)

    @pl.when(pl.program_id(0) == 0)
    def _():
        tq = lax.broadcasted_iota(I32, (t, page_rows), 0)
        kc = lax.broadcasted_iota(I32, (t, page_rows), 1)
        for p in range(n_pages):
            dist = past_len + tq - (p * PAGE_SIZE + (kc >> 2))
            bias_ref[p] = jnp.concatenate(
                [jnp.where((kc & (MB_HEADS - 1)) == hd, _rel_bias(rel_ref, hd, dist), NEG_INF)
                 for hd in range(MB_HEADS)], axis=0)
        ro = lax.broadcasted_iota(I32, (t, rows), 0)
        co = lax.broadcasted_iota(I32, (t, rows), 1)
        for hd in range(MB_HEADS):
            own = _rel_bias(rel_ref, hd, ro - (co - hd * t))
            keep = (co >= hd * t) & (co <= hd * t + ro)
            bias_own_ref[hd * t:(hd + 1) * t, :] = jnp.where(keep, own, NEG_INF)

    q = head_rows(q_ref)
    qb = q.astype(BF16)
    row_head = lax.broadcasted_iota(I32, (rows, 1), 0) // t

    groups = page_rows // 8
    sums = [jnp.sum(kp[p][0].reshape(groups, 8, MB_DIM), axis=0) for p in range(n_pages)]
    blk_sums = []
    for n in range(n_past):
        acc = sums[n * pages_per_blk]
        for pp in range(1, pages_per_blk):
            acc = acc + sums[n * pages_per_blk + pp]
        blk_sums.append(acc)
    g_full = _dot_nt_f32acc(q, jnp.concatenate(blk_sums, axis=0))
    gc = lax.broadcasted_iota(I32, g_full.shape, 1)
    g_full = jnp.where((gc & (MB_HEADS - 1)) == row_head, g_full, 0.0)
    lane = lax.broadcasted_iota(I32, (rows, GATE_PAD), 1)
    gate = jnp.zeros((rows, GATE_PAD), F32)
    for n in range(n_past):
        g_n = jnp.sum(g_full[:, n * 8:(n + 1) * 8], axis=-1, keepdims=True) * (1.0 / MB_BLOCK)
        gate = jnp.where(lane == n, g_n, gate)
    sel = _topk_blocks(gate, n_past, n_past)

    s_own = _dot_nt(q, head_rows(kn_ref)) * scale + bias_own_ref[...]
    m = jnp.max(s_own, axis=-1, keepdims=True)
    s_past = []
    for p in range(n_pages):
        n = p // pages_per_blk
        s = _dot_nt(qb, kp[p][0].astype(BF16)) * scale + bias_ref[p]
        s = jnp.where(sel[:, n:n + 1] > 0.0, s, NEG_INF)
        s_past.append(s)
        m = jnp.maximum(m, jnp.max(s, axis=-1, keepdims=True))
    p_own = jnp.exp(s_own - m)
    l = jnp.sum(p_own, axis=-1, keepdims=True)
    out = _dot(p_own, head_rows(vn_ref))
    for p in range(n_pages):
        pr = jnp.exp(s_past[p] - m)
        l = l + jnp.sum(pr, axis=-1, keepdims=True)
        out = out + _dot(pr.astype(BF16), vp[p][0].astype(BF16))
    out = out / l
    for hd in range(MB_HEADS):
        o_ref[0, :, hd * MB_DIM:(hd + 1) * MB_DIM] = out[hd * t:(hd + 1) * t]


def _moba_sample(h3, cache_k, cache_v, page_table, rel_table):
    db, t, _ = h3.shape
    n_pages = page_table.shape[1]
    past_len = n_pages * PAGE_SIZE
    assert past_len % MB_BLOCK == 0 and t <= MB_BLOCK and past_len // MB_BLOCK < GATE_PAD
    assert MB_HEADS == 4 and t % 8 == 0
    qcol, kcol, vcol = COL_MQ // MB_WIDTH, COL_MK // MB_WIDTH, COL_MV // MB_WIDTH
    page_rows = PAGE_SIZE * MB_HEADS
    new = lambda k: pl.BlockSpec((1, t, MB_WIDTH), lambda i, pt: (i, 0, k))
    page = lambda p: pl.BlockSpec((1, page_rows, MB_DIM), lambda i, pt: (pt[i, p], 0, 0))
    kern = functools.partial(_moba_sample_kernel, n_pages=n_pages, past_len=past_len)
    grid_spec = pltpu.PrefetchScalarGridSpec(
        num_scalar_prefetch=1,
        grid=(db,),
        in_specs=([pl.BlockSpec(memory_space=pltpu.SMEM), new(qcol), new(kcol), new(vcol)]
                  + [page(p) for p in range(n_pages)] * 2),
        out_specs=pl.BlockSpec((1, t, MB_WIDTH), lambda i, pt: (i, 0, 0)),
        scratch_shapes=[pltpu.VMEM((n_pages, MB_HEADS * t, page_rows), F32),
                        pltpu.VMEM((MB_HEADS * t, MB_HEADS * t), F32)],
    )
    return pl.pallas_call(
        kern,
        grid_spec=grid_spec,
        out_shape=jax.ShapeDtypeStruct((db, t, MB_WIDTH), F32),
        compiler_params=_params(("arbitrary",), 40),
        name="moba_sample",
    )(page_table, rel_table, h3, h3, h3, *([cache_k] * n_pages), *([cache_v] * n_pages))


def _merge_kernel(oa_ref, ob_ref, ga0_ref, ga1_ref, gb0_ref, gb1_ref, x_ref,
                  wa_ref, wb_ref, wo_ref, gx_ref, wq_ref, x1_ref, xq_ref):
    ga = jnp.concatenate([ga0_ref[...], ga1_ref[...]], axis=-1)
    gb = jnp.concatenate([gb0_ref[...], gb1_ref[...]], axis=-1)
    pa = _dot(oa_ref[...].astype(BF16), wa_ref[...])
    pb = _dot(ob_ref[...].astype(BF16), wb_ref[...])
    merged = _sigmoid(ga) * pa + _sigmoid(gb) * pb
    x1 = x_ref[...] + _dot(merged.astype(BF16), wo_ref[...])
    x1_ref[...] = x1
    xq_ref[...] = _dot(_rmsnorm(x1, gx_ref[...]).astype(BF16), wq_ref[...])


def _merge(oa, ob, h2, x, wa, wb, wo, gx, wq, *, tm):
    n = x.shape[0]
    assert n % tm == 0
    half = D_MODEL // 2
    tok = lambda w, k=0: pl.BlockSpec((tm, w), lambda i: (i, k))
    full = lambda a: pl.BlockSpec(a.shape, lambda i: (0, 0))
    gx2 = gx.reshape(1, D_MODEL)
    return pl.pallas_call(
        _merge_kernel,
        grid=(n // tm,),
        in_specs=[tok(HG_WIDTH), tok(MB_WIDTH),
                  tok(half, COL_GA // half), tok(half, COL_GA // half + 1),
                  tok(half, COL_GB // half), tok(half, COL_GB // half + 1),
                  tok(D_MODEL), full(wa), full(wb), full(wo), full(gx2), full(wq)],
        out_specs=[tok(D_MODEL), tok(XA_WIDTH)],
        out_shape=[jax.ShapeDtypeStruct((n, D_MODEL), F32),
                   jax.ShapeDtypeStruct((n, XA_WIDTH), F32)],
        compiler_params=_params(("parallel",), 48),
        name="merge_mix",
    )(oa, ob, h2, h2, h2, h2, x, wa, wb, wo, gx2, wq)


def _mem_attn_kernel(q_ref, k_ref, v_ref, o_ref):
    scale = XA_DIM ** -0.5
    for hd in range(XA_HEADS):
        sl = slice(hd * XA_DIM, (hd + 1) * XA_DIM)
        s = _dot_nt(q_ref[0][:, sl].astype(BF16), k_ref[0][:, sl].astype(BF16)) * scale
        m = jnp.max(s, axis=-1, keepdims=True)
        p = jnp.exp(s - m)
        l = jnp.sum(p, axis=-1, keepdims=True)
        o_ref[0, :, sl] = _dot(p.astype(BF16), v_ref[0][:, sl].astype(BF16)) / l


def _mem_attn(xq3, mem_kv, *, tq):
    b, t, _ = xq3.shape
    assert t % tq == 0
    mem = lambda k: pl.BlockSpec((1, mem_kv.shape[1], XA_WIDTH), lambda i, j: (i, 0, k))
    qs = pl.BlockSpec((1, tq, XA_WIDTH), lambda i, j: (i, j, 0))
    return pl.pallas_call(
        _mem_attn_kernel,
        grid=(b, t // tq),
        in_specs=[qs, mem(0), mem(1)],
        out_specs=qs,
        out_shape=jax.ShapeDtypeStruct((b, t, XA_WIDTH), F32),
        compiler_params=_params(("parallel", "parallel"), 32),
        name="mem_attn",
    )(xq3, mem_kv, mem_kv)


def _mem_attn_rows_kernel(q_ref, k_ref, v_ref, o_ref):
    bb, t, _ = q_ref.shape
    rows = XA_HEADS * t
    scale = XA_DIM ** -0.5
    row_head = lax.broadcasted_iota(I32, (rows, 1), 0) // t
    kc = lax.broadcasted_iota(I32, (rows, k_ref.shape[1]), 1)
    same_head = (kc & (XA_HEADS - 1)) == row_head
    for i in range(bb):
        x = q_ref[i]
        q = jnp.concatenate([x[:, hd * XA_DIM:(hd + 1) * XA_DIM] for hd in range(XA_HEADS)], axis=0)
        s = _dot_nt(q.astype(BF16), k_ref[i].astype(BF16)) * scale
        s = jnp.where(same_head, s, NEG_INF)
        m = jnp.max(s, axis=-1, keepdims=True)
        p = jnp.exp(s - m)
        l = jnp.sum(p, axis=-1, keepdims=True)
        out = _dot(p.astype(BF16), v_ref[i].astype(BF16)) / l
        for hd in range(XA_HEADS):
            o_ref[i, :, hd * XA_DIM:(hd + 1) * XA_DIM] = out[hd * t:(hd + 1) * t]


def _mem_attn_rows(xq3, mem_k, mem_v, *, bb):
    b, t, _ = xq3.shape
    assert b % bb == 0 and XA_HEADS == 4 and t % 8 == 0
    mem = pl.BlockSpec((bb,) + mem_k.shape[1:], lambda i: (i, 0, 0))
    qs = pl.BlockSpec((bb, t, XA_WIDTH), lambda i: (i, 0, 0))
    return pl.pallas_call(
        _mem_attn_rows_kernel,
        grid=(b // bb,),
        in_specs=[qs, mem, mem],
        out_specs=qs,
        out_shape=jax.ShapeDtypeStruct((b, t, XA_WIDTH), F32),
        compiler_params=_params(("parallel",), 32),
        name="mem_attn_rows",
    )(xq3, mem_k, mem_v)


def _route(logits):
    lane = lax.broadcasted_iota(I32, logits.shape, 1).astype(F32)
    first = lambda hit: jnp.min(jnp.where(hit, lane, float(ROUTER_LANES)), axis=-1, keepdims=True)
    gl = jnp.where(lane < N_GROUPS, logits, NEG_INF)
    gmax = jnp.max(gl, axis=-1, keepdims=True)
    grp = first(gl == gmax)
    g_prob = 1.0 / jnp.sum(jnp.exp(gl - gmax), axis=-1, keepdims=True)
    e_lo = N_GROUPS + grp * EXPERTS_PER_GROUP
    in_grp = (lane >= e_lo) & (lane < e_lo + EXPERTS_PER_GROUP)
    el = jnp.where(in_grp, logits, NEG_INF)
    top1 = jnp.max(el, axis=-1, keepdims=True)
    idx1 = first(el == top1)
    el2 = jnp.where(lane == idx1, NEG_INF, el)
    top2 = jnp.max(el2, axis=-1, keepdims=True)
    idx2 = first(el2 == top2)
    e2 = jnp.exp(top2 - top1)
    w1 = g_prob / (1.0 + e2)
    w2 = w1 * e2
    return jnp.where(lane == idx1, w1, 0.0) + jnp.where(lane == idx2, w2, 0.0)


def _moe_kernel(x1_ref, at_ref, wxo_ref, gf_ref, wr_ref, br_ref, wg_ref, wu_ref, wd_ref, gn_ref,
                y_ref, x2_ref, xf_ref, comb_ref, acc_ref):
    e = pl.program_id(1)

    @pl.when(e == 0)
    def _():
        x2 = x1_ref[...] + _dot(at_ref[...].astype(BF16), wxo_ref[...])
        x2_ref[...] = x2
        xf = _rmsnorm(x2, gf_ref[...])
        xf_ref[...] = xf.astype(BF16)
        comb_ref[...] = _route(_dot_f32acc(xf, wr_ref[...]) + br_ref[...])
        acc_ref[...] = jnp.zeros(acc_ref.shape, F32)

    comb = comb_ref[...]
    lane = lax.broadcasted_iota(I32, comb.shape, 1)
    w_e = jnp.sum(jnp.where(lane == e + N_GROUPS, comb, 0.0), axis=-1, keepdims=True)
    xf = xf_ref[...]
    hdn = _silu(_dot(xf, wg_ref[0])) * _dot(xf, wu_ref[0])
    acc_ref[...] += _dot((hdn * w_e).astype(BF16), wd_ref[0])

    @pl.when(e == pl.num_programs(1) - 1)
    def _():
        y_ref[...] = _rmsnorm(x2_ref[...] + acc_ref[...], gn_ref[...])


def _moe(x1, attn, wxo, g_ffn, w_router, b_router, wg, wu, wd, g_final, *, tm):
    n = x1.shape[0]
    assert n % tm == 0
    tok = lambda w: pl.BlockSpec((tm, w), lambda i, e: (i, 0))
    full = lambda a: pl.BlockSpec(a.shape, lambda i, e: (0, 0))
    gf2 = g_ffn.reshape(1, D_MODEL)
    gn2 = g_final.reshape(1, D_MODEL)
    return pl.pallas_call(
        _moe_kernel,
        grid=(n // tm, N_EXPERTS),
        in_specs=[tok(D_MODEL), tok(XA_WIDTH), full(wxo), full(gf2), full(w_router), full(b_router),
                  pl.BlockSpec((1, D_MODEL, EXPERT_HIDDEN), lambda i, e: (e, 0, 0)),
                  pl.BlockSpec((1, D_MODEL, EXPERT_HIDDEN), lambda i, e: (e, 0, 0)),
                  pl.BlockSpec((1, EXPERT_HIDDEN, D_MODEL), lambda i, e: (e, 0, 0)),
                  full(gn2)],
        out_specs=tok(D_MODEL),
        out_shape=jax.ShapeDtypeStruct((n, D_MODEL), F32),
        scratch_shapes=[pltpu.VMEM((tm, D_MODEL), F32), pltpu.VMEM((tm, D_MODEL), BF16),
                        pltpu.VMEM((tm, ROUTER_LANES), F32), pltpu.VMEM((tm, D_MODEL), F32)],
        compiler_params=_params(("parallel", "arbitrary"), 24 + 28 * tm // 1024),
        name="xo_moe_norm",
    )(x1, attn, wxo, gf2, w_router, b_router, wg, wu, wd, gn2)


def _token_tile(n, cap=512):
    return cap if n % cap == 0 else n


def _col_tile(n, cap=1536):
    return max(c for c in range(V7X_LANES, cap + 1, V7X_LANES) if n % c == 0)


def _layer(x3, w, s0, moba_fn, mem_fn):
    b, t, _ = x3.shape
    n = b * t
    x2d = x3.reshape(n, D_MODEL)
    tm = _token_tile(n)
    h2 = _norm_matmul(x2d, w["norm_mix"], w["w_in"], tm=_token_tile(n, 1024), tn=_col_tile(IN_COLS))
    h3 = h2.reshape(b, t, IN_COLS)
    oa, s_new = _hgrn(h3, w["hg_lb_logits"], w["hg_norm"], s0)
    ob = moba_fn(h3)
    x1, xq = _merge(oa.reshape(n, HG_WIDTH), ob.reshape(n, MB_WIDTH), h2, x2d,
                    w["w_branch_a"], w["w_branch_b"], w["w_mix_out"], w["norm_xattn"], w["w_xq"], tm=tm)
    attn = mem_fn(xq.reshape(b, t, XA_WIDTH))
    y = _moe(x1, attn.reshape(n, XA_WIDTH), w["w_xo"], w["norm_ffn"], w["w_router"], w["b_router"],
             w["w_expert_gate"], w["w_expert_up"], w["w_expert_down"], w["norm_final"],
             tm=_token_tile(n, 1024))
    k_new = h3[:, :, COL_MK:COL_MV].reshape(b, t, MB_HEADS, MB_DIM)
    v_new = h3[:, :, COL_MV:COL_GA].reshape(b, t, MB_HEADS, MB_DIM)
    return y.reshape(b, t, D_MODEL), k_new, v_new, s_new


def kernel(x_prompt, x_sample, cache_k, cache_v, state_hgrn, cache_mem_k, cache_mem_v, page_table, mem_prompt, norm_mix, w_in, hg_lb_logits, hg_norm, w_branch_a, w_branch_b, w_mix_out, rel_table, norm_xattn, norm_mem, w_xq, w_xk, w_xv, w_xo, norm_ffn, w_group_router, b_group_router, w_expert_router, b_expert_router, w_expert_gate, w_expert_up, w_expert_down, norm_final):
    assert w_in.shape[0] == DEPTH == 1 and hg_lb_logits.shape[0] == DEPTH + 1
    b = x_prompt.shape[0]
    db = x_sample.shape[0]
    n_pool = cache_k.shape[1]
    pad_lanes = ROUTER_LANES - N_GROUPS - N_EXPERTS
    w = {
        "norm_mix": norm_mix[0], "w_in": w_in[0].astype(BF16),
        "hg_lb_logits": hg_lb_logits, "hg_norm": hg_norm[0],
        "w_branch_a": w_branch_a[0].astype(BF16), "w_branch_b": w_branch_b[0].astype(BF16),
        "w_mix_out": w_mix_out[0].astype(BF16), "norm_xattn": norm_xattn[0],
        "w_xq": w_xq[0].astype(BF16), "w_xo": w_xo[0].astype(BF16), "norm_ffn": norm_ffn[0],
        "w_router": jnp.pad(jnp.concatenate([w_group_router[0], w_expert_router[0]], axis=1),
                            ((0, 0), (0, pad_lanes))),
        "b_router": jnp.pad(jnp.concatenate([b_group_router[0], b_expert_router[0]]),
                            (0, pad_lanes)).reshape(1, ROUTER_LANES),
        "w_expert_gate": w_expert_gate[0].reshape(N_EXPERTS, D_MODEL, EXPERT_HIDDEN).astype(BF16),
        "w_expert_up": w_expert_up[0].reshape(N_EXPERTS, D_MODEL, EXPERT_HIDDEN).astype(BF16),
        "w_expert_down": w_expert_down[0].reshape(N_EXPERTS, EXPERT_HIDDEN, D_MODEL).astype(BF16),
        "norm_final": norm_final,
    }

    w_mem = jnp.concatenate([w_xk[0], w_xv[0]], axis=1).astype(BF16)
    mem_kv = _norm_matmul(mem_prompt.reshape(b * MEM_LEN, D_MODEL), norm_mem[0], w_mem,
                          tm=_token_tile(b * MEM_LEN), tn=512).reshape(b, MEM_LEN, 2 * XA_WIDTH)
    mk_p = mem_kv[:, :, :XA_WIDTH]
    mv_p = mem_kv[:, :, XA_WIDTH:]
    s0 = jnp.zeros((b, HG_HEADS, HG_DIM, HG_DIM), F32)
    y_p, k_p, v_p, s_p = _layer(x_prompt, w, s0,
                                functools.partial(_moba_prompt, rel_table=rel_table),
                                functools.partial(_mem_attn, mem_kv=mem_kv, tq=512))

    ck = cache_k[0].reshape(n_pool, PAGE_SIZE * MB_HEADS, MB_DIM)
    cv = cache_v[0].reshape(n_pool, PAGE_SIZE * MB_HEADS, MB_DIM)
    moba_s = functools.partial(_moba_sample, cache_k=ck, cache_v=cv, page_table=page_table,
                               rel_table=rel_table)
    mem_s = functools.partial(_mem_attn_rows, bb=4,
                              mem_k=cache_mem_k[0].reshape(db, MEM_LEN * XA_HEADS, XA_DIM),
                              mem_v=cache_mem_v[0].reshape(db, MEM_LEN * XA_HEADS, XA_DIM))
    y_s, k_s, v_s, s_s = _layer(x_sample, w, state_hgrn[0], moba_s, mem_s)

    heads = lambda a: a.reshape(b, MEM_LEN, XA_HEADS, XA_DIM)[None]
    return (y_p, y_s, k_p[None], v_p[None], s_p[None], heads(mk_p), heads(mv_p),
            k_s[None], v_s[None], s_s[None])
```

```python
import functools
import math

import numpy as np
import jax
import jax.numpy as jnp
from jax import lax
from jax.experimental import pallas as pl
from jax.experimental.pallas import tpu as pltpu

F32 = jnp.float32
BF16 = jnp.bfloat16
I32 = jnp.int32

D_MODEL = 1024
DEPTH = 1
PAGE_SIZE = 128
HG_HEADS = 4
HG_DIM = 128
HG_WIDTH = HG_HEADS * HG_DIM
MB_HEADS = 4
MB_DIM = 128
MB_WIDTH = MB_HEADS * MB_DIM
MB_BLOCK = 256
MB_TOPK = 3
REL_BUCKETS = 32
REL_MAX_DIST = 128
REL_MAX_EXACT = REL_BUCKETS // 2
MEM_LEN = 256
XA_HEADS = 4
XA_DIM = 128
XA_WIDTH = XA_HEADS * XA_DIM
N_GROUPS = 4
EXPERTS_PER_GROUP = 8
N_EXPERTS = N_GROUPS * EXPERTS_PER_GROUP
EXPERT_TOPK = 2
EXPERT_HIDDEN = 256
NORM_EPS = 1e-6
IN_COLS = 4 * HG_WIDTH + 3 * MB_WIDTH + 2 * D_MODEL
COL_MQ = 4 * HG_WIDTH
COL_MK = COL_MQ + MB_WIDTH
COL_MV = COL_MK + MB_WIDTH
COL_GA = COL_MV + MB_WIDTH
COL_GB = COL_GA + D_MODEL

V7X_LANES = 128
V7X_VMEM_BYTES = 64 * 1024 * 1024
MIB = 1024 * 1024

NEG_INF = float("-inf")
MASK_BIG = -1e30
GATE_PAD = 16
ROUTER_LANES = 128
GRP_LANE = 0
POS_SPLIT = 32.0
MOE_CHUNK_ROWS = 320
MOE_EXPERTS_PER_STEP = 4
HGRN_PROMPT_CHUNK = 64
HGRN_MIN_CHUNK = 16
HGRN_CHAINS = 16


def _params(semantics, vmem_mib):
    return pltpu.CompilerParams(dimension_semantics=semantics,
                                vmem_limit_bytes=min(vmem_mib * MIB, V7X_VMEM_BYTES - 8 * MIB))


def _dot(a, b):
    return jnp.dot(a, b, preferred_element_type=F32)


def _dot_nt(a, b):
    return lax.dot_general(a, b, (((1,), (1,)), ((), ())), preferred_element_type=F32)


def _dot_tn(a, b):
    return lax.dot_general(a, b, (((0,), (0,)), ((), ())), preferred_element_type=F32)


def _split2(a):
    hi = a.astype(BF16)
    lo = (a - hi.astype(F32)).astype(BF16)
    return hi, lo


def _dot_nt_f32acc(a, b):
    ah, al = _split2(a)
    bh, bl = _split2(b)
    return _dot_nt(ah, bh) + (_dot_nt(ah, bl) + _dot_nt(al, bh))


def _dot_f32acc(a, b):
    ah, al = _split2(a)
    bh, bl = _split2(b)
    return _dot(ah, bh) + (_dot(ah, bl) + _dot(al, bh))


def _rmsnorm(x, g):
    return x * lax.rsqrt(jnp.mean(x * x, axis=-1, keepdims=True) + NORM_EPS) * g


def _sigmoid(x):
    return 1.0 / (1.0 + jnp.exp(-x))


def _silu(x):
    return x * _sigmoid(x)


def _norm_matmul_kernel(x_ref, g_ref, w_ref, o_ref, xn_ref):
    @pl.when(pl.program_id(1) == 0)
    def _():
        xn_ref[...] = _rmsnorm(x_ref[...], g_ref[...]).astype(BF16)

    o_ref[...] = _dot(xn_ref[...], w_ref[...])


def _norm_matmul(x, g, w_bf16, *, tm, tn):
    m, d = x.shape
    n = w_bf16.shape[1]
    assert m % tm == 0 and n % tn == 0
    vmem = (2 * tm * d * 4 + tm * d * 2 + 2 * d * tn * 2 + 2 * tm * tn * 4) // MIB + 12
    return pl.pallas_call(
        _norm_matmul_kernel,
        grid=(m // tm, n // tn),
        in_specs=[pl.BlockSpec((tm, d), lambda i, j: (i, 0)),
                  pl.BlockSpec((1, d), lambda i, j: (0, 0)),
                  pl.BlockSpec((d, tn), lambda i, j: (0, j))],
        out_specs=pl.BlockSpec((tm, tn), lambda i, j: (i, j)),
        out_shape=jax.ShapeDtypeStruct((m, n), F32),
        scratch_shapes=[pltpu.VMEM((tm, d), BF16)],
        compiler_params=_params(("parallel", "arbitrary"), vmem),
        name="norm_matmul",
    )(x, g.reshape(1, d), w_bf16)


def _hgrn_levels(c):
    return int(round(math.log2(c)))


def _hgrn_sum_masks(c):
    t = np.arange(c)[:, None]
    j = np.arange(c)[None, :]
    rows = [j <= t, j > t]
    for lv in range(_hgrn_levels(c)):
        half = c >> (lv + 1)
        blk = 2 * half
        mid = (t // blk) * blk + half - 1
        upper = (t % blk) >= half
        rows.append((upper & (j > mid) & (j <= t)) | ((~upper) & (j > t) & (j <= mid)))
    return np.concatenate(rows, axis=0).astype(np.float32)


def _hgrn_kernel(hq_ref, hf_ref, hi_ref, hg_ref, lbl_ref, gn_ref, mask_ref, s0_ref,
                 o_ref, s_ref, st_ref, *, c, t_blk, bb):
    j = pl.program_id(1)
    chains = [(bi, hd) for bi in range(bb) for hd in range(HG_HEADS)]

    @pl.when(j == 0)
    def _():
        for bi, hd in chains:
            st_ref[bi * HG_HEADS + hd] = s0_ref[bi, hd].T

    lbl = lbl_ref[...]
    lmax = jnp.max(lbl, axis=0, keepdims=True)
    lexp = jnp.exp(lbl - lmax)
    lb_all = lexp[0:1, :] / jnp.sum(lexp, axis=0, keepdims=True)

    masks = mask_ref[...]
    pad = c - t_blk
    row = lax.broadcasted_iota(I32, (c, HG_DIM), 0)
    rr = lax.broadcasted_iota(I32, (c, c), 0)
    cc = lax.broadcasted_iota(I32, (c, c), 1)

    def padded(a):
        if pad == 0:
            return a
        return jnp.concatenate([a, jnp.zeros((pad, HG_DIM), F32)], axis=0)

    for bi, hd in chains:
        sl = slice(hd * HG_DIM, (hd + 1) * HG_DIM)
        xq = hq_ref[bi][:, sl]
        lb = lb_all[:, sl]
        f = lb + (1.0 - lb) * _sigmoid(hf_ref[bi][:, sl])
        q = padded(_silu(xq))
        logf = padded(jnp.log(f))
        kk = padded(1.0 - f)
        v = padded(hi_ref[bi][:, sl])
        l_hi, l_lo = _split2(logf)
        sums = _dot(masks, jnp.concatenate([l_hi, l_lo], axis=-1))
        ex = jnp.exp(sums[:, :HG_DIM] + sums[:, HG_DIM:])
        eb = ex[0:c]
        eb_rev = ex[c:2 * c]
        scores = jnp.zeros((c, c), F32)
        for lv in range(_hgrn_levels(c)):
            half = c >> (lv + 1)
            upper = (row & half) != 0
            ex_lv = ex[(2 + lv) * c:(3 + lv) * c]
            a = jnp.where(upper, q * ex_lv, 0.0).astype(BF16)
            bm = jnp.where(upper, 0.0, kk * ex_lv).astype(BF16)
            scores = scores + jnp.where((rr ^ cc) < 2 * half, _dot_nt(a, bm), 0.0)
        diag = jnp.sum(q * kk, axis=-1, keepdims=True)
        st = st_ref[bi * HG_HEADS + hd]
        vb = v.astype(BF16)
        o = (_dot_nt((q * eb).astype(BF16), st.astype(BF16))
             + _dot(scores.astype(BF16), vb) + diag * v)
        st_ref[bi * HG_HEADS + hd] = st * eb[c - 1:c, :] + _dot_tn(vb, (kk * eb_rev).astype(BF16))
        o = o[0:t_blk]
        xg = hg_ref[bi][:, sl]
        o_ref[bi, :, sl] = _rmsnorm(o, gn_ref[:, sl]) * _silu(xg)

    @pl.when(j == pl.num_programs(1) - 1)
    def _():
        for bi, hd in chains:
            s_ref[bi, hd] = st_ref[bi * HG_HEADS + hd].T


def _hgrn(h3, lb_logits, hg_norm, s0):
    b, t, _ = h3.shape
    t_blk = math.gcd(t, HGRN_PROMPT_CHUNK)
    c = max(t_blk, HGRN_MIN_CHUNK)
    bb = math.gcd(b, HGRN_CHAINS // HG_HEADS) if t == t_blk else math.gcd(b, 2)
    masks = jnp.asarray(_hgrn_sum_masks(c), BF16)
    col = lambda k: pl.BlockSpec((bb, t_blk, HG_WIDTH), lambda i, j: (i, j, k))
    state_spec = pl.BlockSpec((bb, HG_HEADS, HG_DIM, HG_DIM), lambda i, j: (i, 0, 0, 0))
    kern = functools.partial(_hgrn_kernel, c=c, t_blk=t_blk, bb=bb)
    return pl.pallas_call(
        kern,
        grid=(b // bb, t // t_blk),
        in_specs=[col(0), col(1), col(2), col(3),
                  pl.BlockSpec(lb_logits.shape, lambda i, j: (0, 0)),
                  pl.BlockSpec((1, HG_WIDTH), lambda i, j: (0, 0)),
                  pl.BlockSpec(masks.shape, lambda i, j: (0, 0)),
                  state_spec],
        out_specs=[pl.BlockSpec((bb, t_blk, HG_WIDTH), lambda i, j: (i, j, 0)), state_spec],
        out_shape=[jax.ShapeDtypeStruct((b, t, HG_WIDTH), F32),
                   jax.ShapeDtypeStruct(s0.shape, F32)],
        scratch_shapes=[pltpu.VMEM((bb * HG_HEADS, HG_DIM, HG_DIM), F32)],
        compiler_params=_params(("parallel", "arbitrary"), 32),
        name="hgrn2",
    )(h3, h3, h3, h3, lb_logits, hg_norm.reshape(1, HG_WIDTH), masks, s0)


def _rel_bias(rel_ref, head, dist):
    d = jnp.maximum(dist, 0)
    df = jnp.maximum(d, 1).astype(F32)
    large = REL_MAX_EXACT + (jnp.log(df / REL_MAX_EXACT) / math.log(REL_MAX_DIST / REL_MAX_EXACT)
                             * (REL_BUCKETS - REL_MAX_EXACT)).astype(I32)
    large = jnp.minimum(large, REL_BUCKETS - 1)
    bucket = jnp.where(d < REL_MAX_EXACT, d, large)
    out = jnp.zeros(dist.shape, F32)
    for kb in range(REL_BUCKETS):
        out = jnp.where(bucket == kb, rel_ref[kb, head], out)
    return out


def _stack_rows(rows):
    ridx = lax.broadcasted_iota(I32, (GATE_PAD, MB_DIM), 0)
    out = jnp.zeros((GATE_PAD, MB_DIM), F32)
    for n, r in enumerate(rows):
        out = jnp.where(ridx == n, r, out)
    return out


def _topk_blocks(gate, n_cand, n_past):
    lane = lax.broadcasted_iota(I32, gate.shape, 1)
    past = lane < n_past
    g = jnp.where(past, gate, NEG_INF)
    rank = jnp.zeros(gate.shape, I32)
    for m in range(n_cand):
        gm = g[:, m:m + 1]
        ahead = jnp.where(gm > g, 1, jnp.where((gm == g) & (lane > m), 1, 0))
        rank = rank + ahead
    return jnp.where(past & (rank < MB_TOPK), 1.0, 0.0)


def _topk_blocks_t(gate_t, n_cand, n_past):
    blk_id = lax.broadcasted_iota(I32, gate_t.shape, 0)
    past = blk_id < n_past
    g = jnp.where(past, gate_t, NEG_INF)
    rank = jnp.zeros(gate_t.shape, I32)
    for m in range(n_cand):
        gm = g[m:m + 1, :]
        rank = rank + jnp.where(gm > g, 1, jnp.where((gm == g) & (blk_id > m), 1, 0))
    return past & (rank < MB_TOPK)


def _moba_prompt_kernel(rel_ref, q_ref, k_ref, v_ref, o_ref,
                        kx_ref, vx_ref, km_ref, bias_ref, eye_ref, s_ref, mx_ref, acc_ref, *, n_blk):
    b = pl.program_id(0)
    i = pl.program_id(1)
    blk = MB_BLOCK
    scale = MB_DIM ** -0.5
    heads = range(MB_HEADS)
    hsl = lambda hd: slice(hd * MB_DIM, (hd + 1) * MB_DIM)

    t = k_ref.shape[1]
    own_slot = n_blk - 1

    @pl.when((b == 0) & (i == 0))
    def _():
        r = lax.broadcasted_iota(I32, (blk, blk), 0)
        c = lax.broadcasted_iota(I32, (blk, blk), 1)
        eye_ref[...] = jnp.where(r == c, 1.0, 0.0).astype(BF16)
        kr = lax.broadcasted_iota(I32, (t, MB_DIM), 0)
        kc = lax.broadcasted_iota(I32, (t, MB_DIM), 1)
        blk_onehot = jnp.where(kr // blk == kc, 1.0, 0.0).astype(BF16)
        for hd in heads:
            bias_ref[hd, 0] = jnp.where(c <= r, _rel_bias(rel_ref, hd, r - c), NEG_INF)
            bias_ref[hd, 1] = _rel_bias(rel_ref, hd, r - c + blk)
            kx_ref[hd, :, MB_DIM:] = blk_onehot
            vx_ref[hd, :, MB_DIM:] = jnp.ones((t, MB_DIM), BF16)

    @pl.when(i == 0)
    def _():
        means = [jnp.sum(k_ref[0, n * blk:(n + 1) * blk, :], axis=0, keepdims=True) * (1.0 / blk)
                 for n in range(n_blk)]
        for hd in heads:
            kx_ref[hd, :, :MB_DIM] = k_ref[0][:, hsl(hd)].astype(BF16)
            vx_ref[hd, :, :MB_DIM] = v_ref[0][:, hsl(hd)].astype(BF16)
            km_ref[hd] = _stack_rows([mn[:, hsl(hd)] for mn in means])

    q = q_ref[0]
    blk_id = lax.broadcasted_iota(I32, (GATE_PAD, blk), 0)
    qx = []
    for hd in heads:
        qh = q[:, hsl(hd)]
        sel_t = _topk_blocks_t(_dot_nt_f32acc(km_ref[hd], qh), n_blk, i)
        qm_t = jnp.where(sel_t | (blk_id >= i), 0.0, MASK_BIG).astype(BF16)
        qm_t = jnp.concatenate([qm_t, jnp.zeros((MB_DIM - GATE_PAD, blk), BF16)], axis=0)
        qmask = _dot_nt(eye_ref[...], qm_t).astype(BF16)
        qx.append(jnp.concatenate([qh.astype(BF16), qmask], axis=-1))

    own0 = pl.multiple_of(i * blk, blk)
    for hd in heads:
        s = _dot_nt(qx[hd], kx_ref[hd, pl.ds(own0, blk), :]) * scale + bias_ref[hd, 0]
        s_ref[hd, own_slot] = s
        mx_ref[hd] = s

    for n in range(n_blk - 1):
        @pl.when(n < i)
        def _(n=n):
            for hd in heads:
                far_bias = rel_ref[REL_BUCKETS - 1, hd]
                s = _dot_nt(qx[hd], kx_ref[hd, n * blk:(n + 1) * blk, :]) * scale
                s = s + jnp.where(n == i - 1, bias_ref[hd, 1], far_bias)
                s_ref[hd, n] = s
                mx_ref[hd] = jnp.maximum(mx_ref[hd], s)

    for hd in heads:
        m = jnp.max(mx_ref[hd], axis=-1, keepdims=True)
        mx_ref[hd] = jnp.broadcast_to(m, (blk, 2 * MB_DIM))
        p = jnp.exp(s_ref[hd, own_slot] - mx_ref[hd])
        acc_ref[hd] = _dot(p.astype(BF16), vx_ref[hd, pl.ds(own0, blk), :])

    for n in range(n_blk - 1):
        @pl.when(n < i)
        def _(n=n):
            for hd in heads:
                p = jnp.exp(s_ref[hd, n] - mx_ref[hd])
                acc_ref[hd] += _dot(p.astype(BF16), vx_ref[hd, n * blk:(n + 1) * blk, :])

    for hd in heads:
        acc = acc_ref[hd]
        o_ref[0, :, hsl(hd)] = acc[:, :MB_DIM] / acc[:, MB_DIM:]


def _moba_prompt(h3, rel_table):
    b, t, _ = h3.shape
    assert t % MB_BLOCK == 0
    n_blk = t // MB_BLOCK
    assert n_blk <= GATE_PAD
    qcol, kcol, vcol = COL_MQ // MB_WIDTH, COL_MK // MB_WIDTH, COL_MV // MB_WIDTH
    kern = functools.partial(_moba_prompt_kernel, n_blk=n_blk)
    wide = 2 * MB_DIM
    vmem = (2 * 2 * t * MB_WIDTH * 4 + 2 * MB_HEADS * t * wide * 2
            + MB_HEADS * (n_blk + 4) * MB_BLOCK * wide * 4) // MIB + 12
    return pl.pallas_call(
        kern,
        grid=(b, n_blk),
        in_specs=[pl.BlockSpec(memory_space=pltpu.SMEM),
                  pl.BlockSpec((1, MB_BLOCK, MB_WIDTH), lambda bb, i: (bb, i, qcol)),
                  pl.BlockSpec((1, t, MB_WIDTH), lambda bb, i: (bb, 0, kcol)),
                  pl.BlockSpec((1, t, MB_WIDTH), lambda bb, i: (bb, 0, vcol))],
        out_specs=pl.BlockSpec((1, MB_BLOCK, MB_WIDTH), lambda bb, i: (bb, i, 0)),
        out_shape=jax.ShapeDtypeStruct((b, t, MB_WIDTH), F32),
        scratch_shapes=[pltpu.VMEM((MB_HEADS, t, wide), BF16), pltpu.VMEM((MB_HEADS, t, wide), BF16),
                        pltpu.VMEM((MB_HEADS, GATE_PAD, MB_DIM), F32),
                        pltpu.VMEM((MB_HEADS, 2, MB_BLOCK, MB_BLOCK), F32),
                        pltpu.VMEM((MB_BLOCK, MB_BLOCK), BF16),
                        pltpu.VMEM((MB_HEADS, n_blk, MB_BLOCK, MB_BLOCK), F32),
                        pltpu.VMEM((MB_HEADS, MB_BLOCK, wide), F32),
                        pltpu.VMEM((MB_HEADS, MB_BLOCK, wide), F32)],
        compiler_params=_params(("arbitrary", "arbitrary"), vmem),
        name="moba_prompt",
    )(rel_table, h3, h3, h3)


def _moba_sample_kernel(pt_ref, rel_ref, q_ref, kn_ref, vn_ref, *refs, n_pages, past_len):
    del pt_ref
    kp = refs[:n_pages]
    vp = refs[n_pages:2 * n_pages]
    o_ref, bias_ref, bias_own_ref = refs[2 * n_pages:]
    t = q_ref.shape[1]
    rows = MB_HEADS * t
    page_rows = PAGE_SIZE * MB_HEADS
    pages_per_blk = MB_BLOCK // PAGE_SIZE
    n_past = past_len // MB_BLOCK
    scale = MB_DIM ** -0.5

    def head_rows(ref):
        x = ref[0]
        return jnp.concatenate([x[:, hd * MB_DIM:(hd + 1) * MB_DIM] for hd in range(MB_HEADS)], axis=0)

    @pl.when(pl.program_id(0) == 0)
    def _():
        tq = lax.broadcasted_iota(I32, (t, page_rows), 0)
        kc = lax.broadcasted_iota(I32, (t, page_rows), 1)
        for p in range(n_pages):
            dist = past_len + tq - (p * PAGE_SIZE + (kc >> 2))
            bias_ref[p] = jnp.concatenate(
                [jnp.where((kc & (MB_HEADS - 1)) == hd, _rel_bias(rel_ref, hd, dist), NEG_INF)
                 for hd in range(MB_HEADS)], axis=0)
        ro = lax.broadcasted_iota(I32, (t, rows), 0)
        co = lax.broadcasted_iota(I32, (t, rows), 1)
        for hd in range(MB_HEADS):
            own = _rel_bias(rel_ref, hd, ro - (co - hd * t))
            keep = (co >= hd * t) & (co <= hd * t + ro)
            bias_own_ref[hd * t:(hd + 1) * t, :] = jnp.where(keep, own, NEG_INF)

    q = head_rows(q_ref)
    qb = q.astype(BF16)
    row_head = lax.broadcasted_iota(I32, (rows, 1), 0) // t

    groups = page_rows // 8
    sums = [jnp.sum(kp[p][0].reshape(groups, 8, MB_DIM), axis=0) for p in range(n_pages)]
    blk_sums = []
    for n in range(n_past):
        acc = sums[n * pages_per_blk]
        for pp in range(1, pages_per_blk):
            acc = acc + sums[n * pages_per_blk + pp]
        blk_sums.append(acc)
    g_full = _dot_nt_f32acc(q, jnp.concatenate(blk_sums, axis=0))
    gc = lax.broadcasted_iota(I32, g_full.shape, 1)
    g_full = jnp.where((gc & (MB_HEADS - 1)) == row_head, g_full, 0.0)
    lane = lax.broadcasted_iota(I32, (rows, GATE_PAD), 1)
    gate = jnp.zeros((rows, GATE_PAD), F32)
    for n in range(n_past):
        g_n = jnp.sum(g_full[:, n * 8:(n + 1) * 8], axis=-1, keepdims=True) * (1.0 / MB_BLOCK)
        gate = jnp.where(lane == n, g_n, gate)
    sel = _topk_blocks(gate, n_past, n_past)

    s_own = _dot_nt(q, head_rows(kn_ref)) * scale + bias_own_ref[...]
    m = jnp.max(s_own, axis=-1, keepdims=True)
    s_past = []
    for p in range(n_pages):
        n = p // pages_per_blk
        s = _dot_nt(qb, kp[p][0].astype(BF16)) * scale + bias_ref[p]
        s = jnp.where(sel[:, n:n + 1] > 0.0, s, NEG_INF)
        s_past.append(s)
        m = jnp.maximum(m, jnp.max(s, axis=-1, keepdims=True))
    p_own = jnp.exp(s_own - m)
    l = jnp.sum(p_own, axis=-1, keepdims=True)
    out = _dot(p_own, head_rows(vn_ref))
    for p in range(n_pages):
        pr = jnp.exp(s_past[p] - m)
        l = l + jnp.sum(pr, axis=-1, keepdims=True)
        out = out + _dot(pr.astype(BF16), vp[p][0].astype(BF16))
    out = out / l
    for hd in range(MB_HEADS):
        o_ref[0, :, hd * MB_DIM:(hd + 1) * MB_DIM] = out[hd * t:(hd + 1) * t]


def _moba_sample(h3, cache_k, cache_v, page_table, rel_table):
    db, t, _ = h3.shape
    n_pages = page_table.shape[1]
    past_len = n_pages * PAGE_SIZE
    assert past_len % MB_BLOCK == 0 and t <= MB_BLOCK and past_len // MB_BLOCK < GATE_PAD
    assert MB_HEADS == 4 and t % 8 == 0
    qcol, kcol, vcol = COL_MQ // MB_WIDTH, COL_MK // MB_WIDTH, COL_MV // MB_WIDTH
    page_rows = PAGE_SIZE * MB_HEADS
    new = lambda k: pl.BlockSpec((1, t, MB_WIDTH), lambda i, pt: (i, 0, k))
    page = lambda p: pl.BlockSpec((1, page_rows, MB_DIM), lambda i, pt: (pt[i, p], 0, 0))
    kern = functools.partial(_moba_sample_kernel, n_pages=n_pages, past_len=past_len)
    grid_spec = pltpu.PrefetchScalarGridSpec(
        num_scalar_prefetch=1,
        grid=(db,),
        in_specs=([pl.BlockSpec(memory_space=pltpu.SMEM), new(qcol), new(kcol), new(vcol)]
                  + [page(p) for p in range(n_pages)] * 2),
        out_specs=pl.BlockSpec((1, t, MB_WIDTH), lambda i, pt: (i, 0, 0)),
        scratch_shapes=[pltpu.VMEM((n_pages, MB_HEADS * t, page_rows), F32),
                        pltpu.VMEM((MB_HEADS * t, MB_HEADS * t), F32)],
    )
    return pl.pallas_call(
        kern,
        grid_spec=grid_spec,
        out_shape=jax.ShapeDtypeStruct((db, t, MB_WIDTH), F32),
        compiler_params=_params(("arbitrary",), 40),
        name="moba_sample",
    )(page_table, rel_table, h3, h3, h3, *([cache_k] * n_pages), *([cache_v] * n_pages))


def _merge_kernel(oa_ref, ob_ref, ga0_ref, ga1_ref, gb0_ref, gb1_ref, x_ref,
                  wa_ref, wb_ref, wo_ref, gx_ref, wq_ref, x1_ref, xq_ref):
    ga = jnp.concatenate([ga0_ref[...], ga1_ref[...]], axis=-1)
    gb = jnp.concatenate([gb0_ref[...], gb1_ref[...]], axis=-1)
    pa = _dot(oa_ref[...].astype(BF16), wa_ref[...])
    pb = _dot(ob_ref[...].astype(BF16), wb_ref[...])
    merged = _sigmoid(ga) * pa + _sigmoid(gb) * pb
    x1 = x_ref[...] + _dot(merged.astype(BF16), wo_ref[...])
    x1_ref[...] = x1
    xq_ref[...] = _dot(_rmsnorm(x1, gx_ref[...]).astype(BF16), wq_ref[...])


def _merge(oa, ob, h2, x, wa, wb, wo, gx, wq, *, tm):
    n = x.shape[0]
    assert n % tm == 0
    half = D_MODEL // 2
    tok = lambda w, k=0: pl.BlockSpec((tm, w), lambda i: (i, k))
    full = lambda a: pl.BlockSpec(a.shape, lambda i: (0, 0))
    gx2 = gx.reshape(1, D_MODEL)
    return pl.pallas_call(
        _merge_kernel,
        grid=(n // tm,),
        in_specs=[tok(HG_WIDTH), tok(MB_WIDTH),
                  tok(half, COL_GA // half), tok(half, COL_GA // half + 1),
                  tok(half, COL_GB // half), tok(half, COL_GB // half + 1),
                  tok(D_MODEL), full(wa), full(wb), full(wo), full(gx2), full(wq)],
        out_specs=[tok(D_MODEL), tok(XA_WIDTH)],
        out_shape=[jax.ShapeDtypeStruct((n, D_MODEL), F32),
                   jax.ShapeDtypeStruct((n, XA_WIDTH), F32)],
        compiler_params=_params(("parallel",), 48),
        name="merge_mix",
    )(oa, ob, h2, h2, h2, h2, x, wa, wb, wo, gx2, wq)


def _mem_attn_kernel(q_ref, k_ref, v_ref, o_ref):
    scale = XA_DIM ** -0.5
    for hd in range(XA_HEADS):
        sl = slice(hd * XA_DIM, (hd + 1) * XA_DIM)
        s = _dot_nt(q_ref[0][:, sl].astype(BF16), k_ref[0][:, sl].astype(BF16)) * scale
        m = jnp.max(s, axis=-1, keepdims=True)
        p = jnp.exp(s - m)
        l = jnp.sum(p, axis=-1, keepdims=True)
        o_ref[0, :, sl] = _dot(p.astype(BF16), v_ref[0][:, sl].astype(BF16)) / l


def _mem_attn(xq3, mem_kv, *, tq):
    b, t, _ = xq3.shape
    assert t % tq == 0
    mem = lambda k: pl.BlockSpec((1, mem_kv.shape[1], XA_WIDTH), lambda i, j: (i, 0, k))
    qs = pl.BlockSpec((1, tq, XA_WIDTH), lambda i, j: (i, j, 0))
    return pl.pallas_call(
        _mem_attn_kernel,
        grid=(b, t // tq),
        in_specs=[qs, mem(0), mem(1)],
        out_specs=qs,
        out_shape=jax.ShapeDtypeStruct((b, t, XA_WIDTH), F32),
        compiler_params=_params(("parallel", "parallel"), 32),
        name="mem_attn",
    )(xq3, mem_kv, mem_kv)


def _mem_attn_rows_kernel(q_ref, k_ref, v_ref, o_ref):
    bb, t, _ = q_ref.shape
    rows = XA_HEADS * t
    scale = XA_DIM ** -0.5
    row_head = lax.broadcasted_iota(I32, (rows, 1), 0) // t
    kc = lax.broadcasted_iota(I32, (rows, k_ref.shape[1]), 1)
    same_head = (kc & (XA_HEADS - 1)) == row_head
    for i in range(bb):
        x = q_ref[i]
        q = jnp.concatenate([x[:, hd * XA_DIM:(hd + 1) * XA_DIM] for hd in range(XA_HEADS)], axis=0)
        s = _dot_nt(q.astype(BF16), k_ref[i].astype(BF16)) * scale
        s = jnp.where(same_head, s, NEG_INF)
        m = jnp.max(s, axis=-1, keepdims=True)
        p = jnp.exp(s - m)
        l = jnp.sum(p, axis=-1, keepdims=True)
        out = _dot(p.astype(BF16), v_ref[i].astype(BF16)) / l
        for hd in range(XA_HEADS):
            o_ref[i, :, hd * XA_DIM:(hd + 1) * XA_DIM] = out[hd * t:(hd + 1) * t]


def _mem_attn_rows(xq3, mem_k, mem_v, *, bb):
    b, t, _ = xq3.shape
    assert b % bb == 0 and XA_HEADS == 4 and t % 8 == 0
    mem = pl.BlockSpec((bb,) + mem_k.shape[1:], lambda i: (i, 0, 0))
    qs = pl.BlockSpec((bb, t, XA_WIDTH), lambda i: (i, 0, 0))
    return pl.pallas_call(
        _mem_attn_rows_kernel,
        grid=(b // bb,),
        in_specs=[qs, mem, mem],
        out_specs=qs,
        out_shape=jax.ShapeDtypeStruct((b, t, XA_WIDTH), F32),
        compiler_params=_params(("parallel",), 32),
        name="mem_attn_rows",
    )(xq3, mem_k, mem_v)


def _route(logits):
    lane = lax.broadcasted_iota(I32, logits.shape, 1).astype(F32)
    first = lambda hit: jnp.min(jnp.where(hit, lane, float(ROUTER_LANES)), axis=-1, keepdims=True)
    gl = jnp.where(lane < N_GROUPS, logits, NEG_INF)
    gmax = jnp.max(gl, axis=-1, keepdims=True)
    grp = first(gl == gmax)
    g_prob = 1.0 / jnp.sum(jnp.exp(gl - gmax), axis=-1, keepdims=True)
    e_lo = N_GROUPS + grp * EXPERTS_PER_GROUP
    in_grp = (lane >= e_lo) & (lane < e_lo + EXPERTS_PER_GROUP)
    el = jnp.where(in_grp, logits, NEG_INF)
    top1 = jnp.max(el, axis=-1, keepdims=True)
    idx1 = first(el == top1)
    el2 = jnp.where(lane == idx1, NEG_INF, el)
    top2 = jnp.max(el2, axis=-1, keepdims=True)
    idx2 = first(el2 == top2)
    e2 = jnp.exp(top2 - top1)
    w1 = g_prob / (1.0 + e2)
    w2 = w1 * e2
    comb = jnp.where(lane == idx1, w1, 0.0) + jnp.where(lane == idx2, w2, 0.0)
    return jnp.where(lane == GRP_LANE, grp, comb)


def _xo_route_kernel(x1_ref, at_ref, wxo_ref, gf_ref, wr_ref, br_ref, x2_ref, xf_ref, comb_ref):
    x2 = x1_ref[...] + _dot(at_ref[...].astype(BF16), wxo_ref[...])
    x2_ref[...] = x2
    xf = _rmsnorm(x2, gf_ref[...])
    xf_ref[...] = xf.astype(BF16)
    comb_ref[...] = _route(_dot_f32acc(xf, wr_ref[...]) + br_ref[...])


def _xo_route(x1, attn, wxo, g_ffn, w_router, b_router, *, tm):
    n = x1.shape[0]
    assert n % tm == 0
    tok = lambda w: pl.BlockSpec((tm, w), lambda i: (i, 0))
    full = lambda a: pl.BlockSpec(a.shape, lambda i: (0, 0))
    gf2 = g_ffn.reshape(1, D_MODEL)
    return pl.pallas_call(
        _xo_route_kernel,
        grid=(n // tm,),
        in_specs=[tok(D_MODEL), tok(XA_WIDTH), full(wxo), full(gf2), full(w_router), full(b_router)],
        out_specs=[tok(D_MODEL), tok(D_MODEL), tok(ROUTER_LANES)],
        out_shape=[jax.ShapeDtypeStruct((n, D_MODEL), F32), jax.ShapeDtypeStruct((n, D_MODEL), BF16),
                   jax.ShapeDtypeStruct((n, ROUTER_LANES), F32)],
        compiler_params=_params(("parallel",), 40),
        name="xo_route",
    )(x1, attn, wxo, gf2, w_router, b_router)


def _moe_kernel(xf_ref, comb_ref, x2_ref, wg_ref, wu_ref, wd_ref, gn_ref, y_ref,
                tri_ref, posc_ref, post_ref, cnt_ref, xg_ref, wq_ref, yg_ref, acc_ref, *, r):
    i = pl.program_id(0)
    step = pl.program_id(1)
    tm = xf_ref.shape[0]
    per_step = wg_ref.shape[0]
    steps_per_group = EXPERTS_PER_GROUP // per_step
    g = step // steps_per_group
    g_f = g.astype(F32)
    rp = -(-r // V7X_LANES) * V7X_LANES
    lane = lax.broadcasted_iota(I32, (tm, ROUTER_LANES), 1)

    @pl.when((i == 0) & (step == 0))
    def _():
        rr = lax.broadcasted_iota(I32, (tm, tm), 0)
        cc = lax.broadcasted_iota(I32, (tm, tm), 1)
        tri_ref[...] = jnp.where(cc < rr, 1.0, 0.0).astype(BF16)

    @pl.when(step == 0)
    def _():
        grp = comb_ref[:, GRP_LANE:GRP_LANE + 1]
        onehot = jnp.where((lane.astype(F32) == grp) & (lane < N_GROUPS), 1.0, 0.0)
        before = _dot(tri_ref[...], onehot.astype(BF16))
        pos = jnp.sum(onehot * before, axis=-1, keepdims=True)
        cnt_ref[...] = jnp.sum(onehot, axis=0, keepdims=True)
        posc_ref[...] = jnp.where(lane == 0, pos, jnp.where(lane == 1, grp, 0.0))
        hi = jnp.floor(pos * (1.0 / POS_SPLIT))
        lo = pos - hi * POS_SPLIT
        cols = jnp.where(lane == 0, hi, jnp.where(lane == 1, lo, jnp.where(lane == 2, grp, 0.0)))
        er = lax.broadcasted_iota(I32, (16, ROUTER_LANES), 0)
        ec = lax.broadcasted_iota(I32, (16, ROUTER_LANES), 1)
        post_ref[...] = _dot_nt(jnp.where(er == ec, 1.0, 0.0).astype(BF16), cols.astype(BF16))
        acc_ref[...] = jnp.zeros(acc_ref.shape, F32)

    lane_row = lax.broadcasted_iota(I32, (1, ROUTER_LANES), 1)
    count = jnp.sum(jnp.where(lane_row == g, cnt_ref[...], 0.0)).astype(I32)
    n_chunks = (count + (r - 1)) // r

    @pl.when(step % steps_per_group == 0)
    def _():
        pos_row = post_ref[0:1, :] * POS_SPLIT + post_ref[1:2, :]
        grp_row = post_ref[2:3, :]
        comb_hi, comb_lo = _split2(comb_ref[...])

        def gather(c, carry):
            rows = pl.ds(pl.multiple_of(c * r, 16), r)
            slot = (lax.broadcasted_iota(I32, (r, tm), 0) + c * r).astype(F32)
            pick = jnp.where((pos_row == slot) & (grp_row == g_f), 1.0, 0.0).astype(BF16)
            xg_ref[rows, :] = _dot(pick, xf_ref[...]).astype(BF16)
            wq_ref[rows, :] = _dot(pick, comb_hi) + _dot(pick, comb_lo)
            yg_ref[rows, :] = jnp.zeros((r, D_MODEL), F32)
            return carry

        lax.fori_loop(0, n_chunks, gather, 0)

    def experts(c, carry):
        rows = pl.ds(pl.multiple_of(c * r, 16), r)
        xg = xg_ref[rows, :]
        wq = wq_ref[rows, :]
        wl = lax.broadcasted_iota(I32, (r, ROUTER_LANES), 1)
        y = jnp.zeros((r, D_MODEL), F32)
        for k in range(per_step):
            w_e = jnp.sum(jnp.where(wl == N_GROUPS + step * per_step + k, wq, 0.0), axis=-1, keepdims=True)
            hdn = _silu(_dot(xg, wg_ref[k])) * _dot(xg, wu_ref[k])
            y = y + _dot((hdn * w_e).astype(BF16), wd_ref[k])
        yg_ref[rows, :] += y
        return carry

    lax.fori_loop(0, n_chunks, experts, 0)

    @pl.when(step % steps_per_group == steps_per_group - 1)
    def _():
        pos_col = posc_ref[:, 0:1]
        grp_col = posc_ref[:, 1:2]

        def scatter(c, carry):
            rows = pl.ds(pl.multiple_of(c * r, 16), r)
            sl = lax.broadcasted_iota(I32, (tm, rp), 1)
            slot = (sl + c * r).astype(F32)
            place = jnp.where((pos_col == slot) & (grp_col == g_f) & (sl < r), 1.0, 0.0).astype(BF16)
            yg = yg_ref[rows, :].astype(BF16)
            if rp > r:
                yg = jnp.concatenate([yg, jnp.zeros((rp - r, D_MODEL), BF16)], axis=0)
            acc_ref[...] += _dot(place, yg)
            return carry

        lax.fori_loop(0, n_chunks, scatter, 0)

    @pl.when(step == pl.num_programs(1) - 1)
    def _():
        y_ref[...] = _rmsnorm(x2_ref[...] + acc_ref[...], gn_ref[...])


def _moe(xf, comb, x2, wg, wu, wd, g_final, *, tm):
    n = xf.shape[0]
    assert n % tm == 0 and tm % 16 == 0
    r = min(MOE_CHUNK_ROWS, tm)
    cap = -(-tm // r) * r
    per_step = MOE_EXPERTS_PER_STEP
    tok = lambda w: pl.BlockSpec((tm, w), lambda i, s: (i, 0))
    gn2 = g_final.reshape(1, D_MODEL)
    kern = functools.partial(_moe_kernel, r=r)
    vmem = (2 * tm * D_MODEL * (2 + 4 + 4) + 2 * tm * ROUTER_LANES * 4
            + 2 * 3 * per_step * D_MODEL * EXPERT_HIDDEN * 2 + tm * tm * 2
            + cap * D_MODEL * (2 + 4) + tm * D_MODEL * 4) // MIB + 8
    return pl.pallas_call(
        kern,
        grid=(n // tm, N_EXPERTS // per_step),
        in_specs=[tok(D_MODEL), tok(ROUTER_LANES), tok(D_MODEL),
                  pl.BlockSpec((per_step, D_MODEL, EXPERT_HIDDEN), lambda i, s: (s, 0, 0)),
                  pl.BlockSpec((per_step, D_MODEL, EXPERT_HIDDEN), lambda i, s: (s, 0, 0)),
                  pl.BlockSpec((per_step, EXPERT_HIDDEN, D_MODEL), lambda i, s: (s, 0, 0)),
                  pl.BlockSpec((1, D_MODEL), lambda i, s: (0, 0))],
        out_specs=tok(D_MODEL),
        out_shape=jax.ShapeDtypeStruct((n, D_MODEL), F32),
        scratch_shapes=[pltpu.VMEM((tm, tm), BF16),
                        pltpu.VMEM((tm, ROUTER_LANES), F32), pltpu.VMEM((16, tm), F32),
                        pltpu.VMEM((1, ROUTER_LANES), F32),
                        pltpu.VMEM((cap, D_MODEL), BF16), pltpu.VMEM((cap, ROUTER_LANES), F32),
                        pltpu.VMEM((cap, D_MODEL), F32), pltpu.VMEM((tm, D_MODEL), F32)],
        compiler_params=_params(("arbitrary", "arbitrary"), vmem),
        name="moe_grouped",
    )(xf, comb, x2, wg, wu, wd, gn2)


def _token_tile(n, cap=512):
    return cap if n % cap == 0 else n


def _col_tile(n, cap=1536):
    return max(c for c in range(V7X_LANES, cap + 1, V7X_LANES) if n % c == 0)


def _layer(x3, w, s0, moba_fn, mem_fn):
    b, t, _ = x3.shape
    n = b * t
    x2d = x3.reshape(n, D_MODEL)
    tm = _token_tile(n)
    h2 = _norm_matmul(x2d, w["norm_mix"], w["w_in"], tm=_token_tile(n, 1024), tn=_col_tile(IN_COLS))
    h3 = h2.reshape(b, t, IN_COLS)
    oa, s_new = _hgrn(h3, w["hg_lb_logits"], w["hg_norm"], s0)
    ob = moba_fn(h3)
    x1, xq = _merge(oa.reshape(n, HG_WIDTH), ob.reshape(n, MB_WIDTH), h2, x2d,
                    w["w_branch_a"], w["w_branch_b"], w["w_mix_out"], w["norm_xattn"], w["w_xq"], tm=tm)
    attn = mem_fn(xq.reshape(b, t, XA_WIDTH))
    x2, xf, comb = _xo_route(x1, attn.reshape(n, XA_WIDTH), w["w_xo"], w["norm_ffn"],
                             w["w_router"], w["b_router"], tm=tm)
    y = _moe(xf, comb, x2, w["w_expert_gate"], w["w_expert_up"], w["w_expert_down"],
             w["norm_final"], tm=_token_tile(n, 1024))
    k_new = h3[:, :, COL_MK:COL_MV].reshape(b, t, MB_HEADS, MB_DIM)
    v_new = h3[:, :, COL_MV:COL_GA].reshape(b, t, MB_HEADS, MB_DIM)
    return y.reshape(b, t, D_MODEL), k_new, v_new, s_new


def kernel(x_prompt, x_sample, cache_k, cache_v, state_hgrn, cache_mem_k, cache_mem_v, page_table, mem_prompt, norm_mix, w_in, hg_lb_logits, hg_norm, w_branch_a, w_branch_b, w_mix_out, rel_table, norm_xattn, norm_mem, w_xq, w_xk, w_xv, w_xo, norm_ffn, w_group_router, b_group_router, w_expert_router, b_expert_router, w_expert_gate, w_expert_up, w_expert_down, norm_final):
    assert w_in.shape[0] == DEPTH == 1 and hg_lb_logits.shape[0] == DEPTH + 1
    b = x_prompt.shape[0]
    db = x_sample.shape[0]
    n_pool = cache_k.shape[1]
    pad_lanes = ROUTER_LANES - N_GROUPS - N_EXPERTS
    w = {
        "norm_mix": norm_mix[0], "w_in": w_in[0].astype(BF16),
        "hg_lb_logits": hg_lb_logits, "hg_norm": hg_norm[0],
        "w_branch_a": w_branch_a[0].astype(BF16), "w_branch_b": w_branch_b[0].astype(BF16),
        "w_mix_out": w_mix_out[0].astype(BF16), "norm_xattn": norm_xattn[0],
        "w_xq": w_xq[0].astype(BF16), "w_xo": w_xo[0].astype(BF16), "norm_ffn": norm_ffn[0],
        "w_router": jnp.pad(jnp.concatenate([w_group_router[0], w_expert_router[0]], axis=1),
                            ((0, 0), (0, pad_lanes))),
        "b_router": jnp.pad(jnp.concatenate([b_group_router[0], b_expert_router[0]]),
                            (0, pad_lanes)).reshape(1, ROUTER_LANES),
        "w_expert_gate": w_expert_gate[0].reshape(N_EXPERTS, D_MODEL, EXPERT_HIDDEN).astype(BF16),
        "w_expert_up": w_expert_up[0].reshape(N_EXPERTS, D_MODEL, EXPERT_HIDDEN).astype(BF16),
        "w_expert_down": w_expert_down[0].reshape(N_EXPERTS, EXPERT_HIDDEN, D_MODEL).astype(BF16),
        "norm_final": norm_final,
    }

    w_mem = jnp.concatenate([w_xk[0], w_xv[0]], axis=1).astype(BF16)
    mem_kv = _norm_matmul(mem_prompt.reshape(b * MEM_LEN, D_MODEL), norm_mem[0], w_mem,
                          tm=_token_tile(b * MEM_LEN), tn=512).reshape(b, MEM_LEN, 2 * XA_WIDTH)
    mk_p = mem_kv[:, :, :XA_WIDTH]
    mv_p = mem_kv[:, :, XA_WIDTH:]
    s0 = jnp.zeros((b, HG_HEADS, HG_DIM, HG_DIM), F32)
    y_p, k_p, v_p, s_p = _layer(x_prompt, w, s0,
                                functools.partial(_moba_prompt, rel_table=rel_table),
                                functools.partial(_mem_attn, mem_kv=mem_kv, tq=512))

    ck = cache_k[0].reshape(n_pool, PAGE_SIZE * MB_HEADS, MB_DIM)
    cv = cache_v[0].reshape(n_pool, PAGE_SIZE * MB_HEADS, MB_DIM)
    moba_s = functools.partial(_moba_sample, cache_k=ck, cache_v=cv, page_table=page_table,
                               rel_table=rel_table)
    mem_s = functools.partial(_mem_attn_rows, bb=4,
                              mem_k=cache_mem_k[0].reshape(db, MEM_LEN * XA_HEADS, XA_DIM),
                              mem_v=cache_mem_v[0].reshape(db, MEM_LEN * XA_HEADS, XA_DIM))
    y_s, k_s, v_s, s_s = _layer(x_sample, w, state_hgrn[0], moba_s, mem_s)

    heads = lambda a: a.reshape(b, MEM_LEN, XA_HEADS, XA_DIM)[None]
    return (y_p, y_s, k_p[None], v_p[None], s_p[None], heads(mk_p), heads(mv_p),
            k_s[None], v_s[None], s_s[None])
```

```python
import functools
import math

import numpy as np
import jax
import jax.numpy as jnp
from jax import lax
from jax.experimental import pallas as pl
from jax.experimental.pallas import tpu as pltpu

F32 = jnp.float32
BF16 = jnp.bfloat16
I32 = jnp.int32

D_MODEL = 1024
DEPTH = 1
PAGE_SIZE = 128
HG_HEADS = 4
HG_DIM = 128
HG_WIDTH = HG_HEADS * HG_DIM
MB_HEADS = 4
MB_DIM = 128
MB_WIDTH = MB_HEADS * MB_DIM
MB_BLOCK = 256
MB_TOPK = 3
REL_BUCKETS = 32
REL_MAX_DIST = 128
REL_MAX_EXACT = REL_BUCKETS // 2
MEM_LEN = 256
XA_HEADS = 4
XA_DIM = 128
XA_WIDTH = XA_HEADS * XA_DIM
N_GROUPS = 4
EXPERTS_PER_GROUP = 8
N_EXPERTS = N_GROUPS * EXPERTS_PER_GROUP
EXPERT_TOPK = 2
EXPERT_HIDDEN = 256
NORM_EPS = 1e-6
IN_COLS = 4 * HG_WIDTH + 3 * MB_WIDTH + 2 * D_MODEL
COL_MQ = 4 * HG_WIDTH
COL_MK = COL_MQ + MB_WIDTH
COL_MV = COL_MK + MB_WIDTH
COL_GA = COL_MV + MB_WIDTH
COL_GB = COL_GA + D_MODEL

V7X_LANES = 128
V7X_VMEM_BYTES = 64 * 1024 * 1024
MIB = 1024 * 1024

NEG_INF = float("-inf")
MASK_BIG = -1e30
GATE_PAD = 16
ROUTER_LANES = 128
GRP_LANE = 0
POS_SPLIT = 32.0
MOE_CHUNK_ROWS = 320
MOE_EXPERTS_PER_STEP = 4
HGRN_PROMPT_CHUNK = 128
HGRN_PROMPT_ROWS = 4
HGRN_MIN_CHUNK = 16
HGRN_CHAINS = 16


def _params(semantics, vmem_mib):
    return pltpu.CompilerParams(dimension_semantics=semantics,
                                vmem_limit_bytes=min(vmem_mib * MIB, V7X_VMEM_BYTES - 8 * MIB))


def _dot(a, b):
    return jnp.dot(a, b, preferred_element_type=F32)


def _dot_nt(a, b):
    return lax.dot_general(a, b, (((1,), (1,)), ((), ())), preferred_element_type=F32)


def _dot_tn(a, b):
    return lax.dot_general(a, b, (((0,), (0,)), ((), ())), preferred_element_type=F32)


def _split2(a):
    hi = a.astype(BF16)
    lo = (a - hi.astype(F32)).astype(BF16)
    return hi, lo


def _dot_nt_f32acc(a, b):
    ah, al = _split2(a)
    bh, bl = _split2(b)
    return _dot_nt(ah, bh) + (_dot_nt(ah, bl) + _dot_nt(al, bh))


def _dot_f32acc(a, b):
    ah, al = _split2(a)
    bh, bl = _split2(b)
    return _dot(ah, bh) + (_dot(ah, bl) + _dot(al, bh))


def _rmsnorm(x, g):
    return x * lax.rsqrt(jnp.mean(x * x, axis=-1, keepdims=True) + NORM_EPS) * g


def _sigmoid(x):
    return 1.0 / (1.0 + jnp.exp(-x))


def _silu(x):
    return x * _sigmoid(x)


def _norm_matmul_kernel(x_ref, g_ref, w_ref, o_ref, xn_ref):
    @pl.when(pl.program_id(1) == 0)
    def _():
        xn_ref[...] = _rmsnorm(x_ref[...], g_ref[...]).astype(BF16)

    o_ref[...] = _dot(xn_ref[...], w_ref[...])


def _norm_matmul(x, g, w_bf16, *, tm, tn):
    m, d = x.shape
    n = w_bf16.shape[1]
    assert m % tm == 0 and n % tn == 0
    vmem = (2 * tm * d * 4 + tm * d * 2 + 2 * d * tn * 2 + 2 * tm * tn * 4) // MIB + 12
    return pl.pallas_call(
        _norm_matmul_kernel,
        grid=(m // tm, n // tn),
        in_specs=[pl.BlockSpec((tm, d), lambda i, j: (i, 0)),
                  pl.BlockSpec((1, d), lambda i, j: (0, 0)),
                  pl.BlockSpec((d, tn), lambda i, j: (0, j))],
        out_specs=pl.BlockSpec((tm, tn), lambda i, j: (i, j)),
        out_shape=jax.ShapeDtypeStruct((m, n), F32),
        scratch_shapes=[pltpu.VMEM((tm, d), BF16)],
        compiler_params=_params(("parallel", "arbitrary"), vmem),
        name="norm_matmul",
    )(x, g.reshape(1, d), w_bf16)


def _hgrn_levels(c):
    return int(round(math.log2(c)))


def _hgrn_sum_masks(c):
    t = np.arange(c)[:, None]
    j = np.arange(c)[None, :]
    rows = [j <= t, j > t]
    for lv in range(_hgrn_levels(c)):
        half = c >> (lv + 1)
        blk = 2 * half
        mid = (t // blk) * blk + half - 1
        upper = (t % blk) >= half
        rows.append((upper & (j > mid) & (j <= t)) | ((~upper) & (j > t) & (j <= mid)))
    return np.concatenate(rows, axis=0).astype(np.float32)


def _hgrn_kernel(hq_ref, hf_ref, hi_ref, hg_ref, lbl_ref, gn_ref, mask_ref, s0_ref,
                 o_ref, s_ref, st_ref, *, c, t_blk, bb):
    j = pl.program_id(1)
    chains = [(bi, hd) for bi in range(bb) for hd in range(HG_HEADS)]

    @pl.when(j == 0)
    def _():
        for bi, hd in chains:
            st_ref[bi * HG_HEADS + hd] = s0_ref[bi, hd].T

    lbl = lbl_ref[...]
    lmax = jnp.max(lbl, axis=0, keepdims=True)
    lexp = jnp.exp(lbl - lmax)
    lb_all = lexp[0:1, :] / jnp.sum(lexp, axis=0, keepdims=True)

    masks = mask_ref[...]
    pad = c - t_blk
    row = lax.broadcasted_iota(I32, (c, HG_DIM), 0)
    rr = lax.broadcasted_iota(I32, (c, c), 0)
    cc = lax.broadcasted_iota(I32, (c, c), 1)

    def padded(a):
        if pad == 0:
            return a
        return jnp.concatenate([a, jnp.zeros((pad, HG_DIM), F32)], axis=0)

    hsl = lambda hd: slice(hd * HG_DIM, (hd + 1) * HG_DIM)
    qs, ks, vs, logf_parts = [], [], [], []
    for bi, hd in chains:
        xq = hq_ref[bi][:, hsl(hd)]
        lb = lb_all[:, hsl(hd)]
        f = lb + (1.0 - lb) * _sigmoid(hf_ref[bi][:, hsl(hd)])
        qs.append(padded(_silu(xq)))
        ks.append(padded(1.0 - f))
        vs.append(padded(hi_ref[bi][:, hsl(hd)]))
        logf_parts.append(_split2(padded(jnp.log(f))))
    exs = []
    for n in range(0, len(chains), 2):
        (hi_a, lo_a), (hi_b, lo_b) = logf_parts[n], logf_parts[n + 1]
        stacked = jnp.concatenate([jnp.concatenate([hi_a, hi_b], axis=-1),
                                   jnp.concatenate([lo_a, lo_b], axis=-1)], axis=0)
        ex = jnp.exp(_dot(masks, stacked))
        exs += [ex[:, :HG_DIM], ex[:, HG_DIM:]]

    level_dots = []
    for q, kk, ex in zip(qs, ks, exs):
        per_level = []
        for lv in range(_hgrn_levels(c)):
            half = c >> (lv + 1)
            upper = (row & half) != 0
            ex_lv = ex[(2 + lv) * c:(3 + lv) * c]
            a = jnp.where(upper, q * ex_lv, 0.0).astype(BF16)
            bm = jnp.where(upper, 0.0, kk * ex_lv).astype(BF16)
            per_level.append(_dot_nt(a, bm))
        level_dots.append(per_level)

    outs = []
    for n, (bi, hd) in enumerate(chains):
        q, kk, v, ex = qs[n], ks[n], vs[n], exs[n]
        scores = jnp.zeros((c, c), F32)
        for lv, d in enumerate(level_dots[n]):
            scores = scores + jnp.where((rr ^ cc) < 2 * (c >> (lv + 1)), d, 0.0)
        eb = ex[0:c]
        eb_rev = ex[c:2 * c]
        diag = jnp.sum(q * kk, axis=-1, keepdims=True)
        st = st_ref[bi * HG_HEADS + hd]
        vb = v.astype(BF16)
        o = (_dot_nt((q * eb).astype(BF16), st.astype(BF16))
             + _dot(scores.astype(BF16), vb) + diag * v)
        st_ref[bi * HG_HEADS + hd] = st * eb[c - 1:c, :] + _dot_tn(vb, (kk * eb_rev).astype(BF16))
        outs.append(o[0:t_blk])

    for (bi, hd), o in zip(chains, outs):
        xg = hg_ref[bi][:, hsl(hd)]
        o_ref[bi, :, hsl(hd)] = _rmsnorm(o, gn_ref[:, hsl(hd)]) * _silu(xg)

    @pl.when(j == pl.num_programs(1) - 1)
    def _():
        for bi, hd in chains:
            s_ref[bi, hd] = st_ref[bi * HG_HEADS + hd].T


def _hgrn(h3, lb_logits, hg_norm, s0):
    b, t, _ = h3.shape
    t_blk = math.gcd(t, HGRN_PROMPT_CHUNK)
    c = max(t_blk, HGRN_MIN_CHUNK)
    bb = math.gcd(b, HGRN_CHAINS // HG_HEADS) if t == t_blk else math.gcd(b, HGRN_PROMPT_ROWS)
    m01 = _hgrn_sum_masks(c)
    masks = jnp.asarray(np.concatenate([m01, m01], axis=1), BF16)
    col = lambda k: pl.BlockSpec((bb, t_blk, HG_WIDTH), lambda i, j: (i, j, k))
    state_spec = pl.BlockSpec((bb, HG_HEADS, HG_DIM, HG_DIM), lambda i, j: (i, 0, 0, 0))
    kern = functools.partial(_hgrn_kernel, c=c, t_blk=t_blk, bb=bb)
    return pl.pallas_call(
        kern,
        grid=(b // bb, t // t_blk),
        in_specs=[col(0), col(1), col(2), col(3),
                  pl.BlockSpec(lb_logits.shape, lambda i, j: (0, 0)),
                  pl.BlockSpec((1, HG_WIDTH), lambda i, j: (0, 0)),
                  pl.BlockSpec(masks.shape, lambda i, j: (0, 0)),
                  state_spec],
        out_specs=[pl.BlockSpec((bb, t_blk, HG_WIDTH), lambda i, j: (i, j, 0)), state_spec],
        out_shape=[jax.ShapeDtypeStruct((b, t, HG_WIDTH), F32),
                   jax.ShapeDtypeStruct(s0.shape, F32)],
        scratch_shapes=[pltpu.VMEM((bb * HG_HEADS, HG_DIM, HG_DIM), F32)],
        compiler_params=_params(("parallel", "arbitrary"), 32),
        name="hgrn2",
    )(h3, h3, h3, h3, lb_logits, hg_norm.reshape(1, HG_WIDTH), masks, s0)


def _rel_bias(rel_ref, head, dist):
    d = jnp.maximum(dist, 0)
    df = jnp.maximum(d, 1).astype(F32)
    large = REL_MAX_EXACT + (jnp.log(df / REL_MAX_EXACT) / math.log(REL_MAX_DIST / REL_MAX_EXACT)
                             * (REL_BUCKETS - REL_MAX_EXACT)).astype(I32)
    large = jnp.minimum(large, REL_BUCKETS - 1)
    bucket = jnp.where(d < REL_MAX_EXACT, d, large)
    out = jnp.zeros(dist.shape, F32)
    for kb in range(REL_BUCKETS):
        out = jnp.where(bucket == kb, rel_ref[kb, head], out)
    return out


def _stack_rows(rows):
    ridx = lax.broadcasted_iota(I32, (GATE_PAD, MB_DIM), 0)
    out = jnp.zeros((GATE_PAD, MB_DIM), F32)
    for n, r in enumerate(rows):
        out = jnp.where(ridx == n, r, out)
    return out


def _topk_blocks(gate, n_cand, n_past):
    lane = lax.broadcasted_iota(I32, gate.shape, 1)
    past = lane < n_past
    g = jnp.where(past, gate, NEG_INF)
    rank = jnp.zeros(gate.shape, I32)
    for m in range(n_cand):
        gm = g[:, m:m + 1]
        ahead = jnp.where(gm > g, 1, jnp.where((gm == g) & (lane > m), 1, 0))
        rank = rank + ahead
    return jnp.where(past & (rank < MB_TOPK), 1.0, 0.0)


def _topk_blocks_t(gate_t, n_cand, n_past):
    blk_id = lax.broadcasted_iota(I32, gate_t.shape, 0)
    past = blk_id < n_past
    g = jnp.where(past, gate_t, NEG_INF)
    rank = jnp.zeros(gate_t.shape, I32)
    for m in range(n_cand):
        gm = g[m:m + 1, :]
        rank = rank + jnp.where(gm > g, 1, jnp.where((gm == g) & (blk_id > m), 1, 0))
    return past & (rank < MB_TOPK)


def _moba_prompt_kernel(rel_ref, q_ref, k_ref, v_ref, o_ref,
                        kx_ref, vx_ref, km_ref, bias_ref, eye_ref, s_ref, mx_ref, mb_ref, acc_ref, *, n_blk):
    b = pl.program_id(0)
    i = pl.program_id(1)
    blk = MB_BLOCK
    scale = MB_DIM ** -0.5
    heads = range(MB_HEADS)
    hsl = lambda hd: slice(hd * MB_DIM, (hd + 1) * MB_DIM)

    t = k_ref.shape[1]
    own_slot = n_blk - 1

    @pl.when((b == 0) & (i == 0))
    def _():
        r = lax.broadcasted_iota(I32, (blk, blk), 0)
        c = lax.broadcasted_iota(I32, (blk, blk), 1)
        eye_ref[...] = jnp.where(r == c, 1.0, 0.0).astype(BF16)
        kr = lax.broadcasted_iota(I32, (t, MB_DIM), 0)
        kc = lax.broadcasted_iota(I32, (t, MB_DIM), 1)
        blk_onehot = jnp.where(kr // blk == kc, 1.0, 0.0).astype(BF16)
        for hd in heads:
            bias_ref[hd, 0] = jnp.where(c <= r, _rel_bias(rel_ref, hd, r - c), NEG_INF)
            bias_ref[hd, 1] = _rel_bias(rel_ref, hd, r - c + blk)
            kx_ref[hd, :, MB_DIM:] = blk_onehot
            vx_ref[hd, :, MB_DIM:] = jnp.ones((t, MB_DIM), BF16)

    @pl.when(i == 0)
    def _():
        means = [jnp.sum(k_ref[0, n * blk:(n + 1) * blk, :], axis=0, keepdims=True) * (1.0 / blk)
                 for n in range(n_blk)]
        for hd in heads:
            kx_ref[hd, :, :MB_DIM] = k_ref[0][:, hsl(hd)].astype(BF16)
            vx_ref[hd, :, :MB_DIM] = v_ref[0][:, hsl(hd)].astype(BF16)
            km_ref[hd] = _stack_rows([mn[:, hsl(hd)] for mn in means])

    q = q_ref[0]
    blk_id = lax.broadcasted_iota(I32, (GATE_PAD, blk), 0)
    qh = [q[:, hsl(hd)] for hd in heads]
    gates = [_dot_nt_f32acc(km_ref[hd], qh[hd]) for hd in heads]
    qm_t = []
    for hd in heads:
        sel_t = _topk_blocks_t(gates[hd], n_blk, i)
        m_t = jnp.where(sel_t | (blk_id >= i), 0.0, MASK_BIG).astype(BF16)
        qm_t.append(jnp.concatenate([m_t, jnp.zeros((MB_DIM - GATE_PAD, blk), BF16)], axis=0))
    qmask = [_dot_nt(eye_ref[...], qm_t[hd]) for hd in heads]
    qx = [jnp.concatenate([qh[hd].astype(BF16), qmask[hd].astype(BF16)], axis=-1) for hd in heads]

    own0 = pl.multiple_of(i * blk, blk)
    dots = [_dot_nt(qx[hd], kx_ref[hd, pl.ds(own0, blk), :]) for hd in heads]
    for hd in heads:
        s = dots[hd] * scale + bias_ref[hd, 0]
        s_ref[hd, own_slot] = s
        mx_ref[hd] = jnp.maximum(s[:, :MB_DIM], s[:, MB_DIM:])

    for n in range(n_blk - 1):
        @pl.when(n < i)
        def _(n=n):
            dots = [_dot_nt(qx[hd], kx_ref[hd, n * blk:(n + 1) * blk, :]) for hd in heads]
            for hd in heads:
                far_bias = rel_ref[REL_BUCKETS - 1, hd]
                s = dots[hd] * scale + jnp.where(n == i - 1, bias_ref[hd, 1], far_bias)
                s_ref[hd, n] = s
                mx_ref[hd] = jnp.maximum(mx_ref[hd], jnp.maximum(s[:, :MB_DIM], s[:, MB_DIM:]))

    row_max = [jnp.max(mx_ref[hd], axis=-1, keepdims=True) for hd in heads]
    for hd in heads:
        mb_ref[hd] = jnp.broadcast_to(row_max[hd], (blk, 2 * MB_DIM))
    probs = [jnp.exp(s_ref[hd, own_slot] - mb_ref[hd]).astype(BF16) for hd in heads]
    for hd in heads:
        acc_ref[hd] = _dot(probs[hd], vx_ref[hd, pl.ds(own0, blk), :])

    for n in range(n_blk - 1):
        @pl.when(n < i)
        def _(n=n):
            probs = [jnp.exp(s_ref[hd, n] - mb_ref[hd]).astype(BF16) for hd in heads]
            pv = [_dot(probs[hd], vx_ref[hd, n * blk:(n + 1) * blk, :]) for hd in heads]
            for hd in heads:
                acc_ref[hd] += pv[hd]

    for hd in heads:
        acc = acc_ref[hd]
        o_ref[0, :, hsl(hd)] = acc[:, :MB_DIM] / acc[:, MB_DIM:]


def _moba_prompt(h3, rel_table):
    b, t, _ = h3.shape
    assert t % MB_BLOCK == 0
    n_blk = t // MB_BLOCK
    assert n_blk <= GATE_PAD
    qcol, kcol, vcol = COL_MQ // MB_WIDTH, COL_MK // MB_WIDTH, COL_MV // MB_WIDTH
    kern = functools.partial(_moba_prompt_kernel, n_blk=n_blk)
    wide = 2 * MB_DIM
    vmem = (2 * 2 * t * MB_WIDTH * 4 + 2 * MB_HEADS * t * wide * 2
            + MB_HEADS * (n_blk + 4) * MB_BLOCK * wide * 4) // MIB + 12
    return pl.pallas_call(
        kern,
        grid=(b, n_blk),
        in_specs=[pl.BlockSpec(memory_space=pltpu.SMEM),
                  pl.BlockSpec((1, MB_BLOCK, MB_WIDTH), lambda bb, i: (bb, i, qcol)),
                  pl.BlockSpec((1, t, MB_WIDTH), lambda bb, i: (bb, 0, kcol)),
                  pl.BlockSpec((1, t, MB_WIDTH), lambda bb, i: (bb, 0, vcol))],
        out_specs=pl.BlockSpec((1, MB_BLOCK, MB_WIDTH), lambda bb, i: (bb, i, 0)),
        out_shape=jax.ShapeDtypeStruct((b, t, MB_WIDTH), F32),
        scratch_shapes=[pltpu.VMEM((MB_HEADS, t, wide), BF16), pltpu.VMEM((MB_HEADS, t, wide), BF16),
                        pltpu.VMEM((MB_HEADS, GATE_PAD, MB_DIM), F32),
                        pltpu.VMEM((MB_HEADS, 2, MB_BLOCK, MB_BLOCK), F32),
                        pltpu.VMEM((MB_BLOCK, MB_BLOCK), BF16),
                        pltpu.VMEM((MB_HEADS, n_blk, MB_BLOCK, MB_BLOCK), F32),
                        pltpu.VMEM((MB_HEADS, MB_BLOCK, MB_DIM), F32),
                        pltpu.VMEM((MB_HEADS, MB_BLOCK, wide), F32),
                        pltpu.VMEM((MB_HEADS, MB_BLOCK, wide), F32)],
        compiler_params=_params(("arbitrary", "arbitrary"), vmem),
        name="moba_prompt",
    )(rel_table, h3, h3, h3)


def _moba_sample_kernel(pt_ref, rel_ref, q_ref, kn_ref, vn_ref, *refs, n_pages, past_len):
    del pt_ref
    kp = refs[:n_pages]
    vp = refs[n_pages:2 * n_pages]
    o_ref, bias_ref, bias_own_ref = refs[2 * n_pages:]
    t = q_ref.shape[1]
    rows = MB_HEADS * t
    page_rows = PAGE_SIZE * MB_HEADS
    pages_per_blk = MB_BLOCK // PAGE_SIZE
    n_past = past_len // MB_BLOCK
    scale = MB_DIM ** -0.5

    def head_rows(ref):
        x = ref[0]
        return jnp.concatenate([x[:, hd * MB_DIM:(hd + 1) * MB_DIM] for hd in range(MB_HEADS)], axis=0)

    @pl.when(pl.program_id(0) == 0)
    def _():
        tq = lax.broadcasted_iota(I32, (t, page_rows), 0)
        kc = lax.broadcasted_iota(I32, (t, page_rows), 1)
        for p in range(n_pages):
            dist = past_len + tq - (p * PAGE_SIZE + (kc >> 2))
            bias_ref[p] = jnp.concatenate(
                [jnp.where((kc & (MB_HEADS - 1)) == hd, _rel_bias(rel_ref, hd, dist), NEG_INF)
                 for hd in range(MB_HEADS)], axis=0)
        ro = lax.broadcasted_iota(I32, (t, rows), 0)
        co = lax.broadcasted_iota(I32, (t, rows), 1)
        for hd in range(MB_HEADS):
            own = _rel_bias(rel_ref, hd, ro - (co - hd * t))
            keep = (co >= hd * t) & (co <= hd * t + ro)
            bias_own_ref[hd * t:(hd + 1) * t, :] = jnp.where(keep, own, NEG_INF)

    q = head_rows(q_ref)
    qb = q.astype(BF16)
    row_head = lax.broadcasted_iota(I32, (rows, 1), 0) // t

    groups = page_rows // 8
    sums = [jnp.sum(kp[p][0].reshape(groups, 8, MB_DIM), axis=0) for p in range(n_pages)]
    blk_sums = []
    for n in range(n_past):
        acc = sums[n * pages_per_blk]
        for pp in range(1, pages_per_blk):
            acc = acc + sums[n * pages_per_blk + pp]
        blk_sums.append(acc)
    g_full = _dot_nt_f32acc(q, jnp.concatenate(blk_sums, axis=0))
    gc = lax.broadcasted_iota(I32, g_full.shape, 1)
    g_full = jnp.where((gc & (MB_HEADS - 1)) == row_head, g_full, 0.0)
    lane = lax.broadcasted_iota(I32, (rows, GATE_PAD), 1)
    gate = jnp.zeros((rows, GATE_PAD), F32)
    for n in range(n_past):
        g_n = jnp.sum(g_full[:, n * 8:(n + 1) * 8], axis=-1, keepdims=True) * (1.0 / MB_BLOCK)
        gate = jnp.where(lane == n, g_n, gate)
    sel = _topk_blocks(gate, n_past, n_past)

    s_own = _dot_nt(q, head_rows(kn_ref)) * scale + bias_own_ref[...]
    m = jnp.max(s_own, axis=-1, keepdims=True)
    s_past = []
    for p in range(n_pages):
        n = p // pages_per_blk
        s = _dot_nt(qb, kp[p][0].astype(BF16)) * scale + bias_ref[p]
        s = jnp.where(sel[:, n:n + 1] > 0.0, s, NEG_INF)
        s_past.append(s)
        m = jnp.maximum(m, jnp.max(s, axis=-1, keepdims=True))
    p_own = jnp.exp(s_own - m)
    l = jnp.sum(p_own, axis=-1, keepdims=True)
    out = _dot(p_own, head_rows(vn_ref))
    for p in range(n_pages):
        pr = jnp.exp(s_past[p] - m)
        l = l + jnp.sum(pr, axis=-1, keepdims=True)
        out = out + _dot(pr.astype(BF16), vp[p][0].astype(BF16))
    out = out / l
    for hd in range(MB_HEADS):
        o_ref[0, :, hd * MB_DIM:(hd + 1) * MB_DIM] = out[hd * t:(hd + 1) * t]


def _moba_sample(h3, cache_k, cache_v, page_table, rel_table):
    db, t, _ = h3.shape
    n_pages = page_table.shape[1]
    past_len = n_pages * PAGE_SIZE
    assert past_len % MB_BLOCK == 0 and t <= MB_BLOCK and past_len // MB_BLOCK < GATE_PAD
    assert MB_HEADS == 4 and t % 8 == 0
    qcol, kcol, vcol = COL_MQ // MB_WIDTH, COL_MK // MB_WIDTH, COL_MV // MB_WIDTH
    page_rows = PAGE_SIZE * MB_HEADS
    new = lambda k: pl.BlockSpec((1, t, MB_WIDTH), lambda i, pt: (i, 0, k))
    page = lambda p: pl.BlockSpec((1, page_rows, MB_DIM), lambda i, pt: (pt[i, p], 0, 0))
    kern = functools.partial(_moba_sample_kernel, n_pages=n_pages, past_len=past_len)
    grid_spec = pltpu.PrefetchScalarGridSpec(
        num_scalar_prefetch=1,
        grid=(db,),
        in_specs=([pl.BlockSpec(memory_space=pltpu.SMEM), new(qcol), new(kcol), new(vcol)]
                  + [page(p) for p in range(n_pages)] * 2),
        out_specs=pl.BlockSpec((1, t, MB_WIDTH), lambda i, pt: (i, 0, 0)),
        scratch_shapes=[pltpu.VMEM((n_pages, MB_HEADS * t, page_rows), F32),
                        pltpu.VMEM((MB_HEADS * t, MB_HEADS * t), F32)],
    )
    return pl.pallas_call(
        kern,
        grid_spec=grid_spec,
        out_shape=jax.ShapeDtypeStruct((db, t, MB_WIDTH), F32),
        compiler_params=_params(("arbitrary",), 40),
        name="moba_sample",
    )(page_table, rel_table, h3, h3, h3, *([cache_k] * n_pages), *([cache_v] * n_pages))


def _merge_kernel(oa_ref, ob_ref, ga0_ref, ga1_ref, gb0_ref, gb1_ref, x_ref,
                  wa_ref, wb_ref, wo_ref, gx_ref, wq_ref, x1_ref, xq_ref):
    ga = jnp.concatenate([ga0_ref[...], ga1_ref[...]], axis=-1)
    gb = jnp.concatenate([gb0_ref[...], gb1_ref[...]], axis=-1)
    pa = _dot(oa_ref[...].astype(BF16), wa_ref[...])
    pb = _dot(ob_ref[...].astype(BF16), wb_ref[...])
    merged = _sigmoid(ga) * pa + _sigmoid(gb) * pb
    x1 = x_ref[...] + _dot(merged.astype(BF16), wo_ref[...])
    x1_ref[...] = x1
    xq_ref[...] = _dot(_rmsnorm(x1, gx_ref[...]).astype(BF16), wq_ref[...])


def _merge(oa, ob, h2, x, wa, wb, wo, gx, wq, *, tm):
    n = x.shape[0]
    assert n % tm == 0
    half = D_MODEL // 2
    tok = lambda w, k=0: pl.BlockSpec((tm, w), lambda i: (i, k))
    full = lambda a: pl.BlockSpec(a.shape, lambda i: (0, 0))
    gx2 = gx.reshape(1, D_MODEL)
    return pl.pallas_call(
        _merge_kernel,
        grid=(n // tm,),
        in_specs=[tok(HG_WIDTH), tok(MB_WIDTH),
                  tok(half, COL_GA // half), tok(half, COL_GA // half + 1),
                  tok(half, COL_GB // half), tok(half, COL_GB // half + 1),
                  tok(D_MODEL), full(wa), full(wb), full(wo), full(gx2), full(wq)],
        out_specs=[tok(D_MODEL), tok(XA_WIDTH)],
        out_shape=[jax.ShapeDtypeStruct((n, D_MODEL), F32),
                   jax.ShapeDtypeStruct((n, XA_WIDTH), F32)],
        compiler_params=_params(("parallel",), 48),
        name="merge_mix",
    )(oa, ob, h2, h2, h2, h2, x, wa, wb, wo, gx2, wq)


def _mem_attn_kernel(q_ref, k_ref, v_ref, o_ref):
    scale = XA_DIM ** -0.5
    for hd in range(XA_HEADS):
        sl = slice(hd * XA_DIM, (hd + 1) * XA_DIM)
        s = _dot_nt(q_ref[0][:, sl].astype(BF16), k_ref[0][:, sl].astype(BF16)) * scale
        m = jnp.max(s, axis=-1, keepdims=True)
        p = jnp.exp(s - m)
        l = jnp.sum(p, axis=-1, keepdims=True)
        o_ref[0, :, sl] = _dot(p.astype(BF16), v_ref[0][:, sl].astype(BF16)) / l


def _mem_attn(xq3, mem_kv, *, tq):
    b, t, _ = xq3.shape
    assert t % tq == 0
    mem = lambda k: pl.BlockSpec((1, mem_kv.shape[1], XA_WIDTH), lambda i, j: (i, 0, k))
    qs = pl.BlockSpec((1, tq, XA_WIDTH), lambda i, j: (i, j, 0))
    return pl.pallas_call(
        _mem_attn_kernel,
        grid=(b, t // tq),
        in_specs=[qs, mem(0), mem(1)],
        out_specs=qs,
        out_shape=jax.ShapeDtypeStruct((b, t, XA_WIDTH), F32),
        compiler_params=_params(("parallel", "parallel"), 32),
        name="mem_attn",
    )(xq3, mem_kv, mem_kv)


def _mem_attn_rows_kernel(q_ref, k_ref, v_ref, o_ref):
    bb, t, _ = q_ref.shape
    rows = XA_HEADS * t
    scale = XA_DIM ** -0.5
    row_head = lax.broadcasted_iota(I32, (rows, 1), 0) // t
    kc = lax.broadcasted_iota(I32, (rows, k_ref.shape[1]), 1)
    same_head = (kc & (XA_HEADS - 1)) == row_head
    for i in range(bb):
        x = q_ref[i]
        q = jnp.concatenate([x[:, hd * XA_DIM:(hd + 1) * XA_DIM] for hd in range(XA_HEADS)], axis=0)
        s = _dot_nt(q.astype(BF16), k_ref[i].astype(BF16)) * scale
        s = jnp.where(same_head, s, NEG_INF)
        m = jnp.max(s, axis=-1, keepdims=True)
        p = jnp.exp(s - m)
        l = jnp.sum(p, axis=-1, keepdims=True)
        out = _dot(p.astype(BF16), v_ref[i].astype(BF16)) / l
        for hd in range(XA_HEADS):
            o_ref[i, :, hd * XA_DIM:(hd + 1) * XA_DIM] = out[hd * t:(hd + 1) * t]


def _mem_attn_rows(xq3, mem_k, mem_v, *, bb):
    b, t, _ = xq3.shape
    assert b % bb == 0 and XA_HEADS == 4 and t % 8 == 0
    mem = pl.BlockSpec((bb,) + mem_k.shape[1:], lambda i: (i, 0, 0))
    qs = pl.BlockSpec((bb, t, XA_WIDTH), lambda i: (i, 0, 0))
    return pl.pallas_call(
        _mem_attn_rows_kernel,
        grid=(b // bb,),
        in_specs=[qs, mem, mem],
        out_specs=qs,
        out_shape=jax.ShapeDtypeStruct((b, t, XA_WIDTH), F32),
        compiler_params=_params(("parallel",), 32),
        name="mem_attn_rows",
    )(xq3, mem_k, mem_v)


def _route(logits):
    lane = lax.broadcasted_iota(I32, logits.shape, 1).astype(F32)
    first = lambda hit: jnp.min(jnp.where(hit, lane, float(ROUTER_LANES)), axis=-1, keepdims=True)
    gl = jnp.where(lane < N_GROUPS, logits, NEG_INF)
    gmax = jnp.max(gl, axis=-1, keepdims=True)
    grp = first(gl == gmax)
    g_prob = 1.0 / jnp.sum(jnp.exp(gl - gmax), axis=-1, keepdims=True)
    e_lo = N_GROUPS + grp * EXPERTS_PER_GROUP
    in_grp = (lane >= e_lo) & (lane < e_lo + EXPERTS_PER_GROUP)
    el = jnp.where(in_grp, logits, NEG_INF)
    top1 = jnp.max(el, axis=-1, keepdims=True)
    idx1 = first(el == top1)
    el2 = jnp.where(lane == idx1, NEG_INF, el)
    top2 = jnp.max(el2, axis=-1, keepdims=True)
    idx2 = first(el2 == top2)
    e2 = jnp.exp(top2 - top1)
    w1 = g_prob / (1.0 + e2)
    w2 = w1 * e2
    comb = jnp.where(lane == idx1, w1, 0.0) + jnp.where(lane == idx2, w2, 0.0)
    return jnp.where(lane == GRP_LANE, grp, comb)


def _xo_route_kernel(x1_ref, at_ref, wxo_ref, gf_ref, wr_ref, br_ref, x2_ref, xf_ref, comb_ref):
    x2 = x1_ref[...] + _dot(at_ref[...].astype(BF16), wxo_ref[...])
    x2_ref[...] = x2
    xf = _rmsnorm(x2, gf_ref[...])
    xf_ref[...] = xf.astype(BF16)
    comb_ref[...] = _route(_dot_f32acc(xf, wr_ref[...]) + br_ref[...])


def _xo_route(x1, attn, wxo, g_ffn, w_router, b_router, *, tm):
    n = x1.shape[0]
    assert n % tm == 0
    tok = lambda w: pl.BlockSpec((tm, w), lambda i: (i, 0))
    full = lambda a: pl.BlockSpec(a.shape, lambda i: (0, 0))
    gf2 = g_ffn.reshape(1, D_MODEL)
    return pl.pallas_call(
        _xo_route_kernel,
        grid=(n // tm,),
        in_specs=[tok(D_MODEL), tok(XA_WIDTH), full(wxo), full(gf2), full(w_router), full(b_router)],
        out_specs=[tok(D_MODEL), tok(D_MODEL), tok(ROUTER_LANES)],
        out_shape=[jax.ShapeDtypeStruct((n, D_MODEL), F32), jax.ShapeDtypeStruct((n, D_MODEL), BF16),
                   jax.ShapeDtypeStruct((n, ROUTER_LANES), F32)],
        compiler_params=_params(("parallel",), 40),
        name="xo_route",
    )(x1, attn, wxo, gf2, w_router, b_router)


def _moe_kernel(xf_ref, comb_ref, x2_ref, wg_ref, wu_ref, wd_ref, gn_ref, y_ref,
                tri_ref, posc_ref, post_ref, cnt_ref, xg_ref, wq_ref, yg_ref, acc_ref, *, r):
    i = pl.program_id(0)
    step = pl.program_id(1)
    tm = xf_ref.shape[0]
    per_step = wg_ref.shape[0]
    steps_per_group = EXPERTS_PER_GROUP // per_step
    g = step // steps_per_group
    g_f = g.astype(F32)
    rp = -(-r // V7X_LANES) * V7X_LANES
    lane = lax.broadcasted_iota(I32, (tm, ROUTER_LANES), 1)

    @pl.when((i == 0) & (step == 0))
    def _():
        rr = lax.broadcasted_iota(I32, (tm, tm), 0)
        cc = lax.broadcasted_iota(I32, (tm, tm), 1)
        tri_ref[...] = jnp.where(cc < rr, 1.0, 0.0).astype(BF16)

    @pl.when(step == 0)
    def _():
        grp = comb_ref[:, GRP_LANE:GRP_LANE + 1]
        onehot = jnp.where((lane.astype(F32) == grp) & (lane < N_GROUPS), 1.0, 0.0)
        before = _dot(tri_ref[...], onehot.astype(BF16))
        pos = jnp.sum(onehot * before, axis=-1, keepdims=True)
        cnt_ref[...] = jnp.sum(onehot, axis=0, keepdims=True)
        posc_ref[...] = jnp.where(lane == 0, pos, jnp.where(lane == 1, grp, 0.0))
        hi = jnp.floor(pos * (1.0 / POS_SPLIT))
        lo = pos - hi * POS_SPLIT
        cols = jnp.where(lane == 0, hi, jnp.where(lane == 1, lo, jnp.where(lane == 2, grp, 0.0)))
        er = lax.broadcasted_iota(I32, (16, ROUTER_LANES), 0)
        ec = lax.broadcasted_iota(I32, (16, ROUTER_LANES), 1)
        post_ref[...] = _dot_nt(jnp.where(er == ec, 1.0, 0.0).astype(BF16), cols.astype(BF16))
        acc_ref[...] = jnp.zeros(acc_ref.shape, F32)

    lane_row = lax.broadcasted_iota(I32, (1, ROUTER_LANES), 1)
    count = jnp.sum(jnp.where(lane_row == g, cnt_ref[...], 0.0)).astype(I32)
    n_chunks = (count + (r - 1)) // r

    @pl.when(step % steps_per_group == 0)
    def _():
        pos_row = post_ref[0:1, :] * POS_SPLIT + post_ref[1:2, :]
        grp_row = post_ref[2:3, :]
        comb_hi, comb_lo = _split2(comb_ref[...])

        def gather(c, carry):
            rows = pl.ds(pl.multiple_of(c * r, 16), r)
            slot = (lax.broadcasted_iota(I32, (r, tm), 0) + c * r).astype(F32)
            pick = jnp.where((pos_row == slot) & (grp_row == g_f), 1.0, 0.0).astype(BF16)
            xg_ref[rows, :] = _dot(pick, xf_ref[...]).astype(BF16)
            wq_ref[rows, :] = _dot(pick, comb_hi) + _dot(pick, comb_lo)
            yg_ref[rows, :] = jnp.zeros((r, D_MODEL), F32)
            return carry

        lax.fori_loop(0, n_chunks, gather, 0)

    def experts(c, carry):
        rows = pl.ds(pl.multiple_of(c * r, 16), r)
        xg = xg_ref[rows, :]
        wq = wq_ref[rows, :]
        wl = lax.broadcasted_iota(I32, (r, ROUTER_LANES), 1)
        ups = [(_dot(xg, wg_ref[k]), _dot(xg, wu_ref[k])) for k in range(per_step)]
        hidden = []
        for k, (gate, up) in enumerate(ups):
            w_e = jnp.sum(jnp.where(wl == N_GROUPS + step * per_step + k, wq, 0.0), axis=-1, keepdims=True)
            hidden.append((_silu(gate) * up * w_e).astype(BF16))
        w_down = wd_ref[...].reshape(per_step * EXPERT_HIDDEN, D_MODEL)
        yg_ref[rows, :] += _dot(jnp.concatenate(hidden, axis=-1), w_down)
        return carry

    lax.fori_loop(0, n_chunks, experts, 0)

    @pl.when(step % steps_per_group == steps_per_group - 1)
    def _():
        pos_col = posc_ref[:, 0:1]
        grp_col = posc_ref[:, 1:2]

        def scatter(c, carry):
            rows = pl.ds(pl.multiple_of(c * r, 16), r)
            sl = lax.broadcasted_iota(I32, (tm, rp), 1)
            slot = (sl + c * r).astype(F32)
            place = jnp.where((pos_col == slot) & (grp_col == g_f) & (sl < r), 1.0, 0.0).astype(BF16)
            yg = yg_ref[rows, :].astype(BF16)
            if rp > r:
                yg = jnp.concatenate([yg, jnp.zeros((rp - r, D_MODEL), BF16)], axis=0)
            acc_ref[...] += _dot(place, yg)
            return carry

        lax.fori_loop(0, n_chunks, scatter, 0)

    @pl.when(step == pl.num_programs(1) - 1)
    def _():
        y_ref[...] = _rmsnorm(x2_ref[...] + acc_ref[...], gn_ref[...])


def _moe(xf, comb, x2, wg, wu, wd, g_final, *, tm):
    n = xf.shape[0]
    assert n % tm == 0 and tm % 16 == 0
    r = min(MOE_CHUNK_ROWS, tm)
    cap = -(-tm // r) * r
    per_step = MOE_EXPERTS_PER_STEP
    tok = lambda w: pl.BlockSpec((tm, w), lambda i, s: (i, 0))
    gn2 = g_final.reshape(1, D_MODEL)
    kern = functools.partial(_moe_kernel, r=r)
    vmem = (2 * tm * D_MODEL * (2 + 4 + 4) + 2 * tm * ROUTER_LANES * 4
            + 2 * 3 * per_step * D_MODEL * EXPERT_HIDDEN * 2 + tm * tm * 2
            + cap * D_MODEL * (2 + 4) + tm * D_MODEL * 4) // MIB + 8
    return pl.pallas_call(
        kern,
        grid=(n // tm, N_EXPERTS // per_step),
        in_specs=[tok(D_MODEL), tok(ROUTER_LANES), tok(D_MODEL),
                  pl.BlockSpec((per_step, D_MODEL, EXPERT_HIDDEN), lambda i, s: (s, 0, 0)),
                  pl.BlockSpec((per_step, D_MODEL, EXPERT_HIDDEN), lambda i, s: (s, 0, 0)),
                  pl.BlockSpec((per_step, EXPERT_HIDDEN, D_MODEL), lambda i, s: (s, 0, 0)),
                  pl.BlockSpec((1, D_MODEL), lambda i, s: (0, 0))],
        out_specs=tok(D_MODEL),
        out_shape=jax.ShapeDtypeStruct((n, D_MODEL), F32),
        scratch_shapes=[pltpu.VMEM((tm, tm), BF16),
                        pltpu.VMEM((tm, ROUTER_LANES), F32), pltpu.VMEM((16, tm), F32),
                        pltpu.VMEM((1, ROUTER_LANES), F32),
                        pltpu.VMEM((cap, D_MODEL), BF16), pltpu.VMEM((cap, ROUTER_LANES), F32),
                        pltpu.VMEM((cap, D_MODEL), F32), pltpu.VMEM((tm, D_MODEL), F32)],
        compiler_params=_params(("arbitrary", "arbitrary"), vmem),
        name="moe_grouped",
    )(xf, comb, x2, wg, wu, wd, gn2)


def _token_tile(n, cap=512):
    return cap if n % cap == 0 else n


def _col_tile(n, cap=1536):
    return max(c for c in range(V7X_LANES, cap + 1, V7X_LANES) if n % c == 0)


def _layer(x3, w, s0, moba_fn, mem_fn):
    b, t, _ = x3.shape
    n = b * t
    x2d = x3.reshape(n, D_MODEL)
    tm = _token_tile(n)
    h2 = _norm_matmul(x2d, w["norm_mix"], w["w_in"], tm=_token_tile(n, 1024), tn=_col_tile(IN_COLS))
    h3 = h2.reshape(b, t, IN_COLS)
    oa, s_new = _hgrn(h3, w["hg_lb_logits"], w["hg_norm"], s0)
    ob = moba_fn(h3)
    x1, xq = _merge(oa.reshape(n, HG_WIDTH), ob.reshape(n, MB_WIDTH), h2, x2d,
                    w["w_branch_a"], w["w_branch_b"], w["w_mix_out"], w["norm_xattn"], w["w_xq"], tm=tm)
    attn = mem_fn(xq.reshape(b, t, XA_WIDTH))
    x2, xf, comb = _xo_route(x1, attn.reshape(n, XA_WIDTH), w["w_xo"], w["norm_ffn"],
                             w["w_router"], w["b_router"], tm=tm)
    y = _moe(xf, comb, x2, w["w_expert_gate"], w["w_expert_up"], w["w_expert_down"],
             w["norm_final"], tm=_token_tile(n, 1024))
    k_new = h3[:, :, COL_MK:COL_MV].reshape(b, t, MB_HEADS, MB_DIM)
    v_new = h3[:, :, COL_MV:COL_GA].reshape(b, t, MB_HEADS, MB_DIM)
    return y.reshape(b, t, D_MODEL), k_new, v_new, s_new


def kernel(x_prompt, x_sample, cache_k, cache_v, state_hgrn, cache_mem_k, cache_mem_v, page_table, mem_prompt, norm_mix, w_in, hg_lb_logits, hg_norm, w_branch_a, w_branch_b, w_mix_out, rel_table, norm_xattn, norm_mem, w_xq, w_xk, w_xv, w_xo, norm_ffn, w_group_router, b_group_router, w_expert_router, b_expert_router, w_expert_gate, w_expert_up, w_expert_down, norm_final):
    assert w_in.shape[0] == DEPTH == 1 and hg_lb_logits.shape[0] == DEPTH + 1
    b = x_prompt.shape[0]
    db = x_sample.shape[0]
    n_pool = cache_k.shape[1]
    pad_lanes = ROUTER_LANES - N_GROUPS - N_EXPERTS
    w = {
        "norm_mix": norm_mix[0], "w_in": w_in[0].astype(BF16),
        "hg_lb_logits": hg_lb_logits, "hg_norm": hg_norm[0],
        "w_branch_a": w_branch_a[0].astype(BF16), "w_branch_b": w_branch_b[0].astype(BF16),
        "w_mix_out": w_mix_out[0].astype(BF16), "norm_xattn": norm_xattn[0],
        "w_xq": w_xq[0].astype(BF16), "w_xo": w_xo[0].astype(BF16), "norm_ffn": norm_ffn[0],
        "w_router": jnp.pad(jnp.concatenate([w_group_router[0], w_expert_router[0]], axis=1),
                            ((0, 0), (0, pad_lanes))),
        "b_router": jnp.pad(jnp.concatenate([b_group_router[0], b_expert_router[0]]),
                            (0, pad_lanes)).reshape(1, ROUTER_LANES),
        "w_expert_gate": w_expert_gate[0].reshape(N_EXPERTS, D_MODEL, EXPERT_HIDDEN).astype(BF16),
        "w_expert_up": w_expert_up[0].reshape(N_EXPERTS, D_MODEL, EXPERT_HIDDEN).astype(BF16),
        "w_expert_down": w_expert_down[0].reshape(N_EXPERTS, EXPERT_HIDDEN, D_MODEL).astype(BF16),
        "norm_final": norm_final,
    }

    w_mem = jnp.concatenate([w_xk[0], w_xv[0]], axis=1).astype(BF16)
    mem_kv = _norm_matmul(mem_prompt.reshape(b * MEM_LEN, D_MODEL), norm_mem[0], w_mem,
                          tm=_token_tile(b * MEM_LEN), tn=512).reshape(b, MEM_LEN, 2 * XA_WIDTH)
    mk_p = mem_kv[:, :, :XA_WIDTH]
    mv_p = mem_kv[:, :, XA_WIDTH:]
    s0 = jnp.zeros((b, HG_HEADS, HG_DIM, HG_DIM), F32)
    y_p, k_p, v_p, s_p = _layer(x_prompt, w, s0,
                                functools.partial(_moba_prompt, rel_table=rel_table),
                                functools.partial(_mem_attn, mem_kv=mem_kv, tq=512))

    ck = cache_k[0].reshape(n_pool, PAGE_SIZE * MB_HEADS, MB_DIM)
    cv = cache_v[0].reshape(n_pool, PAGE_SIZE * MB_HEADS, MB_DIM)
    moba_s = functools.partial(_moba_sample, cache_k=ck, cache_v=cv, page_table=page_table,
                               rel_table=rel_table)
    mem_s = functools.partial(_mem_attn_rows, bb=4,
                              mem_k=cache_mem_k[0].reshape(db, MEM_LEN * XA_HEADS, XA_DIM),
                              mem_v=cache_mem_v[0].reshape(db, MEM_LEN * XA_HEADS, XA_DIM))
    y_s, k_s, v_s, s_s = _layer(x_sample, w, state_hgrn[0], moba_s, mem_s)

    heads = lambda a: a.reshape(b, MEM_LEN, XA_HEADS, XA_DIM)[None]
    return (y_p, y_s, k_p[None], v_p[None], s_p[None], heads(mk_p), heads(mv_p),
            k_s[None], v_s[None], s_s[None])
```

```python
import functools
import math

import numpy as np
import jax
import jax.numpy as jnp
from jax import lax
from jax.experimental import pallas as pl
from jax.experimental.pallas import tpu as pltpu

F32 = jnp.float32
BF16 = jnp.bfloat16
I32 = jnp.int32

D_MODEL = 1024
DEPTH = 1
PAGE_SIZE = 128
HG_HEADS = 4
HG_DIM = 128
HG_WIDTH = HG_HEADS * HG_DIM
MB_HEADS = 4
MB_DIM = 128
MB_WIDTH = MB_HEADS * MB_DIM
MB_BLOCK = 256
MB_TOPK = 3
REL_BUCKETS = 32
REL_MAX_DIST = 128
REL_MAX_EXACT = REL_BUCKETS // 2
MEM_LEN = 256
XA_HEADS = 4
XA_DIM = 128
XA_WIDTH = XA_HEADS * XA_DIM
N_GROUPS = 4
EXPERTS_PER_GROUP = 8
N_EXPERTS = N_GROUPS * EXPERTS_PER_GROUP
EXPERT_TOPK = 2
EXPERT_HIDDEN = 256
NORM_EPS = 1e-6
IN_COLS = 4 * HG_WIDTH + 3 * MB_WIDTH + 2 * D_MODEL
COL_MQ = 4 * HG_WIDTH
COL_MK = COL_MQ + MB_WIDTH
COL_MV = COL_MK + MB_WIDTH
COL_GA = COL_MV + MB_WIDTH
COL_GB = COL_GA + D_MODEL

V7X_LANES = 128
V7X_VMEM_BYTES = 64 * 1024 * 1024
MIB = 1024 * 1024

NEG_INF = float("-inf")
MASK_BIG = -1e30
GATE_PAD = 16
ROUTER_LANES = 128
GRP_LANE = 0
POS_SPLIT = 32.0
MOE_CHUNK_ROWS = 320
MOE_EXPERTS_PER_STEP = 4
HGRN_PROMPT_CHUNK = 128
HGRN_PROMPT_ROWS = 4
HGRN_MIN_CHUNK = 16
HGRN_CHAINS = 16


def _params(semantics, vmem_mib):
    return pltpu.CompilerParams(dimension_semantics=semantics,
                                vmem_limit_bytes=min(vmem_mib * MIB, V7X_VMEM_BYTES - 8 * MIB))


def _dot(a, b):
    return jnp.dot(a, b, preferred_element_type=F32)


def _dot_nt(a, b):
    return lax.dot_general(a, b, (((1,), (1,)), ((), ())), preferred_element_type=F32)


def _dot_tn(a, b):
    return lax.dot_general(a, b, (((0,), (0,)), ((), ())), preferred_element_type=F32)


def _split2(a):
    hi = a.astype(BF16)
    lo = (a - hi.astype(F32)).astype(BF16)
    return hi, lo


def _dot_nt_f32acc(a, b):
    ah, al = _split2(a)
    bh, bl = _split2(b)
    return _dot_nt(ah, bh) + (_dot_nt(ah, bl) + _dot_nt(al, bh))


def _dot_f32acc(a, b):
    ah, al = _split2(a)
    bh, bl = _split2(b)
    return _dot(ah, bh) + (_dot(ah, bl) + _dot(al, bh))


def _rmsnorm(x, g):
    return x * lax.rsqrt(jnp.mean(x * x, axis=-1, keepdims=True) + NORM_EPS) * g


def _sigmoid(x):
    return 1.0 / (1.0 + jnp.exp(-x))


def _silu(x):
    return x * _sigmoid(x)


def _norm_matmul_kernel(x_ref, g_ref, w_ref, o_ref, *refs, head_cols, heads, head_dim):
    head_refs, xn_ref = refs[:-1], refs[-1]
    j = pl.program_id(1)
    tm, tn = o_ref.shape

    @pl.when(j == 0)
    def _():
        xn_ref[...] = _rmsnorm(x_ref[...], g_ref[...]).astype(BF16)

    res = _dot(xn_ref[...], w_ref[...])
    o_ref[...] = res
    for col0, href in zip(head_cols, head_refs):
        for hd in range(heads):
            tile, local = divmod(col0 + hd * head_dim, tn)

            @pl.when(j == tile)
            def _(hd=hd, local=local, href=href):
                href[pl.ds(hd, tm, stride=heads), :] = res[:, local:local + head_dim]


def _norm_matmul(x, g, w_bf16, *, tm, tn, head_cols=(), heads=1, head_dim=V7X_LANES):
    m, d = x.shape
    n = w_bf16.shape[1]
    assert m % tm == 0 and n % tn == 0 and tn % head_dim == 0
    vmem = (2 * tm * d * 4 + tm * d * 2 + 2 * d * tn * 2 + 2 * tm * tn * 4
            + len(head_cols) * 2 * tm * heads * head_dim * 4) // MIB + 12
    kern = functools.partial(_norm_matmul_kernel, head_cols=tuple(head_cols), heads=heads,
                             head_dim=head_dim)
    head_spec = pl.BlockSpec((tm * heads, head_dim), lambda i, j: (i, 0))
    head_shape = jax.ShapeDtypeStruct((m * heads, head_dim), F32)
    return pl.pallas_call(
        kern,
        grid=(m // tm, n // tn),
        in_specs=[pl.BlockSpec((tm, d), lambda i, j: (i, 0)),
                  pl.BlockSpec((1, d), lambda i, j: (0, 0)),
                  pl.BlockSpec((d, tn), lambda i, j: (0, j))],
        out_specs=[pl.BlockSpec((tm, tn), lambda i, j: (i, j))] + [head_spec] * len(head_cols),
        out_shape=[jax.ShapeDtypeStruct((m, n), F32)] + [head_shape] * len(head_cols),
        scratch_shapes=[pltpu.VMEM((tm, d), BF16)],
        compiler_params=_params(("parallel", "arbitrary"), vmem),
        name="norm_matmul",
    )(x, g.reshape(1, d), w_bf16)


def _hgrn_levels(c):
    return int(round(math.log2(c)))


def _hgrn_sum_masks(c):
    t = np.arange(c)[:, None]
    j = np.arange(c)[None, :]
    rows = [j <= t, j > t]
    for lv in range(_hgrn_levels(c)):
        half = c >> (lv + 1)
        blk = 2 * half
        mid = (t // blk) * blk + half - 1
        upper = (t % blk) >= half
        rows.append((upper & (j > mid) & (j <= t)) | ((~upper) & (j > t) & (j <= mid)))
    return np.concatenate(rows, axis=0).astype(np.float32)


def _hgrn_kernel(hq_ref, hf_ref, hi_ref, hg_ref, lbl_ref, gn_ref, mask_ref, s0_ref,
                 o_ref, s_ref, st_ref, *, c, t_blk, bb):
    j = pl.program_id(1)
    chains = [(bi, hd) for bi in range(bb) for hd in range(HG_HEADS)]

    @pl.when(j == 0)
    def _():
        for bi, hd in chains:
            st_ref[bi * HG_HEADS + hd] = s0_ref[bi, hd].T

    lbl = lbl_ref[...]
    lmax = jnp.max(lbl, axis=0, keepdims=True)
    lexp = jnp.exp(lbl - lmax)
    lb_all = lexp[0:1, :] / jnp.sum(lexp, axis=0, keepdims=True)

    masks = mask_ref[...]
    pad = c - t_blk
    row = lax.broadcasted_iota(I32, (c, HG_DIM), 0)
    rr = lax.broadcasted_iota(I32, (c, c), 0)
    cc = lax.broadcasted_iota(I32, (c, c), 1)

    def padded(a):
        if pad == 0:
            return a
        return jnp.concatenate([a, jnp.zeros((pad, HG_DIM), F32)], axis=0)

    hsl = lambda hd: slice(hd * HG_DIM, (hd + 1) * HG_DIM)
    qs, ks, vs, logf_parts = [], [], [], []
    for bi, hd in chains:
        xq = hq_ref[bi][:, hsl(hd)]
        lb = lb_all[:, hsl(hd)]
        f = lb + (1.0 - lb) * _sigmoid(hf_ref[bi][:, hsl(hd)])
        qs.append(padded(_silu(xq)))
        ks.append(padded(1.0 - f))
        vs.append(padded(hi_ref[bi][:, hsl(hd)]))
        logf_parts.append(_split2(padded(jnp.log(f))))
    exs = []
    for n in range(0, len(chains), 2):
        (hi_a, lo_a), (hi_b, lo_b) = logf_parts[n], logf_parts[n + 1]
        stacked = jnp.concatenate([jnp.concatenate([hi_a, hi_b], axis=-1),
                                   jnp.concatenate([lo_a, lo_b], axis=-1)], axis=0)
        ex = jnp.exp(_dot(masks, stacked))
        exs += [ex[:, :HG_DIM], ex[:, HG_DIM:]]

    level_dots = []
    for q, kk, ex in zip(qs, ks, exs):
        per_level = []
        for lv in range(_hgrn_levels(c)):
            half = c >> (lv + 1)
            upper = (row & half) != 0
            ex_lv = ex[(2 + lv) * c:(3 + lv) * c]
            a = jnp.where(upper, q * ex_lv, 0.0).astype(BF16)
            bm = jnp.where(upper, 0.0, kk * ex_lv).astype(BF16)
            per_level.append(_dot_nt(a, bm))
        level_dots.append(per_level)

    outs = []
    for n, (bi, hd) in enumerate(chains):
        q, kk, v, ex = qs[n], ks[n], vs[n], exs[n]
        scores = jnp.zeros((c, c), F32)
        for lv, d in enumerate(level_dots[n]):
            scores = scores + jnp.where((rr ^ cc) < 2 * (c >> (lv + 1)), d, 0.0)
        eb = ex[0:c]
        eb_rev = ex[c:2 * c]
        diag = jnp.sum(q * kk, axis=-1, keepdims=True)
        st = st_ref[bi * HG_HEADS + hd]
        vb = v.astype(BF16)
        o = (_dot_nt((q * eb).astype(BF16), st.astype(BF16))
             + _dot(scores.astype(BF16), vb) + diag * v)
        st_ref[bi * HG_HEADS + hd] = st * eb[c - 1:c, :] + _dot_tn(vb, (kk * eb_rev).astype(BF16))
        outs.append(o[0:t_blk])

    for (bi, hd), o in zip(chains, outs):
        xg = hg_ref[bi][:, hsl(hd)]
        o_ref[bi, :, hsl(hd)] = _rmsnorm(o, gn_ref[:, hsl(hd)]) * _silu(xg)

    @pl.when(j == pl.num_programs(1) - 1)
    def _():
        for bi, hd in chains:
            s_ref[bi, hd] = st_ref[bi * HG_HEADS + hd].T


def _hgrn(h3, lb_logits, hg_norm, s0):
    b, t, _ = h3.shape
    t_blk = math.gcd(t, HGRN_PROMPT_CHUNK)
    c = max(t_blk, HGRN_MIN_CHUNK)
    bb = math.gcd(b, HGRN_CHAINS // HG_HEADS) if t == t_blk else math.gcd(b, HGRN_PROMPT_ROWS)
    m01 = _hgrn_sum_masks(c)
    masks = jnp.asarray(np.concatenate([m01, m01], axis=1), BF16)
    col = lambda k: pl.BlockSpec((bb, t_blk, HG_WIDTH), lambda i, j: (i, j, k))
    state_spec = pl.BlockSpec((bb, HG_HEADS, HG_DIM, HG_DIM), lambda i, j: (i, 0, 0, 0))
    kern = functools.partial(_hgrn_kernel, c=c, t_blk=t_blk, bb=bb)
    return pl.pallas_call(
        kern,
        grid=(b // bb, t // t_blk),
        in_specs=[col(0), col(1), col(2), col(3),
                  pl.BlockSpec(lb_logits.shape, lambda i, j: (0, 0)),
                  pl.BlockSpec((1, HG_WIDTH), lambda i, j: (0, 0)),
                  pl.BlockSpec(masks.shape, lambda i, j: (0, 0)),
                  state_spec],
        out_specs=[pl.BlockSpec((bb, t_blk, HG_WIDTH), lambda i, j: (i, j, 0)), state_spec],
        out_shape=[jax.ShapeDtypeStruct((b, t, HG_WIDTH), F32),
                   jax.ShapeDtypeStruct(s0.shape, F32)],
        scratch_shapes=[pltpu.VMEM((bb * HG_HEADS, HG_DIM, HG_DIM), F32)],
        compiler_params=_params(("parallel", "arbitrary"), 32),
        name="hgrn2",
    )(h3, h3, h3, h3, lb_logits, hg_norm.reshape(1, HG_WIDTH), masks, s0)


def _rel_bias(rel_ref, head, dist):
    d = jnp.maximum(dist, 0)
    df = jnp.maximum(d, 1).astype(F32)
    large = REL_MAX_EXACT + (jnp.log(df / REL_MAX_EXACT) / math.log(REL_MAX_DIST / REL_MAX_EXACT)
                             * (REL_BUCKETS - REL_MAX_EXACT)).astype(I32)
    large = jnp.minimum(large, REL_BUCKETS - 1)
    bucket = jnp.where(d < REL_MAX_EXACT, d, large)
    out = jnp.zeros(dist.shape, F32)
    for kb in range(REL_BUCKETS):
        out = jnp.where(bucket == kb, rel_ref[kb, head], out)
    return out


def _stack_rows(rows):
    ridx = lax.broadcasted_iota(I32, (GATE_PAD, MB_DIM), 0)
    out = jnp.zeros((GATE_PAD, MB_DIM), F32)
    for n, r in enumerate(rows):
        out = jnp.where(ridx == n, r, out)
    return out


def _topk_blocks(gate, n_cand, n_past):
    lane = lax.broadcasted_iota(I32, gate.shape, 1)
    past = lane < n_past
    g = jnp.where(past, gate, NEG_INF)
    rank = jnp.zeros(gate.shape, I32)
    for m in range(n_cand):
        gm = g[:, m:m + 1]
        ahead = jnp.where(gm > g, 1, jnp.where((gm == g) & (lane > m), 1, 0))
        rank = rank + ahead
    return jnp.where(past & (rank < MB_TOPK), 1.0, 0.0)


def _topk_blocks_t(gate_t, n_cand, n_past):
    blk_id = lax.broadcasted_iota(I32, gate_t.shape, 0)
    past = blk_id < n_past
    g = jnp.where(past, gate_t, NEG_INF)
    rank = jnp.zeros(gate_t.shape, I32)
    for m in range(n_cand):
        gm = g[m:m + 1, :]
        rank = rank + jnp.where(gm > g, 1, jnp.where((gm == g) & (blk_id > m), 1, 0))
    return past & (rank < MB_TOPK)


def _moba_prompt_kernel(rel_ref, q_ref, k_ref, v_ref, o_ref,
                        kx_ref, vx_ref, km_ref, bias_ref, eye_ref, s_ref, mx_ref, mb_ref, acc_ref, *, n_blk):
    b = pl.program_id(0)
    i = pl.program_id(1)
    blk = MB_BLOCK
    scale = MB_DIM ** -0.5
    heads = range(MB_HEADS)
    hsl = lambda hd: slice(hd * MB_DIM, (hd + 1) * MB_DIM)

    t = k_ref.shape[1]
    own_slot = n_blk - 1

    @pl.when((b == 0) & (i == 0))
    def _():
        r = lax.broadcasted_iota(I32, (blk, blk), 0)
        c = lax.broadcasted_iota(I32, (blk, blk), 1)
        eye_ref[...] = jnp.where(r == c, 1.0, 0.0).astype(BF16)
        kr = lax.broadcasted_iota(I32, (t, MB_DIM), 0)
        kc = lax.broadcasted_iota(I32, (t, MB_DIM), 1)
        blk_onehot = jnp.where(kr // blk == kc, 1.0, 0.0).astype(BF16)
        for hd in heads:
            bias_ref[hd, 0] = jnp.where(c <= r, _rel_bias(rel_ref, hd, r - c), NEG_INF)
            bias_ref[hd, 1] = _rel_bias(rel_ref, hd, r - c + blk)
            kx_ref[hd, :, MB_DIM:] = blk_onehot
            vx_ref[hd, :, MB_DIM:] = jnp.ones((t, MB_DIM), BF16)

    @pl.when(i == 0)
    def _():
        means = [jnp.sum(k_ref[0, n * blk:(n + 1) * blk, :], axis=0, keepdims=True) * (1.0 / blk)
                 for n in range(n_blk)]
        for hd in heads:
            kx_ref[hd, :, :MB_DIM] = k_ref[0][:, hsl(hd)].astype(BF16)
            vx_ref[hd, :, :MB_DIM] = v_ref[0][:, hsl(hd)].astype(BF16)
            km_ref[hd] = _stack_rows([mn[:, hsl(hd)] for mn in means])

    q = q_ref[0]
    blk_id = lax.broadcasted_iota(I32, (GATE_PAD, blk), 0)
    qh = [q[:, hsl(hd)] for hd in heads]
    gates = [_dot_nt_f32acc(km_ref[hd], qh[hd]) for hd in heads]
    qm_t = []
    for hd in heads:
        sel_t = _topk_blocks_t(gates[hd], n_blk, i)
        m_t = jnp.where(sel_t | (blk_id >= i), 0.0, MASK_BIG).astype(BF16)
        qm_t.append(jnp.concatenate([m_t, jnp.zeros((MB_DIM - GATE_PAD, blk), BF16)], axis=0))
    qmask = [_dot_nt(eye_ref[...], qm_t[hd]) for hd in heads]
    qx = [jnp.concatenate([qh[hd].astype(BF16), qmask[hd].astype(BF16)], axis=-1) for hd in heads]

    own0 = pl.multiple_of(i * blk, blk)
    dots = [_dot_nt(qx[hd], kx_ref[hd, pl.ds(own0, blk), :]) for hd in heads]
    for hd in heads:
        s = dots[hd] * scale + bias_ref[hd, 0]
        s_ref[hd, own_slot] = s
        mx_ref[hd] = jnp.maximum(s[:, :MB_DIM], s[:, MB_DIM:])

    for n in range(n_blk - 1):
        @pl.when(n < i)
        def _(n=n):
            dots = [_dot_nt(qx[hd], kx_ref[hd, n * blk:(n + 1) * blk, :]) for hd in heads]
            for hd in heads:
                far_bias = rel_ref[REL_BUCKETS - 1, hd]
                s = dots[hd] * scale + jnp.where(n == i - 1, bias_ref[hd, 1], far_bias)
                s_ref[hd, n] = s
                mx_ref[hd] = jnp.maximum(mx_ref[hd], jnp.maximum(s[:, :MB_DIM], s[:, MB_DIM:]))

    row_max = [jnp.max(mx_ref[hd], axis=-1, keepdims=True) for hd in heads]
    for hd in heads:
        mb_ref[hd] = jnp.broadcast_to(row_max[hd], (blk, 2 * MB_DIM))
    probs = [jnp.exp(s_ref[hd, own_slot] - mb_ref[hd]).astype(BF16) for hd in heads]
    for hd in heads:
        acc_ref[hd] = _dot(probs[hd], vx_ref[hd, pl.ds(own0, blk), :])

    for n in range(n_blk - 1):
        @pl.when(n < i)
        def _(n=n):
            probs = [jnp.exp(s_ref[hd, n] - mb_ref[hd]).astype(BF16) for hd in heads]
            pv = [_dot(probs[hd], vx_ref[hd, n * blk:(n + 1) * blk, :]) for hd in heads]
            for hd in heads:
                acc_ref[hd] += pv[hd]

    for hd in heads:
        acc = acc_ref[hd]
        o_ref[0, :, hsl(hd)] = acc[:, :MB_DIM] / acc[:, MB_DIM:]


def _moba_prompt(h3, rel_table):
    b, t, _ = h3.shape
    assert t % MB_BLOCK == 0
    n_blk = t // MB_BLOCK
    assert n_blk <= GATE_PAD
    qcol, kcol, vcol = COL_MQ // MB_WIDTH, COL_MK // MB_WIDTH, COL_MV // MB_WIDTH
    kern = functools.partial(_moba_prompt_kernel, n_blk=n_blk)
    wide = 2 * MB_DIM
    vmem = (2 * 2 * t * MB_WIDTH * 4 + 2 * MB_HEADS * t * wide * 2
            + MB_HEADS * (n_blk + 4) * MB_BLOCK * wide * 4) // MIB + 12
    return pl.pallas_call(
        kern,
        grid=(b, n_blk),
        in_specs=[pl.BlockSpec(memory_space=pltpu.SMEM),
                  pl.BlockSpec((1, MB_BLOCK, MB_WIDTH), lambda bb, i: (bb, i, qcol)),
                  pl.BlockSpec((1, t, MB_WIDTH), lambda bb, i: (bb, 0, kcol)),
                  pl.BlockSpec((1, t, MB_WIDTH), lambda bb, i: (bb, 0, vcol))],
        out_specs=pl.BlockSpec((1, MB_BLOCK, MB_WIDTH), lambda bb, i: (bb, i, 0)),
        out_shape=jax.ShapeDtypeStruct((b, t, MB_WIDTH), F32),
        scratch_shapes=[pltpu.VMEM((MB_HEADS, t, wide), BF16), pltpu.VMEM((MB_HEADS, t, wide), BF16),
                        pltpu.VMEM((MB_HEADS, GATE_PAD, MB_DIM), F32),
                        pltpu.VMEM((MB_HEADS, 2, MB_BLOCK, MB_BLOCK), F32),
                        pltpu.VMEM((MB_BLOCK, MB_BLOCK), BF16),
                        pltpu.VMEM((MB_HEADS, n_blk, MB_BLOCK, MB_BLOCK), F32),
                        pltpu.VMEM((MB_HEADS, MB_BLOCK, MB_DIM), F32),
                        pltpu.VMEM((MB_HEADS, MB_BLOCK, wide), F32),
                        pltpu.VMEM((MB_HEADS, MB_BLOCK, wide), F32)],
        compiler_params=_params(("arbitrary", "arbitrary"), vmem),
        name="moba_prompt",
    )(rel_table, h3, h3, h3)


def _moba_sample_kernel(pt_ref, rel_ref, q_ref, kn_ref, vn_ref, *refs, n_pages, past_len):
    del pt_ref
    kp = refs[:n_pages]
    vp = refs[n_pages:2 * n_pages]
    o_ref, bias_ref, bias_own_ref = refs[2 * n_pages:]
    t = q_ref.shape[1]
    rows = MB_HEADS * t
    page_rows = PAGE_SIZE * MB_HEADS
    pages_per_blk = MB_BLOCK // PAGE_SIZE
    n_past = past_len // MB_BLOCK
    scale = MB_DIM ** -0.5

    def head_rows(ref):
        x = ref[0]
        return jnp.concatenate([x[:, hd * MB_DIM:(hd + 1) * MB_DIM] for hd in range(MB_HEADS)], axis=0)

    @pl.when(pl.program_id(0) == 0)
    def _():
        tq = lax.broadcasted_iota(I32, (t, page_rows), 0)
        kc = lax.broadcasted_iota(I32, (t, page_rows), 1)
        for p in range(n_pages):
            dist = past_len + tq - (p * PAGE_SIZE + (kc >> 2))
            bias_ref[p] = jnp.concatenate(
                [jnp.where((kc & (MB_HEADS - 1)) == hd, _rel_bias(rel_ref, hd, dist), NEG_INF)
                 for hd in range(MB_HEADS)], axis=0)
        ro = lax.broadcasted_iota(I32, (t, rows), 0)
        co = lax.broadcasted_iota(I32, (t, rows), 1)
        for hd in range(MB_HEADS):
            own = _rel_bias(rel_ref, hd, ro - (co - hd * t))
            keep = (co >= hd * t) & (co <= hd * t + ro)
            bias_own_ref[hd * t:(hd + 1) * t, :] = jnp.where(keep, own, NEG_INF)

    q = head_rows(q_ref)
    qb = q.astype(BF16)
    row_head = lax.broadcasted_iota(I32, (rows, 1), 0) // t

    groups = page_rows // 8
    sums = [jnp.sum(kp[p][0].reshape(groups, 8, MB_DIM), axis=0) for p in range(n_pages)]
    blk_sums = []
    for n in range(n_past):
        acc = sums[n * pages_per_blk]
        for pp in range(1, pages_per_blk):
            acc = acc + sums[n * pages_per_blk + pp]
        blk_sums.append(acc)
    g_full = _dot_nt_f32acc(q, jnp.concatenate(blk_sums, axis=0))
    gc = lax.broadcasted_iota(I32, g_full.shape, 1)
    g_full = jnp.where((gc & (MB_HEADS - 1)) == row_head, g_full, 0.0)
    lane = lax.broadcasted_iota(I32, (rows, GATE_PAD), 1)
    gate = jnp.zeros((rows, GATE_PAD), F32)
    for n in range(n_past):
        g_n = jnp.sum(g_full[:, n * 8:(n + 1) * 8], axis=-1, keepdims=True) * (1.0 / MB_BLOCK)
        gate = jnp.where(lane == n, g_n, gate)
    sel = _topk_blocks(gate, n_past, n_past)

    s_own = _dot_nt(q, head_rows(kn_ref)) * scale + bias_own_ref[...]
    m = jnp.max(s_own, axis=-1, keepdims=True)
    s_past = []
    for p in range(n_pages):
        n = p // pages_per_blk
        s = _dot_nt(qb, kp[p][0].astype(BF16)) * scale + bias_ref[p]
        s = jnp.where(sel[:, n:n + 1] > 0.0, s, NEG_INF)
        s_past.append(s)
        m = jnp.maximum(m, jnp.max(s, axis=-1, keepdims=True))
    p_own = jnp.exp(s_own - m)
    l = jnp.sum(p_own, axis=-1, keepdims=True)
    out = _dot(p_own, head_rows(vn_ref))
    for p in range(n_pages):
        pr = jnp.exp(s_past[p] - m)
        l = l + jnp.sum(pr, axis=-1, keepdims=True)
        out = out + _dot(pr.astype(BF16), vp[p][0].astype(BF16))
    out = out / l
    for hd in range(MB_HEADS):
        o_ref[0, :, hd * MB_DIM:(hd + 1) * MB_DIM] = out[hd * t:(hd + 1) * t]


def _moba_sample(h3, cache_k, cache_v, page_table, rel_table):
    db, t, _ = h3.shape
    n_pages = page_table.shape[1]
    past_len = n_pages * PAGE_SIZE
    assert past_len % MB_BLOCK == 0 and t <= MB_BLOCK and past_len // MB_BLOCK < GATE_PAD
    assert MB_HEADS == 4 and t % 8 == 0
    qcol, kcol, vcol = COL_MQ // MB_WIDTH, COL_MK // MB_WIDTH, COL_MV // MB_WIDTH
    page_rows = PAGE_SIZE * MB_HEADS
    new = lambda k: pl.BlockSpec((1, t, MB_WIDTH), lambda i, pt: (i, 0, k))
    page = lambda p: pl.BlockSpec((1, page_rows, MB_DIM), lambda i, pt: (pt[i, p], 0, 0))
    kern = functools.partial(_moba_sample_kernel, n_pages=n_pages, past_len=past_len)
    grid_spec = pltpu.PrefetchScalarGridSpec(
        num_scalar_prefetch=1,
        grid=(db,),
        in_specs=([pl.BlockSpec(memory_space=pltpu.SMEM), new(qcol), new(kcol), new(vcol)]
                  + [page(p) for p in range(n_pages)] * 2),
        out_specs=pl.BlockSpec((1, t, MB_WIDTH), lambda i, pt: (i, 0, 0)),
        scratch_shapes=[pltpu.VMEM((n_pages, MB_HEADS * t, page_rows), F32),
                        pltpu.VMEM((MB_HEADS * t, MB_HEADS * t), F32)],
    )
    return pl.pallas_call(
        kern,
        grid_spec=grid_spec,
        out_shape=jax.ShapeDtypeStruct((db, t, MB_WIDTH), F32),
        compiler_params=_params(("arbitrary",), 40),
        name="moba_sample",
    )(page_table, rel_table, h3, h3, h3, *([cache_k] * n_pages), *([cache_v] * n_pages))


def _merge_kernel(oa_ref, ob_ref, ga0_ref, ga1_ref, gb0_ref, gb1_ref, x_ref,
                  wa_ref, wb_ref, wo_ref, gx_ref, wq_ref, x1_ref, xq_ref):
    ga = jnp.concatenate([ga0_ref[...], ga1_ref[...]], axis=-1)
    gb = jnp.concatenate([gb0_ref[...], gb1_ref[...]], axis=-1)
    pa = _dot(oa_ref[...].astype(BF16), wa_ref[...])
    pb = _dot(ob_ref[...].astype(BF16), wb_ref[...])
    merged = _sigmoid(ga) * pa + _sigmoid(gb) * pb
    x1 = x_ref[...] + _dot(merged.astype(BF16), wo_ref[...])
    x1_ref[...] = x1
    xq_ref[...] = _dot(_rmsnorm(x1, gx_ref[...]).astype(BF16), wq_ref[...])


def _merge(oa, ob, h2, x, wa, wb, wo, gx, wq, *, tm):
    n = x.shape[0]
    assert n % tm == 0
    half = D_MODEL // 2
    tok = lambda w, k=0: pl.BlockSpec((tm, w), lambda i: (i, k))
    full = lambda a: pl.BlockSpec(a.shape, lambda i: (0, 0))
    gx2 = gx.reshape(1, D_MODEL)
    return pl.pallas_call(
        _merge_kernel,
        grid=(n // tm,),
        in_specs=[tok(HG_WIDTH), tok(MB_WIDTH),
                  tok(half, COL_GA // half), tok(half, COL_GA // half + 1),
                  tok(half, COL_GB // half), tok(half, COL_GB // half + 1),
                  tok(D_MODEL), full(wa), full(wb), full(wo), full(gx2), full(wq)],
        out_specs=[tok(D_MODEL), tok(XA_WIDTH)],
        out_shape=[jax.ShapeDtypeStruct((n, D_MODEL), F32),
                   jax.ShapeDtypeStruct((n, XA_WIDTH), F32)],
        compiler_params=_params(("parallel",), 48),
        name="merge_mix",
    )(oa, ob, h2, h2, h2, h2, x, wa, wb, wo, gx2, wq)


def _mem_attn_kernel(q_ref, k_ref, v_ref, o_ref):
    scale = XA_DIM ** -0.5
    for hd in range(XA_HEADS):
        sl = slice(hd * XA_DIM, (hd + 1) * XA_DIM)
        s = _dot_nt(q_ref[0][:, sl].astype(BF16), k_ref[0][:, sl].astype(BF16)) * scale
        m = jnp.max(s, axis=-1, keepdims=True)
        p = jnp.exp(s - m)
        l = jnp.sum(p, axis=-1, keepdims=True)
        o_ref[0, :, sl] = _dot(p.astype(BF16), v_ref[0][:, sl].astype(BF16)) / l


def _mem_attn(xq3, mem_kv, *, tq):
    b, t, _ = xq3.shape
    assert t % tq == 0
    mem = lambda k: pl.BlockSpec((1, mem_kv.shape[1], XA_WIDTH), lambda i, j: (i, 0, k))
    qs = pl.BlockSpec((1, tq, XA_WIDTH), lambda i, j: (i, j, 0))
    return pl.pallas_call(
        _mem_attn_kernel,
        grid=(b, t // tq),
        in_specs=[qs, mem(0), mem(1)],
        out_specs=qs,
        out_shape=jax.ShapeDtypeStruct((b, t, XA_WIDTH), F32),
        compiler_params=_params(("parallel", "parallel"), 32),
        name="mem_attn",
    )(xq3, mem_kv, mem_kv)


def _mem_attn_rows_kernel(q_ref, k_ref, v_ref, o_ref):
    bb, t, _ = q_ref.shape
    rows = XA_HEADS * t
    scale = XA_DIM ** -0.5
    row_head = lax.broadcasted_iota(I32, (rows, 1), 0) // t
    kc = lax.broadcasted_iota(I32, (rows, k_ref.shape[1]), 1)
    same_head = (kc & (XA_HEADS - 1)) == row_head
    for i in range(bb):
        x = q_ref[i]
        q = jnp.concatenate([x[:, hd * XA_DIM:(hd + 1) * XA_DIM] for hd in range(XA_HEADS)], axis=0)
        s = _dot_nt(q.astype(BF16), k_ref[i].astype(BF16)) * scale
        s = jnp.where(same_head, s, NEG_INF)
        m = jnp.max(s, axis=-1, keepdims=True)
        p = jnp.exp(s - m)
        l = jnp.sum(p, axis=-1, keepdims=True)
        out = _dot(p.astype(BF16), v_ref[i].astype(BF16)) / l
        for hd in range(XA_HEADS):
            o_ref[i, :, hd * XA_DIM:(hd + 1) * XA_DIM] = out[hd * t:(hd + 1) * t]


def _mem_attn_rows(xq3, mem_k, mem_v, *, bb):
    b, t, _ = xq3.shape
    assert b % bb == 0 and XA_HEADS == 4 and t % 8 == 0
    mem = pl.BlockSpec((bb,) + mem_k.shape[1:], lambda i: (i, 0, 0))
    qs = pl.BlockSpec((bb, t, XA_WIDTH), lambda i: (i, 0, 0))
    return pl.pallas_call(
        _mem_attn_rows_kernel,
        grid=(b // bb,),
        in_specs=[qs, mem, mem],
        out_specs=qs,
        out_shape=jax.ShapeDtypeStruct((b, t, XA_WIDTH), F32),
        compiler_params=_params(("parallel",), 32),
        name="mem_attn_rows",
    )(xq3, mem_k, mem_v)


def _route(logits):
    lane = lax.broadcasted_iota(I32, logits.shape, 1).astype(F32)
    first = lambda hit: jnp.min(jnp.where(hit, lane, float(ROUTER_LANES)), axis=-1, keepdims=True)
    gl = jnp.where(lane < N_GROUPS, logits, NEG_INF)
    gmax = jnp.max(gl, axis=-1, keepdims=True)
    grp = first(gl == gmax)
    g_prob = 1.0 / jnp.sum(jnp.exp(gl - gmax), axis=-1, keepdims=True)
    e_lo = N_GROUPS + grp * EXPERTS_PER_GROUP
    in_grp = (lane >= e_lo) & (lane < e_lo + EXPERTS_PER_GROUP)
    el = jnp.where(in_grp, logits, NEG_INF)
    top1 = jnp.max(el, axis=-1, keepdims=True)
    idx1 = first(el == top1)
    el2 = jnp.where(lane == idx1, NEG_INF, el)
    top2 = jnp.max(el2, axis=-1, keepdims=True)
    idx2 = first(el2 == top2)
    e2 = jnp.exp(top2 - top1)
    w1 = g_prob / (1.0 + e2)
    w2 = w1 * e2
    comb = jnp.where(lane == idx1, w1, 0.0) + jnp.where(lane == idx2, w2, 0.0)
    return jnp.where(lane == GRP_LANE, grp, comb)


def _xo_route_kernel(x1_ref, at_ref, wxo_ref, gf_ref, wr_ref, br_ref, x2_ref, xf_ref, comb_ref):
    x2 = x1_ref[...] + _dot(at_ref[...].astype(BF16), wxo_ref[...])
    x2_ref[...] = x2
    xf = _rmsnorm(x2, gf_ref[...])
    xf_ref[...] = xf.astype(BF16)
    comb_ref[...] = _route(_dot_f32acc(xf, wr_ref[...]) + br_ref[...])


def _xo_route(x1, attn, wxo, g_ffn, w_router, b_router, *, tm):
    n = x1.shape[0]
    assert n % tm == 0
    tok = lambda w: pl.BlockSpec((tm, w), lambda i: (i, 0))
    full = lambda a: pl.BlockSpec(a.shape, lambda i: (0, 0))
    gf2 = g_ffn.reshape(1, D_MODEL)
    return pl.pallas_call(
        _xo_route_kernel,
        grid=(n // tm,),
        in_specs=[tok(D_MODEL), tok(XA_WIDTH), full(wxo), full(gf2), full(w_router), full(b_router)],
        out_specs=[tok(D_MODEL), tok(D_MODEL), tok(ROUTER_LANES)],
        out_shape=[jax.ShapeDtypeStruct((n, D_MODEL), F32), jax.ShapeDtypeStruct((n, D_MODEL), BF16),
                   jax.ShapeDtypeStruct((n, ROUTER_LANES), F32)],
        compiler_params=_params(("parallel",), 40),
        name="xo_route",
    )(x1, attn, wxo, gf2, w_router, b_router)


def _moe_kernel(xf_ref, comb_ref, x2_ref, wg_ref, wu_ref, wd_ref, gn_ref, y_ref,
                tri_ref, posc_ref, post_ref, cnt_ref, xg_ref, wq_ref, yg_ref, acc_ref, *, r):
    i = pl.program_id(0)
    step = pl.program_id(1)
    tm = xf_ref.shape[0]
    per_step = wg_ref.shape[0]
    steps_per_group = EXPERTS_PER_GROUP // per_step
    g = step // steps_per_group
    g_f = g.astype(F32)
    rp = -(-r // V7X_LANES) * V7X_LANES
    lane = lax.broadcasted_iota(I32, (tm, ROUTER_LANES), 1)

    @pl.when((i == 0) & (step == 0))
    def _():
        rr = lax.broadcasted_iota(I32, (tm, tm), 0)
        cc = lax.broadcasted_iota(I32, (tm, tm), 1)
        tri_ref[...] = jnp.where(cc < rr, 1.0, 0.0).astype(BF16)

    @pl.when(step == 0)
    def _():
        grp = comb_ref[:, GRP_LANE:GRP_LANE + 1]
        onehot = jnp.where((lane.astype(F32) == grp) & (lane < N_GROUPS), 1.0, 0.0)
        before = _dot(tri_ref[...], onehot.astype(BF16))
        pos = jnp.sum(onehot * before, axis=-1, keepdims=True)
        cnt_ref[...] = jnp.sum(onehot, axis=0, keepdims=True)
        posc_ref[...] = jnp.where(lane == 0, pos, jnp.where(lane == 1, grp, 0.0))
        hi = jnp.floor(pos * (1.0 / POS_SPLIT))
        lo = pos - hi * POS_SPLIT
        cols = jnp.where(lane == 0, hi, jnp.where(lane == 1, lo, jnp.where(lane == 2, grp, 0.0)))
        er = lax.broadcasted_iota(I32, (16, ROUTER_LANES), 0)
        ec = lax.broadcasted_iota(I32, (16, ROUTER_LANES), 1)
        post_ref[...] = _dot_nt(jnp.where(er == ec, 1.0, 0.0).astype(BF16), cols.astype(BF16))
        acc_ref[...] = jnp.zeros(acc_ref.shape, F32)

    lane_row = lax.broadcasted_iota(I32, (1, ROUTER_LANES), 1)
    count = jnp.sum(jnp.where(lane_row == g, cnt_ref[...], 0.0)).astype(I32)
    n_chunks = (count + (r - 1)) // r

    @pl.when(step % steps_per_group == 0)
    def _():
        pos_row = post_ref[0:1, :] * POS_SPLIT + post_ref[1:2, :]
        grp_row = post_ref[2:3, :]
        comb_hi, comb_lo = _split2(comb_ref[...])

        def gather(c, carry):
            rows = pl.ds(pl.multiple_of(c * r, 16), r)
            slot = (lax.broadcasted_iota(I32, (r, tm), 0) + c * r).astype(F32)
            pick = jnp.where((pos_row == slot) & (grp_row == g_f), 1.0, 0.0).astype(BF16)
            xg_ref[rows, :] = _dot(pick, xf_ref[...]).astype(BF16)
            wq_ref[rows, :] = _dot(pick, comb_hi) + _dot(pick, comb_lo)
            yg_ref[rows, :] = jnp.zeros((r, D_MODEL), F32)
            return carry

        lax.fori_loop(0, n_chunks, gather, 0)

    def experts(c, carry):
        rows = pl.ds(pl.multiple_of(c * r, 16), r)
        xg = xg_ref[rows, :]
        wq = wq_ref[rows, :]
        wl = lax.broadcasted_iota(I32, (r, ROUTER_LANES), 1)
        ups = [(_dot(xg, wg_ref[k]), _dot(xg, wu_ref[k])) for k in range(per_step)]
        hidden = []
        for k, (gate, up) in enumerate(ups):
            w_e = jnp.sum(jnp.where(wl == N_GROUPS + step * per_step + k, wq, 0.0), axis=-1, keepdims=True)
            hidden.append((_silu(gate) * up * w_e).astype(BF16))
        w_down = wd_ref[...].reshape(per_step * EXPERT_HIDDEN, D_MODEL)
        yg_ref[rows, :] += _dot(jnp.concatenate(hidden, axis=-1), w_down)
        return carry

    lax.fori_loop(0, n_chunks, experts, 0)

    @pl.when(step % steps_per_group == steps_per_group - 1)
    def _():
        pos_col = posc_ref[:, 0:1]
        grp_col = posc_ref[:, 1:2]

        def scatter(c, carry):
            rows = pl.ds(pl.multiple_of(c * r, 16), r)
            sl = lax.broadcasted_iota(I32, (tm, rp), 1)
            slot = (sl + c * r).astype(F32)
            place = jnp.where((pos_col == slot) & (grp_col == g_f) & (sl < r), 1.0, 0.0).astype(BF16)
            yg = yg_ref[rows, :].astype(BF16)
            if rp > r:
                yg = jnp.concatenate([yg, jnp.zeros((rp - r, D_MODEL), BF16)], axis=0)
            acc_ref[...] += _dot(place, yg)
            return carry

        lax.fori_loop(0, n_chunks, scatter, 0)

    @pl.when(step == pl.num_programs(1) - 1)
    def _():
        y_ref[...] = _rmsnorm(x2_ref[...] + acc_ref[...], gn_ref[...])


def _moe(xf, comb, x2, wg, wu, wd, g_final, *, tm):
    n = xf.shape[0]
    assert n % tm == 0 and tm % 16 == 0
    r = min(MOE_CHUNK_ROWS, tm)
    cap = -(-tm // r) * r
    per_step = MOE_EXPERTS_PER_STEP
    tok = lambda w: pl.BlockSpec((tm, w), lambda i, s: (i, 0))
    gn2 = g_final.reshape(1, D_MODEL)
    kern = functools.partial(_moe_kernel, r=r)
    vmem = (2 * tm * D_MODEL * (2 + 4 + 4) + 2 * tm * ROUTER_LANES * 4
            + 2 * 3 * per_step * D_MODEL * EXPERT_HIDDEN * 2 + tm * tm * 2
            + cap * D_MODEL * (2 + 4) + tm * D_MODEL * 4) // MIB + 8
    return pl.pallas_call(
        kern,
        grid=(n // tm, N_EXPERTS // per_step),
        in_specs=[tok(D_MODEL), tok(ROUTER_LANES), tok(D_MODEL),
                  pl.BlockSpec((per_step, D_MODEL, EXPERT_HIDDEN), lambda i, s: (s, 0, 0)),
                  pl.BlockSpec((per_step, D_MODEL, EXPERT_HIDDEN), lambda i, s: (s, 0, 0)),
                  pl.BlockSpec((per_step, EXPERT_HIDDEN, D_MODEL), lambda i, s: (s, 0, 0)),
                  pl.BlockSpec((1, D_MODEL), lambda i, s: (0, 0))],
        out_specs=tok(D_MODEL),
        out_shape=jax.ShapeDtypeStruct((n, D_MODEL), F32),
        scratch_shapes=[pltpu.VMEM((tm, tm), BF16),
                        pltpu.VMEM((tm, ROUTER_LANES), F32), pltpu.VMEM((16, tm), F32),
                        pltpu.VMEM((1, ROUTER_LANES), F32),
                        pltpu.VMEM((cap, D_MODEL), BF16), pltpu.VMEM((cap, ROUTER_LANES), F32),
                        pltpu.VMEM((cap, D_MODEL), F32), pltpu.VMEM((tm, D_MODEL), F32)],
        compiler_params=_params(("arbitrary", "arbitrary"), vmem),
        name="moe_grouped",
    )(xf, comb, x2, wg, wu, wd, gn2)


def _token_tile(n, cap=512):
    return cap if n % cap == 0 else n


def _col_tile(n, cap=1536):
    return max(c for c in range(V7X_LANES, cap + 1, V7X_LANES) if n % c == 0)


def _layer(x3, w, s0, moba_fn, mem_fn):
    b, t, _ = x3.shape
    n = b * t
    x2d = x3.reshape(n, D_MODEL)
    tm = _token_tile(n)
    h2, k4, v4 = _norm_matmul(x2d, w["norm_mix"], w["w_in"], tm=_token_tile(n, 1024),
                              tn=_col_tile(IN_COLS), head_cols=(COL_MK, COL_MV), heads=MB_HEADS,
                              head_dim=MB_DIM)
    h3 = h2.reshape(b, t, IN_COLS)
    oa, s_new = _hgrn(h3, w["hg_lb_logits"], w["hg_norm"], s0)
    ob = moba_fn(h3)
    x1, xq = _merge(oa.reshape(n, HG_WIDTH), ob.reshape(n, MB_WIDTH), h2, x2d,
                    w["w_branch_a"], w["w_branch_b"], w["w_mix_out"], w["norm_xattn"], w["w_xq"], tm=tm)
    attn = mem_fn(xq.reshape(b, t, XA_WIDTH))
    x2, xf, comb = _xo_route(x1, attn.reshape(n, XA_WIDTH), w["w_xo"], w["norm_ffn"],
                             w["w_router"], w["b_router"], tm=tm)
    y = _moe(xf, comb, x2, w["w_expert_gate"], w["w_expert_up"], w["w_expert_down"],
             w["norm_final"], tm=_token_tile(n, 1024))
    k_new = k4.reshape(b, t, MB_HEADS, MB_DIM)
    v_new = v4.reshape(b, t, MB_HEADS, MB_DIM)
    return y.reshape(b, t, D_MODEL), k_new, v_new, s_new


def kernel(x_prompt, x_sample, cache_k, cache_v, state_hgrn, cache_mem_k, cache_mem_v, page_table, mem_prompt, norm_mix, w_in, hg_lb_logits, hg_norm, w_branch_a, w_branch_b, w_mix_out, rel_table, norm_xattn, norm_mem, w_xq, w_xk, w_xv, w_xo, norm_ffn, w_group_router, b_group_router, w_expert_router, b_expert_router, w_expert_gate, w_expert_up, w_expert_down, norm_final):
    assert w_in.shape[0] == DEPTH == 1 and hg_lb_logits.shape[0] == DEPTH + 1
    b = x_prompt.shape[0]
    db = x_sample.shape[0]
    n_pool = cache_k.shape[1]
    pad_lanes = ROUTER_LANES - N_GROUPS - N_EXPERTS
    w = {
        "norm_mix": norm_mix[0], "w_in": w_in[0].astype(BF16),
        "hg_lb_logits": hg_lb_logits, "hg_norm": hg_norm[0],
        "w_branch_a": w_branch_a[0].astype(BF16), "w_branch_b": w_branch_b[0].astype(BF16),
        "w_mix_out": w_mix_out[0].astype(BF16), "norm_xattn": norm_xattn[0],
        "w_xq": w_xq[0].astype(BF16), "w_xo": w_xo[0].astype(BF16), "norm_ffn": norm_ffn[0],
        "w_router": jnp.pad(jnp.concatenate([w_group_router[0], w_expert_router[0]], axis=1),
                            ((0, 0), (0, pad_lanes))),
        "b_router": jnp.pad(jnp.concatenate([b_group_router[0], b_expert_router[0]]),
                            (0, pad_lanes)).reshape(1, ROUTER_LANES),
        "w_expert_gate": w_expert_gate[0].reshape(N_EXPERTS, D_MODEL, EXPERT_HIDDEN).astype(BF16),
        "w_expert_up": w_expert_up[0].reshape(N_EXPERTS, D_MODEL, EXPERT_HIDDEN).astype(BF16),
        "w_expert_down": w_expert_down[0].reshape(N_EXPERTS, EXPERT_HIDDEN, D_MODEL).astype(BF16),
        "norm_final": norm_final,
    }

    w_mem = jnp.concatenate([w_xk[0], w_xv[0]], axis=1).astype(BF16)
    mem_kv, mk_p, mv_p = _norm_matmul(mem_prompt.reshape(b * MEM_LEN, D_MODEL), norm_mem[0], w_mem,
                                      tm=_token_tile(b * MEM_LEN), tn=512, head_cols=(0, XA_WIDTH),
                                      heads=XA_HEADS, head_dim=XA_DIM)
    mem_kv = mem_kv.reshape(b, MEM_LEN, 2 * XA_WIDTH)
    s0 = jnp.zeros((b, HG_HEADS, HG_DIM, HG_DIM), F32)
    y_p, k_p, v_p, s_p = _layer(x_prompt, w, s0,
                                functools.partial(_moba_prompt, rel_table=rel_table),
                                functools.partial(_mem_attn, mem_kv=mem_kv, tq=512))

    ck = cache_k[0].reshape(n_pool, PAGE_SIZE * MB_HEADS, MB_DIM)
    cv = cache_v[0].reshape(n_pool, PAGE_SIZE * MB_HEADS, MB_DIM)
    moba_s = functools.partial(_moba_sample, cache_k=ck, cache_v=cv, page_table=page_table,
                               rel_table=rel_table)
    mem_s = functools.partial(_mem_attn_rows, bb=4,
                              mem_k=cache_mem_k[0].reshape(db, MEM_LEN * XA_HEADS, XA_DIM),
                              mem_v=cache_mem_v[0].reshape(db, MEM_LEN * XA_HEADS, XA_DIM))
    y_s, k_s, v_s, s_s = _layer(x_sample, w, state_hgrn[0], moba_s, mem_s)

    heads = lambda a: a.reshape(b, MEM_LEN, XA_HEADS, XA_DIM)[None]
    return (y_p, y_s, k_p[None], v_p[None], s_p[None], heads(mk_p), heads(mv_p),
            k_s[None], v_s[None], s_s[None])
```

```python
import functools
import math

import numpy as np
import jax
import jax.numpy as jnp
from jax import lax
from jax.experimental import pallas as pl
from jax.experimental.pallas import tpu as pltpu

F32 = jnp.float32
BF16 = jnp.bfloat16
I32 = jnp.int32

D_MODEL = 1024
DEPTH = 1
PAGE_SIZE = 128
HG_HEADS = 4
HG_DIM = 128
HG_WIDTH = HG_HEADS * HG_DIM
MB_HEADS = 4
MB_DIM = 128
MB_WIDTH = MB_HEADS * MB_DIM
MB_BLOCK = 256
MB_TOPK = 3
REL_BUCKETS = 32
REL_MAX_DIST = 128
REL_MAX_EXACT = REL_BUCKETS // 2
MEM_LEN = 256
XA_HEADS = 4
XA_DIM = 128
XA_WIDTH = XA_HEADS * XA_DIM
N_GROUPS = 4
EXPERTS_PER_GROUP = 8
N_EXPERTS = N_GROUPS * EXPERTS_PER_GROUP
EXPERT_TOPK = 2
EXPERT_HIDDEN = 256
NORM_EPS = 1e-6
IN_COLS = 4 * HG_WIDTH + 3 * MB_WIDTH + 2 * D_MODEL
COL_MQ = 4 * HG_WIDTH
COL_MK = COL_MQ + MB_WIDTH
COL_MV = COL_MK + MB_WIDTH
COL_GA = COL_MV + MB_WIDTH
COL_GB = COL_GA + D_MODEL

V7X_LANES = 128
V7X_VMEM_BYTES = 64 * 1024 * 1024
MIB = 1024 * 1024

NEG_INF = float("-inf")
MASK_BIG = -1e30
GATE_PAD = 16
ROUTER_LANES = 128
GRP_LANE = 0
EXPERT_LANE0 = 8
POS_SPLIT = 32.0
MOE_CHUNK_ROWS = 320
MOE_EXPERTS_PER_STEP = 4
HGRN_PROMPT_CHUNK = 128
HGRN_PROMPT_ROWS = 4
HGRN_MIN_CHUNK = 16
HGRN_CHAINS = 16


def _params(semantics, vmem_mib):
    return pltpu.CompilerParams(dimension_semantics=semantics,
                                vmem_limit_bytes=min(vmem_mib * MIB, V7X_VMEM_BYTES - 8 * MIB))


def _dot(a, b):
    return jnp.dot(a, b, preferred_element_type=F32)


def _dot_nt(a, b):
    return lax.dot_general(a, b, (((1,), (1,)), ((), ())), preferred_element_type=F32)


def _dot_tn(a, b):
    return lax.dot_general(a, b, (((0,), (0,)), ((), ())), preferred_element_type=F32)


def _split2(a):
    hi = a.astype(BF16)
    lo = (a - hi.astype(F32)).astype(BF16)
    return hi, lo


def _dot_nt_f32acc(a, b):
    ah, al = _split2(a)
    bh, bl = _split2(b)
    return _dot_nt(ah, bh) + (_dot_nt(ah, bl) + _dot_nt(al, bh))


def _dot_f32acc(a, b):
    ah, al = _split2(a)
    bh, bl = _split2(b)
    return _dot(ah, bh) + (_dot(ah, bl) + _dot(al, bh))


def _rmsnorm(x, g):
    return x * lax.rsqrt(jnp.mean(x * x, axis=-1, keepdims=True) + NORM_EPS) * g


def _sigmoid(x):
    return 1.0 / (1.0 + jnp.exp(-x))


def _silu(x):
    return x * _sigmoid(x)


def _norm_matmul_kernel(x_ref, g_ref, w_ref, o_ref, *refs, head_cols, heads, head_dim):
    head_refs, xn_ref = refs[:-1], refs[-1]
    j = pl.program_id(1)
    tm, tn = o_ref.shape

    @pl.when(j == 0)
    def _():
        xn_ref[...] = _rmsnorm(x_ref[...], g_ref[...]).astype(BF16)

    res = _dot(xn_ref[...], w_ref[...])
    o_ref[...] = res
    for col0, href in zip(head_cols, head_refs):
        for hd in range(heads):
            tile, local = divmod(col0 + hd * head_dim, tn)

            @pl.when(j == tile)
            def _(hd=hd, local=local, href=href):
                href[pl.ds(hd, tm, stride=heads), :] = res[:, local:local + head_dim]


def _norm_matmul(x, g, w_bf16, *, tm, tn, head_cols=(), heads=1, head_dim=V7X_LANES):
    m, d = x.shape
    n = w_bf16.shape[1]
    assert m % tm == 0 and n % tn == 0 and tn % head_dim == 0
    vmem = (2 * tm * d * 4 + tm * d * 2 + 2 * d * tn * 2 + 2 * tm * tn * 4
            + len(head_cols) * 2 * tm * heads * head_dim * 4) // MIB + 12
    kern = functools.partial(_norm_matmul_kernel, head_cols=tuple(head_cols), heads=heads,
                             head_dim=head_dim)
    head_spec = pl.BlockSpec((tm * heads, head_dim), lambda i, j: (i, 0))
    head_shape = jax.ShapeDtypeStruct((m * heads, head_dim), F32)
    return pl.pallas_call(
        kern,
        grid=(m // tm, n // tn),
        in_specs=[pl.BlockSpec((tm, d), lambda i, j: (i, 0)),
                  pl.BlockSpec((1, d), lambda i, j: (0, 0)),
                  pl.BlockSpec((d, tn), lambda i, j: (0, j))],
        out_specs=[pl.BlockSpec((tm, tn), lambda i, j: (i, j))] + [head_spec] * len(head_cols),
        out_shape=[jax.ShapeDtypeStruct((m, n), F32)] + [head_shape] * len(head_cols),
        scratch_shapes=[pltpu.VMEM((tm, d), BF16)],
        compiler_params=_params(("parallel", "arbitrary"), vmem),
        name="norm_matmul",
    )(x, g.reshape(1, d), w_bf16)


def _hgrn_levels(c):
    return int(round(math.log2(c)))


def _hgrn_sum_masks(c):
    t = np.arange(c)[:, None]
    j = np.arange(c)[None, :]
    rows = [j <= t, j > t]
    for lv in range(_hgrn_levels(c)):
        half = c >> (lv + 1)
        blk = 2 * half
        mid = (t // blk) * blk + half - 1
        upper = (t % blk) >= half
        rows.append((upper & (j > mid) & (j <= t)) | ((~upper) & (j > t) & (j <= mid)))
    return np.concatenate(rows, axis=0).astype(np.float32)


def _hgrn_kernel(hq_ref, hf_ref, hi_ref, hg_ref, lbl_ref, gn_ref, mask_ref, s0_ref,
                 o_ref, s_ref, st_ref, *, c, t_blk, bb):
    j = pl.program_id(1)
    chains = [(bi, hd) for bi in range(bb) for hd in range(HG_HEADS)]

    @pl.when(j == 0)
    def _():
        for bi, hd in chains:
            st_ref[bi * HG_HEADS + hd] = s0_ref[bi, hd].T

    lbl = lbl_ref[...]
    lmax = jnp.max(lbl, axis=0, keepdims=True)
    lexp = jnp.exp(lbl - lmax)
    lb_all = lexp[0:1, :] / jnp.sum(lexp, axis=0, keepdims=True)

    masks = mask_ref[...]
    pad = c - t_blk
    row = lax.broadcasted_iota(I32, (c, HG_DIM), 0)
    rr = lax.broadcasted_iota(I32, (c, c), 0)
    cc = lax.broadcasted_iota(I32, (c, c), 1)

    def padded(a):
        if pad == 0:
            return a
        return jnp.concatenate([a, jnp.zeros((pad, HG_DIM), F32)], axis=0)

    hsl = lambda hd: slice(hd * HG_DIM, (hd + 1) * HG_DIM)
    qs, ks, vs, logf_parts = [], [], [], []
    for bi, hd in chains:
        xq = hq_ref[bi][:, hsl(hd)]
        lb = lb_all[:, hsl(hd)]
        f = lb + (1.0 - lb) * _sigmoid(hf_ref[bi][:, hsl(hd)])
        qs.append(padded(_silu(xq)))
        ks.append(padded(1.0 - f))
        vs.append(padded(hi_ref[bi][:, hsl(hd)]))
        logf_parts.append(_split2(padded(jnp.log(f))))
    exs = []
    for n in range(0, len(chains), 2):
        (hi_a, lo_a), (hi_b, lo_b) = logf_parts[n], logf_parts[n + 1]
        stacked = jnp.concatenate([jnp.concatenate([hi_a, hi_b], axis=-1),
                                   jnp.concatenate([lo_a, lo_b], axis=-1)], axis=0)
        ex = jnp.exp(_dot(masks, stacked))
        exs += [ex[:, :HG_DIM], ex[:, HG_DIM:]]

    level_dots = []
    for q, kk, ex in zip(qs, ks, exs):
        per_level = []
        for lv in range(_hgrn_levels(c)):
            half = c >> (lv + 1)
            upper = (row & half) != 0
            ex_lv = ex[(2 + lv) * c:(3 + lv) * c]
            a = jnp.where(upper, q * ex_lv, 0.0).astype(BF16)
            bm = jnp.where(upper, 0.0, kk * ex_lv).astype(BF16)
            per_level.append(_dot_nt(a, bm))
        level_dots.append(per_level)

    outs = []
    for n, (bi, hd) in enumerate(chains):
        q, kk, v, ex = qs[n], ks[n], vs[n], exs[n]
        scores = jnp.zeros((c, c), F32)
        for lv, d in enumerate(level_dots[n]):
            scores = scores + jnp.where((rr ^ cc) < 2 * (c >> (lv + 1)), d, 0.0)
        eb = ex[0:c]
        eb_rev = ex[c:2 * c]
        diag = jnp.sum(q * kk, axis=-1, keepdims=True)
        st = st_ref[bi * HG_HEADS + hd]
        vb = v.astype(BF16)
        o = (_dot_nt((q * eb).astype(BF16), st.astype(BF16))
             + _dot(scores.astype(BF16), vb) + diag * v)
        st_ref[bi * HG_HEADS + hd] = st * eb[c - 1:c, :] + _dot_tn(vb, (kk * eb_rev).astype(BF16))
        outs.append(o[0:t_blk])

    for (bi, hd), o in zip(chains, outs):
        xg = hg_ref[bi][:, hsl(hd)]
        o_ref[bi, :, hsl(hd)] = _rmsnorm(o, gn_ref[:, hsl(hd)]) * _silu(xg)

    @pl.when(j == pl.num_programs(1) - 1)
    def _():
        for bi, hd in chains:
            s_ref[bi, hd] = st_ref[bi * HG_HEADS + hd].T


def _hgrn(h3, lb_logits, hg_norm, s0):
    b, t, _ = h3.shape
    t_blk = math.gcd(t, HGRN_PROMPT_CHUNK)
    c = max(t_blk, HGRN_MIN_CHUNK)
    bb = math.gcd(b, HGRN_CHAINS // HG_HEADS) if t == t_blk else math.gcd(b, HGRN_PROMPT_ROWS)
    m01 = _hgrn_sum_masks(c)
    masks = jnp.asarray(np.concatenate([m01, m01], axis=1), BF16)
    col = lambda k: pl.BlockSpec((bb, t_blk, HG_WIDTH), lambda i, j: (i, j, k))
    state_spec = pl.BlockSpec((bb, HG_HEADS, HG_DIM, HG_DIM), lambda i, j: (i, 0, 0, 0))
    kern = functools.partial(_hgrn_kernel, c=c, t_blk=t_blk, bb=bb)
    return pl.pallas_call(
        kern,
        grid=(b // bb, t // t_blk),
        in_specs=[col(0), col(1), col(2), col(3),
                  pl.BlockSpec(lb_logits.shape, lambda i, j: (0, 0)),
                  pl.BlockSpec((1, HG_WIDTH), lambda i, j: (0, 0)),
                  pl.BlockSpec(masks.shape, lambda i, j: (0, 0)),
                  state_spec],
        out_specs=[pl.BlockSpec((bb, t_blk, HG_WIDTH), lambda i, j: (i, j, 0)), state_spec],
        out_shape=[jax.ShapeDtypeStruct((b, t, HG_WIDTH), F32),
                   jax.ShapeDtypeStruct(s0.shape, F32)],
        scratch_shapes=[pltpu.VMEM((bb * HG_HEADS, HG_DIM, HG_DIM), F32)],
        compiler_params=_params(("parallel", "arbitrary"), 32),
        name="hgrn2",
    )(h3, h3, h3, h3, lb_logits, hg_norm.reshape(1, HG_WIDTH), masks, s0)


def _rel_bias(rel_ref, head, dist):
    d = jnp.maximum(dist, 0)
    df = jnp.maximum(d, 1).astype(F32)
    large = REL_MAX_EXACT + (jnp.log(df / REL_MAX_EXACT) / math.log(REL_MAX_DIST / REL_MAX_EXACT)
                             * (REL_BUCKETS - REL_MAX_EXACT)).astype(I32)
    large = jnp.minimum(large, REL_BUCKETS - 1)
    bucket = jnp.where(d < REL_MAX_EXACT, d, large)
    out = jnp.zeros(dist.shape, F32)
    for kb in range(REL_BUCKETS):
        out = jnp.where(bucket == kb, rel_ref[kb, head], out)
    return out


def _stack_rows(rows):
    ridx = lax.broadcasted_iota(I32, (GATE_PAD, MB_DIM), 0)
    out = jnp.zeros((GATE_PAD, MB_DIM), F32)
    for n, r in enumerate(rows):
        out = jnp.where(ridx == n, r, out)
    return out


def _topk_blocks(gate, n_cand, n_past):
    lane = lax.broadcasted_iota(I32, gate.shape, 1)
    past = lane < n_past
    g = jnp.where(past, gate, NEG_INF)
    rank = jnp.zeros(gate.shape, I32)
    for m in range(n_cand):
        gm = g[:, m:m + 1]
        ahead = jnp.where(gm > g, 1, jnp.where((gm == g) & (lane > m), 1, 0))
        rank = rank + ahead
    return jnp.where(past & (rank < MB_TOPK), 1.0, 0.0)


def _topk_blocks_t(gate_t, n_cand, n_past):
    blk_id = lax.broadcasted_iota(I32, gate_t.shape, 0)
    past = blk_id < n_past
    g = jnp.where(past, gate_t, NEG_INF)
    rank = jnp.zeros(gate_t.shape, I32)
    for m in range(n_cand):
        gm = g[m:m + 1, :]
        rank = rank + jnp.where(gm > g, 1, jnp.where((gm == g) & (blk_id > m), 1, 0))
    return past & (rank < MB_TOPK)


def _moba_prompt_kernel(rel_ref, q_ref, k_ref, v_ref, o_ref,
                        kx_ref, vx_ref, km_ref, bias_ref, eye_ref, s_ref, mx_ref, mb_ref, acc_ref, *, n_blk):
    b = pl.program_id(0)
    i = pl.program_id(1)
    blk = MB_BLOCK
    scale = MB_DIM ** -0.5
    heads = range(MB_HEADS)
    hsl = lambda hd: slice(hd * MB_DIM, (hd + 1) * MB_DIM)

    t = k_ref.shape[1]
    own_slot = n_blk - 1

    @pl.when((b == 0) & (i == 0))
    def _():
        r = lax.broadcasted_iota(I32, (blk, blk), 0)
        c = lax.broadcasted_iota(I32, (blk, blk), 1)
        eye_ref[...] = jnp.where(r == c, 1.0, 0.0).astype(BF16)
        kr = lax.broadcasted_iota(I32, (t, MB_DIM), 0)
        kc = lax.broadcasted_iota(I32, (t, MB_DIM), 1)
        blk_onehot = jnp.where(kr // blk == kc, 1.0, 0.0).astype(BF16)
        for hd in heads:
            bias_ref[hd, 0] = jnp.where(c <= r, _rel_bias(rel_ref, hd, r - c), NEG_INF)
            bias_ref[hd, 1] = _rel_bias(rel_ref, hd, r - c + blk)
            kx_ref[hd, :, MB_DIM:] = blk_onehot
            vx_ref[hd, :, MB_DIM:] = jnp.ones((t, MB_DIM), BF16)

    @pl.when(i == 0)
    def _():
        means = [jnp.sum(k_ref[0, n * blk:(n + 1) * blk, :], axis=0, keepdims=True) * (1.0 / blk)
                 for n in range(n_blk)]
        for hd in heads:
            kx_ref[hd, :, :MB_DIM] = k_ref[0][:, hsl(hd)].astype(BF16)
            vx_ref[hd, :, :MB_DIM] = v_ref[0][:, hsl(hd)].astype(BF16)
            km_ref[hd] = _stack_rows([mn[:, hsl(hd)] for mn in means])

    q = q_ref[0]
    blk_id = lax.broadcasted_iota(I32, (GATE_PAD, blk), 0)
    qh = [q[:, hsl(hd)] for hd in heads]
    gates = [_dot_nt_f32acc(km_ref[hd], qh[hd]) for hd in heads]
    qm_t = []
    for hd in heads:
        sel_t = _topk_blocks_t(gates[hd], n_blk, i)
        m_t = jnp.where(sel_t | (blk_id >= i), 0.0, MASK_BIG).astype(BF16)
        qm_t.append(jnp.concatenate([m_t, jnp.zeros((MB_DIM - GATE_PAD, blk), BF16)], axis=0))
    qmask = [_dot_nt(eye_ref[...], qm_t[hd]) for hd in heads]
    qx = [jnp.concatenate([qh[hd].astype(BF16), qmask[hd].astype(BF16)], axis=-1) for hd in heads]

    own0 = pl.multiple_of(i * blk, blk)
    dots = [_dot_nt(qx[hd], kx_ref[hd, pl.ds(own0, blk), :]) for hd in heads]
    for hd in heads:
        s = dots[hd] * scale + bias_ref[hd, 0]
        s_ref[hd, own_slot] = s
        mx_ref[hd] = jnp.maximum(s[:, :MB_DIM], s[:, MB_DIM:])

    for n in range(n_blk - 1):
        @pl.when(n < i)
        def _(n=n):
            dots = [_dot_nt(qx[hd], kx_ref[hd, n * blk:(n + 1) * blk, :]) for hd in heads]
            for hd in heads:
                far_bias = rel_ref[REL_BUCKETS - 1, hd]
                s = dots[hd] * scale + jnp.where(n == i - 1, bias_ref[hd, 1], far_bias)
                s_ref[hd, n] = s
                mx_ref[hd] = jnp.maximum(mx_ref[hd], jnp.maximum(s[:, :MB_DIM], s[:, MB_DIM:]))

    row_max = [jnp.max(mx_ref[hd], axis=-1, keepdims=True) for hd in heads]
    for hd in heads:
        mb_ref[hd] = jnp.broadcast_to(row_max[hd], (blk, 2 * MB_DIM))
    probs = [jnp.exp(s_ref[hd, own_slot] - mb_ref[hd]).astype(BF16) for hd in heads]
    for hd in heads:
        acc_ref[hd] = _dot(probs[hd], vx_ref[hd, pl.ds(own0, blk), :])

    for n in range(n_blk - 1):
        @pl.when(n < i)
        def _(n=n):
            probs = [jnp.exp(s_ref[hd, n] - mb_ref[hd]).astype(BF16) for hd in heads]
            pv = [_dot(probs[hd], vx_ref[hd, n * blk:(n + 1) * blk, :]) for hd in heads]
            for hd in heads:
                acc_ref[hd] += pv[hd]

    for hd in heads:
        acc = acc_ref[hd]
        o_ref[0, :, hsl(hd)] = acc[:, :MB_DIM] / acc[:, MB_DIM:]


def _moba_prompt(h3, rel_table):
    b, t, _ = h3.shape
    assert t % MB_BLOCK == 0
    n_blk = t // MB_BLOCK
    assert n_blk <= GATE_PAD
    qcol, kcol, vcol = COL_MQ // MB_WIDTH, COL_MK // MB_WIDTH, COL_MV // MB_WIDTH
    kern = functools.partial(_moba_prompt_kernel, n_blk=n_blk)
    wide = 2 * MB_DIM
    vmem = (2 * 2 * t * MB_WIDTH * 4 + 2 * MB_HEADS * t * wide * 2
            + MB_HEADS * (n_blk + 4) * MB_BLOCK * wide * 4) // MIB + 12
    return pl.pallas_call(
        kern,
        grid=(b, n_blk),
        in_specs=[pl.BlockSpec(memory_space=pltpu.SMEM),
                  pl.BlockSpec((1, MB_BLOCK, MB_WIDTH), lambda bb, i: (bb, i, qcol)),
                  pl.BlockSpec((1, t, MB_WIDTH), lambda bb, i: (bb, 0, kcol)),
                  pl.BlockSpec((1, t, MB_WIDTH), lambda bb, i: (bb, 0, vcol))],
        out_specs=pl.BlockSpec((1, MB_BLOCK, MB_WIDTH), lambda bb, i: (bb, i, 0)),
        out_shape=jax.ShapeDtypeStruct((b, t, MB_WIDTH), F32),
        scratch_shapes=[pltpu.VMEM((MB_HEADS, t, wide), BF16), pltpu.VMEM((MB_HEADS, t, wide), BF16),
                        pltpu.VMEM((MB_HEADS, GATE_PAD, MB_DIM), F32),
                        pltpu.VMEM((MB_HEADS, 2, MB_BLOCK, MB_BLOCK), F32),
                        pltpu.VMEM((MB_BLOCK, MB_BLOCK), BF16),
                        pltpu.VMEM((MB_HEADS, n_blk, MB_BLOCK, MB_BLOCK), F32),
                        pltpu.VMEM((MB_HEADS, MB_BLOCK, MB_DIM), F32),
                        pltpu.VMEM((MB_HEADS, MB_BLOCK, wide), F32),
                        pltpu.VMEM((MB_HEADS, MB_BLOCK, wide), F32)],
        compiler_params=_params(("arbitrary", "arbitrary"), vmem),
        name="moba_prompt",
    )(rel_table, h3, h3, h3)


def _moba_sample_kernel(pt_ref, rel_ref, q_ref, kn_ref, vn_ref, *refs, n_pages, past_len):
    del pt_ref
    kp = refs[:n_pages]
    vp = refs[n_pages:2 * n_pages]
    o_ref, bias_ref, bias_own_ref = refs[2 * n_pages:]
    t = q_ref.shape[1]
    rows = MB_HEADS * t
    page_rows = PAGE_SIZE * MB_HEADS
    pages_per_blk = MB_BLOCK // PAGE_SIZE
    n_past = past_len // MB_BLOCK
    scale = MB_DIM ** -0.5

    def head_rows(ref):
        x = ref[0]
        return jnp.concatenate([x[:, hd * MB_DIM:(hd + 1) * MB_DIM] for hd in range(MB_HEADS)], axis=0)

    @pl.when(pl.program_id(0) == 0)
    def _():
        tq = lax.broadcasted_iota(I32, (t, page_rows), 0)
        kc = lax.broadcasted_iota(I32, (t, page_rows), 1)
        for p in range(n_pages):
            dist = past_len + tq - (p * PAGE_SIZE + (kc >> 2))
            bias_ref[p] = jnp.concatenate(
                [jnp.where((kc & (MB_HEADS - 1)) == hd, _rel_bias(rel_ref, hd, dist), NEG_INF)
                 for hd in range(MB_HEADS)], axis=0)
        ro = lax.broadcasted_iota(I32, (t, rows), 0)
        co = lax.broadcasted_iota(I32, (t, rows), 1)
        for hd in range(MB_HEADS):
            own = _rel_bias(rel_ref, hd, ro - (co - hd * t))
            keep = (co >= hd * t) & (co <= hd * t + ro)
            bias_own_ref[hd * t:(hd + 1) * t, :] = jnp.where(keep, own, NEG_INF)

    q = head_rows(q_ref)
    qb = q.astype(BF16)
    row_head = lax.broadcasted_iota(I32, (rows, 1), 0) // t

    groups = page_rows // 8
    sums = [jnp.sum(kp[p][0].reshape(groups, 8, MB_DIM), axis=0) for p in range(n_pages)]
    blk_sums = []
    for n in range(n_past):
        acc = sums[n * pages_per_blk]
        for pp in range(1, pages_per_blk):
            acc = acc + sums[n * pages_per_blk + pp]
        blk_sums.append(acc)
    g_full = _dot_nt_f32acc(q, jnp.concatenate(blk_sums, axis=0))
    gc = lax.broadcasted_iota(I32, g_full.shape, 1)
    g_full = jnp.where((gc & (MB_HEADS - 1)) == row_head, g_full, 0.0)
    lane = lax.broadcasted_iota(I32, (rows, GATE_PAD), 1)
    gate = jnp.zeros((rows, GATE_PAD), F32)
    for n in range(n_past):
        g_n = jnp.sum(g_full[:, n * 8:(n + 1) * 8], axis=-1, keepdims=True) * (1.0 / MB_BLOCK)
        gate = jnp.where(lane == n, g_n, gate)
    sel = _topk_blocks(gate, n_past, n_past)

    s_own = _dot_nt(q, head_rows(kn_ref)) * scale + bias_own_ref[...]
    m = jnp.max(s_own, axis=-1, keepdims=True)
    s_past = []
    for p in range(n_pages):
        n = p // pages_per_blk
        s = _dot_nt(qb, kp[p][0].astype(BF16)) * scale + bias_ref[p]
        s = jnp.where(sel[:, n:n + 1] > 0.0, s, NEG_INF)
        s_past.append(s)
        m = jnp.maximum(m, jnp.max(s, axis=-1, keepdims=True))
    p_own = jnp.exp(s_own - m)
    l = jnp.sum(p_own, axis=-1, keepdims=True)
    out = _dot(p_own, head_rows(vn_ref))
    for p in range(n_pages):
        pr = jnp.exp(s_past[p] - m)
        l = l + jnp.sum(pr, axis=-1, keepdims=True)
        out = out + _dot(pr.astype(BF16), vp[p][0].astype(BF16))
    out = out / l
    for hd in range(MB_HEADS):
        o_ref[0, :, hd * MB_DIM:(hd + 1) * MB_DIM] = out[hd * t:(hd + 1) * t]


def _moba_sample(h3, cache_k, cache_v, page_table, rel_table):
    db, t, _ = h3.shape
    n_pages = page_table.shape[1]
    past_len = n_pages * PAGE_SIZE
    assert past_len % MB_BLOCK == 0 and t <= MB_BLOCK and past_len // MB_BLOCK < GATE_PAD
    assert MB_HEADS == 4 and t % 8 == 0
    qcol, kcol, vcol = COL_MQ // MB_WIDTH, COL_MK // MB_WIDTH, COL_MV // MB_WIDTH
    page_rows = PAGE_SIZE * MB_HEADS
    new = lambda k: pl.BlockSpec((1, t, MB_WIDTH), lambda i, pt: (i, 0, k))
    page = lambda p: pl.BlockSpec((1, page_rows, MB_DIM), lambda i, pt: (pt[i, p], 0, 0))
    kern = functools.partial(_moba_sample_kernel, n_pages=n_pages, past_len=past_len)
    grid_spec = pltpu.PrefetchScalarGridSpec(
        num_scalar_prefetch=1,
        grid=(db,),
        in_specs=([pl.BlockSpec(memory_space=pltpu.SMEM), new(qcol), new(kcol), new(vcol)]
                  + [page(p) for p in range(n_pages)] * 2),
        out_specs=pl.BlockSpec((1, t, MB_WIDTH), lambda i, pt: (i, 0, 0)),
        scratch_shapes=[pltpu.VMEM((n_pages, MB_HEADS * t, page_rows), F32),
                        pltpu.VMEM((MB_HEADS * t, MB_HEADS * t), F32)],
    )
    return pl.pallas_call(
        kern,
        grid_spec=grid_spec,
        out_shape=jax.ShapeDtypeStruct((db, t, MB_WIDTH), F32),
        compiler_params=_params(("arbitrary",), 40),
        name="moba_sample",
    )(page_table, rel_table, h3, h3, h3, *([cache_k] * n_pages), *([cache_v] * n_pages))


def _merge_kernel(oa_ref, ob_ref, ga0_ref, ga1_ref, gb0_ref, gb1_ref, x_ref,
                  wa_ref, wb_ref, wo_ref, gx_ref, wq_ref, x1_ref, xq_ref):
    ga = jnp.concatenate([ga0_ref[...], ga1_ref[...]], axis=-1)
    gb = jnp.concatenate([gb0_ref[...], gb1_ref[...]], axis=-1)
    pa = _dot(oa_ref[...].astype(BF16), wa_ref[...])
    pb = _dot(ob_ref[...].astype(BF16), wb_ref[...])
    merged = _sigmoid(ga) * pa + _sigmoid(gb) * pb
    x1 = x_ref[...] + _dot(merged.astype(BF16), wo_ref[...])
    x1_ref[...] = x1
    xq_ref[...] = _dot(_rmsnorm(x1, gx_ref[...]).astype(BF16), wq_ref[...])


def _merge(oa, ob, h2, x, wa, wb, wo, gx, wq, *, tm):
    n = x.shape[0]
    assert n % tm == 0
    half = D_MODEL // 2
    tok = lambda w, k=0: pl.BlockSpec((tm, w), lambda i: (i, k))
    full = lambda a: pl.BlockSpec(a.shape, lambda i: (0, 0))
    gx2 = gx.reshape(1, D_MODEL)
    return pl.pallas_call(
        _merge_kernel,
        grid=(n // tm,),
        in_specs=[tok(HG_WIDTH), tok(MB_WIDTH),
                  tok(half, COL_GA // half), tok(half, COL_GA // half + 1),
                  tok(half, COL_GB // half), tok(half, COL_GB // half + 1),
                  tok(D_MODEL), full(wa), full(wb), full(wo), full(gx2), full(wq)],
        out_specs=[tok(D_MODEL), tok(XA_WIDTH)],
        out_shape=[jax.ShapeDtypeStruct((n, D_MODEL), F32),
                   jax.ShapeDtypeStruct((n, XA_WIDTH), F32)],
        compiler_params=_params(("parallel",), 48),
        name="merge_mix",
    )(oa, ob, h2, h2, h2, h2, x, wa, wb, wo, gx2, wq)


def _mem_attn_kernel(q_ref, k_ref, v_ref, o_ref):
    scale = XA_DIM ** -0.5
    for hd in range(XA_HEADS):
        sl = slice(hd * XA_DIM, (hd + 1) * XA_DIM)
        s = _dot_nt(q_ref[0][:, sl].astype(BF16), k_ref[0][:, sl].astype(BF16)) * scale
        m = jnp.max(s, axis=-1, keepdims=True)
        p = jnp.exp(s - m)
        l = jnp.sum(p, axis=-1, keepdims=True)
        o_ref[0, :, sl] = _dot(p.astype(BF16), v_ref[0][:, sl].astype(BF16)) / l


def _mem_attn(xq3, mem_kv, *, tq):
    b, t, _ = xq3.shape
    assert t % tq == 0
    mem = lambda k: pl.BlockSpec((1, mem_kv.shape[1], XA_WIDTH), lambda i, j: (i, 0, k))
    qs = pl.BlockSpec((1, tq, XA_WIDTH), lambda i, j: (i, j, 0))
    return pl.pallas_call(
        _mem_attn_kernel,
        grid=(b, t // tq),
        in_specs=[qs, mem(0), mem(1)],
        out_specs=qs,
        out_shape=jax.ShapeDtypeStruct((b, t, XA_WIDTH), F32),
        compiler_params=_params(("parallel", "parallel"), 32),
        name="mem_attn",
    )(xq3, mem_kv, mem_kv)


def _mem_attn_rows_kernel(q_ref, k_ref, v_ref, o_ref):
    bb, t, _ = q_ref.shape
    rows = XA_HEADS * t
    scale = XA_DIM ** -0.5
    row_head = lax.broadcasted_iota(I32, (rows, 1), 0) // t
    kc = lax.broadcasted_iota(I32, (rows, k_ref.shape[1]), 1)
    same_head = (kc & (XA_HEADS - 1)) == row_head
    for i in range(bb):
        x = q_ref[i]
        q = jnp.concatenate([x[:, hd * XA_DIM:(hd + 1) * XA_DIM] for hd in range(XA_HEADS)], axis=0)
        s = _dot_nt(q.astype(BF16), k_ref[i].astype(BF16)) * scale
        s = jnp.where(same_head, s, NEG_INF)
        m = jnp.max(s, axis=-1, keepdims=True)
        p = jnp.exp(s - m)
        l = jnp.sum(p, axis=-1, keepdims=True)
        out = _dot(p.astype(BF16), v_ref[i].astype(BF16)) / l
        for hd in range(XA_HEADS):
            o_ref[i, :, hd * XA_DIM:(hd + 1) * XA_DIM] = out[hd * t:(hd + 1) * t]


def _mem_attn_rows(xq3, mem_k, mem_v, *, bb):
    b, t, _ = xq3.shape
    assert b % bb == 0 and XA_HEADS == 4 and t % 8 == 0
    mem = pl.BlockSpec((bb,) + mem_k.shape[1:], lambda i: (i, 0, 0))
    qs = pl.BlockSpec((bb, t, XA_WIDTH), lambda i: (i, 0, 0))
    return pl.pallas_call(
        _mem_attn_rows_kernel,
        grid=(b // bb,),
        in_specs=[qs, mem, mem],
        out_specs=qs,
        out_shape=jax.ShapeDtypeStruct((b, t, XA_WIDTH), F32),
        compiler_params=_params(("parallel",), 32),
        name="mem_attn_rows",
    )(xq3, mem_k, mem_v)


def _route(logits):
    lane = lax.broadcasted_iota(I32, logits.shape, 1).astype(F32)
    first = lambda hit: jnp.min(jnp.where(hit, lane, float(ROUTER_LANES)), axis=-1, keepdims=True)
    gl = jnp.where(lane < N_GROUPS, logits, NEG_INF)
    gmax = jnp.max(gl, axis=-1, keepdims=True)
    grp = first(gl == gmax)
    g_prob = 1.0 / jnp.sum(jnp.exp(gl - gmax), axis=-1, keepdims=True)
    e_lo = EXPERT_LANE0 + grp * EXPERTS_PER_GROUP
    in_grp = (lane >= e_lo) & (lane < e_lo + EXPERTS_PER_GROUP)
    el = jnp.where(in_grp, logits, NEG_INF)
    top1 = jnp.max(el, axis=-1, keepdims=True)
    idx1 = first(el == top1)
    el2 = jnp.where(lane == idx1, NEG_INF, el)
    top2 = jnp.max(el2, axis=-1, keepdims=True)
    idx2 = first(el2 == top2)
    e2 = jnp.exp(top2 - top1)
    w1 = g_prob / (1.0 + e2)
    w2 = w1 * e2
    comb = jnp.where(lane == idx1, w1, 0.0) + jnp.where(lane == idx2, w2, 0.0)
    return jnp.where(lane == GRP_LANE, grp, comb)


def _xo_route_kernel(x1_ref, at_ref, wxo_ref, gf_ref, wr_ref, br_ref, x2_ref, xf_ref, comb_ref):
    x2 = x1_ref[...] + _dot(at_ref[...].astype(BF16), wxo_ref[...])
    x2_ref[...] = x2
    xf = _rmsnorm(x2, gf_ref[...])
    xf_ref[...] = xf.astype(BF16)
    comb_ref[...] = _route(_dot_f32acc(xf, wr_ref[...]) + br_ref[...])


def _xo_route(x1, attn, wxo, g_ffn, w_router_t, b_router_t, *, tm):
    n = x1.shape[0]
    assert n % tm == 0
    tok = lambda w: pl.BlockSpec((tm, w), lambda i: (i, 0))
    full = lambda a: pl.BlockSpec(a.shape, lambda i: (0, 0))
    gf2 = g_ffn.reshape(1, D_MODEL)
    return pl.pallas_call(
        _xo_route_kernel,
        grid=(n // tm,),
        in_specs=[tok(D_MODEL), tok(XA_WIDTH), full(wxo), full(gf2), full(w_router_t), full(b_router_t)],
        out_specs=[tok(D_MODEL), tok(D_MODEL), tok(ROUTER_LANES)],
        out_shape=[jax.ShapeDtypeStruct((n, D_MODEL), F32), jax.ShapeDtypeStruct((n, D_MODEL), BF16),
                   jax.ShapeDtypeStruct((n, ROUTER_LANES), F32)],
        compiler_params=_params(("parallel",), 40),
        name="xo_route",
    )(x1, attn, wxo, gf2, w_router_t, b_router_t)


def _moe_kernel(xf_ref, comb_ref, x2_ref, wg_ref, wu_ref, wd_ref, gn_ref, y_ref,
                tri_ref, posc_ref, cnt_ref, xs_ref, ws_ref, ys_ref, *, r):
    i = pl.program_id(0)
    step = pl.program_id(1)
    tm = xf_ref.shape[0]
    per_step = wg_ref.shape[0]
    steps_per_group = EXPERTS_PER_GROUP // per_step
    g = step // steps_per_group
    lane = lax.broadcasted_iota(I32, (tm, ROUTER_LANES), 1)

    @pl.when((i == 0) & (step == 0))
    def _():
        rr = lax.broadcasted_iota(I32, (tm, tm), 0)
        cc = lax.broadcasted_iota(I32, (tm, tm), 1)
        tri_ref[...] = jnp.where(cc < rr, 1.0, 0.0).astype(BF16)

    @pl.when(step == 0)
    def _():
        grp = comb_ref[:, GRP_LANE:GRP_LANE + 1]
        lane_f = lane.astype(F32)
        onehot = jnp.where((lane_f == grp) & (lane < N_GROUPS), 1.0, 0.0)
        before = _dot(tri_ref[...], onehot.astype(BF16))
        cnt = jnp.sum(onehot, axis=0, keepdims=True)
        cnt_ref[...] = cnt
        pos = jnp.sum(jnp.where(lane_f < grp, cnt, 0.0) + onehot * before, axis=-1, keepdims=True)
        posc_ref[...] = jnp.where(lane == 0, pos, 0.0)
        hi = jnp.floor(pos * (1.0 / POS_SPLIT))
        lo = pos - hi * POS_SPLIT
        cols = jnp.where(lane == 0, hi, jnp.where(lane == 1, lo, 0.0))
        er = lax.broadcasted_iota(I32, (16, ROUTER_LANES), 0)
        ec = lax.broadcasted_iota(I32, (16, ROUTER_LANES), 1)
        post = _dot_nt(jnp.where(er == ec, 1.0, 0.0).astype(BF16), cols.astype(BF16))
        pos_row = post[0:1, :] * POS_SPLIT + post[1:2, :]
        slot = lax.broadcasted_iota(I32, (tm, tm), 0).astype(F32)
        perm = jnp.where(pos_row == slot, 1.0, 0.0).astype(BF16)
        comb_hi, comb_lo = _split2(comb_ref[...])
        xs_ref[0:tm, :] = _dot(perm, xf_ref[...]).astype(BF16)
        ws_ref[0:tm, :] = _dot(perm, comb_hi) + _dot(perm, comb_lo)
        xs_ref[tm:, :] = jnp.zeros((r, D_MODEL), BF16)
        ws_ref[tm:, :] = jnp.zeros((r, ROUTER_LANES), F32)
        ys_ref[...] = jnp.zeros(ys_ref.shape, F32)

    lane_row = lax.broadcasted_iota(I32, (1, ROUTER_LANES), 1)
    first = jnp.sum(jnp.where(lane_row < g, cnt_ref[...], 0.0)).astype(I32)
    count = jnp.sum(jnp.where(lane_row == g, cnt_ref[...], 0.0)).astype(I32)
    start = (first // 16) * 16
    n_windows = (first + count - start + (r - 1)) // r

    def experts(c, carry):
        rows = pl.ds(pl.multiple_of(start + c * r, 16), r)
        xg = xs_ref[rows, :]
        wq = ws_ref[rows, :]
        wl = lax.broadcasted_iota(I32, (r, ROUTER_LANES), 1)
        ups = [(_dot(xg, wg_ref[k]), _dot(xg, wu_ref[k])) for k in range(per_step)]
        hidden = []
        for k, (gate, up) in enumerate(ups):
            w_e = jnp.sum(jnp.where(wl == EXPERT_LANE0 + step * per_step + k, wq, 0.0), axis=-1, keepdims=True)
            hidden.append((_silu(gate) * up * w_e).astype(BF16))
        w_down = wd_ref[...].reshape(per_step * EXPERT_HIDDEN, D_MODEL)
        ys_ref[rows, :] += _dot(jnp.concatenate(hidden, axis=-1), w_down)
        return carry

    lax.fori_loop(0, n_windows, experts, 0)

    @pl.when(step == pl.num_programs(1) - 1)
    def _():
        slot = lax.broadcasted_iota(I32, (tm, tm), 1).astype(F32)
        place = jnp.where(posc_ref[:, 0:1] == slot, 1.0, 0.0).astype(BF16)
        moe = _dot(place, ys_ref[0:tm, :].astype(BF16))
        y_ref[...] = _rmsnorm(x2_ref[...] + moe, gn_ref[...])


def _moe(xf, comb, x2, wg, wu, wd, g_final, *, tm):
    n = xf.shape[0]
    assert n % tm == 0 and tm % 16 == 0
    r = min(MOE_CHUNK_ROWS, tm)
    cap = tm + r
    per_step = MOE_EXPERTS_PER_STEP
    tok = lambda w: pl.BlockSpec((tm, w), lambda i, s: (i, 0))
    gn2 = g_final.reshape(1, D_MODEL)
    kern = functools.partial(_moe_kernel, r=r)
    vmem = (2 * tm * D_MODEL * (2 + 4 + 4) + 2 * tm * ROUTER_LANES * 4
            + 2 * 3 * per_step * D_MODEL * EXPERT_HIDDEN * 2 + tm * tm * 2
            + cap * D_MODEL * (2 + 4) + tm * D_MODEL * 4) // MIB + 8
    return pl.pallas_call(
        kern,
        grid=(n // tm, N_EXPERTS // per_step),
        in_specs=[tok(D_MODEL), tok(ROUTER_LANES), tok(D_MODEL),
                  pl.BlockSpec((per_step, D_MODEL, EXPERT_HIDDEN), lambda i, s: (s, 0, 0)),
                  pl.BlockSpec((per_step, D_MODEL, EXPERT_HIDDEN), lambda i, s: (s, 0, 0)),
                  pl.BlockSpec((per_step, EXPERT_HIDDEN, D_MODEL), lambda i, s: (s, 0, 0)),
                  pl.BlockSpec((1, D_MODEL), lambda i, s: (0, 0))],
        out_specs=tok(D_MODEL),
        out_shape=jax.ShapeDtypeStruct((n, D_MODEL), F32),
        scratch_shapes=[pltpu.VMEM((tm, tm), BF16),
                        pltpu.VMEM((tm, ROUTER_LANES), F32), pltpu.VMEM((1, ROUTER_LANES), F32),
                        pltpu.VMEM((cap, D_MODEL), BF16), pltpu.VMEM((cap, ROUTER_LANES), F32),
                        pltpu.VMEM((cap, D_MODEL), F32)],
        compiler_params=_params(("arbitrary", "arbitrary"), vmem),
        name="moe_grouped",
    )(xf, comb, x2, wg, wu, wd, gn2)


def _token_tile(n, cap=512):
    return cap if n % cap == 0 else n


def _col_tile(n, cap=1536):
    return max(c for c in range(V7X_LANES, cap + 1, V7X_LANES) if n % c == 0)


def _layer(x3, w, s0, moba_fn, mem_fn):
    b, t, _ = x3.shape
    n = b * t
    x2d = x3.reshape(n, D_MODEL)
    tm = _token_tile(n)
    h2, k4, v4 = _norm_matmul(x2d, w["norm_mix"], w["w_in"], tm=_token_tile(n, 1024),
                              tn=_col_tile(IN_COLS), head_cols=(COL_MK, COL_MV), heads=MB_HEADS,
                              head_dim=MB_DIM)
    h3 = h2.reshape(b, t, IN_COLS)
    oa, s_new = _hgrn(h3, w["hg_lb_logits"], w["hg_norm"], s0)
    ob = moba_fn(h3)
    x1, xq = _merge(oa.reshape(n, HG_WIDTH), ob.reshape(n, MB_WIDTH), h2, x2d,
                    w["w_branch_a"], w["w_branch_b"], w["w_mix_out"], w["norm_xattn"], w["w_xq"], tm=tm)
    attn = mem_fn(xq.reshape(b, t, XA_WIDTH))
    x2, xf, comb = _xo_route(x1, attn.reshape(n, XA_WIDTH), w["w_xo"], w["norm_ffn"],
                             w["w_router"], w["b_router"], tm=tm)
    y = _moe(xf, comb, x2, w["w_expert_gate"], w["w_expert_up"], w["w_expert_down"],
             w["norm_final"], tm=_token_tile(n, 1024))
    k_new = k4.reshape(b, t, MB_HEADS, MB_DIM)
    v_new = v4.reshape(b, t, MB_HEADS, MB_DIM)
    return y.reshape(b, t, D_MODEL), k_new, v_new, s_new


def kernel(x_prompt, x_sample, cache_k, cache_v, state_hgrn, cache_mem_k, cache_mem_v, page_table, mem_prompt, norm_mix, w_in, hg_lb_logits, hg_norm, w_branch_a, w_branch_b, w_mix_out, rel_table, norm_xattn, norm_mem, w_xq, w_xk, w_xv, w_xo, norm_ffn, w_group_router, b_group_router, w_expert_router, b_expert_router, w_expert_gate, w_expert_up, w_expert_down, norm_final):
    assert w_in.shape[0] == DEPTH == 1 and hg_lb_logits.shape[0] == DEPTH + 1
    b = x_prompt.shape[0]
    db = x_sample.shape[0]
    n_pool = cache_k.shape[1]
    pad_lanes = ROUTER_LANES - EXPERT_LANE0 - N_EXPERTS
    w = {
        "norm_mix": norm_mix[0], "w_in": w_in[0].astype(BF16),
        "hg_lb_logits": hg_lb_logits, "hg_norm": hg_norm[0],
        "w_branch_a": w_branch_a[0].astype(BF16), "w_branch_b": w_branch_b[0].astype(BF16),
        "w_mix_out": w_mix_out[0].astype(BF16), "norm_xattn": norm_xattn[0],
        "w_xq": w_xq[0].astype(BF16), "w_xo": w_xo[0].astype(BF16), "norm_ffn": norm_ffn[0],
        "w_router": jnp.pad(jnp.concatenate(
            [w_group_router[0], jnp.zeros((D_MODEL, EXPERT_LANE0 - N_GROUPS), F32), w_expert_router[0]],
            axis=1), ((0, 0), (0, pad_lanes))),
        "b_router": jnp.pad(jnp.concatenate(
            [b_group_router[0], jnp.zeros((EXPERT_LANE0 - N_GROUPS,), F32), b_expert_router[0]]),
            (0, pad_lanes)).reshape(1, ROUTER_LANES),
        "w_expert_gate": w_expert_gate[0].reshape(N_EXPERTS, D_MODEL, EXPERT_HIDDEN).astype(BF16),
        "w_expert_up": w_expert_up[0].reshape(N_EXPERTS, D_MODEL, EXPERT_HIDDEN).astype(BF16),
        "w_expert_down": w_expert_down[0].reshape(N_EXPERTS, EXPERT_HIDDEN, D_MODEL).astype(BF16),
        "norm_final": norm_final,
    }

    w_mem = jnp.concatenate([w_xk[0], w_xv[0]], axis=1).astype(BF16)
    mem_kv, mk_p, mv_p = _norm_matmul(mem_prompt.reshape(b * MEM_LEN, D_MODEL), norm_mem[0], w_mem,
                                      tm=_token_tile(b * MEM_LEN), tn=512, head_cols=(0, XA_WIDTH),
                                      heads=XA_HEADS, head_dim=XA_DIM)
    mem_kv = mem_kv.reshape(b, MEM_LEN, 2 * XA_WIDTH)
    s0 = jnp.zeros((b, HG_HEADS, HG_DIM, HG_DIM), F32)
    y_p, k_p, v_p, s_p = _layer(x_prompt, w, s0,
                                functools.partial(_moba_prompt, rel_table=rel_table),
                                functools.partial(_mem_attn, mem_kv=mem_kv, tq=512))

    ck = cache_k[0].reshape(n_pool, PAGE_SIZE * MB_HEADS, MB_DIM)
    cv = cache_v[0].reshape(n_pool, PAGE_SIZE * MB_HEADS, MB_DIM)
    moba_s = functools.partial(_moba_sample, cache_k=ck, cache_v=cv, page_table=page_table,
                               rel_table=rel_table)
    mem_s = functools.partial(_mem_attn_rows, bb=4,
                              mem_k=cache_mem_k[0].reshape(db, MEM_LEN * XA_HEADS, XA_DIM),
                              mem_v=cache_mem_v[0].reshape(db, MEM_LEN * XA_HEADS, XA_DIM))
    y_s, k_s, v_s, s_s = _layer(x_sample, w, state_hgrn[0], moba_s, mem_s)

    heads = lambda a: a.reshape(b, MEM_LEN, XA_HEADS, XA_DIM)[None]
    return (y_p, y_s, k_p[None], v_p[None], s_p[None], heads(mk_p), heads(mv_p),
            k_s[None], v_s[None], s_s[None])
```

```python
import functools
import math

import numpy as np
import jax
import jax.numpy as jnp
from jax import lax
from jax.experimental import pallas as pl
from jax.experimental.pallas import tpu as pltpu

F32 = jnp.float32
BF16 = jnp.bfloat16
I32 = jnp.int32

D_MODEL = 1024
DEPTH = 1
PAGE_SIZE = 128
HG_HEADS = 4
HG_DIM = 128
HG_WIDTH = HG_HEADS * HG_DIM
MB_HEADS = 4
MB_DIM = 128
MB_WIDTH = MB_HEADS * MB_DIM
MB_BLOCK = 256
MB_TOPK = 3
REL_BUCKETS = 32
REL_MAX_DIST = 128
REL_MAX_EXACT = REL_BUCKETS // 2
MEM_LEN = 256
XA_HEADS = 4
XA_DIM = 128
XA_WIDTH = XA_HEADS * XA_DIM
N_GROUPS = 4
EXPERTS_PER_GROUP = 8
N_EXPERTS = N_GROUPS * EXPERTS_PER_GROUP
EXPERT_TOPK = 2
EXPERT_HIDDEN = 256
NORM_EPS = 1e-6
IN_COLS = 4 * HG_WIDTH + 3 * MB_WIDTH + 2 * D_MODEL
COL_MQ = 4 * HG_WIDTH
COL_MK = COL_MQ + MB_WIDTH
COL_MV = COL_MK + MB_WIDTH
COL_GA = COL_MV + MB_WIDTH
COL_GB = COL_GA + D_MODEL

V7X_LANES = 128
V7X_VMEM_BYTES = 64 * 1024 * 1024
MIB = 1024 * 1024

NEG_INF = float("-inf")
MASK_BIG = -1e30
GATE_PAD = 16
ROUTER_LANES = 128
GRP_LANE = 0
EXPERT_LANE0 = 8
POS_SPLIT = 32.0
MOE_CHUNK_ROWS = 256
MOE_EXPERTS_PER_STEP = EXPERTS_PER_GROUP // 2
MOE_CLASSES = 3
HGRN_PROMPT_CHUNK = 128
HGRN_PROMPT_ROWS = 4
HGRN_MIN_CHUNK = 16
HGRN_CHAINS = 16


def _params(semantics, vmem_mib):
    return pltpu.CompilerParams(dimension_semantics=semantics,
                                vmem_limit_bytes=min(vmem_mib * MIB, V7X_VMEM_BYTES - 8 * MIB))


def _dot(a, b):
    return jnp.dot(a, b, preferred_element_type=F32)


def _dot_nt(a, b):
    return lax.dot_general(a, b, (((1,), (1,)), ((), ())), preferred_element_type=F32)


def _dot_tn(a, b):
    return lax.dot_general(a, b, (((0,), (0,)), ((), ())), preferred_element_type=F32)


def _split2(a):
    hi = a.astype(BF16)
    lo = (a - hi.astype(F32)).astype(BF16)
    return hi, lo


def _dot_nt_f32acc(a, b):
    ah, al = _split2(a)
    bh, bl = _split2(b)
    return _dot_nt(ah, bh) + (_dot_nt(ah, bl) + _dot_nt(al, bh))


def _dot_f32acc(a, b):
    ah, al = _split2(a)
    bh, bl = _split2(b)
    return _dot(ah, bh) + (_dot(ah, bl) + _dot(al, bh))


def _rmsnorm(x, g):
    return x * lax.rsqrt(jnp.mean(x * x, axis=-1, keepdims=True) + NORM_EPS) * g


def _sigmoid(x):
    return 1.0 / (1.0 + jnp.exp(-x))


def _silu(x):
    return x * _sigmoid(x)


def _norm_matmul_kernel(x_ref, g_ref, w_ref, o_ref, *refs, head_cols, heads, head_dim):
    head_refs, xn_ref = refs[:-1], refs[-1]
    j = pl.program_id(1)
    tm, tn = o_ref.shape

    @pl.when(j == 0)
    def _():
        xn_ref[...] = _rmsnorm(x_ref[...], g_ref[...]).astype(BF16)

    res = _dot(xn_ref[...], w_ref[...])
    o_ref[...] = res
    for col0, href in zip(head_cols, head_refs):
        for hd in range(heads):
            tile, local = divmod(col0 + hd * head_dim, tn)

            @pl.when(j == tile)
            def _(hd=hd, local=local, href=href):
                href[pl.ds(hd, tm, stride=heads), :] = res[:, local:local + head_dim]


def _norm_matmul(x, g, w_bf16, *, tm, tn, head_cols=(), heads=1, head_dim=V7X_LANES):
    m, d = x.shape
    n = w_bf16.shape[1]
    assert m % tm == 0 and n % tn == 0 and tn % head_dim == 0
    vmem = (2 * tm * d * 4 + tm * d * 2 + 2 * d * tn * 2 + 2 * tm * tn * 4
            + len(head_cols) * 2 * tm * heads * head_dim * 4) // MIB + 12
    kern = functools.partial(_norm_matmul_kernel, head_cols=tuple(head_cols), heads=heads,
                             head_dim=head_dim)
    head_spec = pl.BlockSpec((tm * heads, head_dim), lambda i, j: (i, 0))
    head_shape = jax.ShapeDtypeStruct((m * heads, head_dim), F32)
    return pl.pallas_call(
        kern,
        grid=(m // tm, n // tn),
        in_specs=[pl.BlockSpec((tm, d), lambda i, j: (i, 0)),
                  pl.BlockSpec((1, d), lambda i, j: (0, 0)),
                  pl.BlockSpec((d, tn), lambda i, j: (0, j))],
        out_specs=[pl.BlockSpec((tm, tn), lambda i, j: (i, j))] + [head_spec] * len(head_cols),
        out_shape=[jax.ShapeDtypeStruct((m, n), F32)] + [head_shape] * len(head_cols),
        scratch_shapes=[pltpu.VMEM((tm, d), BF16)],
        compiler_params=_params(("parallel", "arbitrary"), vmem),
        name="norm_matmul",
    )(x, g.reshape(1, d), w_bf16)


def _hgrn_levels(c):
    return int(round(math.log2(c)))


def _hgrn_sum_masks(c):
    t = np.arange(c)[:, None]
    j = np.arange(c)[None, :]
    rows = [j <= t, j > t]
    for lv in range(_hgrn_levels(c)):
        half = c >> (lv + 1)
        blk = 2 * half
        mid = (t // blk) * blk + half - 1
        upper = (t % blk) >= half
        rows.append((upper & (j > mid) & (j <= t)) | ((~upper) & (j > t) & (j <= mid)))
    return np.concatenate(rows, axis=0).astype(np.float32)


def _hgrn_kernel(hq_ref, hf_ref, hi_ref, hg_ref, lbl_ref, gn_ref, mask_ref, s0_ref,
                 o_ref, s_ref, st_ref, *, c, t_blk, bb):
    j = pl.program_id(1)
    chains = [(bi, hd) for bi in range(bb) for hd in range(HG_HEADS)]

    @pl.when(j == 0)
    def _():
        for bi, hd in chains:
            st_ref[bi * HG_HEADS + hd] = s0_ref[bi, hd].T

    lbl = lbl_ref[...]
    lmax = jnp.max(lbl, axis=0, keepdims=True)
    lexp = jnp.exp(lbl - lmax)
    lb_all = lexp[0:1, :] / jnp.sum(lexp, axis=0, keepdims=True)

    masks = mask_ref[...]
    pad = c - t_blk
    row = lax.broadcasted_iota(I32, (c, HG_DIM), 0)
    rr = lax.broadcasted_iota(I32, (c, c), 0)
    cc = lax.broadcasted_iota(I32, (c, c), 1)

    def padded(a):
        if pad == 0:
            return a
        return jnp.concatenate([a, jnp.zeros((pad, HG_DIM), F32)], axis=0)

    hsl = lambda hd: slice(hd * HG_DIM, (hd + 1) * HG_DIM)
    qs, ks, vs, logf_parts = [], [], [], []
    for bi, hd in chains:
        xq = hq_ref[bi][:, hsl(hd)]
        lb = lb_all[:, hsl(hd)]
        f = lb + (1.0 - lb) * _sigmoid(hf_ref[bi][:, hsl(hd)])
        qs.append(padded(_silu(xq)))
        ks.append(padded(1.0 - f))
        vs.append(padded(hi_ref[bi][:, hsl(hd)]))
        logf_parts.append(_split2(padded(jnp.log(f))))
    exs = []
    for n in range(0, len(chains), 2):
        (hi_a, lo_a), (hi_b, lo_b) = logf_parts[n], logf_parts[n + 1]
        stacked = jnp.concatenate([jnp.concatenate([hi_a, hi_b], axis=-1),
                                   jnp.concatenate([lo_a, lo_b], axis=-1)], axis=0)
        ex = jnp.exp(_dot(masks, stacked))
        exs += [ex[:, :HG_DIM], ex[:, HG_DIM:]]

    level_dots = []
    for q, kk, ex in zip(qs, ks, exs):
        per_level = []
        for lv in range(_hgrn_levels(c)):
            half = c >> (lv + 1)
            upper = (row & half) != 0
            ex_lv = ex[(2 + lv) * c:(3 + lv) * c]
            a = jnp.where(upper, q * ex_lv, 0.0).astype(BF16)
            bm = jnp.where(upper, 0.0, kk * ex_lv).astype(BF16)
            per_level.append(_dot_nt(a, bm))
        level_dots.append(per_level)

    outs = []
    for n, (bi, hd) in enumerate(chains):
        q, kk, v, ex = qs[n], ks[n], vs[n], exs[n]
        scores = jnp.zeros((c, c), F32)
        for lv, d in enumerate(level_dots[n]):
            scores = scores + jnp.where((rr ^ cc) < 2 * (c >> (lv + 1)), d, 0.0)
        eb = ex[0:c]
        eb_rev = ex[c:2 * c]
        diag = jnp.sum(q * kk, axis=-1, keepdims=True)
        st = st_ref[bi * HG_HEADS + hd]
        vb = v.astype(BF16)
        o = (_dot_nt((q * eb).astype(BF16), st.astype(BF16))
             + _dot(scores.astype(BF16), vb) + diag * v)
        st_ref[bi * HG_HEADS + hd] = st * eb[c - 1:c, :] + _dot_tn(vb, (kk * eb_rev).astype(BF16))
        outs.append(o[0:t_blk])

    for (bi, hd), o in zip(chains, outs):
        xg = hg_ref[bi][:, hsl(hd)]
        o_ref[bi, :, hsl(hd)] = _rmsnorm(o, gn_ref[:, hsl(hd)]) * _silu(xg)

    @pl.when(j == pl.num_programs(1) - 1)
    def _():
        for bi, hd in chains:
            s_ref[bi, hd] = st_ref[bi * HG_HEADS + hd].T


def _hgrn(h3, lb_logits, hg_norm, s0):
    b, t, _ = h3.shape
    t_blk = math.gcd(t, HGRN_PROMPT_CHUNK)
    c = max(t_blk, HGRN_MIN_CHUNK)
    bb = math.gcd(b, HGRN_CHAINS // HG_HEADS) if t == t_blk else math.gcd(b, HGRN_PROMPT_ROWS)
    m01 = _hgrn_sum_masks(c)
    masks = jnp.asarray(np.concatenate([m01, m01], axis=1), BF16)
    col = lambda k: pl.BlockSpec((bb, t_blk, HG_WIDTH), lambda i, j: (i, j, k))
    state_spec = pl.BlockSpec((bb, HG_HEADS, HG_DIM, HG_DIM), lambda i, j: (i, 0, 0, 0))
    kern = functools.partial(_hgrn_kernel, c=c, t_blk=t_blk, bb=bb)
    return pl.pallas_call(
        kern,
        grid=(b // bb, t // t_blk),
        in_specs=[col(0), col(1), col(2), col(3),
                  pl.BlockSpec(lb_logits.shape, lambda i, j: (0, 0)),
                  pl.BlockSpec((1, HG_WIDTH), lambda i, j: (0, 0)),
                  pl.BlockSpec(masks.shape, lambda i, j: (0, 0)),
                  state_spec],
        out_specs=[pl.BlockSpec((bb, t_blk, HG_WIDTH), lambda i, j: (i, j, 0)), state_spec],
        out_shape=[jax.ShapeDtypeStruct((b, t, HG_WIDTH), F32),
                   jax.ShapeDtypeStruct(s0.shape, F32)],
        scratch_shapes=[pltpu.VMEM((bb * HG_HEADS, HG_DIM, HG_DIM), F32)],
        compiler_params=_params(("parallel", "arbitrary"), 32),
        name="hgrn2",
    )(h3, h3, h3, h3, lb_logits, hg_norm.reshape(1, HG_WIDTH), masks, s0)


def _rel_bias(rel_ref, head, dist):
    d = jnp.maximum(dist, 0)
    df = jnp.maximum(d, 1).astype(F32)
    large = REL_MAX_EXACT + (jnp.log(df / REL_MAX_EXACT) / math.log(REL_MAX_DIST / REL_MAX_EXACT)
                             * (REL_BUCKETS - REL_MAX_EXACT)).astype(I32)
    large = jnp.minimum(large, REL_BUCKETS - 1)
    bucket = jnp.where(d < REL_MAX_EXACT, d, large)
    out = jnp.zeros(dist.shape, F32)
    for kb in range(REL_BUCKETS):
        out = jnp.where(bucket == kb, rel_ref[kb, head], out)
    return out


def _stack_rows(rows):
    ridx = lax.broadcasted_iota(I32, (GATE_PAD, MB_DIM), 0)
    out = jnp.zeros((GATE_PAD, MB_DIM), F32)
    for n, r in enumerate(rows):
        out = jnp.where(ridx == n, r, out)
    return out


def _topk_blocks(gate, n_cand, n_past):
    lane = lax.broadcasted_iota(I32, gate.shape, 1)
    past = lane < n_past
    g = jnp.where(past, gate, NEG_INF)
    rank = jnp.zeros(gate.shape, I32)
    for m in range(n_cand):
        gm = g[:, m:m + 1]
        ahead = jnp.where(gm > g, 1, jnp.where((gm == g) & (lane > m), 1, 0))
        rank = rank + ahead
    return jnp.where(past & (rank < MB_TOPK), 1.0, 0.0)


def _topk_blocks_t(gate_t, n_cand, n_past):
    blk_id = lax.broadcasted_iota(I32, gate_t.shape, 0)
    past = blk_id < n_past
    g = jnp.where(past, gate_t, NEG_INF)
    rank = jnp.zeros(gate_t.shape, I32)
    for m in range(n_cand):
        gm = g[m:m + 1, :]
        rank = rank + jnp.where(gm > g, 1, jnp.where((gm == g) & (blk_id > m), 1, 0))
    return past & (rank < MB_TOPK)


def _moba_prompt_kernel(rel_ref, q_ref, k_ref, v_ref, o_ref,
                        kx_ref, vx_ref, km_ref, bias_ref, eye_ref, s_ref, mx_ref, mb_ref, acc_ref, *, n_blk):
    b = pl.program_id(0)
    i = pl.program_id(1)
    blk = MB_BLOCK
    scale = MB_DIM ** -0.5
    heads = range(MB_HEADS)
    hsl = lambda hd: slice(hd * MB_DIM, (hd + 1) * MB_DIM)

    t = k_ref.shape[1]
    own_slot = n_blk - 1

    @pl.when((b == 0) & (i == 0))
    def _():
        r = lax.broadcasted_iota(I32, (blk, blk), 0)
        c = lax.broadcasted_iota(I32, (blk, blk), 1)
        eye_ref[...] = jnp.where(r == c, 1.0, 0.0).astype(BF16)
        kr = lax.broadcasted_iota(I32, (t, MB_DIM), 0)
        kc = lax.broadcasted_iota(I32, (t, MB_DIM), 1)
        blk_onehot = jnp.where(kr // blk == kc, 1.0, 0.0).astype(BF16)
        for hd in heads:
            bias_ref[hd, 0] = jnp.where(c <= r, _rel_bias(rel_ref, hd, r - c), NEG_INF)
            bias_ref[hd, 1] = _rel_bias(rel_ref, hd, r - c + blk)
            kx_ref[hd, :, MB_DIM:] = blk_onehot
            vx_ref[hd, :, MB_DIM:] = jnp.ones((t, MB_DIM), BF16)

    @pl.when(i == 0)
    def _():
        means = [jnp.sum(k_ref[0, n * blk:(n + 1) * blk, :], axis=0, keepdims=True) * (1.0 / blk)
                 for n in range(n_blk)]
        for hd in heads:
            kx_ref[hd, :, :MB_DIM] = k_ref[0][:, hsl(hd)].astype(BF16)
            vx_ref[hd, :, :MB_DIM] = v_ref[0][:, hsl(hd)].astype(BF16)
            km_ref[hd] = _stack_rows([mn[:, hsl(hd)] for mn in means])

    q = q_ref[0]
    blk_id = lax.broadcasted_iota(I32, (GATE_PAD, blk), 0)
    qh = [q[:, hsl(hd)] for hd in heads]
    gates = [_dot_nt_f32acc(km_ref[hd], qh[hd]) for hd in heads]
    qm_t = []
    for hd in heads:
        sel_t = _topk_blocks_t(gates[hd], n_blk, i)
        m_t = jnp.where(sel_t | (blk_id >= i), 0.0, MASK_BIG).astype(BF16)
        qm_t.append(jnp.concatenate([m_t, jnp.zeros((MB_DIM - GATE_PAD, blk), BF16)], axis=0))
    qmask = [_dot_nt(eye_ref[...], qm_t[hd]) for hd in heads]
    qx = [jnp.concatenate([qh[hd].astype(BF16), qmask[hd].astype(BF16)], axis=-1) for hd in heads]

    own0 = pl.multiple_of(i * blk, blk)
    dots = [_dot_nt(qx[hd], kx_ref[hd, pl.ds(own0, blk), :]) for hd in heads]
    for hd in heads:
        s = dots[hd] * scale + bias_ref[hd, 0]
        s_ref[hd, own_slot] = s
        mx_ref[hd] = jnp.maximum(s[:, :MB_DIM], s[:, MB_DIM:])

    for n in range(n_blk - 1):
        @pl.when(n < i)
        def _(n=n):
            dots = [_dot_nt(qx[hd], kx_ref[hd, n * blk:(n + 1) * blk, :]) for hd in heads]
            for hd in heads:
                far_bias = rel_ref[REL_BUCKETS - 1, hd]
                s = dots[hd] * scale + jnp.where(n == i - 1, bias_ref[hd, 1], far_bias)
                s_ref[hd, n] = s
                mx_ref[hd] = jnp.maximum(mx_ref[hd], jnp.maximum(s[:, :MB_DIM], s[:, MB_DIM:]))

    row_max = [jnp.max(mx_ref[hd], axis=-1, keepdims=True) for hd in heads]
    for hd in heads:
        mb_ref[hd] = jnp.broadcast_to(row_max[hd], (blk, 2 * MB_DIM))
    probs = [jnp.exp(s_ref[hd, own_slot] - mb_ref[hd]).astype(BF16) for hd in heads]
    for hd in heads:
        acc_ref[hd] = _dot(probs[hd], vx_ref[hd, pl.ds(own0, blk), :])

    for n in range(n_blk - 1):
        @pl.when(n < i)
        def _(n=n):
            probs = [jnp.exp(s_ref[hd, n] - mb_ref[hd]).astype(BF16) for hd in heads]
            pv = [_dot(probs[hd], vx_ref[hd, n * blk:(n + 1) * blk, :]) for hd in heads]
            for hd in heads:
                acc_ref[hd] += pv[hd]

    for hd in heads:
        acc = acc_ref[hd]
        o_ref[0, :, hsl(hd)] = acc[:, :MB_DIM] / acc[:, MB_DIM:]


def _moba_prompt(h3, rel_table):
    b, t, _ = h3.shape
    assert t % MB_BLOCK == 0
    n_blk = t // MB_BLOCK
    assert n_blk <= GATE_PAD
    qcol, kcol, vcol = COL_MQ // MB_WIDTH, COL_MK // MB_WIDTH, COL_MV // MB_WIDTH
    kern = functools.partial(_moba_prompt_kernel, n_blk=n_blk)
    wide = 2 * MB_DIM
    vmem = (2 * 2 * t * MB_WIDTH * 4 + 2 * MB_HEADS * t * wide * 2
            + MB_HEADS * (n_blk + 4) * MB_BLOCK * wide * 4) // MIB + 12
    return pl.pallas_call(
        kern,
        grid=(b, n_blk),
        in_specs=[pl.BlockSpec(memory_space=pltpu.SMEM),
                  pl.BlockSpec((1, MB_BLOCK, MB_WIDTH), lambda bb, i: (bb, i, qcol)),
                  pl.BlockSpec((1, t, MB_WIDTH), lambda bb, i: (bb, 0, kcol)),
                  pl.BlockSpec((1, t, MB_WIDTH), lambda bb, i: (bb, 0, vcol))],
        out_specs=pl.BlockSpec((1, MB_BLOCK, MB_WIDTH), lambda bb, i: (bb, i, 0)),
        out_shape=jax.ShapeDtypeStruct((b, t, MB_WIDTH), F32),
        scratch_shapes=[pltpu.VMEM((MB_HEADS, t, wide), BF16), pltpu.VMEM((MB_HEADS, t, wide), BF16),
                        pltpu.VMEM((MB_HEADS, GATE_PAD, MB_DIM), F32),
                        pltpu.VMEM((MB_HEADS, 2, MB_BLOCK, MB_BLOCK), F32),
                        pltpu.VMEM((MB_BLOCK, MB_BLOCK), BF16),
                        pltpu.VMEM((MB_HEADS, n_blk, MB_BLOCK, MB_BLOCK), F32),
                        pltpu.VMEM((MB_HEADS, MB_BLOCK, MB_DIM), F32),
                        pltpu.VMEM((MB_HEADS, MB_BLOCK, wide), F32),
                        pltpu.VMEM((MB_HEADS, MB_BLOCK, wide), F32)],
        compiler_params=_params(("arbitrary", "arbitrary"), vmem),
        name="moba_prompt",
    )(rel_table, h3, h3, h3)


def _moba_sample_kernel(pt_ref, rel_ref, q_ref, kn_ref, vn_ref, *refs, n_pages, past_len):
    del pt_ref
    kp = refs[:n_pages]
    vp = refs[n_pages:2 * n_pages]
    o_ref, bias_ref, bias_own_ref = refs[2 * n_pages:]
    t = q_ref.shape[1]
    rows = MB_HEADS * t
    page_rows = PAGE_SIZE * MB_HEADS
    pages_per_blk = MB_BLOCK // PAGE_SIZE
    n_past = past_len // MB_BLOCK
    scale = MB_DIM ** -0.5

    def head_rows(ref):
        x = ref[0]
        return jnp.concatenate([x[:, hd * MB_DIM:(hd + 1) * MB_DIM] for hd in range(MB_HEADS)], axis=0)

    @pl.when(pl.program_id(0) == 0)
    def _():
        tq = lax.broadcasted_iota(I32, (t, page_rows), 0)
        kc = lax.broadcasted_iota(I32, (t, page_rows), 1)
        for p in range(n_pages):
            dist = past_len + tq - (p * PAGE_SIZE + (kc >> 2))
            bias_ref[p] = jnp.concatenate(
                [jnp.where((kc & (MB_HEADS - 1)) == hd, _rel_bias(rel_ref, hd, dist), NEG_INF)
                 for hd in range(MB_HEADS)], axis=0)
        ro = lax.broadcasted_iota(I32, (t, rows), 0)
        co = lax.broadcasted_iota(I32, (t, rows), 1)
        for hd in range(MB_HEADS):
            own = _rel_bias(rel_ref, hd, ro - (co - hd * t))
            keep = (co >= hd * t) & (co <= hd * t + ro)
            bias_own_ref[hd * t:(hd + 1) * t, :] = jnp.where(keep, own, NEG_INF)

    q = head_rows(q_ref)
    qb = q.astype(BF16)
    row_head = lax.broadcasted_iota(I32, (rows, 1), 0) // t

    groups = page_rows // 8
    sums = [jnp.sum(kp[p][0].reshape(groups, 8, MB_DIM), axis=0) for p in range(n_pages)]
    blk_sums = []
    for n in range(n_past):
        acc = sums[n * pages_per_blk]
        for pp in range(1, pages_per_blk):
            acc = acc + sums[n * pages_per_blk + pp]
        blk_sums.append(acc)
    g_full = _dot_nt_f32acc(q, jnp.concatenate(blk_sums, axis=0))
    gc = lax.broadcasted_iota(I32, g_full.shape, 1)
    g_full = jnp.where((gc & (MB_HEADS - 1)) == row_head, g_full, 0.0)
    lane = lax.broadcasted_iota(I32, (rows, GATE_PAD), 1)
    gate = jnp.zeros((rows, GATE_PAD), F32)
    for n in range(n_past):
        g_n = jnp.sum(g_full[:, n * 8:(n + 1) * 8], axis=-1, keepdims=True) * (1.0 / MB_BLOCK)
        gate = jnp.where(lane == n, g_n, gate)
    sel = _topk_blocks(gate, n_past, n_past)

    s_own = _dot_nt(q, head_rows(kn_ref)) * scale + bias_own_ref[...]
    m = jnp.max(s_own, axis=-1, keepdims=True)
    s_past = []
    for p in range(n_pages):
        n = p // pages_per_blk
        s = _dot_nt(qb, kp[p][0].astype(BF16)) * scale + bias_ref[p]
        s = jnp.where(sel[:, n:n + 1] > 0.0, s, NEG_INF)
        s_past.append(s)
        m = jnp.maximum(m, jnp.max(s, axis=-1, keepdims=True))
    p_own = jnp.exp(s_own - m)
    l = jnp.sum(p_own, axis=-1, keepdims=True)
    out = _dot(p_own, head_rows(vn_ref))
    for p in range(n_pages):
        pr = jnp.exp(s_past[p] - m)
        l = l + jnp.sum(pr, axis=-1, keepdims=True)
        out = out + _dot(pr.astype(BF16), vp[p][0].astype(BF16))
    out = out / l
    for hd in range(MB_HEADS):
        o_ref[0, :, hd * MB_DIM:(hd + 1) * MB_DIM] = out[hd * t:(hd + 1) * t]


def _moba_sample(h3, cache_k, cache_v, page_table, rel_table):
    db, t, _ = h3.shape
    n_pages = page_table.shape[1]
    past_len = n_pages * PAGE_SIZE
    assert past_len % MB_BLOCK == 0 and t <= MB_BLOCK and past_len // MB_BLOCK < GATE_PAD
    assert MB_HEADS == 4 and t % 8 == 0
    qcol, kcol, vcol = COL_MQ // MB_WIDTH, COL_MK // MB_WIDTH, COL_MV // MB_WIDTH
    page_rows = PAGE_SIZE * MB_HEADS
    new = lambda k: pl.BlockSpec((1, t, MB_WIDTH), lambda i, pt: (i, 0, k))
    page = lambda p: pl.BlockSpec((1, page_rows, MB_DIM), lambda i, pt: (pt[i, p], 0, 0))
    kern = functools.partial(_moba_sample_kernel, n_pages=n_pages, past_len=past_len)
    grid_spec = pltpu.PrefetchScalarGridSpec(
        num_scalar_prefetch=1,
        grid=(db,),
        in_specs=([pl.BlockSpec(memory_space=pltpu.SMEM), new(qcol), new(kcol), new(vcol)]
                  + [page(p) for p in range(n_pages)] * 2),
        out_specs=pl.BlockSpec((1, t, MB_WIDTH), lambda i, pt: (i, 0, 0)),
        scratch_shapes=[pltpu.VMEM((n_pages, MB_HEADS * t, page_rows), F32),
                        pltpu.VMEM((MB_HEADS * t, MB_HEADS * t), F32)],
    )
    return pl.pallas_call(
        kern,
        grid_spec=grid_spec,
        out_shape=jax.ShapeDtypeStruct((db, t, MB_WIDTH), F32),
        compiler_params=_params(("arbitrary",), 40),
        name="moba_sample",
    )(page_table, rel_table, h3, h3, h3, *([cache_k] * n_pages), *([cache_v] * n_pages))


def _merge_kernel(oa_ref, ob_ref, ga0_ref, ga1_ref, gb0_ref, gb1_ref, x_ref,
                  wa_ref, wb_ref, wo_ref, gx_ref, wq_ref, x1_ref, xq_ref):
    ga = jnp.concatenate([ga0_ref[...], ga1_ref[...]], axis=-1)
    gb = jnp.concatenate([gb0_ref[...], gb1_ref[...]], axis=-1)
    pa = _dot(oa_ref[...].astype(BF16), wa_ref[...])
    pb = _dot(ob_ref[...].astype(BF16), wb_ref[...])
    merged = _sigmoid(ga) * pa + _sigmoid(gb) * pb
    x1 = x_ref[...] + _dot(merged.astype(BF16), wo_ref[...])
    x1_ref[...] = x1
    xq_ref[...] = _dot(_rmsnorm(x1, gx_ref[...]).astype(BF16), wq_ref[...])


def _merge(oa, ob, h2, x, wa, wb, wo, gx, wq, *, tm):
    n = x.shape[0]
    assert n % tm == 0
    half = D_MODEL // 2
    tok = lambda w, k=0: pl.BlockSpec((tm, w), lambda i: (i, k))
    full = lambda a: pl.BlockSpec(a.shape, lambda i: (0, 0))
    gx2 = gx.reshape(1, D_MODEL)
    return pl.pallas_call(
        _merge_kernel,
        grid=(n // tm,),
        in_specs=[tok(HG_WIDTH), tok(MB_WIDTH),
                  tok(half, COL_GA // half), tok(half, COL_GA // half + 1),
                  tok(half, COL_GB // half), tok(half, COL_GB // half + 1),
                  tok(D_MODEL), full(wa), full(wb), full(wo), full(gx2), full(wq)],
        out_specs=[tok(D_MODEL), tok(XA_WIDTH)],
        out_shape=[jax.ShapeDtypeStruct((n, D_MODEL), F32),
                   jax.ShapeDtypeStruct((n, XA_WIDTH), F32)],
        compiler_params=_params(("parallel",), 48),
        name="merge_mix",
    )(oa, ob, h2, h2, h2, h2, x, wa, wb, wo, gx2, wq)


def _mem_attn_kernel(q_ref, k_ref, v_ref, o_ref):
    scale = XA_DIM ** -0.5
    heads = range(XA_HEADS)
    hsl = lambda hd: slice(hd * XA_DIM, (hd + 1) * XA_DIM)
    logits = [_dot_nt(q_ref[0][:, hsl(hd)].astype(BF16), k_ref[0][:, hsl(hd)].astype(BF16)) * scale
              for hd in heads]
    row_max = [jnp.max(s, axis=-1, keepdims=True) for s in logits]
    probs = [jnp.exp(s - m) for s, m in zip(logits, row_max)]
    denom = [jnp.sum(p, axis=-1, keepdims=True) for p in probs]
    outs = [_dot(probs[hd].astype(BF16), v_ref[0][:, hsl(hd)].astype(BF16)) for hd in heads]
    for hd in heads:
        o_ref[0, :, hsl(hd)] = outs[hd] / denom[hd]


def _mem_attn(xq3, mem_kv, *, tq):
    b, t, _ = xq3.shape
    assert t % tq == 0
    mem = lambda k: pl.BlockSpec((1, mem_kv.shape[1], XA_WIDTH), lambda i, j: (i, 0, k))
    qs = pl.BlockSpec((1, tq, XA_WIDTH), lambda i, j: (i, j, 0))
    return pl.pallas_call(
        _mem_attn_kernel,
        grid=(b, t // tq),
        in_specs=[qs, mem(0), mem(1)],
        out_specs=qs,
        out_shape=jax.ShapeDtypeStruct((b, t, XA_WIDTH), F32),
        compiler_params=_params(("parallel", "parallel"), 32),
        name="mem_attn",
    )(xq3, mem_kv, mem_kv)


def _mem_attn_rows_kernel(q_ref, k_ref, v_ref, o_ref):
    bb, t, _ = q_ref.shape
    rows = XA_HEADS * t
    scale = XA_DIM ** -0.5
    row_head = lax.broadcasted_iota(I32, (rows, 1), 0) // t
    kc = lax.broadcasted_iota(I32, (rows, k_ref.shape[1]), 1)
    same_head = (kc & (XA_HEADS - 1)) == row_head
    qs = []
    for i in range(bb):
        x = q_ref[i]
        qs.append(jnp.concatenate([x[:, hd * XA_DIM:(hd + 1) * XA_DIM] for hd in range(XA_HEADS)],
                                  axis=0).astype(BF16))
    logits = [jnp.where(same_head, _dot_nt(qs[i], k_ref[i].astype(BF16)) * scale, NEG_INF)
              for i in range(bb)]
    row_max = [jnp.max(s, axis=-1, keepdims=True) for s in logits]
    probs = [jnp.exp(s - m) for s, m in zip(logits, row_max)]
    denom = [jnp.sum(p, axis=-1, keepdims=True) for p in probs]
    outs = [_dot(probs[i].astype(BF16), v_ref[i].astype(BF16)) / denom[i] for i in range(bb)]
    for i in range(bb):
        for hd in range(XA_HEADS):
            o_ref[i, :, hd * XA_DIM:(hd + 1) * XA_DIM] = outs[i][hd * t:(hd + 1) * t]


def _mem_attn_rows(xq3, mem_k, mem_v, *, bb):
    b, t, _ = xq3.shape
    assert b % bb == 0 and XA_HEADS == 4 and t % 8 == 0
    mem = pl.BlockSpec((bb,) + mem_k.shape[1:], lambda i: (i, 0, 0))
    qs = pl.BlockSpec((bb, t, XA_WIDTH), lambda i: (i, 0, 0))
    return pl.pallas_call(
        _mem_attn_rows_kernel,
        grid=(b // bb,),
        in_specs=[qs, mem, mem],
        out_specs=qs,
        out_shape=jax.ShapeDtypeStruct((b, t, XA_WIDTH), F32),
        compiler_params=_params(("parallel",), 32),
        name="mem_attn_rows",
    )(xq3, mem_k, mem_v)


def _route(logits):
    lane = lax.broadcasted_iota(I32, logits.shape, 1).astype(F32)
    first = lambda hit: jnp.min(jnp.where(hit, lane, float(ROUTER_LANES)), axis=-1, keepdims=True)
    gl = jnp.where(lane < N_GROUPS, logits, NEG_INF)
    gmax = jnp.max(gl, axis=-1, keepdims=True)
    grp = first(gl == gmax)
    g_prob = 1.0 / jnp.sum(jnp.exp(gl - gmax), axis=-1, keepdims=True)
    e_lo = EXPERT_LANE0 + grp * EXPERTS_PER_GROUP
    in_grp = (lane >= e_lo) & (lane < e_lo + EXPERTS_PER_GROUP)
    el = jnp.where(in_grp, logits, NEG_INF)
    top1 = jnp.max(el, axis=-1, keepdims=True)
    idx1 = first(el == top1)
    el2 = jnp.where(lane == idx1, NEG_INF, el)
    top2 = jnp.max(el2, axis=-1, keepdims=True)
    idx2 = first(el2 == top2)
    e2 = jnp.exp(top2 - top1)
    w1 = g_prob / (1.0 + e2)
    w2 = w1 * e2
    comb = jnp.where(lane == idx1, w1, 0.0) + jnp.where(lane == idx2, w2, 0.0)
    return jnp.where(lane == GRP_LANE, grp, comb)


def _xo_route_kernel(x1_ref, at_ref, wxo_ref, gf_ref, wr_ref, br_ref, x2_ref, xf_ref, comb_ref):
    x2 = x1_ref[...] + _dot(at_ref[...].astype(BF16), wxo_ref[...])
    x2_ref[...] = x2
    xf = _rmsnorm(x2, gf_ref[...])
    xf_ref[...] = xf.astype(BF16)
    comb_ref[...] = _route(_dot_f32acc(xf, wr_ref[...]) + br_ref[...])


def _xo_route(x1, attn, wxo, g_ffn, w_router_t, b_router_t, *, tm):
    n = x1.shape[0]
    assert n % tm == 0
    tok = lambda w: pl.BlockSpec((tm, w), lambda i: (i, 0))
    full = lambda a: pl.BlockSpec(a.shape, lambda i: (0, 0))
    gf2 = g_ffn.reshape(1, D_MODEL)
    return pl.pallas_call(
        _xo_route_kernel,
        grid=(n // tm,),
        in_specs=[tok(D_MODEL), tok(XA_WIDTH), full(wxo), full(gf2), full(w_router_t), full(b_router_t)],
        out_specs=[tok(D_MODEL), tok(D_MODEL), tok(ROUTER_LANES)],
        out_shape=[jax.ShapeDtypeStruct((n, D_MODEL), F32), jax.ShapeDtypeStruct((n, D_MODEL), BF16),
                   jax.ShapeDtypeStruct((n, ROUTER_LANES), F32)],
        compiler_params=_params(("parallel",), 40),
        name="xo_route",
    )(x1, attn, wxo, gf2, w_router_t, b_router_t)


def _moe_kernel(xf_ref, comb_ref, x2_ref, wg_ref, wu_ref, wd_ref, gn_ref, y_ref,
                tri_ref, posc_ref, cnt_ref, xs_ref, ws_ref, ys_ref, *, r):
    i = pl.program_id(0)
    step = pl.program_id(1)
    tm = xf_ref.shape[0]
    per_step = wg_ref.shape[0]
    steps_per_group = EXPERTS_PER_GROUP // per_step
    g = step // steps_per_group
    lane = lax.broadcasted_iota(I32, (tm, ROUTER_LANES), 1)

    @pl.when((i == 0) & (step == 0))
    def _():
        rr = lax.broadcasted_iota(I32, (tm, tm), 0)
        cc = lax.broadcasted_iota(I32, (tm, tm), 1)
        tri_ref[...] = jnp.where(cc < rr, 1.0, 0.0).astype(BF16)

    @pl.when(step == 0)
    def _():
        comb = comb_ref[...]
        grp = comb[:, GRP_LANE:GRP_LANE + 1]
        lane_f = lane.astype(F32)
        rel = lane - EXPERT_LANE0
        used = (comb > 0.0) & (rel >= 0) & (rel < N_EXPERTS)
        second = (rel & (EXPERTS_PER_GROUP - 1)) >= EXPERTS_PER_GROUP // 2
        uses_a = jnp.max(jnp.where(used & jnp.logical_not(second), 1.0, 0.0), axis=-1, keepdims=True)
        uses_b = jnp.max(jnp.where(used & second, 1.0, 0.0), axis=-1, keepdims=True)
        key = grp * MOE_CLASSES + jnp.where(uses_b == 0.0, 0.0, jnp.where(uses_a == 0.0, 2.0, 1.0))
        onehot = jnp.where((lane_f == key) & (lane < N_GROUPS * MOE_CLASSES), 1.0, 0.0)
        before = _dot(tri_ref[...], onehot.astype(BF16))
        cnt = jnp.sum(onehot, axis=0, keepdims=True)
        cnt_ref[...] = cnt
        pos = jnp.sum(jnp.where(lane_f < key, cnt, 0.0) + onehot * before, axis=-1, keepdims=True)
        posc_ref[...] = jnp.where(lane == 0, pos, 0.0)
        hi = jnp.floor(pos * (1.0 / POS_SPLIT))
        lo = pos - hi * POS_SPLIT
        cols = jnp.where(lane == 0, hi, jnp.where(lane == 1, lo, 0.0))
        er = lax.broadcasted_iota(I32, (16, ROUTER_LANES), 0)
        ec = lax.broadcasted_iota(I32, (16, ROUTER_LANES), 1)
        post = _dot_nt(jnp.where(er == ec, 1.0, 0.0).astype(BF16), cols.astype(BF16))
        pos_row = post[0:1, :] * POS_SPLIT + post[1:2, :]
        slot = lax.broadcasted_iota(I32, (tm, tm), 0).astype(F32)
        perm = jnp.where(pos_row == slot, 1.0, 0.0).astype(BF16)
        comb_hi, comb_lo = _split2(comb)
        xs_ref[0:tm, :] = _dot(perm, xf_ref[...]).astype(BF16)
        ws_ref[0:tm, :] = _dot(perm, comb_hi) + _dot(perm, comb_lo)
        xs_ref[tm:, :] = jnp.zeros((r, D_MODEL), BF16)
        ws_ref[tm:, :] = jnp.zeros((r, ROUTER_LANES), F32)
        ys_ref[...] = jnp.zeros(ys_ref.shape, F32)

    lane_row = lax.broadcasted_iota(I32, (1, ROUTER_LANES), 1)
    key0 = g * MOE_CLASSES + step % steps_per_group
    first = jnp.sum(jnp.where(lane_row < key0, cnt_ref[...], 0.0)).astype(I32)
    count = jnp.sum(jnp.where((lane_row == key0) | (lane_row == key0 + 1), cnt_ref[...], 0.0)).astype(I32)
    start = (first // 16) * 16
    n_windows = (first + count - start + (r - 1)) // r

    def experts(c, carry):
        rows = pl.ds(pl.multiple_of(start + c * r, 16), r)
        xg = xs_ref[rows, :]
        wq = ws_ref[rows, :]
        wl = lax.broadcasted_iota(I32, (r, ROUTER_LANES), 1)
        ups = [(_dot(xg, wg_ref[k]), _dot(xg, wu_ref[k])) for k in range(per_step)]
        hidden = []
        for k, (gate, up) in enumerate(ups):
            w_e = jnp.sum(jnp.where(wl == EXPERT_LANE0 + step * per_step + k, wq, 0.0), axis=-1, keepdims=True)
            hidden.append((_silu(gate) * up * w_e).astype(BF16))
        w_down = wd_ref[...].reshape(per_step * EXPERT_HIDDEN, D_MODEL)
        ys_ref[rows, :] += _dot(jnp.concatenate(hidden, axis=-1), w_down)
        return carry

    lax.fori_loop(0, n_windows, experts, 0)

    @pl.when(step == pl.num_programs(1) - 1)
    def _():
        slot = lax.broadcasted_iota(I32, (tm, tm), 1).astype(F32)
        place = jnp.where(posc_ref[:, 0:1] == slot, 1.0, 0.0).astype(BF16)
        moe = _dot(place, ys_ref[0:tm, :].astype(BF16))
        y_ref[...] = _rmsnorm(x2_ref[...] + moe, gn_ref[...])


def _moe(xf, comb, x2, wg, wu, wd, g_final, *, tm):
    n = xf.shape[0]
    assert n % tm == 0 and tm % 16 == 0
    assert 2 * MOE_EXPERTS_PER_STEP == EXPERTS_PER_GROUP and N_GROUPS * MOE_CLASSES <= ROUTER_LANES
    r = min(MOE_CHUNK_ROWS, tm)
    cap = tm + r
    per_step = MOE_EXPERTS_PER_STEP
    tok = lambda w: pl.BlockSpec((tm, w), lambda i, s: (i, 0))
    gn2 = g_final.reshape(1, D_MODEL)
    kern = functools.partial(_moe_kernel, r=r)
    vmem = (2 * tm * D_MODEL * (2 + 4 + 4) + 2 * tm * ROUTER_LANES * 4
            + 2 * 3 * per_step * D_MODEL * EXPERT_HIDDEN * 2 + tm * tm * 2
            + cap * D_MODEL * (2 + 4) + tm * D_MODEL * 4) // MIB + 8
    return pl.pallas_call(
        kern,
        grid=(n // tm, N_EXPERTS // per_step),
        in_specs=[tok(D_MODEL), tok(ROUTER_LANES), tok(D_MODEL),
                  pl.BlockSpec((per_step, D_MODEL, EXPERT_HIDDEN), lambda i, s: (s, 0, 0)),
                  pl.BlockSpec((per_step, D_MODEL, EXPERT_HIDDEN), lambda i, s: (s, 0, 0)),
                  pl.BlockSpec((per_step, EXPERT_HIDDEN, D_MODEL), lambda i, s: (s, 0, 0)),
                  pl.BlockSpec((1, D_MODEL), lambda i, s: (0, 0))],
        out_specs=tok(D_MODEL),
        out_shape=jax.ShapeDtypeStruct((n, D_MODEL), F32),
        scratch_shapes=[pltpu.VMEM((tm, tm), BF16),
                        pltpu.VMEM((tm, ROUTER_LANES), F32), pltpu.VMEM((1, ROUTER_LANES), F32),
                        pltpu.VMEM((cap, D_MODEL), BF16), pltpu.VMEM((cap, ROUTER_LANES), F32),
                        pltpu.VMEM((cap, D_MODEL), F32)],
        compiler_params=_params(("arbitrary", "arbitrary"), vmem),
        name="moe_grouped",
    )(xf, comb, x2, wg, wu, wd, gn2)


def _token_tile(n, cap=512):
    return cap if n % cap == 0 else n


def _col_tile(n, cap=1536):
    return max(c for c in range(V7X_LANES, cap + 1, V7X_LANES) if n % c == 0)


def _layer(x3, w, s0, moba_fn, mem_fn):
    b, t, _ = x3.shape
    n = b * t
    x2d = x3.reshape(n, D_MODEL)
    tm = _token_tile(n)
    h2, k4, v4 = _norm_matmul(x2d, w["norm_mix"], w["w_in"], tm=_token_tile(n, 1024),
                              tn=_col_tile(IN_COLS), head_cols=(COL_MK, COL_MV), heads=MB_HEADS,
                              head_dim=MB_DIM)
    h3 = h2.reshape(b, t, IN_COLS)
    oa, s_new = _hgrn(h3, w["hg_lb_logits"], w["hg_norm"], s0)
    ob = moba_fn(h3)
    x1, xq = _merge(oa.reshape(n, HG_WIDTH), ob.reshape(n, MB_WIDTH), h2, x2d,
                    w["w_branch_a"], w["w_branch_b"], w["w_mix_out"], w["norm_xattn"], w["w_xq"], tm=tm)
    attn = mem_fn(xq.reshape(b, t, XA_WIDTH))
    x2, xf, comb = _xo_route(x1, attn.reshape(n, XA_WIDTH), w["w_xo"], w["norm_ffn"],
                             w["w_router"], w["b_router"], tm=tm)
    y = _moe(xf, comb, x2, w["w_expert_gate"], w["w_expert_up"], w["w_expert_down"],
             w["norm_final"], tm=_token_tile(n, 1024))
    k_new = k4.reshape(b, t, MB_HEADS, MB_DIM)
    v_new = v4.reshape(b, t, MB_HEADS, MB_DIM)
    return y.reshape(b, t, D_MODEL), k_new, v_new, s_new


def kernel(x_prompt, x_sample, cache_k, cache_v, state_hgrn, cache_mem_k, cache_mem_v, page_table, mem_prompt, norm_mix, w_in, hg_lb_logits, hg_norm, w_branch_a, w_branch_b, w_mix_out, rel_table, norm_xattn, norm_mem, w_xq, w_xk, w_xv, w_xo, norm_ffn, w_group_router, b_group_router, w_expert_router, b_expert_router, w_expert_gate, w_expert_up, w_expert_down, norm_final):
    assert w_in.shape[0] == DEPTH == 1 and hg_lb_logits.shape[0] == DEPTH + 1
    b = x_prompt.shape[0]
    db = x_sample.shape[0]
    n_pool = cache_k.shape[1]
    pad_lanes = ROUTER_LANES - EXPERT_LANE0 - N_EXPERTS
    w = {
        "norm_mix": norm_mix[0], "w_in": w_in[0].astype(BF16),
        "hg_lb_logits": hg_lb_logits, "hg_norm": hg_norm[0],
        "w_branch_a": w_branch_a[0].astype(BF16), "w_branch_b": w_branch_b[0].astype(BF16),
        "w_mix_out": w_mix_out[0].astype(BF16), "norm_xattn": norm_xattn[0],
        "w_xq": w_xq[0].astype(BF16), "w_xo": w_xo[0].astype(BF16), "norm_ffn": norm_ffn[0],
        "w_router": jnp.pad(jnp.concatenate(
            [w_group_router[0], jnp.zeros((D_MODEL, EXPERT_LANE0 - N_GROUPS), F32), w_expert_router[0]],
            axis=1), ((0, 0), (0, pad_lanes))),
        "b_router": jnp.pad(jnp.concatenate(
            [b_group_router[0], jnp.zeros((EXPERT_LANE0 - N_GROUPS,), F32), b_expert_router[0]]),
            (0, pad_lanes)).reshape(1, ROUTER_LANES),
        "w_expert_gate": w_expert_gate[0].reshape(N_EXPERTS, D_MODEL, EXPERT_HIDDEN).astype(BF16),
        "w_expert_up": w_expert_up[0].reshape(N_EXPERTS, D_MODEL, EXPERT_HIDDEN).astype(BF16),
        "w_expert_down": w_expert_down[0].reshape(N_EXPERTS, EXPERT_HIDDEN, D_MODEL).astype(BF16),
        "norm_final": norm_final,
    }

    w_mem = jnp.concatenate([w_xk[0], w_xv[0]], axis=1).astype(BF16)
    mem_kv, mk_p, mv_p = _norm_matmul(mem_prompt.reshape(b * MEM_LEN, D_MODEL), norm_mem[0], w_mem,
                                      tm=_token_tile(b * MEM_LEN), tn=512, head_cols=(0, XA_WIDTH),
                                      heads=XA_HEADS, head_dim=XA_DIM)
    mem_kv = mem_kv.reshape(b, MEM_LEN, 2 * XA_WIDTH)
    s0 = jnp.zeros((b, HG_HEADS, HG_DIM, HG_DIM), F32)
    y_p, k_p, v_p, s_p = _layer(x_prompt, w, s0,
                                functools.partial(_moba_prompt, rel_table=rel_table),
                                functools.partial(_mem_attn, mem_kv=mem_kv, tq=512))

    ck = cache_k[0].reshape(n_pool, PAGE_SIZE * MB_HEADS, MB_DIM)
    cv = cache_v[0].reshape(n_pool, PAGE_SIZE * MB_HEADS, MB_DIM)
    moba_s = functools.partial(_moba_sample, cache_k=ck, cache_v=cv, page_table=page_table,
                               rel_table=rel_table)
    mem_s = functools.partial(_mem_attn_rows, bb=4,
                              mem_k=cache_mem_k[0].reshape(db, MEM_LEN * XA_HEADS, XA_DIM),
                              mem_v=cache_mem_v[0].reshape(db, MEM_LEN * XA_HEADS, XA_DIM))
    y_s, k_s, v_s, s_s = _layer(x_sample, w, state_hgrn[0], moba_s, mem_s)

    heads = lambda a: a.reshape(b, MEM_LEN, XA_HEADS, XA_DIM)[None]
    return (y_p, y_s, k_p[None], v_p[None], s_p[None], heads(mk_p), heads(mv_p),
            k_s[None], v_s[None], s_s[None])
```

```python
import functools
import math

import numpy as np
import jax
import jax.numpy as jnp
from jax import lax
from jax.experimental import pallas as pl
from jax.experimental.pallas import tpu as pltpu

F32 = jnp.float32
BF16 = jnp.bfloat16
I32 = jnp.int32

D_MODEL = 1024
DEPTH = 1
PAGE_SIZE = 128
HG_HEADS = 4
HG_DIM = 128
HG_WIDTH = HG_HEADS * HG_DIM
MB_HEADS = 4
MB_DIM = 128
MB_WIDTH = MB_HEADS * MB_DIM
MB_BLOCK = 256
MB_TOPK = 3
REL_BUCKETS = 32
REL_MAX_DIST = 128
REL_MAX_EXACT = REL_BUCKETS // 2
MEM_LEN = 256
XA_HEADS = 4
XA_DIM = 128
XA_WIDTH = XA_HEADS * XA_DIM
N_GROUPS = 4
EXPERTS_PER_GROUP = 8
N_EXPERTS = N_GROUPS * EXPERTS_PER_GROUP
EXPERT_TOPK = 2
EXPERT_HIDDEN = 256
NORM_EPS = 1e-6
IN_COLS = 4 * HG_WIDTH + 3 * MB_WIDTH + 2 * D_MODEL
COL_MQ = 4 * HG_WIDTH
COL_MK = COL_MQ + MB_WIDTH
COL_MV = COL_MK + MB_WIDTH
COL_GA = COL_MV + MB_WIDTH
COL_GB = COL_GA + D_MODEL

V7X_LANES = 128
V7X_VMEM_BYTES = 64 * 1024 * 1024
MIB = 1024 * 1024

NEG_INF = float("-inf")
MASK_BIG = -1e30
GATE_PAD = 16
ROUTER_LANES = 128
GRP_LANE = 0
EXPERT_LANE0 = 8
POS_SPLIT = 32.0
MOE_CHUNK_ROWS = 256
MOE_EXPERTS_PER_STEP = EXPERTS_PER_GROUP // 2
MOE_CLASSES = 3
HGRN_PROMPT_CHUNK = 128
HGRN_PROMPT_ROWS = 4
HGRN_MIN_CHUNK = 16
HGRN_CHAINS = 16


def _params(semantics, vmem_mib):
    return pltpu.CompilerParams(dimension_semantics=semantics,
                                vmem_limit_bytes=min(vmem_mib * MIB, V7X_VMEM_BYTES - 8 * MIB))


def _dot(a, b):
    return jnp.dot(a, b, preferred_element_type=F32)


def _dot_nt(a, b):
    return lax.dot_general(a, b, (((1,), (1,)), ((), ())), preferred_element_type=F32)


def _dot_tn(a, b):
    return lax.dot_general(a, b, (((0,), (0,)), ((), ())), preferred_element_type=F32)


def _split2(a):
    hi = a.astype(BF16)
    lo = (a - hi.astype(F32)).astype(BF16)
    return hi, lo


def _dot_nt_f32acc(a, b):
    ah, al = _split2(a)
    bh, bl = _split2(b)
    return _dot_nt(ah, bh) + (_dot_nt(ah, bl) + _dot_nt(al, bh))


def _dot_f32acc(a, b):
    ah, al = _split2(a)
    bh, bl = _split2(b)
    return _dot(ah, bh) + (_dot(ah, bl) + _dot(al, bh))


def _operand_dtype(t):
    return BF16 if t % 16 == 0 else F32


def _rmsnorm(x, g):
    return x * lax.rsqrt(jnp.mean(x * x, axis=-1, keepdims=True) + NORM_EPS) * g


def _sigmoid(x):
    return 1.0 / (1.0 + jnp.exp(-x))


def _silu(x):
    return x * _sigmoid(x)


def _norm_matmul_kernel(x_ref, g_ref, w_ref, o_ref, *refs, head_cols, heads, head_dim):
    head_refs, xn_ref = refs[:-1], refs[-1]
    j = pl.program_id(1)
    tm, tn = o_ref.shape

    @pl.when(j == 0)
    def _():
        xn_ref[...] = _rmsnorm(x_ref[...], g_ref[...]).astype(BF16)

    res = _dot(xn_ref[...], w_ref[...])
    o_ref[...] = res
    for col0, href in zip(head_cols, head_refs):
        for hd in range(heads):
            tile, local = divmod(col0 + hd * head_dim, tn)

            @pl.when(j == tile)
            def _(hd=hd, local=local, href=href):
                href[pl.ds(hd, tm, stride=heads), :] = res[:, local:local + head_dim]


def _norm_matmul(x, g, w_bf16, *, tm, tn, head_cols=(), heads=1, head_dim=V7X_LANES):
    m, d = x.shape
    n = w_bf16.shape[1]
    assert m % tm == 0 and n % tn == 0 and tn % head_dim == 0
    vmem = (2 * tm * d * 4 + tm * d * 2 + 2 * d * tn * 2 + 2 * tm * tn * 4
            + len(head_cols) * 2 * tm * heads * head_dim * 4) // MIB + 12
    kern = functools.partial(_norm_matmul_kernel, head_cols=tuple(head_cols), heads=heads,
                             head_dim=head_dim)
    head_spec = pl.BlockSpec((tm * heads, head_dim), lambda i, j: (i, 0))
    head_shape = jax.ShapeDtypeStruct((m * heads, head_dim), F32)
    return pl.pallas_call(
        kern,
        grid=(m // tm, n // tn),
        in_specs=[pl.BlockSpec((tm, d), lambda i, j: (i, 0)),
                  pl.BlockSpec((1, d), lambda i, j: (0, 0)),
                  pl.BlockSpec((d, tn), lambda i, j: (0, j))],
        out_specs=[pl.BlockSpec((tm, tn), lambda i, j: (i, j))] + [head_spec] * len(head_cols),
        out_shape=[jax.ShapeDtypeStruct((m, n), F32)] + [head_shape] * len(head_cols),
        scratch_shapes=[pltpu.VMEM((tm, d), BF16)],
        compiler_params=_params(("parallel", "arbitrary"), vmem),
        name="norm_matmul",
    )(x, g.reshape(1, d), w_bf16)


def _hgrn_levels(c):
    return int(round(math.log2(c)))


def _hgrn_sum_masks(c):
    t = np.arange(c)[:, None]
    j = np.arange(c)[None, :]
    rows = [j <= t, j > t]
    for lv in range(_hgrn_levels(c)):
        half = c >> (lv + 1)
        blk = 2 * half
        mid = (t // blk) * blk + half - 1
        upper = (t % blk) >= half
        rows.append((upper & (j > mid) & (j <= t)) | ((~upper) & (j > t) & (j <= mid)))
    return np.concatenate(rows, axis=0).astype(np.float32)


def _hgrn_kernel(hq_ref, hf_ref, hi_ref, hg_ref, lbl_ref, gn_ref, mask_ref, s0_ref,
                 o_ref, s_ref, st_ref, *, c, t_blk, bb):
    j = pl.program_id(1)
    chains = [(bi, hd) for bi in range(bb) for hd in range(HG_HEADS)]

    @pl.when(j == 0)
    def _():
        for bi, hd in chains:
            st_ref[bi * HG_HEADS + hd] = s0_ref[bi, hd].T

    lbl = lbl_ref[...]
    lmax = jnp.max(lbl, axis=0, keepdims=True)
    lexp = jnp.exp(lbl - lmax)
    lb_all = lexp[0:1, :] / jnp.sum(lexp, axis=0, keepdims=True)

    masks = mask_ref[...]
    pad = c - t_blk
    row = lax.broadcasted_iota(I32, (c, HG_DIM), 0)
    rr = lax.broadcasted_iota(I32, (c, c), 0)
    cc = lax.broadcasted_iota(I32, (c, c), 1)

    def padded(a):
        if pad == 0:
            return a
        return jnp.concatenate([a, jnp.zeros((pad, HG_DIM), F32)], axis=0)

    hsl = lambda hd: slice(hd * HG_DIM, (hd + 1) * HG_DIM)
    qs, ks, vs, logf_parts = [], [], [], []
    for bi, hd in chains:
        xq = hq_ref[bi][:, hsl(hd)]
        lb = lb_all[:, hsl(hd)]
        f = lb + (1.0 - lb) * _sigmoid(hf_ref[bi][:, hsl(hd)])
        qs.append(padded(_silu(xq)))
        ks.append(padded(1.0 - f))
        vs.append(padded(hi_ref[bi][:, hsl(hd)]))
        logf_parts.append(_split2(padded(jnp.log(f))))
    exs = []
    for n in range(0, len(chains), 2):
        (hi_a, lo_a), (hi_b, lo_b) = logf_parts[n], logf_parts[n + 1]
        stacked = jnp.concatenate([jnp.concatenate([hi_a, hi_b], axis=-1),
                                   jnp.concatenate([lo_a, lo_b], axis=-1)], axis=0)
        ex = jnp.exp(_dot(masks, stacked))
        exs += [ex[:, :HG_DIM], ex[:, HG_DIM:]]

    level_dots = []
    for q, kk, ex in zip(qs, ks, exs):
        per_level = []
        for lv in range(_hgrn_levels(c)):
            half = c >> (lv + 1)
            upper = (row & half) != 0
            ex_lv = ex[(2 + lv) * c:(3 + lv) * c]
            a = jnp.where(upper, q * ex_lv, 0.0).astype(BF16)
            bm = jnp.where(upper, 0.0, kk * ex_lv).astype(BF16)
            per_level.append(_dot_nt(a, bm))
        level_dots.append(per_level)

    outs = []
    for n, (bi, hd) in enumerate(chains):
        q, kk, v, ex = qs[n], ks[n], vs[n], exs[n]
        scores = jnp.zeros((c, c), F32)
        for lv, d in enumerate(level_dots[n]):
            scores = scores + jnp.where((rr ^ cc) < 2 * (c >> (lv + 1)), d, 0.0)
        eb = ex[0:c]
        eb_rev = ex[c:2 * c]
        diag = jnp.sum(q * kk, axis=-1, keepdims=True)
        st = st_ref[bi * HG_HEADS + hd]
        vb = v.astype(BF16)
        o = (_dot_nt((q * eb).astype(BF16), st.astype(BF16))
             + _dot(scores.astype(BF16), vb) + diag * v)
        st_ref[bi * HG_HEADS + hd] = st * eb[c - 1:c, :] + _dot_tn(vb, (kk * eb_rev).astype(BF16))
        outs.append(o[0:t_blk])

    for (bi, hd), o in zip(chains, outs):
        xg = hg_ref[bi][:, hsl(hd)]
        o_ref[bi, :, hsl(hd)] = (_rmsnorm(o, gn_ref[:, hsl(hd)]) * _silu(xg)).astype(o_ref.dtype)

    @pl.when(j == pl.num_programs(1) - 1)
    def _():
        for bi, hd in chains:
            s_ref[bi, hd] = st_ref[bi * HG_HEADS + hd].T


def _hgrn(h3, lb_logits, hg_norm, s0):
    b, t, _ = h3.shape
    t_blk = math.gcd(t, HGRN_PROMPT_CHUNK)
    c = max(t_blk, HGRN_MIN_CHUNK)
    bb = math.gcd(b, HGRN_CHAINS // HG_HEADS) if t == t_blk else math.gcd(b, HGRN_PROMPT_ROWS)
    m01 = _hgrn_sum_masks(c)
    masks = jnp.asarray(np.concatenate([m01, m01], axis=1), BF16)
    col = lambda k: pl.BlockSpec((bb, t_blk, HG_WIDTH), lambda i, j: (i, j, k))
    state_spec = pl.BlockSpec((bb, HG_HEADS, HG_DIM, HG_DIM), lambda i, j: (i, 0, 0, 0))
    kern = functools.partial(_hgrn_kernel, c=c, t_blk=t_blk, bb=bb)
    return pl.pallas_call(
        kern,
        grid=(b // bb, t // t_blk),
        in_specs=[col(0), col(1), col(2), col(3),
                  pl.BlockSpec(lb_logits.shape, lambda i, j: (0, 0)),
                  pl.BlockSpec((1, HG_WIDTH), lambda i, j: (0, 0)),
                  pl.BlockSpec(masks.shape, lambda i, j: (0, 0)),
                  state_spec],
        out_specs=[pl.BlockSpec((bb, t_blk, HG_WIDTH), lambda i, j: (i, j, 0)), state_spec],
        out_shape=[jax.ShapeDtypeStruct((b, t, HG_WIDTH), _operand_dtype(t)),
                   jax.ShapeDtypeStruct(s0.shape, F32)],
        scratch_shapes=[pltpu.VMEM((bb * HG_HEADS, HG_DIM, HG_DIM), F32)],
        compiler_params=_params(("parallel", "arbitrary"), 32),
        name="hgrn2",
    )(h3, h3, h3, h3, lb_logits, hg_norm.reshape(1, HG_WIDTH), masks, s0)


def _rel_bucket_edges():
    d = np.arange(2 * REL_MAX_DIST)
    df = np.maximum(d, 1).astype(np.float32)
    large = REL_MAX_EXACT + (np.log(df / np.float32(REL_MAX_EXACT))
                             / np.float32(math.log(REL_MAX_DIST / REL_MAX_EXACT))
                             * np.float32(REL_BUCKETS - REL_MAX_EXACT)).astype(np.int32)
    bucket = np.where(d < REL_MAX_EXACT, d, np.minimum(large, REL_BUCKETS - 1))
    assert np.all(np.diff(bucket) >= 0) and bucket[-1] == REL_BUCKETS - 1
    return [int(np.argmax(bucket >= kb)) for kb in range(REL_BUCKETS)]


def _rel_bias(rel_ref, head, dist):
    edges = _rel_bucket_edges()
    out = jnp.full(dist.shape, rel_ref[0, head], F32)
    for kb in range(1, REL_BUCKETS):
        out = jnp.where(dist >= edges[kb], rel_ref[kb, head], out)
    return out


def _stack_rows(rows):
    ridx = lax.broadcasted_iota(I32, (GATE_PAD, MB_DIM), 0)
    out = jnp.zeros((GATE_PAD, MB_DIM), F32)
    for n, r in enumerate(rows):
        out = jnp.where(ridx == n, r, out)
    return out


def _topk_blocks(gate, n_cand, n_past):
    lane = lax.broadcasted_iota(I32, gate.shape, 1)
    past = lane < n_past
    g = jnp.where(past, gate, NEG_INF)
    rank = jnp.zeros(gate.shape, I32)
    for m in range(n_cand):
        gm = g[:, m:m + 1]
        ahead = jnp.where(gm > g, 1, jnp.where((gm == g) & (lane > m), 1, 0))
        rank = rank + ahead
    return jnp.where(past & (rank < MB_TOPK), 1.0, 0.0)


def _topk_blocks_t(gate_t, n_cand, n_past):
    blk_id = lax.broadcasted_iota(I32, gate_t.shape, 0)
    past = blk_id < n_past
    g = jnp.where(past, gate_t, NEG_INF)
    rank = jnp.zeros(gate_t.shape, I32)
    for m in range(n_cand):
        gm = g[m:m + 1, :]
        rank = rank + jnp.where(gm > g, 1, jnp.where((gm == g) & (blk_id > m), 1, 0))
    return past & (rank < MB_TOPK)


def _moba_prompt_kernel(rel_ref, q_ref, k_ref, v_ref, o_ref,
                        kx_ref, vx_ref, km_ref, bias_ref, eye_ref, s_ref, mx_ref, mb_ref, acc_ref, *, n_blk):
    b = pl.program_id(0)
    i = pl.program_id(1)
    blk = MB_BLOCK
    scale = MB_DIM ** -0.5
    heads = range(MB_HEADS)
    hsl = lambda hd: slice(hd * MB_DIM, (hd + 1) * MB_DIM)

    t = k_ref.shape[1]
    own_slot = n_blk - 1

    @pl.when((b == 0) & (i == 0))
    def _():
        r = lax.broadcasted_iota(I32, (blk, blk), 0)
        c = lax.broadcasted_iota(I32, (blk, blk), 1)
        eye_ref[...] = jnp.where(r == c, 1.0, 0.0).astype(BF16)
        kr = lax.broadcasted_iota(I32, (t, MB_DIM), 0)
        kc = lax.broadcasted_iota(I32, (t, MB_DIM), 1)
        blk_onehot = jnp.where(kr // blk == kc, 1.0, 0.0).astype(BF16)
        for hd in heads:
            bias_ref[hd, 0] = jnp.where(c <= r, _rel_bias(rel_ref, hd, r - c), NEG_INF)
            bias_ref[hd, 1] = _rel_bias(rel_ref, hd, r - c + blk)
            kx_ref[hd, :, MB_DIM:] = blk_onehot
            vx_ref[hd, :, MB_DIM:] = jnp.ones((t, MB_DIM), BF16)

    @pl.when(i == 0)
    def _():
        means = [jnp.sum(k_ref[0, n * blk:(n + 1) * blk, :], axis=0, keepdims=True) * (1.0 / blk)
                 for n in range(n_blk)]
        for hd in heads:
            kx_ref[hd, :, :MB_DIM] = k_ref[0][:, hsl(hd)].astype(BF16)
            vx_ref[hd, :, :MB_DIM] = v_ref[0][:, hsl(hd)].astype(BF16)
            km_ref[hd] = _stack_rows([mn[:, hsl(hd)] for mn in means])

    q = q_ref[0]
    blk_id = lax.broadcasted_iota(I32, (GATE_PAD, blk), 0)
    qh = [q[:, hsl(hd)] for hd in heads]
    gates = [_dot_nt_f32acc(km_ref[hd], qh[hd]) for hd in heads]
    qm_t = []
    for hd in heads:
        sel_t = _topk_blocks_t(gates[hd], n_blk, i)
        m_t = jnp.where(sel_t | (blk_id >= i), 0.0, MASK_BIG).astype(BF16)
        qm_t.append(jnp.concatenate([m_t, jnp.zeros((MB_DIM - GATE_PAD, blk), BF16)], axis=0))
    qmask = [_dot_nt(eye_ref[...], qm_t[hd]) for hd in heads]
    qx = [jnp.concatenate([qh[hd].astype(BF16), qmask[hd].astype(BF16)], axis=-1) for hd in heads]

    own0 = pl.multiple_of(i * blk, blk)
    dots = [_dot_nt(qx[hd], kx_ref[hd, pl.ds(own0, blk), :]) for hd in heads]
    for hd in heads:
        s = dots[hd] * scale + bias_ref[hd, 0]
        s_ref[hd, own_slot] = s
        mx_ref[hd] = jnp.maximum(s[:, :MB_DIM], s[:, MB_DIM:])

    for n in range(n_blk - 1):
        @pl.when(n < i)
        def _(n=n):
            dots = [_dot_nt(qx[hd], kx_ref[hd, n * blk:(n + 1) * blk, :]) for hd in heads]
            for hd in heads:
                far_bias = rel_ref[REL_BUCKETS - 1, hd]
                s = dots[hd] * scale + jnp.where(n == i - 1, bias_ref[hd, 1], far_bias)
                s_ref[hd, n] = s
                mx_ref[hd] = jnp.maximum(mx_ref[hd], jnp.maximum(s[:, :MB_DIM], s[:, MB_DIM:]))

    row_max = [jnp.max(mx_ref[hd], axis=-1, keepdims=True) for hd in heads]
    for hd in heads:
        mb_ref[hd] = jnp.broadcast_to(row_max[hd], (blk, 2 * MB_DIM))
    probs = [jnp.exp(s_ref[hd, own_slot] - mb_ref[hd]).astype(BF16) for hd in heads]
    for hd in heads:
        acc_ref[hd] = _dot(probs[hd], vx_ref[hd, pl.ds(own0, blk), :])

    for n in range(n_blk - 1):
        @pl.when(n < i)
        def _(n=n):
            probs = [jnp.exp(s_ref[hd, n] - mb_ref[hd]).astype(BF16) for hd in heads]
            pv = [_dot(probs[hd], vx_ref[hd, n * blk:(n + 1) * blk, :]) for hd in heads]
            for hd in heads:
                acc_ref[hd] += pv[hd]

    for hd in heads:
        acc = acc_ref[hd]
        o_ref[0, :, hsl(hd)] = (acc[:, :MB_DIM] / acc[:, MB_DIM:]).astype(o_ref.dtype)


def _moba_prompt(h3, rel_table):
    b, t, _ = h3.shape
    assert t % MB_BLOCK == 0
    n_blk = t // MB_BLOCK
    assert n_blk <= GATE_PAD
    qcol, kcol, vcol = COL_MQ // MB_WIDTH, COL_MK // MB_WIDTH, COL_MV // MB_WIDTH
    kern = functools.partial(_moba_prompt_kernel, n_blk=n_blk)
    wide = 2 * MB_DIM
    vmem = (2 * 2 * t * MB_WIDTH * 4 + 2 * MB_HEADS * t * wide * 2
            + MB_HEADS * (n_blk + 4) * MB_BLOCK * wide * 4) // MIB + 12
    return pl.pallas_call(
        kern,
        grid=(b, n_blk),
        in_specs=[pl.BlockSpec(memory_space=pltpu.SMEM),
                  pl.BlockSpec((1, MB_BLOCK, MB_WIDTH), lambda bb, i: (bb, i, qcol)),
                  pl.BlockSpec((1, t, MB_WIDTH), lambda bb, i: (bb, 0, kcol)),
                  pl.BlockSpec((1, t, MB_WIDTH), lambda bb, i: (bb, 0, vcol))],
        out_specs=pl.BlockSpec((1, MB_BLOCK, MB_WIDTH), lambda bb, i: (bb, i, 0)),
        out_shape=jax.ShapeDtypeStruct((b, t, MB_WIDTH), _operand_dtype(t)),
        scratch_shapes=[pltpu.VMEM((MB_HEADS, t, wide), BF16), pltpu.VMEM((MB_HEADS, t, wide), BF16),
                        pltpu.VMEM((MB_HEADS, GATE_PAD, MB_DIM), F32),
                        pltpu.VMEM((MB_HEADS, 2, MB_BLOCK, MB_BLOCK), F32),
                        pltpu.VMEM((MB_BLOCK, MB_BLOCK), BF16),
                        pltpu.VMEM((MB_HEADS, n_blk, MB_BLOCK, MB_BLOCK), F32),
                        pltpu.VMEM((MB_HEADS, MB_BLOCK, MB_DIM), F32),
                        pltpu.VMEM((MB_HEADS, MB_BLOCK, wide), F32),
                        pltpu.VMEM((MB_HEADS, MB_BLOCK, wide), F32)],
        compiler_params=_params(("arbitrary", "arbitrary"), vmem),
        name="moba_prompt",
    )(rel_table, h3, h3, h3)


def _moba_sample_kernel(pt_ref, rel_ref, q_ref, kn_ref, vn_ref, *refs, n_pages, past_len):
    del pt_ref
    kp = refs[:n_pages]
    vp = refs[n_pages:2 * n_pages]
    o_ref, bias_ref, bias_own_ref = refs[2 * n_pages:]
    t = q_ref.shape[1]
    rows = MB_HEADS * t
    page_rows = PAGE_SIZE * MB_HEADS
    pages_per_blk = MB_BLOCK // PAGE_SIZE
    n_past = past_len // MB_BLOCK
    scale = MB_DIM ** -0.5

    def head_rows(ref):
        x = ref[0]
        return jnp.concatenate([x[:, hd * MB_DIM:(hd + 1) * MB_DIM] for hd in range(MB_HEADS)], axis=0)

    @pl.when(pl.program_id(0) == 0)
    def _():
        tq = lax.broadcasted_iota(I32, (t, page_rows), 0)
        kc = lax.broadcasted_iota(I32, (t, page_rows), 1)
        for p in range(n_pages):
            dist = past_len + tq - (p * PAGE_SIZE + (kc >> 2))
            bias_ref[p] = jnp.concatenate(
                [jnp.where((kc & (MB_HEADS - 1)) == hd, _rel_bias(rel_ref, hd, dist), NEG_INF)
                 for hd in range(MB_HEADS)], axis=0)
        ro = lax.broadcasted_iota(I32, (t, rows), 0)
        co = lax.broadcasted_iota(I32, (t, rows), 1)
        for hd in range(MB_HEADS):
            own = _rel_bias(rel_ref, hd, ro - (co - hd * t))
            keep = (co >= hd * t) & (co <= hd * t + ro)
            bias_own_ref[hd * t:(hd + 1) * t, :] = jnp.where(keep, own, NEG_INF)

    q = head_rows(q_ref)
    qb = q.astype(BF16)
    row_head = lax.broadcasted_iota(I32, (rows, 1), 0) // t

    groups = page_rows // 8
    sums = [jnp.sum(kp[p][0].reshape(groups, 8, MB_DIM), axis=0) for p in range(n_pages)]
    blk_sums = []
    for n in range(n_past):
        acc = sums[n * pages_per_blk]
        for pp in range(1, pages_per_blk):
            acc = acc + sums[n * pages_per_blk + pp]
        blk_sums.append(acc)
    g_full = _dot_nt_f32acc(q, jnp.concatenate(blk_sums, axis=0))
    gc = lax.broadcasted_iota(I32, g_full.shape, 1)
    g_full = jnp.where((gc & (MB_HEADS - 1)) == row_head, g_full, 0.0)
    lane = lax.broadcasted_iota(I32, (rows, GATE_PAD), 1)
    gate = jnp.zeros((rows, GATE_PAD), F32)
    for n in range(n_past):
        g_n = jnp.sum(g_full[:, n * 8:(n + 1) * 8], axis=-1, keepdims=True) * (1.0 / MB_BLOCK)
        gate = jnp.where(lane == n, g_n, gate)
    sel = _topk_blocks(gate, n_past, n_past)

    s_own = _dot_nt(q, head_rows(kn_ref)) * scale + bias_own_ref[...]
    m = jnp.max(s_own, axis=-1, keepdims=True)
    s_past = []
    for p in range(n_pages):
        n = p // pages_per_blk
        s = _dot_nt(qb, kp[p][0].astype(BF16)) * scale + bias_ref[p]
        s = jnp.where(sel[:, n:n + 1] > 0.0, s, NEG_INF)
        s_past.append(s)
        m = jnp.maximum(m, jnp.max(s, axis=-1, keepdims=True))
    p_own = jnp.exp(s_own - m)
    l = jnp.sum(p_own, axis=-1, keepdims=True)
    out = _dot(p_own, head_rows(vn_ref))
    for p in range(n_pages):
        pr = jnp.exp(s_past[p] - m)
        l = l + jnp.sum(pr, axis=-1, keepdims=True)
        out = out + _dot(pr.astype(BF16), vp[p][0].astype(BF16))
    out = out / l
    for hd in range(MB_HEADS):
        o_ref[0, :, hd * MB_DIM:(hd + 1) * MB_DIM] = out[hd * t:(hd + 1) * t]


def _moba_sample(h3, cache_k, cache_v, page_table, rel_table):
    db, t, _ = h3.shape
    n_pages = page_table.shape[1]
    past_len = n_pages * PAGE_SIZE
    assert past_len % MB_BLOCK == 0 and t <= MB_BLOCK and past_len // MB_BLOCK < GATE_PAD
    assert MB_HEADS == 4 and t % 8 == 0
    qcol, kcol, vcol = COL_MQ // MB_WIDTH, COL_MK // MB_WIDTH, COL_MV // MB_WIDTH
    page_rows = PAGE_SIZE * MB_HEADS
    new = lambda k: pl.BlockSpec((1, t, MB_WIDTH), lambda i, pt: (i, 0, k))
    page = lambda p: pl.BlockSpec((1, page_rows, MB_DIM), lambda i, pt: (pt[i, p], 0, 0))
    kern = functools.partial(_moba_sample_kernel, n_pages=n_pages, past_len=past_len)
    grid_spec = pltpu.PrefetchScalarGridSpec(
        num_scalar_prefetch=1,
        grid=(db,),
        in_specs=([pl.BlockSpec(memory_space=pltpu.SMEM), new(qcol), new(kcol), new(vcol)]
                  + [page(p) for p in range(n_pages)] * 2),
        out_specs=pl.BlockSpec((1, t, MB_WIDTH), lambda i, pt: (i, 0, 0)),
        scratch_shapes=[pltpu.VMEM((n_pages, MB_HEADS * t, page_rows), F32),
                        pltpu.VMEM((MB_HEADS * t, MB_HEADS * t), F32)],
    )
    return pl.pallas_call(
        kern,
        grid_spec=grid_spec,
        out_shape=jax.ShapeDtypeStruct((db, t, MB_WIDTH), F32),
        compiler_params=_params(("arbitrary",), 40),
        name="moba_sample",
    )(page_table, rel_table, h3, h3, h3, *([cache_k] * n_pages), *([cache_v] * n_pages))


def _merge_kernel(oa_ref, ob_ref, ga0_ref, ga1_ref, gb0_ref, gb1_ref, x_ref,
                  wa_ref, wb_ref, wo_ref, gx_ref, wq_ref, x1_ref, xq_ref):
    ga = jnp.concatenate([ga0_ref[...], ga1_ref[...]], axis=-1)
    gb = jnp.concatenate([gb0_ref[...], gb1_ref[...]], axis=-1)
    pa = _dot(oa_ref[...].astype(BF16), wa_ref[...])
    pb = _dot(ob_ref[...].astype(BF16), wb_ref[...])
    merged = _sigmoid(ga) * pa + _sigmoid(gb) * pb
    x1 = x_ref[...] + _dot(merged.astype(BF16), wo_ref[...])
    x1_ref[...] = x1
    xq_ref[...] = _dot(_rmsnorm(x1, gx_ref[...]).astype(BF16), wq_ref[...]).astype(xq_ref.dtype)


def _merge(oa, ob, h2, x, wa, wb, wo, gx, wq, *, tm, xq_dtype):
    n = x.shape[0]
    assert n % tm == 0
    half = D_MODEL // 2
    tok = lambda w, k=0: pl.BlockSpec((tm, w), lambda i: (i, k))
    full = lambda a: pl.BlockSpec(a.shape, lambda i: (0, 0))
    gx2 = gx.reshape(1, D_MODEL)
    return pl.pallas_call(
        _merge_kernel,
        grid=(n // tm,),
        in_specs=[tok(HG_WIDTH), tok(MB_WIDTH),
                  tok(half, COL_GA // half), tok(half, COL_GA // half + 1),
                  tok(half, COL_GB // half), tok(half, COL_GB // half + 1),
                  tok(D_MODEL), full(wa), full(wb), full(wo), full(gx2), full(wq)],
        out_specs=[tok(D_MODEL), tok(XA_WIDTH)],
        out_shape=[jax.ShapeDtypeStruct((n, D_MODEL), F32),
                   jax.ShapeDtypeStruct((n, XA_WIDTH), xq_dtype)],
        compiler_params=_params(("parallel",), 48),
        name="merge_mix",
    )(oa, ob, h2, h2, h2, h2, x, wa, wb, wo, gx2, wq)


def _mem_attn_kernel(q_ref, k_ref, v_ref, o_ref):
    scale = XA_DIM ** -0.5
    heads = range(XA_HEADS)
    hsl = lambda hd: slice(hd * XA_DIM, (hd + 1) * XA_DIM)
    logits = [_dot_nt(q_ref[0][:, hsl(hd)].astype(BF16), k_ref[0][:, hsl(hd)].astype(BF16)) * scale
              for hd in heads]
    row_max = [jnp.max(s, axis=-1, keepdims=True) for s in logits]
    probs = [jnp.exp(s - m) for s, m in zip(logits, row_max)]
    denom = [jnp.sum(p, axis=-1, keepdims=True) for p in probs]
    outs = [_dot(probs[hd].astype(BF16), v_ref[0][:, hsl(hd)].astype(BF16)) for hd in heads]
    for hd in heads:
        o_ref[0, :, hsl(hd)] = (outs[hd] / denom[hd]).astype(o_ref.dtype)


def _mem_attn(xq3, mem_kv, *, tq):
    b, t, _ = xq3.shape
    assert t % tq == 0
    mem = lambda k: pl.BlockSpec((1, mem_kv.shape[1], XA_WIDTH), lambda i, j: (i, 0, k))
    qs = pl.BlockSpec((1, tq, XA_WIDTH), lambda i, j: (i, j, 0))
    return pl.pallas_call(
        _mem_attn_kernel,
        grid=(b, t // tq),
        in_specs=[qs, mem(0), mem(1)],
        out_specs=qs,
        out_shape=jax.ShapeDtypeStruct((b, t, XA_WIDTH), _operand_dtype(t)),
        compiler_params=_params(("parallel", "parallel"), 32),
        name="mem_attn",
    )(xq3, mem_kv, mem_kv)


def _mem_attn_rows_kernel(q_ref, k_ref, v_ref, o_ref):
    bb, t, _ = q_ref.shape
    rows = XA_HEADS * t
    scale = XA_DIM ** -0.5
    row_head = lax.broadcasted_iota(I32, (rows, 1), 0) // t
    kc = lax.broadcasted_iota(I32, (rows, k_ref.shape[1]), 1)
    same_head = (kc & (XA_HEADS - 1)) == row_head
    qs = []
    for i in range(bb):
        x = q_ref[i]
        qs.append(jnp.concatenate([x[:, hd * XA_DIM:(hd + 1) * XA_DIM] for hd in range(XA_HEADS)],
                                  axis=0).astype(BF16))
    logits = [jnp.where(same_head, _dot_nt(qs[i], k_ref[i].astype(BF16)) * scale, NEG_INF)
              for i in range(bb)]
    row_max = [jnp.max(s, axis=-1, keepdims=True) for s in logits]
    probs = [jnp.exp(s - m) for s, m in zip(logits, row_max)]
    denom = [jnp.sum(p, axis=-1, keepdims=True) for p in probs]
    outs = [_dot(probs[i].astype(BF16), v_ref[i].astype(BF16)) / denom[i] for i in range(bb)]
    for i in range(bb):
        for hd in range(XA_HEADS):
            o_ref[i, :, hd * XA_DIM:(hd + 1) * XA_DIM] = outs[i][hd * t:(hd + 1) * t]


def _mem_attn_rows(xq3, mem_k, mem_v, *, bb):
    b, t, _ = xq3.shape
    assert b % bb == 0 and XA_HEADS == 4 and t % 8 == 0
    mem = pl.BlockSpec((bb,) + mem_k.shape[1:], lambda i: (i, 0, 0))
    qs = pl.BlockSpec((bb, t, XA_WIDTH), lambda i: (i, 0, 0))
    return pl.pallas_call(
        _mem_attn_rows_kernel,
        grid=(b // bb,),
        in_specs=[qs, mem, mem],
        out_specs=qs,
        out_shape=jax.ShapeDtypeStruct((b, t, XA_WIDTH), F32),
        compiler_params=_params(("parallel",), 32),
        name="mem_attn_rows",
    )(xq3, mem_k, mem_v)


def _route(logits):
    lane = lax.broadcasted_iota(I32, logits.shape, 1).astype(F32)
    first = lambda hit: jnp.min(jnp.where(hit, lane, float(ROUTER_LANES)), axis=-1, keepdims=True)
    gl = jnp.where(lane < N_GROUPS, logits, NEG_INF)
    gmax = jnp.max(gl, axis=-1, keepdims=True)
    grp = first(gl == gmax)
    g_prob = 1.0 / jnp.sum(jnp.exp(gl - gmax), axis=-1, keepdims=True)
    e_lo = EXPERT_LANE0 + grp * EXPERTS_PER_GROUP
    in_grp = (lane >= e_lo) & (lane < e_lo + EXPERTS_PER_GROUP)
    el = jnp.where(in_grp, logits, NEG_INF)
    top1 = jnp.max(el, axis=-1, keepdims=True)
    idx1 = first(el == top1)
    el2 = jnp.where(lane == idx1, NEG_INF, el)
    top2 = jnp.max(el2, axis=-1, keepdims=True)
    idx2 = first(el2 == top2)
    e2 = jnp.exp(top2 - top1)
    w1 = g_prob / (1.0 + e2)
    w2 = w1 * e2
    comb = jnp.where(lane == idx1, w1, 0.0) + jnp.where(lane == idx2, w2, 0.0)
    return jnp.where(lane == GRP_LANE, grp, comb)


def _xo_route_kernel(x1_ref, at_ref, wxo_ref, gf_ref, wr_ref, br_ref, x2_ref, xf_ref, comb_ref):
    x2 = x1_ref[...] + _dot(at_ref[...].astype(BF16), wxo_ref[...])
    x2_ref[...] = x2
    xf = _rmsnorm(x2, gf_ref[...])
    xf_ref[...] = xf.astype(BF16)
    comb_ref[...] = _route(_dot_f32acc(xf, wr_ref[...]) + br_ref[...])


def _xo_route(x1, attn, wxo, g_ffn, w_router_t, b_router_t, *, tm):
    n = x1.shape[0]
    assert n % tm == 0
    tok = lambda w: pl.BlockSpec((tm, w), lambda i: (i, 0))
    full = lambda a: pl.BlockSpec(a.shape, lambda i: (0, 0))
    gf2 = g_ffn.reshape(1, D_MODEL)
    return pl.pallas_call(
        _xo_route_kernel,
        grid=(n // tm,),
        in_specs=[tok(D_MODEL), tok(XA_WIDTH), full(wxo), full(gf2), full(w_router_t), full(b_router_t)],
        out_specs=[tok(D_MODEL), tok(D_MODEL), tok(ROUTER_LANES)],
        out_shape=[jax.ShapeDtypeStruct((n, D_MODEL), F32), jax.ShapeDtypeStruct((n, D_MODEL), BF16),
                   jax.ShapeDtypeStruct((n, ROUTER_LANES), F32)],
        compiler_params=_params(("parallel",), 40),
        name="xo_route",
    )(x1, attn, wxo, gf2, w_router_t, b_router_t)


def _moe_kernel(xf_ref, comb_ref, x2_ref, wg_ref, wu_ref, wd_ref, gn_ref, y_ref,
                tri_ref, posc_ref, cnt_ref, xs_ref, ws_ref, ys_ref, *, r):
    i = pl.program_id(0)
    step = pl.program_id(1)
    tm = xf_ref.shape[0]
    per_step = wg_ref.shape[0]
    steps_per_group = EXPERTS_PER_GROUP // per_step
    g = step // steps_per_group
    lane = lax.broadcasted_iota(I32, (tm, ROUTER_LANES), 1)

    @pl.when((i == 0) & (step == 0))
    def _():
        rr = lax.broadcasted_iota(I32, (tm, tm), 0)
        cc = lax.broadcasted_iota(I32, (tm, tm), 1)
        tri_ref[...] = jnp.where(cc < rr, 1.0, 0.0).astype(BF16)

    @pl.when(step == 0)
    def _():
        comb = comb_ref[...]
        grp = comb[:, GRP_LANE:GRP_LANE + 1]
        lane_f = lane.astype(F32)
        rel = lane - EXPERT_LANE0
        used = (comb > 0.0) & (rel >= 0) & (rel < N_EXPERTS)
        second = (rel & (EXPERTS_PER_GROUP - 1)) >= EXPERTS_PER_GROUP // 2
        uses_a = jnp.max(jnp.where(used & jnp.logical_not(second), 1.0, 0.0), axis=-1, keepdims=True)
        uses_b = jnp.max(jnp.where(used & second, 1.0, 0.0), axis=-1, keepdims=True)
        key = grp * MOE_CLASSES + jnp.where(uses_b == 0.0, 0.0, jnp.where(uses_a == 0.0, 2.0, 1.0))
        onehot = jnp.where((lane_f == key) & (lane < N_GROUPS * MOE_CLASSES), 1.0, 0.0)
        before = _dot(tri_ref[...], onehot.astype(BF16))
        cnt = jnp.sum(onehot, axis=0, keepdims=True)
        cnt_ref[...] = cnt
        pos = jnp.sum(jnp.where(lane_f < key, cnt, 0.0) + onehot * before, axis=-1, keepdims=True)
        posc_ref[...] = jnp.where(lane == 0, pos, 0.0)
        hi = jnp.floor(pos * (1.0 / POS_SPLIT))
        lo = pos - hi * POS_SPLIT
        cols = jnp.where(lane == 0, hi, jnp.where(lane == 1, lo, 0.0))
        er = lax.broadcasted_iota(I32, (16, ROUTER_LANES), 0)
        ec = lax.broadcasted_iota(I32, (16, ROUTER_LANES), 1)
        post = _dot_nt(jnp.where(er == ec, 1.0, 0.0).astype(BF16), cols.astype(BF16))
        pos_row = post[0:1, :] * POS_SPLIT + post[1:2, :]
        slot = lax.broadcasted_iota(I32, (tm, tm), 0).astype(F32)
        perm = jnp.where(pos_row == slot, 1.0, 0.0).astype(BF16)
        comb_hi, comb_lo = _split2(comb)
        xs_ref[0:tm, :] = _dot(perm, xf_ref[...]).astype(BF16)
        ws_ref[0:tm, :] = _dot(perm, comb_hi) + _dot(perm, comb_lo)
        xs_ref[tm:, :] = jnp.zeros((r, D_MODEL), BF16)
        ws_ref[tm:, :] = jnp.zeros((r, ROUTER_LANES), F32)
        ys_ref[...] = jnp.zeros(ys_ref.shape, F32)

    lane_row = lax.broadcasted_iota(I32, (1, ROUTER_LANES), 1)
    key0 = g * MOE_CLASSES + step % steps_per_group
    first = jnp.sum(jnp.where(lane_row < key0, cnt_ref[...], 0.0)).astype(I32)
    count = jnp.sum(jnp.where((lane_row == key0) | (lane_row == key0 + 1), cnt_ref[...], 0.0)).astype(I32)
    start = (first // 16) * 16
    n_windows = (first + count - start + (r - 1)) // r

    def experts(c, carry):
        rows = pl.ds(pl.multiple_of(start + c * r, 16), r)
        xg = xs_ref[rows, :]
        wq = ws_ref[rows, :]
        wl = lax.broadcasted_iota(I32, (r, ROUTER_LANES), 1)
        ups = [(_dot(xg, wg_ref[k]), _dot(xg, wu_ref[k])) for k in range(per_step)]
        hidden = []
        for k, (gate, up) in enumerate(ups):
            w_e = jnp.sum(jnp.where(wl == EXPERT_LANE0 + step * per_step + k, wq, 0.0), axis=-1, keepdims=True)
            hidden.append((_silu(gate) * up * w_e).astype(BF16))
        w_down = wd_ref[...].reshape(per_step * EXPERT_HIDDEN, D_MODEL)
        ys_ref[rows, :] += _dot(jnp.concatenate(hidden, axis=-1), w_down)
        return carry

    lax.fori_loop(0, n_windows, experts, 0)

    @pl.when(step == pl.num_programs(1) - 1)
    def _():
        slot = lax.broadcasted_iota(I32, (tm, tm), 1).astype(F32)
        place = jnp.where(posc_ref[:, 0:1] == slot, 1.0, 0.0).astype(BF16)
        moe = _dot(place, ys_ref[0:tm, :].astype(BF16))
        y_ref[...] = _rmsnorm(x2_ref[...] + moe, gn_ref[...])


def _moe(xf, comb, x2, wg, wu, wd, g_final, *, tm):
    n = xf.shape[0]
    assert n % tm == 0 and tm % 16 == 0
    assert 2 * MOE_EXPERTS_PER_STEP == EXPERTS_PER_GROUP and N_GROUPS * MOE_CLASSES <= ROUTER_LANES
    r = min(MOE_CHUNK_ROWS, tm)
    cap = tm + r
    per_step = MOE_EXPERTS_PER_STEP
    tok = lambda w: pl.BlockSpec((tm, w), lambda i, s: (i, 0))
    gn2 = g_final.reshape(1, D_MODEL)
    kern = functools.partial(_moe_kernel, r=r)
    vmem = (2 * tm * D_MODEL * (2 + 4 + 4) + 2 * tm * ROUTER_LANES * 4
            + 2 * 3 * per_step * D_MODEL * EXPERT_HIDDEN * 2 + tm * tm * 2
            + cap * D_MODEL * (2 + 4) + tm * D_MODEL * 4) // MIB + 8
    return pl.pallas_call(
        kern,
        grid=(n // tm, N_EXPERTS // per_step),
        in_specs=[tok(D_MODEL), tok(ROUTER_LANES), tok(D_MODEL),
                  pl.BlockSpec((per_step, D_MODEL, EXPERT_HIDDEN), lambda i, s: (s, 0, 0)),
                  pl.BlockSpec((per_step, D_MODEL, EXPERT_HIDDEN), lambda i, s: (s, 0, 0)),
                  pl.BlockSpec((per_step, EXPERT_HIDDEN, D_MODEL), lambda i, s: (s, 0, 0)),
                  pl.BlockSpec((1, D_MODEL), lambda i, s: (0, 0))],
        out_specs=tok(D_MODEL),
        out_shape=jax.ShapeDtypeStruct((n, D_MODEL), F32),
        scratch_shapes=[pltpu.VMEM((tm, tm), BF16),
                        pltpu.VMEM((tm, ROUTER_LANES), F32), pltpu.VMEM((1, ROUTER_LANES), F32),
                        pltpu.VMEM((cap, D_MODEL), BF16), pltpu.VMEM((cap, ROUTER_LANES), F32),
                        pltpu.VMEM((cap, D_MODEL), F32)],
        compiler_params=_params(("arbitrary", "arbitrary"), vmem),
        name="moe_grouped",
    )(xf, comb, x2, wg, wu, wd, gn2)


def _token_tile(n, cap=512):
    return cap if n % cap == 0 else n


def _col_tile(n, cap=1536):
    return max(c for c in range(V7X_LANES, cap + 1, V7X_LANES) if n % c == 0)


def _layer(x3, w, s0, moba_fn, mem_fn):
    b, t, _ = x3.shape
    n = b * t
    x2d = x3.reshape(n, D_MODEL)
    tm = _token_tile(n)
    h2, k4, v4 = _norm_matmul(x2d, w["norm_mix"], w["w_in"], tm=_token_tile(n, 1024),
                              tn=_col_tile(IN_COLS), head_cols=(COL_MK, COL_MV), heads=MB_HEADS,
                              head_dim=MB_DIM)
    h3 = h2.reshape(b, t, IN_COLS)
    oa, s_new = _hgrn(h3, w["hg_lb_logits"], w["hg_norm"], s0)
    ob = moba_fn(h3)
    x1, xq = _merge(oa.reshape(n, HG_WIDTH), ob.reshape(n, MB_WIDTH), h2, x2d,
                    w["w_branch_a"], w["w_branch_b"], w["w_mix_out"], w["norm_xattn"], w["w_xq"], tm=tm,
                    xq_dtype=_operand_dtype(t))
    attn = mem_fn(xq.reshape(b, t, XA_WIDTH))
    x2, xf, comb = _xo_route(x1, attn.reshape(n, XA_WIDTH), w["w_xo"], w["norm_ffn"],
                             w["w_router"], w["b_router"], tm=tm)
    y = _moe(xf, comb, x2, w["w_expert_gate"], w["w_expert_up"], w["w_expert_down"],
             w["norm_final"], tm=_token_tile(n, 1024))
    k_new = k4.reshape(b, t, MB_HEADS, MB_DIM)
    v_new = v4.reshape(b, t, MB_HEADS, MB_DIM)
    return y.reshape(b, t, D_MODEL), k_new, v_new, s_new


def kernel(x_prompt, x_sample, cache_k, cache_v, state_hgrn, cache_mem_k, cache_mem_v, page_table, mem_prompt, norm_mix, w_in, hg_lb_logits, hg_norm, w_branch_a, w_branch_b, w_mix_out, rel_table, norm_xattn, norm_mem, w_xq, w_xk, w_xv, w_xo, norm_ffn, w_group_router, b_group_router, w_expert_router, b_expert_router, w_expert_gate, w_expert_up, w_expert_down, norm_final):
    assert w_in.shape[0] == DEPTH == 1 and hg_lb_logits.shape[0] == DEPTH + 1
    b = x_prompt.shape[0]
    db = x_sample.shape[0]
    n_pool = cache_k.shape[1]
    pad_lanes = ROUTER_LANES - EXPERT_LANE0 - N_EXPERTS
    w = {
        "norm_mix": norm_mix[0], "w_in": w_in[0].astype(BF16),
        "hg_lb_logits": hg_lb_logits, "hg_norm": hg_norm[0],
        "w_branch_a": w_branch_a[0].astype(BF16), "w_branch_b": w_branch_b[0].astype(BF16),
        "w_mix_out": w_mix_out[0].astype(BF16), "norm_xattn": norm_xattn[0],
        "w_xq": w_xq[0].astype(BF16), "w_xo": w_xo[0].astype(BF16), "norm_ffn": norm_ffn[0],
        "w_router": jnp.pad(jnp.concatenate(
            [w_group_router[0], jnp.zeros((D_MODEL, EXPERT_LANE0 - N_GROUPS), F32), w_expert_router[0]],
            axis=1), ((0, 0), (0, pad_lanes))),
        "b_router": jnp.pad(jnp.concatenate(
            [b_group_router[0], jnp.zeros((EXPERT_LANE0 - N_GROUPS,), F32), b_expert_router[0]]),
            (0, pad_lanes)).reshape(1, ROUTER_LANES),
        "w_expert_gate": w_expert_gate[0].reshape(N_EXPERTS, D_MODEL, EXPERT_HIDDEN).astype(BF16),
        "w_expert_up": w_expert_up[0].reshape(N_EXPERTS, D_MODEL, EXPERT_HIDDEN).astype(BF16),
        "w_expert_down": w_expert_down[0].reshape(N_EXPERTS, EXPERT_HIDDEN, D_MODEL).astype(BF16),
        "norm_final": norm_final,
    }

    w_mem = jnp.concatenate([w_xk[0], w_xv[0]], axis=1).astype(BF16)
    mem_kv, mk_p, mv_p = _norm_matmul(mem_prompt.reshape(b * MEM_LEN, D_MODEL), norm_mem[0], w_mem,
                                      tm=_token_tile(b * MEM_LEN), tn=512, head_cols=(0, XA_WIDTH),
                                      heads=XA_HEADS, head_dim=XA_DIM)
    mem_kv = mem_kv.reshape(b, MEM_LEN, 2 * XA_WIDTH)
    s0 = jnp.zeros((b, HG_HEADS, HG_DIM, HG_DIM), F32)
    y_p, k_p, v_p, s_p = _layer(x_prompt, w, s0,
                                functools.partial(_moba_prompt, rel_table=rel_table),
                                functools.partial(_mem_attn, mem_kv=mem_kv, tq=512))

    ck = cache_k[0].reshape(n_pool, PAGE_SIZE * MB_HEADS, MB_DIM)
    cv = cache_v[0].reshape(n_pool, PAGE_SIZE * MB_HEADS, MB_DIM)
    moba_s = functools.partial(_moba_sample, cache_k=ck, cache_v=cv, page_table=page_table,
                               rel_table=rel_table)
    mem_s = functools.partial(_mem_attn_rows, bb=4,
                              mem_k=cache_mem_k[0].reshape(db, MEM_LEN * XA_HEADS, XA_DIM),
                              mem_v=cache_mem_v[0].reshape(db, MEM_LEN * XA_HEADS, XA_DIM))
    y_s, k_s, v_s, s_s = _layer(x_sample, w, state_hgrn[0], moba_s, mem_s)

    heads = lambda a: a.reshape(b, MEM_LEN, XA_HEADS, XA_DIM)[None]
    return (y_p, y_s, k_p[None], v_p[None], s_p[None], heads(mk_p), heads(mv_p),
            k_s[None], v_s[None], s_s[None])
```

```python
import functools
import math

import numpy as np
import jax
import jax.numpy as jnp
from jax import lax
from jax.experimental import pallas as pl
from jax.experimental.pallas import tpu as pltpu

F32 = jnp.float32
BF16 = jnp.bfloat16
I32 = jnp.int32

D_MODEL = 1024
DEPTH = 1
PAGE_SIZE = 128
HG_HEADS = 4
HG_DIM = 128
HG_WIDTH = HG_HEADS * HG_DIM
MB_HEADS = 4
MB_DIM = 128
MB_WIDTH = MB_HEADS * MB_DIM
MB_BLOCK = 256
MB_TOPK = 3
REL_BUCKETS = 32
REL_MAX_DIST = 128
REL_MAX_EXACT = REL_BUCKETS // 2
MEM_LEN = 256
XA_HEADS = 4
XA_DIM = 128
XA_WIDTH = XA_HEADS * XA_DIM
N_GROUPS = 4
EXPERTS_PER_GROUP = 8
N_EXPERTS = N_GROUPS * EXPERTS_PER_GROUP
EXPERT_TOPK = 2
EXPERT_HIDDEN = 256
NORM_EPS = 1e-6
IN_COLS = 4 * HG_WIDTH + 3 * MB_WIDTH + 2 * D_MODEL
COL_MQ = 4 * HG_WIDTH
COL_MK = COL_MQ + MB_WIDTH
COL_MV = COL_MK + MB_WIDTH
COL_GA = COL_MV + MB_WIDTH
COL_GB = COL_GA + D_MODEL

V7X_LANES = 128
V7X_VMEM_BYTES = 64 * 1024 * 1024
MIB = 1024 * 1024

NEG_INF = float("-inf")
MASK_BIG = -1e30
GATE_PAD = 16
ROUTER_LANES = 128
GRP_LANE = 0
EXPERT_LANE0 = 8
POS_SPLIT = 32.0
MOE_CHUNK_ROWS = 256
MOE_EXPERTS_PER_STEP = EXPERTS_PER_GROUP // 2
MOE_CLASSES = 3
HGRN_PROMPT_CHUNK = 128
HGRN_PROMPT_ROWS = 4
HGRN_MIN_CHUNK = 16
HGRN_CHAINS = 16


def _params(semantics, vmem_mib):
    return pltpu.CompilerParams(dimension_semantics=semantics,
                                vmem_limit_bytes=min(vmem_mib * MIB, V7X_VMEM_BYTES - 8 * MIB))


def _dot(a, b):
    return jnp.dot(a, b, preferred_element_type=F32)


def _dot_nt(a, b):
    return lax.dot_general(a, b, (((1,), (1,)), ((), ())), preferred_element_type=F32)


def _dot_tn(a, b):
    return lax.dot_general(a, b, (((0,), (0,)), ((), ())), preferred_element_type=F32)


def _split2(a):
    hi = a.astype(BF16)
    lo = (a - hi.astype(F32)).astype(BF16)
    return hi, lo


def _dot_nt_f32acc(a, b):
    ah, al = _split2(a)
    bh, bl = _split2(b)
    return _dot_nt(ah, bh) + (_dot_nt(ah, bl) + _dot_nt(al, bh))


def _dot_f32acc(a, b):
    ah, al = _split2(a)
    bh, bl = _split2(b)
    return _dot(ah, bh) + (_dot(ah, bl) + _dot(al, bh))


def _operand_dtype(t):
    return BF16 if t % 16 == 0 else F32


def _rmsnorm(x, g):
    return x * lax.rsqrt(jnp.mean(x * x, axis=-1, keepdims=True) + NORM_EPS) * g


def _sigmoid(x):
    return 1.0 / (1.0 + jnp.exp(-x))


def _silu(x):
    return x * _sigmoid(x)


def _norm_matmul_kernel(x_ref, g_ref, w_ref, o_ref, *refs, head_cols, heads, head_dim):
    head_refs, xn_ref = refs[:-1], refs[-1]
    j = pl.program_id(1)
    tm, tn = o_ref.shape

    @pl.when(j == 0)
    def _():
        xn_ref[...] = _rmsnorm(x_ref[...], g_ref[...]).astype(BF16)

    res = _dot(xn_ref[...], w_ref[...])
    o_ref[...] = res
    for col0, href in zip(head_cols, head_refs):
        for hd in range(heads):
            tile, local = divmod(col0 + hd * head_dim, tn)

            @pl.when(j == tile)
            def _(hd=hd, local=local, href=href):
                href[pl.ds(hd, tm, stride=heads), :] = res[:, local:local + head_dim]


def _norm_matmul(x, g, w_bf16, *, tm, tn, head_cols=(), heads=1, head_dim=V7X_LANES):
    m, d = x.shape
    n = w_bf16.shape[1]
    assert m % tm == 0 and n % tn == 0 and tn % head_dim == 0
    vmem = (2 * tm * d * 4 + tm * d * 2 + 2 * d * tn * 2 + 2 * tm * tn * 4
            + len(head_cols) * 2 * tm * heads * head_dim * 4) // MIB + 12
    kern = functools.partial(_norm_matmul_kernel, head_cols=tuple(head_cols), heads=heads,
                             head_dim=head_dim)
    head_spec = pl.BlockSpec((tm * heads, head_dim), lambda i, j: (i, 0))
    head_shape = jax.ShapeDtypeStruct((m * heads, head_dim), F32)
    return pl.pallas_call(
        kern,
        grid=(m // tm, n // tn),
        in_specs=[pl.BlockSpec((tm, d), lambda i, j: (i, 0)),
                  pl.BlockSpec((1, d), lambda i, j: (0, 0)),
                  pl.BlockSpec((d, tn), lambda i, j: (0, j))],
        out_specs=[pl.BlockSpec((tm, tn), lambda i, j: (i, j))] + [head_spec] * len(head_cols),
        out_shape=[jax.ShapeDtypeStruct((m, n), F32)] + [head_shape] * len(head_cols),
        scratch_shapes=[pltpu.VMEM((tm, d), BF16)],
        compiler_params=_params(("parallel", "arbitrary"), vmem),
        name="norm_matmul",
    )(x, g.reshape(1, d), w_bf16)


def _hgrn_levels(c):
    return int(round(math.log2(c)))


def _hgrn_sum_masks(c):
    t = np.arange(c)[:, None]
    j = np.arange(c)[None, :]
    rows = [j <= t, j > t]
    for lv in range(_hgrn_levels(c)):
        half = c >> (lv + 1)
        blk = 2 * half
        mid = (t // blk) * blk + half - 1
        upper = (t % blk) >= half
        rows.append((upper & (j > mid) & (j <= t)) | ((~upper) & (j > t) & (j <= mid)))
    return np.concatenate(rows, axis=0).astype(np.float32)


def _hgrn_kernel(hq_ref, hf_ref, hi_ref, hg_ref, lbl_ref, gn_ref, mask_ref, s0_ref,
                 o_ref, s_ref, st_ref, *, c, t_blk, bb):
    j = pl.program_id(1)
    chains = [(bi, hd) for bi in range(bb) for hd in range(HG_HEADS)]

    @pl.when(j == 0)
    def _():
        for bi, hd in chains:
            st_ref[bi * HG_HEADS + hd] = s0_ref[bi, hd].T

    lbl = lbl_ref[...]
    lmax = jnp.max(lbl, axis=0, keepdims=True)
    lexp = jnp.exp(lbl - lmax)
    lb_all = lexp[0:1, :] / jnp.sum(lexp, axis=0, keepdims=True)

    masks = mask_ref[...]
    pad = c - t_blk
    row = lax.broadcasted_iota(I32, (c, HG_DIM), 0)
    rr = lax.broadcasted_iota(I32, (c, c), 0)
    cc = lax.broadcasted_iota(I32, (c, c), 1)

    def padded(a):
        if pad == 0:
            return a
        return jnp.concatenate([a, jnp.zeros((pad, HG_DIM), F32)], axis=0)

    hsl = lambda hd: slice(hd * HG_DIM, (hd + 1) * HG_DIM)
    qs, ks, vs, logf_parts = [], [], [], []
    for bi, hd in chains:
        xq = hq_ref[bi][:, hsl(hd)]
        lb = lb_all[:, hsl(hd)]
        f = lb + (1.0 - lb) * _sigmoid(hf_ref[bi][:, hsl(hd)])
        qs.append(padded(_silu(xq)))
        ks.append(padded(1.0 - f))
        vs.append(padded(hi_ref[bi][:, hsl(hd)]))
        logf_parts.append(_split2(padded(jnp.log(f))))
    exs = []
    for n in range(0, len(chains), 2):
        (hi_a, lo_a), (hi_b, lo_b) = logf_parts[n], logf_parts[n + 1]
        stacked = jnp.concatenate([jnp.concatenate([hi_a, hi_b], axis=-1),
                                   jnp.concatenate([lo_a, lo_b], axis=-1)], axis=0)
        ex = jnp.exp(_dot(masks, stacked))
        exs += [ex[:, :HG_DIM], ex[:, HG_DIM:]]

    level_dots = []
    for q, kk, ex in zip(qs, ks, exs):
        per_level = []
        for lv in range(_hgrn_levels(c)):
            half = c >> (lv + 1)
            upper = (row & half) != 0
            ex_lv = ex[(2 + lv) * c:(3 + lv) * c]
            a = jnp.where(upper, q * ex_lv, 0.0).astype(BF16)
            bm = jnp.where(upper, 0.0, kk * ex_lv).astype(BF16)
            per_level.append(_dot_nt(a, bm))
        level_dots.append(per_level)

    outs = []
    for n, (bi, hd) in enumerate(chains):
        q, kk, v, ex = qs[n], ks[n], vs[n], exs[n]
        scores = jnp.zeros((c, c), F32)
        for lv, d in enumerate(level_dots[n]):
            scores = scores + jnp.where((rr ^ cc) < 2 * (c >> (lv + 1)), d, 0.0)
        eb = ex[0:c]
        eb_rev = ex[c:2 * c]
        diag = jnp.sum(q * kk, axis=-1, keepdims=True)
        st = st_ref[bi * HG_HEADS + hd]
        vb = v.astype(BF16)
        o = (_dot_nt((q * eb).astype(BF16), st.astype(BF16))
             + _dot(scores.astype(BF16), vb) + diag * v)
        st_ref[bi * HG_HEADS + hd] = st * eb[c - 1:c, :] + _dot_tn(vb, (kk * eb_rev).astype(BF16))
        outs.append(o[0:t_blk])

    for (bi, hd), o in zip(chains, outs):
        xg = hg_ref[bi][:, hsl(hd)]
        o_ref[bi, :, hsl(hd)] = (_rmsnorm(o, gn_ref[:, hsl(hd)]) * _silu(xg)).astype(o_ref.dtype)

    @pl.when(j == pl.num_programs(1) - 1)
    def _():
        for bi, hd in chains:
            s_ref[bi, hd] = st_ref[bi * HG_HEADS + hd].T


def _hgrn(h3, lb_logits, hg_norm, s0):
    b, t, _ = h3.shape
    t_blk = math.gcd(t, HGRN_PROMPT_CHUNK)
    c = max(t_blk, HGRN_MIN_CHUNK)
    bb = math.gcd(b, HGRN_CHAINS // HG_HEADS) if t == t_blk else math.gcd(b, HGRN_PROMPT_ROWS)
    m01 = _hgrn_sum_masks(c)
    masks = jnp.asarray(np.concatenate([m01, m01], axis=1), BF16)
    col = lambda k: pl.BlockSpec((bb, t_blk, HG_WIDTH), lambda i, j: (i, j, k))
    state_spec = pl.BlockSpec((bb, HG_HEADS, HG_DIM, HG_DIM), lambda i, j: (i, 0, 0, 0))
    kern = functools.partial(_hgrn_kernel, c=c, t_blk=t_blk, bb=bb)
    return pl.pallas_call(
        kern,
        grid=(b // bb, t // t_blk),
        in_specs=[col(0), col(1), col(2), col(3),
                  pl.BlockSpec(lb_logits.shape, lambda i, j: (0, 0)),
                  pl.BlockSpec((1, HG_WIDTH), lambda i, j: (0, 0)),
                  pl.BlockSpec(masks.shape, lambda i, j: (0, 0)),
                  state_spec],
        out_specs=[pl.BlockSpec((bb, t_blk, HG_WIDTH), lambda i, j: (i, j, 0)), state_spec],
        out_shape=[jax.ShapeDtypeStruct((b, t, HG_WIDTH), _operand_dtype(t)),
                   jax.ShapeDtypeStruct(s0.shape, F32)],
        scratch_shapes=[pltpu.VMEM((bb * HG_HEADS, HG_DIM, HG_DIM), F32)],
        compiler_params=_params(("parallel", "arbitrary"), 32),
        name="hgrn2",
    )(h3, h3, h3, h3, lb_logits, hg_norm.reshape(1, HG_WIDTH), masks, s0)


def _rel_bucket_edges():
    d = np.arange(2 * REL_MAX_DIST)
    df = np.maximum(d, 1).astype(np.float32)
    large = REL_MAX_EXACT + (np.log(df / np.float32(REL_MAX_EXACT))
                             / np.float32(math.log(REL_MAX_DIST / REL_MAX_EXACT))
                             * np.float32(REL_BUCKETS - REL_MAX_EXACT)).astype(np.int32)
    bucket = np.where(d < REL_MAX_EXACT, d, np.minimum(large, REL_BUCKETS - 1))
    assert np.all(np.diff(bucket) >= 0) and bucket[-1] == REL_BUCKETS - 1
    return [int(np.argmax(bucket >= kb)) for kb in range(REL_BUCKETS)]


def _rel_bias(rel_ref, head, dist):
    edges = _rel_bucket_edges()
    out = jnp.full(dist.shape, rel_ref[0, head], F32)
    for kb in range(1, REL_BUCKETS):
        out = jnp.where(dist >= edges[kb], rel_ref[kb, head], out)
    return out


def _stack_rows(rows):
    ridx = lax.broadcasted_iota(I32, (GATE_PAD, MB_DIM), 0)
    out = jnp.zeros((GATE_PAD, MB_DIM), F32)
    for n, r in enumerate(rows):
        out = jnp.where(ridx == n, r, out)
    return out


def _topk_blocks(gate, n_cand, n_past):
    lane = lax.broadcasted_iota(I32, gate.shape, 1)
    past = lane < n_past
    g = jnp.where(past, gate, NEG_INF)
    rank = jnp.zeros(gate.shape, I32)
    for m in range(n_cand):
        gm = g[:, m:m + 1]
        ahead = jnp.where(gm > g, 1, jnp.where((gm == g) & (lane > m), 1, 0))
        rank = rank + ahead
    return jnp.where(past & (rank < MB_TOPK), 1.0, 0.0)


def _topk_blocks_t(gate_t, n_cand, n_past):
    blk_id = lax.broadcasted_iota(I32, gate_t.shape, 0)
    past = blk_id < n_past
    g = jnp.where(past, gate_t, NEG_INF)
    rank = jnp.zeros(gate_t.shape, I32)
    for m in range(n_cand):
        gm = g[m:m + 1, :]
        rank = rank + jnp.where(gm > g, 1, jnp.where((gm == g) & (blk_id > m), 1, 0))
    return past & (rank < MB_TOPK)


def _moba_prompt_kernel(rel_ref, q_ref, k_ref, v_ref, o_ref,
                        kx_ref, vx_ref, km_ref, bias_ref, eye_ref, s_ref, mx_ref, mb_ref, acc_ref, *, n_blk):
    b = pl.program_id(0)
    i = pl.program_id(1)
    blk = MB_BLOCK
    scale = MB_DIM ** -0.5
    heads = range(MB_HEADS)
    hsl = lambda hd: slice(hd * MB_DIM, (hd + 1) * MB_DIM)

    t = k_ref.shape[1]
    own_slot = n_blk - 1

    @pl.when((b == 0) & (i == 0))
    def _():
        r = lax.broadcasted_iota(I32, (blk, blk), 0)
        c = lax.broadcasted_iota(I32, (blk, blk), 1)
        eye_ref[...] = jnp.where(r == c, 1.0, 0.0).astype(BF16)
        kr = lax.broadcasted_iota(I32, (t, MB_DIM), 0)
        kc = lax.broadcasted_iota(I32, (t, MB_DIM), 1)
        blk_onehot = jnp.where(kr // blk == kc, 1.0, 0.0).astype(BF16)
        for hd in heads:
            bias_ref[hd, 0] = jnp.where(c <= r, _rel_bias(rel_ref, hd, r - c), NEG_INF)
            bias_ref[hd, 1] = _rel_bias(rel_ref, hd, r - c + blk)
            kx_ref[hd, :, MB_DIM:] = blk_onehot
            vx_ref[hd, :, MB_DIM:] = jnp.ones((t, MB_DIM), BF16)

    @pl.when(i == 0)
    def _():
        means = [jnp.sum(k_ref[0, n * blk:(n + 1) * blk, :], axis=0, keepdims=True) * (1.0 / blk)
                 for n in range(n_blk)]
        for hd in heads:
            kx_ref[hd, :, :MB_DIM] = k_ref[0][:, hsl(hd)].astype(BF16)
            vx_ref[hd, :, :MB_DIM] = v_ref[0][:, hsl(hd)].astype(BF16)
            km_ref[hd] = _stack_rows([mn[:, hsl(hd)] for mn in means])

    q = q_ref[0]
    blk_id = lax.broadcasted_iota(I32, (GATE_PAD, blk), 0)
    qh = [q[:, hsl(hd)] for hd in heads]
    gates = [_dot_nt_f32acc(km_ref[hd], qh[hd]) for hd in heads]
    qm_t = []
    for hd in heads:
        sel_t = _topk_blocks_t(gates[hd], n_blk, i)
        m_t = jnp.where(sel_t | (blk_id >= i), 0.0, MASK_BIG).astype(BF16)
        qm_t.append(jnp.concatenate([m_t, jnp.zeros((MB_DIM - GATE_PAD, blk), BF16)], axis=0))
    qmask = [_dot_nt(eye_ref[...], qm_t[hd]) for hd in heads]
    qx = [jnp.concatenate([qh[hd].astype(BF16), qmask[hd].astype(BF16)], axis=-1) for hd in heads]

    own0 = pl.multiple_of(i * blk, blk)
    dots = [_dot_nt(qx[hd], kx_ref[hd, pl.ds(own0, blk), :]) for hd in heads]
    for hd in heads:
        s = dots[hd] * scale + bias_ref[hd, 0]
        s_ref[hd, own_slot] = s
        mx_ref[hd] = jnp.maximum(s[:, :MB_DIM], s[:, MB_DIM:])

    for n in range(n_blk - 1):
        @pl.when(n < i)
        def _(n=n):
            dots = [_dot_nt(qx[hd], kx_ref[hd, n * blk:(n + 1) * blk, :]) for hd in heads]
            for hd in heads:
                far_bias = rel_ref[REL_BUCKETS - 1, hd]
                s = dots[hd] * scale + jnp.where(n == i - 1, bias_ref[hd, 1], far_bias)
                s_ref[hd, n] = s
                mx_ref[hd] = jnp.maximum(mx_ref[hd], jnp.maximum(s[:, :MB_DIM], s[:, MB_DIM:]))

    row_max = [jnp.max(mx_ref[hd], axis=-1, keepdims=True) for hd in heads]
    for hd in heads:
        mb_ref[hd] = jnp.broadcast_to(row_max[hd], (blk, 2 * MB_DIM))
    probs = [jnp.exp(s_ref[hd, own_slot] - mb_ref[hd]).astype(BF16) for hd in heads]
    for hd in heads:
        acc_ref[hd] = _dot(probs[hd], vx_ref[hd, pl.ds(own0, blk), :])

    for n in range(n_blk - 1):
        @pl.when(n < i)
        def _(n=n):
            probs = [jnp.exp(s_ref[hd, n] - mb_ref[hd]).astype(BF16) for hd in heads]
            pv = [_dot(probs[hd], vx_ref[hd, n * blk:(n + 1) * blk, :]) for hd in heads]
            for hd in heads:
                acc_ref[hd] += pv[hd]

    for hd in heads:
        acc = acc_ref[hd]
        o_ref[0, :, hsl(hd)] = (acc[:, :MB_DIM] / acc[:, MB_DIM:]).astype(o_ref.dtype)


def _moba_prompt(h3, rel_table):
    b, t, _ = h3.shape
    assert t % MB_BLOCK == 0
    n_blk = t // MB_BLOCK
    assert n_blk <= GATE_PAD
    qcol, kcol, vcol = COL_MQ // MB_WIDTH, COL_MK // MB_WIDTH, COL_MV // MB_WIDTH
    kern = functools.partial(_moba_prompt_kernel, n_blk=n_blk)
    wide = 2 * MB_DIM
    vmem = (2 * 2 * t * MB_WIDTH * 4 + 2 * MB_HEADS * t * wide * 2
            + MB_HEADS * (n_blk + 4) * MB_BLOCK * wide * 4) // MIB + 12
    return pl.pallas_call(
        kern,
        grid=(b, n_blk),
        in_specs=[pl.BlockSpec(memory_space=pltpu.SMEM),
                  pl.BlockSpec((1, MB_BLOCK, MB_WIDTH), lambda bb, i: (bb, i, qcol)),
                  pl.BlockSpec((1, t, MB_WIDTH), lambda bb, i: (bb, 0, kcol)),
                  pl.BlockSpec((1, t, MB_WIDTH), lambda bb, i: (bb, 0, vcol))],
        out_specs=pl.BlockSpec((1, MB_BLOCK, MB_WIDTH), lambda bb, i: (bb, i, 0)),
        out_shape=jax.ShapeDtypeStruct((b, t, MB_WIDTH), _operand_dtype(t)),
        scratch_shapes=[pltpu.VMEM((MB_HEADS, t, wide), BF16), pltpu.VMEM((MB_HEADS, t, wide), BF16),
                        pltpu.VMEM((MB_HEADS, GATE_PAD, MB_DIM), F32),
                        pltpu.VMEM((MB_HEADS, 2, MB_BLOCK, MB_BLOCK), F32),
                        pltpu.VMEM((MB_BLOCK, MB_BLOCK), BF16),
                        pltpu.VMEM((MB_HEADS, n_blk, MB_BLOCK, MB_BLOCK), F32),
                        pltpu.VMEM((MB_HEADS, MB_BLOCK, MB_DIM), F32),
                        pltpu.VMEM((MB_HEADS, MB_BLOCK, wide), F32),
                        pltpu.VMEM((MB_HEADS, MB_BLOCK, wide), F32)],
        compiler_params=_params(("arbitrary", "arbitrary"), vmem),
        name="moba_prompt",
    )(rel_table, h3, h3, h3)


def _moba_sample_kernel(pt_ref, rel_ref, q_ref, kn_ref, vn_ref, *refs, n_pages, past_len):
    del pt_ref
    kp = refs[:n_pages]
    vp = refs[n_pages:2 * n_pages]
    o_ref, bias_ref, bias_own_ref = refs[2 * n_pages:]
    t = q_ref.shape[1]
    rows = MB_HEADS * t
    page_rows = PAGE_SIZE * MB_HEADS
    pages_per_blk = MB_BLOCK // PAGE_SIZE
    n_past = past_len // MB_BLOCK
    scale = MB_DIM ** -0.5

    def head_rows(ref):
        x = ref[0]
        return jnp.concatenate([x[:, hd * MB_DIM:(hd + 1) * MB_DIM] for hd in range(MB_HEADS)], axis=0)

    @pl.when(pl.program_id(0) == 0)
    def _():
        tq = lax.broadcasted_iota(I32, (t, page_rows), 0)
        kc = lax.broadcasted_iota(I32, (t, page_rows), 1)
        for p in range(n_pages):
            dist = past_len + tq - (p * PAGE_SIZE + (kc >> 2))
            bias_ref[p] = jnp.concatenate(
                [jnp.where((kc & (MB_HEADS - 1)) == hd, _rel_bias(rel_ref, hd, dist), NEG_INF)
                 for hd in range(MB_HEADS)], axis=0)
        ro = lax.broadcasted_iota(I32, (t, rows), 0)
        co = lax.broadcasted_iota(I32, (t, rows), 1)
        for hd in range(MB_HEADS):
            own = _rel_bias(rel_ref, hd, ro - (co - hd * t))
            keep = (co >= hd * t) & (co <= hd * t + ro)
            bias_own_ref[hd * t:(hd + 1) * t, :] = jnp.where(keep, own, NEG_INF)

    q = head_rows(q_ref)
    qb = q.astype(BF16)
    row_head = lax.broadcasted_iota(I32, (rows, 1), 0) // t

    groups = page_rows // 8
    sums = [jnp.sum(kp[p][0].reshape(groups, 8, MB_DIM), axis=0) for p in range(n_pages)]
    blk_sums = []
    for n in range(n_past):
        acc = sums[n * pages_per_blk]
        for pp in range(1, pages_per_blk):
            acc = acc + sums[n * pages_per_blk + pp]
        blk_sums.append(acc)
    g_full = _dot_nt_f32acc(q, jnp.concatenate(blk_sums, axis=0))
    gc = lax.broadcasted_iota(I32, g_full.shape, 1)
    g_full = jnp.where((gc & (MB_HEADS - 1)) == row_head, g_full, 0.0)
    lane = lax.broadcasted_iota(I32, (rows, GATE_PAD), 1)
    gate = jnp.zeros((rows, GATE_PAD), F32)
    for n in range(n_past):
        g_n = jnp.sum(g_full[:, n * 8:(n + 1) * 8], axis=-1, keepdims=True) * (1.0 / MB_BLOCK)
        gate = jnp.where(lane == n, g_n, gate)
    sel = _topk_blocks(gate, n_past, n_past)

    s_own = _dot_nt(q, head_rows(kn_ref)) * scale + bias_own_ref[...]
    m = jnp.max(s_own, axis=-1, keepdims=True)
    s_past = []
    for p in range(n_pages):
        n = p // pages_per_blk
        s = _dot_nt(qb, kp[p][0].astype(BF16)) * scale + bias_ref[p]
        s = jnp.where(sel[:, n:n + 1] > 0.0, s, NEG_INF)
        s_past.append(s)
        m = jnp.maximum(m, jnp.max(s, axis=-1, keepdims=True))
    p_own = jnp.exp(s_own - m)
    l = jnp.sum(p_own, axis=-1, keepdims=True)
    out = _dot(p_own, head_rows(vn_ref))
    for p in range(n_pages):
        pr = jnp.exp(s_past[p] - m)
        l = l + jnp.sum(pr, axis=-1, keepdims=True)
        out = out + _dot(pr.astype(BF16), vp[p][0].astype(BF16))
    out = out / l
    for hd in range(MB_HEADS):
        o_ref[0, :, hd * MB_DIM:(hd + 1) * MB_DIM] = out[hd * t:(hd + 1) * t]


def _moba_sample(h3, cache_k, cache_v, page_table, rel_table):
    db, t, _ = h3.shape
    n_pages = page_table.shape[1]
    past_len = n_pages * PAGE_SIZE
    assert past_len % MB_BLOCK == 0 and t <= MB_BLOCK and past_len // MB_BLOCK < GATE_PAD
    assert MB_HEADS == 4 and t % 8 == 0
    qcol, kcol, vcol = COL_MQ // MB_WIDTH, COL_MK // MB_WIDTH, COL_MV // MB_WIDTH
    page_rows = PAGE_SIZE * MB_HEADS
    new = lambda k: pl.BlockSpec((1, t, MB_WIDTH), lambda i, pt: (i, 0, k))
    page = lambda p: pl.BlockSpec((1, page_rows, MB_DIM), lambda i, pt: (pt[i, p], 0, 0))
    kern = functools.partial(_moba_sample_kernel, n_pages=n_pages, past_len=past_len)
    grid_spec = pltpu.PrefetchScalarGridSpec(
        num_scalar_prefetch=1,
        grid=(db,),
        in_specs=([pl.BlockSpec(memory_space=pltpu.SMEM), new(qcol), new(kcol), new(vcol)]
                  + [page(p) for p in range(n_pages)] * 2),
        out_specs=pl.BlockSpec((1, t, MB_WIDTH), lambda i, pt: (i, 0, 0)),
        scratch_shapes=[pltpu.VMEM((n_pages, MB_HEADS * t, page_rows), F32),
                        pltpu.VMEM((MB_HEADS * t, MB_HEADS * t), F32)],
    )
    return pl.pallas_call(
        kern,
        grid_spec=grid_spec,
        out_shape=jax.ShapeDtypeStruct((db, t, MB_WIDTH), F32),
        compiler_params=_params(("arbitrary",), 40),
        name="moba_sample",
    )(page_table, rel_table, h3, h3, h3, *([cache_k] * n_pages), *([cache_v] * n_pages))


def _merge_kernel(oa_ref, ob_ref, ga0_ref, ga1_ref, gb0_ref, gb1_ref, x_ref,
                  wa_ref, wb_ref, wo_ref, gx_ref, wq_ref, x1_ref, xq_ref):
    ga = jnp.concatenate([ga0_ref[...], ga1_ref[...]], axis=-1)
    gb = jnp.concatenate([gb0_ref[...], gb1_ref[...]], axis=-1)
    pa = _dot(oa_ref[...].astype(BF16), wa_ref[...])
    pb = _dot(ob_ref[...].astype(BF16), wb_ref[...])
    merged = _sigmoid(ga) * pa + _sigmoid(gb) * pb
    x1 = x_ref[...] + _dot(merged.astype(BF16), wo_ref[...])
    x1_ref[...] = x1
    xq_ref[...] = _dot(_rmsnorm(x1, gx_ref[...]).astype(BF16), wq_ref[...]).astype(xq_ref.dtype)


def _merge(oa, ob, h2, x, wa, wb, wo, gx, wq, *, tm, xq_dtype):
    n = x.shape[0]
    assert n % tm == 0
    half = D_MODEL // 2
    tok = lambda w, k=0: pl.BlockSpec((tm, w), lambda i: (i, k))
    full = lambda a: pl.BlockSpec(a.shape, lambda i: (0, 0))
    gx2 = gx.reshape(1, D_MODEL)
    return pl.pallas_call(
        _merge_kernel,
        grid=(n // tm,),
        in_specs=[tok(HG_WIDTH), tok(MB_WIDTH),
                  tok(half, COL_GA // half), tok(half, COL_GA // half + 1),
                  tok(half, COL_GB // half), tok(half, COL_GB // half + 1),
                  tok(D_MODEL), full(wa), full(wb), full(wo), full(gx2), full(wq)],
        out_specs=[tok(D_MODEL), tok(XA_WIDTH)],
        out_shape=[jax.ShapeDtypeStruct((n, D_MODEL), F32),
                   jax.ShapeDtypeStruct((n, XA_WIDTH), xq_dtype)],
        compiler_params=_params(("parallel",), 48),
        name="merge_mix",
    )(oa, ob, h2, h2, h2, h2, x, wa, wb, wo, gx2, wq)


def _mem_attn_kernel(q_ref, k_ref, v_ref, o_ref):
    scale = XA_DIM ** -0.5
    heads = range(XA_HEADS)
    hsl = lambda hd: slice(hd * XA_DIM, (hd + 1) * XA_DIM)
    logits = [_dot_nt(q_ref[0][:, hsl(hd)].astype(BF16), k_ref[0][:, hsl(hd)].astype(BF16)) * scale
              for hd in heads]
    row_max = [jnp.max(s, axis=-1, keepdims=True) for s in logits]
    probs = [jnp.exp(s - m) for s, m in zip(logits, row_max)]
    denom = [jnp.sum(p, axis=-1, keepdims=True) for p in probs]
    outs = [_dot(probs[hd].astype(BF16), v_ref[0][:, hsl(hd)].astype(BF16)) for hd in heads]
    for hd in heads:
        o_ref[0, :, hsl(hd)] = (outs[hd] / denom[hd]).astype(o_ref.dtype)


def _mem_attn(xq3, mem_kv, *, tq):
    b, t, _ = xq3.shape
    assert t % tq == 0
    mem = lambda k: pl.BlockSpec((1, mem_kv.shape[1], XA_WIDTH), lambda i, j: (i, 0, k))
    qs = pl.BlockSpec((1, tq, XA_WIDTH), lambda i, j: (i, j, 0))
    return pl.pallas_call(
        _mem_attn_kernel,
        grid=(b, t // tq),
        in_specs=[qs, mem(0), mem(1)],
        out_specs=qs,
        out_shape=jax.ShapeDtypeStruct((b, t, XA_WIDTH), _operand_dtype(t)),
        compiler_params=_params(("parallel", "parallel"), 32),
        name="mem_attn",
    )(xq3, mem_kv, mem_kv)


def _mem_attn_rows_kernel(q_ref, k_ref, v_ref, o_ref):
    bb, t, _ = q_ref.shape
    rows = XA_HEADS * t
    scale = XA_DIM ** -0.5
    row_head = lax.broadcasted_iota(I32, (rows, 1), 0) // t
    kc = lax.broadcasted_iota(I32, (rows, k_ref.shape[1]), 1)
    same_head = (kc & (XA_HEADS - 1)) == row_head
    qs = []
    for i in range(bb):
        x = q_ref[i]
        qs.append(jnp.concatenate([x[:, hd * XA_DIM:(hd + 1) * XA_DIM] for hd in range(XA_HEADS)],
                                  axis=0).astype(BF16))
    logits = [jnp.where(same_head, _dot_nt(qs[i], k_ref[i].astype(BF16)) * scale, NEG_INF)
              for i in range(bb)]
    row_max = [jnp.max(s, axis=-1, keepdims=True) for s in logits]
    probs = [jnp.exp(s - m) for s, m in zip(logits, row_max)]
    denom = [jnp.sum(p, axis=-1, keepdims=True) for p in probs]
    outs = [_dot(probs[i].astype(BF16), v_ref[i].astype(BF16)) / denom[i] for i in range(bb)]
    for i in range(bb):
        for hd in range(XA_HEADS):
            o_ref[i, :, hd * XA_DIM:(hd + 1) * XA_DIM] = outs[i][hd * t:(hd + 1) * t]


def _mem_attn_rows(xq3, mem_k, mem_v, *, bb):
    b, t, _ = xq3.shape
    assert b % bb == 0 and XA_HEADS == 4 and t % 8 == 0
    mem = pl.BlockSpec((bb,) + mem_k.shape[1:], lambda i: (i, 0, 0))
    qs = pl.BlockSpec((bb, t, XA_WIDTH), lambda i: (i, 0, 0))
    return pl.pallas_call(
        _mem_attn_rows_kernel,
        grid=(b // bb,),
        in_specs=[qs, mem, mem],
        out_specs=qs,
        out_shape=jax.ShapeDtypeStruct((b, t, XA_WIDTH), F32),
        compiler_params=_params(("parallel",), 32),
        name="mem_attn_rows",
    )(xq3, mem_k, mem_v)


def _route(logits):
    lane = lax.broadcasted_iota(I32, logits.shape, 1).astype(F32)
    first = lambda hit: jnp.min(jnp.where(hit, lane, float(ROUTER_LANES)), axis=-1, keepdims=True)
    gl = jnp.where(lane < N_GROUPS, logits, NEG_INF)
    gmax = jnp.max(gl, axis=-1, keepdims=True)
    grp = first(gl == gmax)
    g_prob = 1.0 / jnp.sum(jnp.exp(gl - gmax), axis=-1, keepdims=True)
    e_lo = EXPERT_LANE0 + grp * EXPERTS_PER_GROUP
    in_grp = (lane >= e_lo) & (lane < e_lo + EXPERTS_PER_GROUP)
    el = jnp.where(in_grp, logits, NEG_INF)
    top1 = jnp.max(el, axis=-1, keepdims=True)
    idx1 = first(el == top1)
    el2 = jnp.where(lane == idx1, NEG_INF, el)
    top2 = jnp.max(el2, axis=-1, keepdims=True)
    idx2 = first(el2 == top2)
    e2 = jnp.exp(top2 - top1)
    w1 = g_prob / (1.0 + e2)
    w2 = w1 * e2
    comb = jnp.where(lane == idx1, w1, 0.0) + jnp.where(lane == idx2, w2, 0.0)
    return jnp.where(lane == GRP_LANE, grp, comb)


def _xo_route_kernel(x1_ref, at_ref, wxo_ref, gf_ref, wr_ref, br_ref, x2_ref, xf_ref, comb_ref):
    x2 = x1_ref[...] + _dot(at_ref[...].astype(BF16), wxo_ref[...])
    x2_ref[...] = x2
    xf = _rmsnorm(x2, gf_ref[...])
    xf_ref[...] = xf.astype(BF16)
    comb_ref[...] = _route(_dot_f32acc(xf, wr_ref[...]) + br_ref[...])


def _xo_route(x1, attn, wxo, g_ffn, w_router_t, b_router_t, *, tm):
    n = x1.shape[0]
    assert n % tm == 0
    tok = lambda w: pl.BlockSpec((tm, w), lambda i: (i, 0))
    full = lambda a: pl.BlockSpec(a.shape, lambda i: (0, 0))
    gf2 = g_ffn.reshape(1, D_MODEL)
    return pl.pallas_call(
        _xo_route_kernel,
        grid=(n // tm,),
        in_specs=[tok(D_MODEL), tok(XA_WIDTH), full(wxo), full(gf2), full(w_router_t), full(b_router_t)],
        out_specs=[tok(D_MODEL), tok(D_MODEL), tok(ROUTER_LANES)],
        out_shape=[jax.ShapeDtypeStruct((n, D_MODEL), F32), jax.ShapeDtypeStruct((n, D_MODEL), BF16),
                   jax.ShapeDtypeStruct((n, ROUTER_LANES), F32)],
        compiler_params=_params(("parallel",), 40),
        name="xo_route",
    )(x1, attn, wxo, gf2, w_router_t, b_router_t)


def _moe_kernel(xf_ref, comb_ref, x2_ref, wg_ref, wu_ref, wd_ref, gn_ref, y_ref,
                tri_ref, posc_ref, cnt_ref, xs_ref, ws_ref, ys_ref, *, r):
    i = pl.program_id(0)
    step = pl.program_id(1)
    tm = xf_ref.shape[0]
    per_step = wg_ref.shape[0]
    steps_per_group = EXPERTS_PER_GROUP // per_step
    g = step // steps_per_group
    lane = lax.broadcasted_iota(I32, (tm, ROUTER_LANES), 1)

    @pl.when((i == 0) & (step == 0))
    def _():
        rr = lax.broadcasted_iota(I32, (tm, tm), 0)
        cc = lax.broadcasted_iota(I32, (tm, tm), 1)
        tri_ref[...] = jnp.where(cc < rr, 1.0, 0.0).astype(BF16)

    @pl.when(step == 0)
    def _():
        comb = comb_ref[...]
        grp = comb[:, GRP_LANE:GRP_LANE + 1]
        lane_f = lane.astype(F32)
        rel = lane - EXPERT_LANE0
        used = (comb > 0.0) & (rel >= 0) & (rel < N_EXPERTS)
        second = (rel & (EXPERTS_PER_GROUP - 1)) >= EXPERTS_PER_GROUP // 2
        uses_a = jnp.max(jnp.where(used & jnp.logical_not(second), 1.0, 0.0), axis=-1, keepdims=True)
        uses_b = jnp.max(jnp.where(used & second, 1.0, 0.0), axis=-1, keepdims=True)
        key = grp * MOE_CLASSES + jnp.where(uses_b == 0.0, 0.0, jnp.where(uses_a == 0.0, 2.0, 1.0))
        onehot = jnp.where((lane_f == key) & (lane < N_GROUPS * MOE_CLASSES), 1.0, 0.0)
        before = _dot(tri_ref[...], onehot.astype(BF16))
        cnt = jnp.sum(onehot, axis=0, keepdims=True)
        cnt_ref[...] = cnt
        pos = jnp.sum(jnp.where(lane_f < key, cnt, 0.0) + onehot * before, axis=-1, keepdims=True)
        posc_ref[...] = jnp.where(lane == 0, pos, 0.0)
        hi = jnp.floor(pos * (1.0 / POS_SPLIT))
        lo = pos - hi * POS_SPLIT
        cols = jnp.where(lane == 0, hi, jnp.where(lane == 1, lo, 0.0))
        er = lax.broadcasted_iota(I32, (16, ROUTER_LANES), 0)
        ec = lax.broadcasted_iota(I32, (16, ROUTER_LANES), 1)
        post = _dot_nt(jnp.where(er == ec, 1.0, 0.0).astype(BF16), cols.astype(BF16))
        pos_row = post[0:1, :] * POS_SPLIT + post[1:2, :]
        slot = lax.broadcasted_iota(I32, (tm, tm), 0).astype(F32)
        perm = jnp.where(pos_row == slot, 1.0, 0.0).astype(BF16)
        comb_hi, comb_lo = _split2(comb)
        xs_ref[0:tm, :] = _dot(perm, xf_ref[...]).astype(BF16)
        ws_ref[0:tm, :] = _dot(perm, comb_hi) + _dot(perm, comb_lo)
        xs_ref[tm:, :] = jnp.zeros((r, D_MODEL), BF16)
        ws_ref[tm:, :] = jnp.zeros((r, ROUTER_LANES), F32)
        ys_ref[...] = jnp.zeros(ys_ref.shape, F32)

    lane_row = lax.broadcasted_iota(I32, (1, ROUTER_LANES), 1)
    key0 = g * MOE_CLASSES + step % steps_per_group
    first = jnp.sum(jnp.where(lane_row < key0, cnt_ref[...], 0.0)).astype(I32)
    count = jnp.sum(jnp.where((lane_row == key0) | (lane_row == key0 + 1), cnt_ref[...], 0.0)).astype(I32)
    start = (first // 16) * 16
    n_windows = (first + count - start + (r - 1)) // r

    def experts(c, carry):
        rows = pl.ds(pl.multiple_of(start + c * r, 16), r)
        xg = xs_ref[rows, :]
        wq = ws_ref[rows, :]
        wl = lax.broadcasted_iota(I32, (r, ROUTER_LANES), 1)
        ups = [(_dot(xg, wg_ref[k]), _dot(xg, wu_ref[k])) for k in range(per_step)]
        hidden = []
        for k, (gate, up) in enumerate(ups):
            w_e = jnp.sum(jnp.where(wl == EXPERT_LANE0 + step * per_step + k, wq, 0.0), axis=-1, keepdims=True)
            hidden.append((_silu(gate) * up * w_e).astype(BF16))
        w_down = wd_ref[...].reshape(per_step * EXPERT_HIDDEN, D_MODEL)
        ys_ref[rows, :] += _dot(jnp.concatenate(hidden, axis=-1), w_down)
        return carry

    lax.fori_loop(0, n_windows, experts, 0)

    @pl.when(step == pl.num_programs(1) - 1)
    def _():
        slot = lax.broadcasted_iota(I32, (tm, tm), 1).astype(F32)
        place = jnp.where(posc_ref[:, 0:1] == slot, 1.0, 0.0).astype(BF16)
        moe = _dot(place, ys_ref[0:tm, :].astype(BF16))
        y_ref[...] = _rmsnorm(x2_ref[...] + moe, gn_ref[...])


def _moe(xf, comb, x2, wg, wu, wd, g_final, *, tm):
    n = xf.shape[0]
    assert n % tm == 0 and tm % 16 == 0
    assert 2 * MOE_EXPERTS_PER_STEP == EXPERTS_PER_GROUP and N_GROUPS * MOE_CLASSES <= ROUTER_LANES
    r = min(MOE_CHUNK_ROWS, tm)
    cap = tm + r
    per_step = MOE_EXPERTS_PER_STEP
    tok = lambda w: pl.BlockSpec((tm, w), lambda i, s: (i, 0))
    gn2 = g_final.reshape(1, D_MODEL)
    kern = functools.partial(_moe_kernel, r=r)
    vmem = (2 * tm * D_MODEL * (2 + 4 + 4) + 2 * tm * ROUTER_LANES * 4
            + 2 * 3 * per_step * D_MODEL * EXPERT_HIDDEN * 2 + tm * tm * 2
            + cap * D_MODEL * (2 + 4) + tm * D_MODEL * 4) // MIB + 8
    return pl.pallas_call(
        kern,
        grid=(n // tm, N_EXPERTS // per_step),
        in_specs=[tok(D_MODEL), tok(ROUTER_LANES), tok(D_MODEL),
                  pl.BlockSpec((per_step, D_MODEL, EXPERT_HIDDEN), lambda i, s: (s, 0, 0)),
                  pl.BlockSpec((per_step, D_MODEL, EXPERT_HIDDEN), lambda i, s: (s, 0, 0)),
                  pl.BlockSpec((per_step, EXPERT_HIDDEN, D_MODEL), lambda i, s: (s, 0, 0)),
                  pl.BlockSpec((1, D_MODEL), lambda i, s: (0, 0))],
        out_specs=tok(D_MODEL),
        out_shape=jax.ShapeDtypeStruct((n, D_MODEL), F32),
        scratch_shapes=[pltpu.VMEM((tm, tm), BF16),
                        pltpu.VMEM((tm, ROUTER_LANES), F32), pltpu.VMEM((1, ROUTER_LANES), F32),
                        pltpu.VMEM((cap, D_MODEL), BF16), pltpu.VMEM((cap, ROUTER_LANES), F32),
                        pltpu.VMEM((cap, D_MODEL), F32)],
        compiler_params=_params(("arbitrary", "arbitrary"), vmem),
        name="moe_grouped",
    )(xf, comb, x2, wg, wu, wd, gn2)


def _token_tile(n, cap=512):
    return cap if n % cap == 0 else n


def _col_tile(n, cap=1536):
    return max(c for c in range(V7X_LANES, cap + 1, V7X_LANES) if n % c == 0)


def _layer(x3, w, s0, moba_fn, mem_fn):
    b, t, _ = x3.shape
    n = b * t
    x2d = x3.reshape(n, D_MODEL)
    tm = _token_tile(n)
    h2, k4, v4 = _norm_matmul(x2d, w["norm_mix"], w["w_in"], tm=_token_tile(n, 1024),
                              tn=_col_tile(IN_COLS), head_cols=(COL_MK, COL_MV), heads=MB_HEADS,
                              head_dim=MB_DIM)
    h3 = h2.reshape(b, t, IN_COLS)
    oa, s_new = _hgrn(h3, w["hg_lb_logits"], w["hg_norm"], s0)
    ob = moba_fn(h3)
    x1, xq = _merge(oa.reshape(n, HG_WIDTH), ob.reshape(n, MB_WIDTH), h2, x2d,
                    w["w_branch_a"], w["w_branch_b"], w["w_mix_out"], w["norm_xattn"], w["w_xq"], tm=tm,
                    xq_dtype=_operand_dtype(t))
    attn = mem_fn(xq.reshape(b, t, XA_WIDTH))
    x2, xf, comb = _xo_route(x1, attn.reshape(n, XA_WIDTH), w["w_xo"], w["norm_ffn"],
                             w["w_router"], w["b_router"], tm=_token_tile(n, 1024))
    y = _moe(xf, comb, x2, w["w_expert_gate"], w["w_expert_up"], w["w_expert_down"],
             w["norm_final"], tm=_token_tile(n, 1024))
    k_new = k4.reshape(b, t, MB_HEADS, MB_DIM)
    v_new = v4.reshape(b, t, MB_HEADS, MB_DIM)
    return y.reshape(b, t, D_MODEL), k_new, v_new, s_new


def kernel(x_prompt, x_sample, cache_k, cache_v, state_hgrn, cache_mem_k, cache_mem_v, page_table, mem_prompt, norm_mix, w_in, hg_lb_logits, hg_norm, w_branch_a, w_branch_b, w_mix_out, rel_table, norm_xattn, norm_mem, w_xq, w_xk, w_xv, w_xo, norm_ffn, w_group_router, b_group_router, w_expert_router, b_expert_router, w_expert_gate, w_expert_up, w_expert_down, norm_final):
    assert w_in.shape[0] == DEPTH == 1 and hg_lb_logits.shape[0] == DEPTH + 1
    b = x_prompt.shape[0]
    db = x_sample.shape[0]
    n_pool = cache_k.shape[1]
    pad_lanes = ROUTER_LANES - EXPERT_LANE0 - N_EXPERTS
    w = {
        "norm_mix": norm_mix[0], "w_in": w_in[0].astype(BF16),
        "hg_lb_logits": hg_lb_logits, "hg_norm": hg_norm[0],
        "w_branch_a": w_branch_a[0].astype(BF16), "w_branch_b": w_branch_b[0].astype(BF16),
        "w_mix_out": w_mix_out[0].astype(BF16), "norm_xattn": norm_xattn[0],
        "w_xq": w_xq[0].astype(BF16), "w_xo": w_xo[0].astype(BF16), "norm_ffn": norm_ffn[0],
        "w_router": jnp.pad(jnp.concatenate(
            [w_group_router[0], jnp.zeros((D_MODEL, EXPERT_LANE0 - N_GROUPS), F32), w_expert_router[0]],
            axis=1), ((0, 0), (0, pad_lanes))),
        "b_router": jnp.pad(jnp.concatenate(
            [b_group_router[0], jnp.zeros((EXPERT_LANE0 - N_GROUPS,), F32), b_expert_router[0]]),
            (0, pad_lanes)).reshape(1, ROUTER_LANES),
        "w_expert_gate": w_expert_gate[0].reshape(N_EXPERTS, D_MODEL, EXPERT_HIDDEN).astype(BF16),
        "w_expert_up": w_expert_up[0].reshape(N_EXPERTS, D_MODEL, EXPERT_HIDDEN).astype(BF16),
        "w_expert_down": w_expert_down[0].reshape(N_EXPERTS, EXPERT_HIDDEN, D_MODEL).astype(BF16),
        "norm_final": norm_final,
    }

    w_mem = jnp.concatenate([w_xk[0], w_xv[0]], axis=1).astype(BF16)
    mem_kv, mk_p, mv_p = _norm_matmul(mem_prompt.reshape(b * MEM_LEN, D_MODEL), norm_mem[0], w_mem,
                                      tm=_token_tile(b * MEM_LEN), tn=512, head_cols=(0, XA_WIDTH),
                                      heads=XA_HEADS, head_dim=XA_DIM)
    mem_kv = mem_kv.reshape(b, MEM_LEN, 2 * XA_WIDTH)
    s0 = jnp.zeros((b, HG_HEADS, HG_DIM, HG_DIM), F32)
    y_p, k_p, v_p, s_p = _layer(x_prompt, w, s0,
                                functools.partial(_moba_prompt, rel_table=rel_table),
                                functools.partial(_mem_attn, mem_kv=mem_kv, tq=512))

    ck = cache_k[0].reshape(n_pool, PAGE_SIZE * MB_HEADS, MB_DIM)
    cv = cache_v[0].reshape(n_pool, PAGE_SIZE * MB_HEADS, MB_DIM)
    moba_s = functools.partial(_moba_sample, cache_k=ck, cache_v=cv, page_table=page_table,
                               rel_table=rel_table)
    mem_s = functools.partial(_mem_attn_rows, bb=math.gcd(db, 8),
                              mem_k=cache_mem_k[0].reshape(db, MEM_LEN * XA_HEADS, XA_DIM),
                              mem_v=cache_mem_v[0].reshape(db, MEM_LEN * XA_HEADS, XA_DIM))
    y_s, k_s, v_s, s_s = _layer(x_sample, w, state_hgrn[0], moba_s, mem_s)

    heads = lambda a: a.reshape(b, MEM_LEN, XA_HEADS, XA_DIM)[None]
    return (y_p, y_s, k_p[None], v_p[None], s_p[None], heads(mk_p), heads(mv_p),
            k_s[None], v_s[None], s_s[None])
```

```python
import functools
import math

import numpy as np
import jax
import jax.numpy as jnp
from jax import lax
from jax.experimental import pallas as pl
from jax.experimental.pallas import tpu as pltpu

F32 = jnp.float32
BF16 = jnp.bfloat16
I32 = jnp.int32

D_MODEL = 1024
DEPTH = 1
PAGE_SIZE = 128
HG_HEADS = 4
HG_DIM = 128
HG_WIDTH = HG_HEADS * HG_DIM
MB_HEADS = 4
MB_DIM = 128
MB_WIDTH = MB_HEADS * MB_DIM
MB_BLOCK = 256
MB_TOPK = 3
REL_BUCKETS = 32
REL_MAX_DIST = 128
REL_MAX_EXACT = REL_BUCKETS // 2
MEM_LEN = 256
XA_HEADS = 4
XA_DIM = 128
XA_WIDTH = XA_HEADS * XA_DIM
N_GROUPS = 4
EXPERTS_PER_GROUP = 8
N_EXPERTS = N_GROUPS * EXPERTS_PER_GROUP
EXPERT_TOPK = 2
EXPERT_HIDDEN = 256
NORM_EPS = 1e-6
IN_COLS = 4 * HG_WIDTH + 3 * MB_WIDTH + 2 * D_MODEL
COL_MQ = 4 * HG_WIDTH
COL_MK = COL_MQ + MB_WIDTH
COL_MV = COL_MK + MB_WIDTH
COL_GA = COL_MV + MB_WIDTH
COL_GB = COL_GA + D_MODEL

V7X_LANES = 128
V7X_BF16_SUBLANES = 16
V7X_VMEM_BYTES = 64 * 1024 * 1024
MIB = 1024 * 1024

NEG_INF = float("-inf")
MASK_BIG = -1e30
GATE_PAD = V7X_BF16_SUBLANES
ROUTER_LANES = 128
GRP_LANE = 0
EXPERT_LANE0 = 8
POS_SPLIT = 32.0
MOE_CHUNK_ROWS = 256
MOE_EXPERTS_PER_STEP = EXPERTS_PER_GROUP // 2
MOE_CLASSES = 3
HGRN_PROMPT_CHUNK = 128
HGRN_PROMPT_ROWS = 4
HGRN_MIN_CHUNK = V7X_BF16_SUBLANES
HGRN_CHAINS = 32


def _params(semantics, vmem_mib):
    return pltpu.CompilerParams(dimension_semantics=semantics,
                                vmem_limit_bytes=min(vmem_mib * MIB, V7X_VMEM_BYTES - 8 * MIB))


def _dot(a, b):
    return jnp.dot(a, b, preferred_element_type=F32)


def _dot_nt(a, b):
    return lax.dot_general(a, b, (((1,), (1,)), ((), ())), preferred_element_type=F32)


def _dot_tn(a, b):
    return lax.dot_general(a, b, (((0,), (0,)), ((), ())), preferred_element_type=F32)


def _split2(a):
    hi = a.astype(BF16)
    lo = (a - hi.astype(F32)).astype(BF16)
    return hi, lo


def _dot_nt_f32acc(a, b):
    ah, al = _split2(a)
    bh, bl = _split2(b)
    return _dot_nt(ah, bh) + (_dot_nt(ah, bl) + _dot_nt(al, bh))


def _dot_f32acc(a, b):
    ah, al = _split2(a)
    bh, bl = _split2(b)
    return _dot(ah, bh) + (_dot(ah, bl) + _dot(al, bh))


def _operand_dtype(t):
    return BF16 if t % V7X_BF16_SUBLANES == 0 else F32


def _rmsnorm(x, g):
    return x * lax.rsqrt(jnp.mean(x * x, axis=-1, keepdims=True) + NORM_EPS) * g


def _sigmoid(x):
    return 1.0 / (1.0 + jnp.exp(-x))


def _silu(x):
    return x * _sigmoid(x)


def _norm_matmul_kernel(x_ref, g_ref, w_ref, o_ref, *refs, head_cols, heads, head_dim):
    head_refs, xn_ref = refs[:-1], refs[-1]
    j = pl.program_id(1)
    tm, tn = o_ref.shape

    @pl.when(j == 0)
    def _():
        xn_ref[...] = _rmsnorm(x_ref[...], g_ref[...]).astype(BF16)

    res = _dot(xn_ref[...], w_ref[...])
    o_ref[...] = res
    for col0, href in zip(head_cols, head_refs):
        for hd in range(heads):
            tile, local = divmod(col0 + hd * head_dim, tn)

            @pl.when(j == tile)
            def _(hd=hd, local=local, href=href):
                href[pl.ds(hd, tm, stride=heads), :] = res[:, local:local + head_dim]


def _norm_matmul(x, g, w_bf16, *, tm, tn, head_cols=(), heads=1, head_dim=V7X_LANES):
    m, d = x.shape
    n = w_bf16.shape[1]
    assert m % tm == 0 and n % tn == 0 and tn % head_dim == 0
    vmem = (2 * tm * d * 4 + tm * d * 2 + 2 * d * tn * 2 + 2 * tm * tn * 4
            + len(head_cols) * 2 * tm * heads * head_dim * 4) // MIB + 12
    kern = functools.partial(_norm_matmul_kernel, head_cols=tuple(head_cols), heads=heads,
                             head_dim=head_dim)
    head_spec = pl.BlockSpec((tm * heads, head_dim), lambda i, j: (i, 0))
    head_shape = jax.ShapeDtypeStruct((m * heads, head_dim), F32)
    return pl.pallas_call(
        kern,
        grid=(m // tm, n // tn),
        in_specs=[pl.BlockSpec((tm, d), lambda i, j: (i, 0)),
                  pl.BlockSpec((1, d), lambda i, j: (0, 0)),
                  pl.BlockSpec((d, tn), lambda i, j: (0, j))],
        out_specs=[pl.BlockSpec((tm, tn), lambda i, j: (i, j))] + [head_spec] * len(head_cols),
        out_shape=[jax.ShapeDtypeStruct((m, n), F32)] + [head_shape] * len(head_cols),
        scratch_shapes=[pltpu.VMEM((tm, d), BF16)],
        compiler_params=_params(("parallel", "arbitrary"), vmem),
        name="norm_matmul",
    )(x, g.reshape(1, d), w_bf16)


def _hgrn_levels(c):
    return int(round(math.log2(c)))


def _hgrn_sum_masks(c):
    t = np.arange(c)[:, None]
    j = np.arange(c)[None, :]
    rows = [j <= t, j > t]
    for lv in range(_hgrn_levels(c)):
        half = c >> (lv + 1)
        blk = 2 * half
        mid = (t // blk) * blk + half - 1
        upper = (t % blk) >= half
        rows.append((upper & (j > mid) & (j <= t)) | ((~upper) & (j > t) & (j <= mid)))
    return np.concatenate(rows, axis=0).astype(np.float32)


def _hgrn_kernel(hq_ref, hf_ref, hi_ref, hg_ref, lbl_ref, gn_ref, mask_ref, s0_ref,
                 o_ref, s_ref, st_ref, *, c, t_blk, bb):
    j = pl.program_id(1)
    chains = [(bi, hd) for bi in range(bb) for hd in range(HG_HEADS)]

    @pl.when(j == 0)
    def _():
        for bi, hd in chains:
            st_ref[bi * HG_HEADS + hd] = s0_ref[bi, hd].T

    lbl = lbl_ref[...]
    lmax = jnp.max(lbl, axis=0, keepdims=True)
    lexp = jnp.exp(lbl - lmax)
    lb_all = lexp[0:1, :] / jnp.sum(lexp, axis=0, keepdims=True)

    masks = mask_ref[...]
    pad = c - t_blk
    row = lax.broadcasted_iota(I32, (c, HG_DIM), 0)
    rr = lax.broadcasted_iota(I32, (c, c), 0)
    cc = lax.broadcasted_iota(I32, (c, c), 1)

    def padded(a):
        if pad == 0:
            return a
        return jnp.concatenate([a, jnp.zeros((pad, HG_DIM), F32)], axis=0)

    hsl = lambda hd: slice(hd * HG_DIM, (hd + 1) * HG_DIM)
    qs, ks, vs, logf_parts = [], [], [], []
    for bi, hd in chains:
        xq = hq_ref[bi][:, hsl(hd)]
        lb = lb_all[:, hsl(hd)]
        f = lb + (1.0 - lb) * _sigmoid(hf_ref[bi][:, hsl(hd)])
        qs.append(padded(_silu(xq)))
        ks.append(padded(1.0 - f))
        vs.append(padded(hi_ref[bi][:, hsl(hd)]))
        logf_parts.append(_split2(padded(jnp.log(f))))
    exs = []
    for n in range(0, len(chains), 2):
        (hi_a, lo_a), (hi_b, lo_b) = logf_parts[n], logf_parts[n + 1]
        stacked = jnp.concatenate([jnp.concatenate([hi_a, hi_b], axis=-1),
                                   jnp.concatenate([lo_a, lo_b], axis=-1)], axis=0)
        ex = jnp.exp(_dot(masks, stacked))
        exs += [ex[:, :HG_DIM], ex[:, HG_DIM:]]

    level_dots = []
    for q, kk, ex in zip(qs, ks, exs):
        per_level = []
        for lv in range(_hgrn_levels(c)):
            half = c >> (lv + 1)
            upper = (row & half) != 0
            ex_lv = ex[(2 + lv) * c:(3 + lv) * c]
            a = jnp.where(upper, q * ex_lv, 0.0).astype(BF16)
            bm = jnp.where(upper, 0.0, kk * ex_lv).astype(BF16)
            per_level.append(_dot_nt(a, bm))
        level_dots.append(per_level)

    outs = []
    for n, (bi, hd) in enumerate(chains):
        q, kk, v, ex = qs[n], ks[n], vs[n], exs[n]
        scores = jnp.zeros((c, c), F32)
        for lv, d in enumerate(level_dots[n]):
            scores = scores + jnp.where((rr ^ cc) < 2 * (c >> (lv + 1)), d, 0.0)
        eb = ex[0:c]
        eb_rev = ex[c:2 * c]
        diag = jnp.sum(q * kk, axis=-1, keepdims=True)
        st = st_ref[bi * HG_HEADS + hd]
        vb = v.astype(BF16)
        o = (_dot_nt((q * eb).astype(BF16), st.astype(BF16))
             + _dot(scores.astype(BF16), vb) + diag * v)
        st_ref[bi * HG_HEADS + hd] = st * eb[c - 1:c, :] + _dot_tn(vb, (kk * eb_rev).astype(BF16))
        outs.append(o[0:t_blk])

    for (bi, hd), o in zip(chains, outs):
        xg = hg_ref[bi][:, hsl(hd)]
        o_ref[bi, :, hsl(hd)] = (_rmsnorm(o, gn_ref[:, hsl(hd)]) * _silu(xg)).astype(o_ref.dtype)

    @pl.when(j == pl.num_programs(1) - 1)
    def _():
        for bi, hd in chains:
            s_ref[bi, hd] = st_ref[bi * HG_HEADS + hd].T


def _hgrn(h3, lb_logits, hg_norm, s0):
    b, t, _ = h3.shape
    t_blk = math.gcd(t, HGRN_PROMPT_CHUNK)
    c = max(t_blk, HGRN_MIN_CHUNK)
    bb = math.gcd(b, HGRN_CHAINS // HG_HEADS) if t == t_blk else math.gcd(b, HGRN_PROMPT_ROWS)
    m01 = _hgrn_sum_masks(c)
    masks = jnp.asarray(np.concatenate([m01, m01], axis=1), BF16)
    col = lambda k: pl.BlockSpec((bb, t_blk, HG_WIDTH), lambda i, j: (i, j, k))
    state_spec = pl.BlockSpec((bb, HG_HEADS, HG_DIM, HG_DIM), lambda i, j: (i, 0, 0, 0))
    kern = functools.partial(_hgrn_kernel, c=c, t_blk=t_blk, bb=bb)
    return pl.pallas_call(
        kern,
        grid=(b // bb, t // t_blk),
        in_specs=[col(0), col(1), col(2), col(3),
                  pl.BlockSpec(lb_logits.shape, lambda i, j: (0, 0)),
                  pl.BlockSpec((1, HG_WIDTH), lambda i, j: (0, 0)),
                  pl.BlockSpec(masks.shape, lambda i, j: (0, 0)),
                  state_spec],
        out_specs=[pl.BlockSpec((bb, t_blk, HG_WIDTH), lambda i, j: (i, j, 0)), state_spec],
        out_shape=[jax.ShapeDtypeStruct((b, t, HG_WIDTH), _operand_dtype(t)),
                   jax.ShapeDtypeStruct(s0.shape, F32)],
        scratch_shapes=[pltpu.VMEM((bb * HG_HEADS, HG_DIM, HG_DIM), F32)],
        compiler_params=_params(("parallel", "arbitrary"), 32),
        name="hgrn2",
    )(h3, h3, h3, h3, lb_logits, hg_norm.reshape(1, HG_WIDTH), masks, s0)


def _rel_bucket_edges():
    d = np.arange(2 * REL_MAX_DIST)
    df = np.maximum(d, 1).astype(np.float32)
    large = REL_MAX_EXACT + (np.log(df / np.float32(REL_MAX_EXACT))
                             / np.float32(math.log(REL_MAX_DIST / REL_MAX_EXACT))
                             * np.float32(REL_BUCKETS - REL_MAX_EXACT)).astype(np.int32)
    bucket = np.where(d < REL_MAX_EXACT, d, np.minimum(large, REL_BUCKETS - 1))
    assert np.all(np.diff(bucket) >= 0) and bucket[-1] == REL_BUCKETS - 1
    return [int(np.argmax(bucket >= kb)) for kb in range(REL_BUCKETS)]


def _rel_bias(rel_ref, head, dist):
    edges = _rel_bucket_edges()
    out = jnp.full(dist.shape, rel_ref[0, head], F32)
    for kb in range(1, REL_BUCKETS):
        out = jnp.where(dist >= edges[kb], rel_ref[kb, head], out)
    return out


def _stack_rows(rows):
    ridx = lax.broadcasted_iota(I32, (GATE_PAD, MB_DIM), 0)
    out = jnp.zeros((GATE_PAD, MB_DIM), F32)
    for n, r in enumerate(rows):
        out = jnp.where(ridx == n, r, out)
    return out


def _topk_blocks(gate, n_cand, n_past):
    lane = lax.broadcasted_iota(I32, gate.shape, 1)
    past = lane < n_past
    g = jnp.where(past, gate, NEG_INF)
    rank = jnp.zeros(gate.shape, I32)
    for m in range(n_cand):
        gm = g[:, m:m + 1]
        ahead = jnp.where(gm > g, 1, jnp.where((gm == g) & (lane > m), 1, 0))
        rank = rank + ahead
    return jnp.where(past & (rank < MB_TOPK), 1.0, 0.0)


def _topk_blocks_t(gate_t, n_cand, n_past):
    blk_id = lax.broadcasted_iota(I32, gate_t.shape, 0)
    past = blk_id < n_past
    g = jnp.where(past, gate_t, NEG_INF)
    rank = jnp.zeros(gate_t.shape, I32)
    for m in range(n_cand):
        gm = g[m:m + 1, :]
        rank = rank + jnp.where(gm > g, 1, jnp.where((gm == g) & (blk_id > m), 1, 0))
    return past & (rank < MB_TOPK)


def _moba_prompt_kernel(rel_ref, q_ref, k_ref, v_ref, o_ref,
                        kx_ref, vx_ref, km_ref, bias_ref, eye_ref, s_ref, mx_ref, mb_ref, acc_ref, *, n_blk):
    b = pl.program_id(0)
    i = pl.program_id(1)
    blk = MB_BLOCK
    scale = MB_DIM ** -0.5
    heads = range(MB_HEADS)
    hsl = lambda hd: slice(hd * MB_DIM, (hd + 1) * MB_DIM)

    t = k_ref.shape[1]
    own_slot = n_blk - 1

    @pl.when((b == 0) & (i == 0))
    def _():
        r = lax.broadcasted_iota(I32, (blk, blk), 0)
        c = lax.broadcasted_iota(I32, (blk, blk), 1)
        eye_ref[...] = jnp.where(r == c, 1.0, 0.0).astype(BF16)
        kr = lax.broadcasted_iota(I32, (t, MB_DIM), 0)
        kc = lax.broadcasted_iota(I32, (t, MB_DIM), 1)
        blk_onehot = jnp.where(kr // blk == kc, 1.0, 0.0).astype(BF16)
        for hd in heads:
            bias_ref[hd, 0] = jnp.where(c <= r, _rel_bias(rel_ref, hd, r - c), NEG_INF)
            bias_ref[hd, 1] = _rel_bias(rel_ref, hd, r - c + blk)
            kx_ref[hd, :, MB_DIM:] = blk_onehot
            vx_ref[hd, :, MB_DIM:] = jnp.ones((t, MB_DIM), BF16)

    @pl.when(i == 0)
    def _():
        means = [jnp.sum(k_ref[0, n * blk:(n + 1) * blk, :], axis=0, keepdims=True) * (1.0 / blk)
                 for n in range(n_blk)]
        for hd in heads:
            kx_ref[hd, :, :MB_DIM] = k_ref[0][:, hsl(hd)].astype(BF16)
            vx_ref[hd, :, :MB_DIM] = v_ref[0][:, hsl(hd)].astype(BF16)
            km_ref[hd] = _stack_rows([mn[:, hsl(hd)] for mn in means])

    q = q_ref[0]
    blk_id = lax.broadcasted_iota(I32, (GATE_PAD, blk), 0)
    qh = [q[:, hsl(hd)] for hd in heads]
    gates = [_dot_nt_f32acc(km_ref[hd], qh[hd]) for hd in heads]
    qm_t = []
    for hd in heads:
        sel_t = _topk_blocks_t(gates[hd], n_blk, i)
        m_t = jnp.where(sel_t | (blk_id >= i), 0.0, MASK_BIG).astype(BF16)
        qm_t.append(jnp.concatenate([m_t, jnp.zeros((MB_DIM - GATE_PAD, blk), BF16)], axis=0))
    qmask = [_dot_nt(eye_ref[...], qm_t[hd]) for hd in heads]
    qx = [jnp.concatenate([qh[hd].astype(BF16), qmask[hd].astype(BF16)], axis=-1) for hd in heads]

    own0 = pl.multiple_of(i * blk, blk)
    dots = [_dot_nt(qx[hd], kx_ref[hd, pl.ds(own0, blk), :]) for hd in heads]
    for hd in heads:
        s = dots[hd] * scale + bias_ref[hd, 0]
        s_ref[hd, own_slot] = s
        mx_ref[hd] = jnp.maximum(s[:, :MB_DIM], s[:, MB_DIM:])

    for n in range(n_blk - 1):
        @pl.when(n < i)
        def _(n=n):
            dots = [_dot_nt(qx[hd], kx_ref[hd, n * blk:(n + 1) * blk, :]) for hd in heads]
            for hd in heads:
                far_bias = rel_ref[REL_BUCKETS - 1, hd]
                s = dots[hd] * scale + jnp.where(n == i - 1, bias_ref[hd, 1], far_bias)
                s_ref[hd, n] = s
                mx_ref[hd] = jnp.maximum(mx_ref[hd], jnp.maximum(s[:, :MB_DIM], s[:, MB_DIM:]))

    row_max = [jnp.max(mx_ref[hd], axis=-1, keepdims=True) for hd in heads]
    for hd in heads:
        mb_ref[hd] = jnp.broadcast_to(row_max[hd], (blk, 2 * MB_DIM))
    probs = [jnp.exp(s_ref[hd, own_slot] - mb_ref[hd]).astype(BF16) for hd in heads]
    for hd in heads:
        acc_ref[hd] = _dot(probs[hd], vx_ref[hd, pl.ds(own0, blk), :])

    for n in range(n_blk - 1):
        @pl.when(n < i)
        def _(n=n):
            probs = [jnp.exp(s_ref[hd, n] - mb_ref[hd]).astype(BF16) for hd in heads]
            pv = [_dot(probs[hd], vx_ref[hd, n * blk:(n + 1) * blk, :]) for hd in heads]
            for hd in heads:
                acc_ref[hd] += pv[hd]

    for hd in heads:
        acc = acc_ref[hd]
        o_ref[0, :, hsl(hd)] = (acc[:, :MB_DIM] / acc[:, MB_DIM:]).astype(o_ref.dtype)


def _moba_prompt(h3, rel_table):
    b, t, _ = h3.shape
    assert t % MB_BLOCK == 0
    n_blk = t // MB_BLOCK
    assert n_blk <= GATE_PAD
    qcol, kcol, vcol = COL_MQ // MB_WIDTH, COL_MK // MB_WIDTH, COL_MV // MB_WIDTH
    kern = functools.partial(_moba_prompt_kernel, n_blk=n_blk)
    wide = 2 * MB_DIM
    vmem = (2 * 2 * t * MB_WIDTH * 4 + 2 * MB_HEADS * t * wide * 2
            + MB_HEADS * (n_blk + 4) * MB_BLOCK * wide * 4) // MIB + 12
    return pl.pallas_call(
        kern,
        grid=(b, n_blk),
        in_specs=[pl.BlockSpec(memory_space=pltpu.SMEM),
                  pl.BlockSpec((1, MB_BLOCK, MB_WIDTH), lambda bb, i: (bb, i, qcol)),
                  pl.BlockSpec((1, t, MB_WIDTH), lambda bb, i: (bb, 0, kcol)),
                  pl.BlockSpec((1, t, MB_WIDTH), lambda bb, i: (bb, 0, vcol))],
        out_specs=pl.BlockSpec((1, MB_BLOCK, MB_WIDTH), lambda bb, i: (bb, i, 0)),
        out_shape=jax.ShapeDtypeStruct((b, t, MB_WIDTH), _operand_dtype(t)),
        scratch_shapes=[pltpu.VMEM((MB_HEADS, t, wide), BF16), pltpu.VMEM((MB_HEADS, t, wide), BF16),
                        pltpu.VMEM((MB_HEADS, GATE_PAD, MB_DIM), F32),
                        pltpu.VMEM((MB_HEADS, 2, MB_BLOCK, MB_BLOCK), F32),
                        pltpu.VMEM((MB_BLOCK, MB_BLOCK), BF16),
                        pltpu.VMEM((MB_HEADS, n_blk, MB_BLOCK, MB_BLOCK), F32),
                        pltpu.VMEM((MB_HEADS, MB_BLOCK, MB_DIM), F32),
                        pltpu.VMEM((MB_HEADS, MB_BLOCK, wide), F32),
                        pltpu.VMEM((MB_HEADS, MB_BLOCK, wide), F32)],
        compiler_params=_params(("arbitrary", "arbitrary"), vmem),
        name="moba_prompt",
    )(rel_table, h3, h3, h3)


def _moba_sample_kernel(pt_ref, rel_ref, q_ref, kn_ref, vn_ref, *refs, n_pages, past_len):
    del pt_ref
    kp = refs[:n_pages]
    vp = refs[n_pages:2 * n_pages]
    o_ref, bias_ref, bias_own_ref = refs[2 * n_pages:]
    t = q_ref.shape[1]
    rows = MB_HEADS * t
    page_rows = PAGE_SIZE * MB_HEADS
    pages_per_blk = MB_BLOCK // PAGE_SIZE
    n_past = past_len // MB_BLOCK
    scale = MB_DIM ** -0.5

    def head_rows(ref):
        x = ref[0]
        return jnp.concatenate([x[:, hd * MB_DIM:(hd + 1) * MB_DIM] for hd in range(MB_HEADS)], axis=0)

    @pl.when(pl.program_id(0) == 0)
    def _():
        tq = lax.broadcasted_iota(I32, (t, page_rows), 0)
        kc = lax.broadcasted_iota(I32, (t, page_rows), 1)
        for p in range(n_pages):
            dist = past_len + tq - (p * PAGE_SIZE + (kc >> 2))
            bias_ref[p] = jnp.concatenate(
                [jnp.where((kc & (MB_HEADS - 1)) == hd, _rel_bias(rel_ref, hd, dist), NEG_INF)
                 for hd in range(MB_HEADS)], axis=0)
        ro = lax.broadcasted_iota(I32, (t, rows), 0)
        co = lax.broadcasted_iota(I32, (t, rows), 1)
        for hd in range(MB_HEADS):
            own = _rel_bias(rel_ref, hd, ro - (co - hd * t))
            keep = (co >= hd * t) & (co <= hd * t + ro)
            bias_own_ref[hd * t:(hd + 1) * t, :] = jnp.where(keep, own, NEG_INF)

    q = head_rows(q_ref)
    qb = q.astype(BF16)
    row_head = lax.broadcasted_iota(I32, (rows, 1), 0) // t

    groups = page_rows // 8
    sums = [jnp.sum(kp[p][0].reshape(groups, 8, MB_DIM), axis=0) for p in range(n_pages)]
    blk_sums = []
    for n in range(n_past):
        acc = sums[n * pages_per_blk]
        for pp in range(1, pages_per_blk):
            acc = acc + sums[n * pages_per_blk + pp]
        blk_sums.append(acc)
    g_full = _dot_nt_f32acc(q, jnp.concatenate(blk_sums, axis=0))
    gc = lax.broadcasted_iota(I32, g_full.shape, 1)
    g_full = jnp.where((gc & (MB_HEADS - 1)) == row_head, g_full, 0.0)
    lane = lax.broadcasted_iota(I32, (rows, GATE_PAD), 1)
    gate = jnp.zeros((rows, GATE_PAD), F32)
    for n in range(n_past):
        g_n = jnp.sum(g_full[:, n * 8:(n + 1) * 8], axis=-1, keepdims=True) * (1.0 / MB_BLOCK)
        gate = jnp.where(lane == n, g_n, gate)
    sel = _topk_blocks(gate, n_past, n_past)

    s_own = _dot_nt(q, head_rows(kn_ref)) * scale + bias_own_ref[...]
    m = jnp.max(s_own, axis=-1, keepdims=True)
    s_past = []
    for p in range(n_pages):
        n = p // pages_per_blk
        s = _dot_nt(qb, kp[p][0].astype(BF16)) * scale + bias_ref[p]
        s = jnp.where(sel[:, n:n + 1] > 0.0, s, NEG_INF)
        s_past.append(s)
        m = jnp.maximum(m, jnp.max(s, axis=-1, keepdims=True))
    p_own = jnp.exp(s_own - m)
    l = jnp.sum(p_own, axis=-1, keepdims=True)
    out = _dot(p_own, head_rows(vn_ref))
    for p in range(n_pages):
        pr = jnp.exp(s_past[p] - m)
        l = l + jnp.sum(pr, axis=-1, keepdims=True)
        out = out + _dot(pr.astype(BF16), vp[p][0].astype(BF16))
    out = out / l
    for hd in range(MB_HEADS):
        o_ref[0, :, hd * MB_DIM:(hd + 1) * MB_DIM] = out[hd * t:(hd + 1) * t]


def _moba_sample(h3, cache_k, cache_v, page_table, rel_table):
    db, t, _ = h3.shape
    n_pages = page_table.shape[1]
    past_len = n_pages * PAGE_SIZE
    assert past_len % MB_BLOCK == 0 and t <= MB_BLOCK and past_len // MB_BLOCK < GATE_PAD
    assert MB_HEADS == 4 and t % 8 == 0
    qcol, kcol, vcol = COL_MQ // MB_WIDTH, COL_MK // MB_WIDTH, COL_MV // MB_WIDTH
    page_rows = PAGE_SIZE * MB_HEADS
    new = lambda k: pl.BlockSpec((1, t, MB_WIDTH), lambda i, pt: (i, 0, k))
    page = lambda p: pl.BlockSpec((1, page_rows, MB_DIM), lambda i, pt: (pt[i, p], 0, 0))
    kern = functools.partial(_moba_sample_kernel, n_pages=n_pages, past_len=past_len)
    grid_spec = pltpu.PrefetchScalarGridSpec(
        num_scalar_prefetch=1,
        grid=(db,),
        in_specs=([pl.BlockSpec(memory_space=pltpu.SMEM), new(qcol), new(kcol), new(vcol)]
                  + [page(p) for p in range(n_pages)] * 2),
        out_specs=pl.BlockSpec((1, t, MB_WIDTH), lambda i, pt: (i, 0, 0)),
        scratch_shapes=[pltpu.VMEM((n_pages, MB_HEADS * t, page_rows), F32),
                        pltpu.VMEM((MB_HEADS * t, MB_HEADS * t), F32)],
    )
    return pl.pallas_call(
        kern,
        grid_spec=grid_spec,
        out_shape=jax.ShapeDtypeStruct((db, t, MB_WIDTH), F32),
        compiler_params=_params(("arbitrary",), 40),
        name="moba_sample",
    )(page_table, rel_table, h3, h3, h3, *([cache_k] * n_pages), *([cache_v] * n_pages))


def _merge_kernel(oa_ref, ob_ref, ga0_ref, ga1_ref, gb0_ref, gb1_ref, x_ref,
                  wa_ref, wb_ref, wo_ref, gx_ref, wq_ref, x1_ref, xq_ref):
    ga = jnp.concatenate([ga0_ref[...], ga1_ref[...]], axis=-1)
    gb = jnp.concatenate([gb0_ref[...], gb1_ref[...]], axis=-1)
    pa = _dot(oa_ref[...].astype(BF16), wa_ref[...])
    pb = _dot(ob_ref[...].astype(BF16), wb_ref[...])
    merged = _sigmoid(ga) * pa + _sigmoid(gb) * pb
    x1 = x_ref[...] + _dot(merged.astype(BF16), wo_ref[...])
    x1_ref[...] = x1
    xq_ref[...] = _dot(_rmsnorm(x1, gx_ref[...]).astype(BF16), wq_ref[...]).astype(xq_ref.dtype)


def _merge(oa, ob, h2, x, wa, wb, wo, gx, wq, *, tm, xq_dtype):
    n = x.shape[0]
    assert n % tm == 0
    half = D_MODEL // 2
    tok = lambda w, k=0: pl.BlockSpec((tm, w), lambda i: (i, k))
    full = lambda a: pl.BlockSpec(a.shape, lambda i: (0, 0))
    gx2 = gx.reshape(1, D_MODEL)
    return pl.pallas_call(
        _merge_kernel,
        grid=(n // tm,),
        in_specs=[tok(HG_WIDTH), tok(MB_WIDTH),
                  tok(half, COL_GA // half), tok(half, COL_GA // half + 1),
                  tok(half, COL_GB // half), tok(half, COL_GB // half + 1),
                  tok(D_MODEL), full(wa), full(wb), full(wo), full(gx2), full(wq)],
        out_specs=[tok(D_MODEL), tok(XA_WIDTH)],
        out_shape=[jax.ShapeDtypeStruct((n, D_MODEL), F32),
                   jax.ShapeDtypeStruct((n, XA_WIDTH), xq_dtype)],
        compiler_params=_params(("parallel",), 48),
        name="merge_mix",
    )(oa, ob, h2, h2, h2, h2, x, wa, wb, wo, gx2, wq)


def _mem_attn_kernel(q_ref, k_ref, v_ref, o_ref):
    scale = XA_DIM ** -0.5
    heads = range(XA_HEADS)
    hsl = lambda hd: slice(hd * XA_DIM, (hd + 1) * XA_DIM)
    logits = [_dot_nt(q_ref[0][:, hsl(hd)].astype(BF16), k_ref[0][:, hsl(hd)].astype(BF16)) * scale
              for hd in heads]
    row_max = [jnp.max(s, axis=-1, keepdims=True) for s in logits]
    probs = [jnp.exp(s - m) for s, m in zip(logits, row_max)]
    denom = [jnp.sum(p, axis=-1, keepdims=True) for p in probs]
    outs = [_dot(probs[hd].astype(BF16), v_ref[0][:, hsl(hd)].astype(BF16)) for hd in heads]
    for hd in heads:
        o_ref[0, :, hsl(hd)] = (outs[hd] / denom[hd]).astype(o_ref.dtype)


def _mem_attn(xq3, mem_kv, *, tq):
    b, t, _ = xq3.shape
    assert t % tq == 0
    mem = lambda k: pl.BlockSpec((1, mem_kv.shape[1], XA_WIDTH), lambda i, j: (i, 0, k))
    qs = pl.BlockSpec((1, tq, XA_WIDTH), lambda i, j: (i, j, 0))
    return pl.pallas_call(
        _mem_attn_kernel,
        grid=(b, t // tq),
        in_specs=[qs, mem(0), mem(1)],
        out_specs=qs,
        out_shape=jax.ShapeDtypeStruct((b, t, XA_WIDTH), _operand_dtype(t)),
        compiler_params=_params(("parallel", "parallel"), 32),
        name="mem_attn",
    )(xq3, mem_kv, mem_kv)


def _mem_attn_rows_kernel(q_ref, k_ref, v_ref, o_ref):
    bb, t, _ = q_ref.shape
    rows = XA_HEADS * t
    scale = XA_DIM ** -0.5
    row_head = lax.broadcasted_iota(I32, (rows, 1), 0) // t
    kc = lax.broadcasted_iota(I32, (rows, k_ref.shape[1]), 1)
    same_head = (kc & (XA_HEADS - 1)) == row_head
    qs = []
    for i in range(bb):
        x = q_ref[i]
        qs.append(jnp.concatenate([x[:, hd * XA_DIM:(hd + 1) * XA_DIM] for hd in range(XA_HEADS)],
                                  axis=0).astype(BF16))
    logits = [jnp.where(same_head, _dot_nt(qs[i], k_ref[i].astype(BF16)) * scale, NEG_INF)
              for i in range(bb)]
    row_max = [jnp.max(s, axis=-1, keepdims=True) for s in logits]
    probs = [jnp.exp(s - m) for s, m in zip(logits, row_max)]
    denom = [jnp.sum(p, axis=-1, keepdims=True) for p in probs]
    outs = [_dot(probs[i].astype(BF16), v_ref[i].astype(BF16)) / denom[i] for i in range(bb)]
    for i in range(bb):
        for hd in range(XA_HEADS):
            o_ref[i, :, hd * XA_DIM:(hd + 1) * XA_DIM] = outs[i][hd * t:(hd + 1) * t]


def _mem_attn_rows(xq3, mem_k, mem_v, *, bb):
    b, t, _ = xq3.shape
    assert b % bb == 0 and XA_HEADS == 4 and t % 8 == 0
    mem = pl.BlockSpec((bb,) + mem_k.shape[1:], lambda i: (i, 0, 0))
    qs = pl.BlockSpec((bb, t, XA_WIDTH), lambda i: (i, 0, 0))
    return pl.pallas_call(
        _mem_attn_rows_kernel,
        grid=(b // bb,),
        in_specs=[qs, mem, mem],
        out_specs=qs,
        out_shape=jax.ShapeDtypeStruct((b, t, XA_WIDTH), F32),
        compiler_params=_params(("parallel",), 32),
        name="mem_attn_rows",
    )(xq3, mem_k, mem_v)


def _route(logits):
    lane = lax.broadcasted_iota(I32, logits.shape, 1).astype(F32)
    first = lambda hit: jnp.min(jnp.where(hit, lane, float(ROUTER_LANES)), axis=-1, keepdims=True)
    gl = jnp.where(lane < N_GROUPS, logits, NEG_INF)
    gmax = jnp.max(gl, axis=-1, keepdims=True)
    grp = first(gl == gmax)
    g_prob = 1.0 / jnp.sum(jnp.exp(gl - gmax), axis=-1, keepdims=True)
    e_lo = EXPERT_LANE0 + grp * EXPERTS_PER_GROUP
    in_grp = (lane >= e_lo) & (lane < e_lo + EXPERTS_PER_GROUP)
    el = jnp.where(in_grp, logits, NEG_INF)
    top1 = jnp.max(el, axis=-1, keepdims=True)
    idx1 = first(el == top1)
    el2 = jnp.where(lane == idx1, NEG_INF, el)
    top2 = jnp.max(el2, axis=-1, keepdims=True)
    idx2 = first(el2 == top2)
    e2 = jnp.exp(top2 - top1)
    w1 = g_prob / (1.0 + e2)
    w2 = w1 * e2
    comb = jnp.where(lane == idx1, w1, 0.0) + jnp.where(lane == idx2, w2, 0.0)
    return jnp.where(lane == GRP_LANE, grp, comb)


def _xo_route_kernel(x1_ref, at_ref, wxo_ref, gf_ref, wr_ref, br_ref, x2_ref, xf_ref, comb_ref):
    x2 = x1_ref[...] + _dot(at_ref[...].astype(BF16), wxo_ref[...])
    x2_ref[...] = x2
    xf = _rmsnorm(x2, gf_ref[...])
    xf_ref[...] = xf.astype(BF16)
    comb_ref[...] = _route(_dot_f32acc(xf, wr_ref[...]) + br_ref[...])


def _xo_route(x1, attn, wxo, g_ffn, w_router_t, b_router_t, *, tm):
    n = x1.shape[0]
    assert n % tm == 0
    tok = lambda w: pl.BlockSpec((tm, w), lambda i: (i, 0))
    full = lambda a: pl.BlockSpec(a.shape, lambda i: (0, 0))
    gf2 = g_ffn.reshape(1, D_MODEL)
    return pl.pallas_call(
        _xo_route_kernel,
        grid=(n // tm,),
        in_specs=[tok(D_MODEL), tok(XA_WIDTH), full(wxo), full(gf2), full(w_router_t), full(b_router_t)],
        out_specs=[tok(D_MODEL), tok(D_MODEL), tok(ROUTER_LANES)],
        out_shape=[jax.ShapeDtypeStruct((n, D_MODEL), F32), jax.ShapeDtypeStruct((n, D_MODEL), BF16),
                   jax.ShapeDtypeStruct((n, ROUTER_LANES), F32)],
        compiler_params=_params(("parallel",), 40),
        name="xo_route",
    )(x1, attn, wxo, gf2, w_router_t, b_router_t)


def _moe_kernel(xf_ref, comb_ref, x2_ref, wg_ref, wu_ref, wd_ref, gn_ref, y_ref,
                tri_ref, posc_ref, cnt_ref, xs_ref, ws_ref, ys_ref, *, r):
    i = pl.program_id(0)
    step = pl.program_id(1)
    tm = xf_ref.shape[0]
    per_step = wg_ref.shape[0]
    steps_per_group = EXPERTS_PER_GROUP // per_step
    g = step // steps_per_group
    lane = lax.broadcasted_iota(I32, (tm, ROUTER_LANES), 1)

    @pl.when((i == 0) & (step == 0))
    def _():
        rr = lax.broadcasted_iota(I32, (tm, tm), 0)
        cc = lax.broadcasted_iota(I32, (tm, tm), 1)
        tri_ref[...] = jnp.where(cc < rr, 1.0, 0.0).astype(BF16)

    @pl.when(step == 0)
    def _():
        comb = comb_ref[...]
        grp = comb[:, GRP_LANE:GRP_LANE + 1]
        lane_f = lane.astype(F32)
        rel = lane - EXPERT_LANE0
        used = (comb > 0.0) & (rel >= 0) & (rel < N_EXPERTS)
        second = (rel & (EXPERTS_PER_GROUP - 1)) >= EXPERTS_PER_GROUP // 2
        uses_a = jnp.max(jnp.where(used & jnp.logical_not(second), 1.0, 0.0), axis=-1, keepdims=True)
        uses_b = jnp.max(jnp.where(used & second, 1.0, 0.0), axis=-1, keepdims=True)
        key = grp * MOE_CLASSES + jnp.where(uses_b == 0.0, 0.0, jnp.where(uses_a == 0.0, 2.0, 1.0))
        onehot = jnp.where((lane_f == key) & (lane < N_GROUPS * MOE_CLASSES), 1.0, 0.0)
        before = _dot(tri_ref[...], onehot.astype(BF16))
        cnt = jnp.sum(onehot, axis=0, keepdims=True)
        cnt_ref[...] = cnt
        pos = jnp.sum(jnp.where(lane_f < key, cnt, 0.0) + onehot * before, axis=-1, keepdims=True)
        posc_ref[...] = jnp.where(lane == 0, pos, 0.0)
        hi = jnp.floor(pos * (1.0 / POS_SPLIT))
        lo = pos - hi * POS_SPLIT
        cols = jnp.where(lane == 0, hi, jnp.where(lane == 1, lo, 0.0))
        er = lax.broadcasted_iota(I32, (V7X_BF16_SUBLANES, ROUTER_LANES), 0)
        ec = lax.broadcasted_iota(I32, (V7X_BF16_SUBLANES, ROUTER_LANES), 1)
        post = _dot_nt(jnp.where(er == ec, 1.0, 0.0).astype(BF16), cols.astype(BF16))
        pos_row = post[0:1, :] * POS_SPLIT + post[1:2, :]
        slot = lax.broadcasted_iota(I32, (tm, tm), 0).astype(F32)
        perm = jnp.where(pos_row == slot, 1.0, 0.0).astype(BF16)
        comb_hi, comb_lo = _split2(comb)
        xs_ref[0:tm, :] = _dot(perm, xf_ref[...]).astype(BF16)
        ws_ref[0:tm, :] = _dot(perm, comb_hi) + _dot(perm, comb_lo)
        xs_ref[tm:, :] = jnp.zeros((r, D_MODEL), BF16)
        ws_ref[tm:, :] = jnp.zeros((r, ROUTER_LANES), F32)
        ys_ref[...] = jnp.zeros(ys_ref.shape, F32)

    lane_row = lax.broadcasted_iota(I32, (1, ROUTER_LANES), 1)
    key0 = g * MOE_CLASSES + step % steps_per_group
    first = jnp.sum(jnp.where(lane_row < key0, cnt_ref[...], 0.0)).astype(I32)
    count = jnp.sum(jnp.where((lane_row == key0) | (lane_row == key0 + 1), cnt_ref[...], 0.0)).astype(I32)
    start = (first // V7X_BF16_SUBLANES) * V7X_BF16_SUBLANES
    n_windows = (first + count - start + (r - 1)) // r

    def experts(c, carry):
        rows = pl.ds(pl.multiple_of(start + c * r, V7X_BF16_SUBLANES), r)
        xg = xs_ref[rows, :]
        wq = ws_ref[rows, :]
        wl = lax.broadcasted_iota(I32, (r, ROUTER_LANES), 1)
        ups = [(_dot(xg, wg_ref[k]), _dot(xg, wu_ref[k])) for k in range(per_step)]
        hidden = []
        for k, (gate, up) in enumerate(ups):
            w_e = jnp.sum(jnp.where(wl == EXPERT_LANE0 + step * per_step + k, wq, 0.0), axis=-1, keepdims=True)
            hidden.append((_silu(gate) * up * w_e).astype(BF16))
        w_down = wd_ref[...].reshape(per_step * EXPERT_HIDDEN, D_MODEL)
        ys_ref[rows, :] += _dot(jnp.concatenate(hidden, axis=-1), w_down)
        return carry

    lax.fori_loop(0, n_windows, experts, 0)

    @pl.when(step == pl.num_programs(1) - 1)
    def _():
        slot = lax.broadcasted_iota(I32, (tm, tm), 1).astype(F32)
        place = jnp.where(posc_ref[:, 0:1] == slot, 1.0, 0.0).astype(BF16)
        moe = _dot(place, ys_ref[0:tm, :].astype(BF16))
        y_ref[...] = _rmsnorm(x2_ref[...] + moe, gn_ref[...])


def _moe(xf, comb, x2, wg, wu, wd, g_final, *, tm):
    n = xf.shape[0]
    assert n % tm == 0 and tm % V7X_BF16_SUBLANES == 0
    assert 2 * MOE_EXPERTS_PER_STEP == EXPERTS_PER_GROUP and N_GROUPS * MOE_CLASSES <= ROUTER_LANES
    r = min(MOE_CHUNK_ROWS, tm)
    cap = tm + r
    per_step = MOE_EXPERTS_PER_STEP
    tok = lambda w: pl.BlockSpec((tm, w), lambda i, s: (i, 0))
    gn2 = g_final.reshape(1, D_MODEL)
    kern = functools.partial(_moe_kernel, r=r)
    vmem = (2 * tm * D_MODEL * (2 + 4 + 4) + 2 * tm * ROUTER_LANES * 4
            + 2 * 3 * per_step * D_MODEL * EXPERT_HIDDEN * 2 + tm * tm * 2
            + cap * D_MODEL * (2 + 4) + tm * D_MODEL * 4) // MIB + 8
    return pl.pallas_call(
        kern,
        grid=(n // tm, N_EXPERTS // per_step),
        in_specs=[tok(D_MODEL), tok(ROUTER_LANES), tok(D_MODEL),
                  pl.BlockSpec((per_step, D_MODEL, EXPERT_HIDDEN), lambda i, s: (s, 0, 0)),
                  pl.BlockSpec((per_step, D_MODEL, EXPERT_HIDDEN), lambda i, s: (s, 0, 0)),
                  pl.BlockSpec((per_step, EXPERT_HIDDEN, D_MODEL), lambda i, s: (s, 0, 0)),
                  pl.BlockSpec((1, D_MODEL), lambda i, s: (0, 0))],
        out_specs=tok(D_MODEL),
        out_shape=jax.ShapeDtypeStruct((n, D_MODEL), F32),
        scratch_shapes=[pltpu.VMEM((tm, tm), BF16),
                        pltpu.VMEM((tm, ROUTER_LANES), F32), pltpu.VMEM((1, ROUTER_LANES), F32),
                        pltpu.VMEM((cap, D_MODEL), BF16), pltpu.VMEM((cap, ROUTER_LANES), F32),
                        pltpu.VMEM((cap, D_MODEL), F32)],
        compiler_params=_params(("arbitrary", "arbitrary"), vmem),
        name="moe_grouped",
    )(xf, comb, x2, wg, wu, wd, gn2)


def _token_tile(n, cap=512):
    return cap if n % cap == 0 else n


def _col_tile(n, cap=1536):
    return max(c for c in range(V7X_LANES, cap + 1, V7X_LANES) if n % c == 0)


def _layer(x3, w, s0, moba_fn, mem_fn):
    b, t, _ = x3.shape
    n = b * t
    x2d = x3.reshape(n, D_MODEL)
    tm = _token_tile(n)
    h2, k4, v4 = _norm_matmul(x2d, w["norm_mix"], w["w_in"], tm=_token_tile(n, 1024),
                              tn=_col_tile(IN_COLS), head_cols=(COL_MK, COL_MV), heads=MB_HEADS,
                              head_dim=MB_DIM)
    h3 = h2.reshape(b, t, IN_COLS)
    oa, s_new = _hgrn(h3, w["hg_lb_logits"], w["hg_norm"], s0)
    ob = moba_fn(h3)
    x1, xq = _merge(oa.reshape(n, HG_WIDTH), ob.reshape(n, MB_WIDTH), h2, x2d,
                    w["w_branch_a"], w["w_branch_b"], w["w_mix_out"], w["norm_xattn"], w["w_xq"], tm=tm,
                    xq_dtype=_operand_dtype(t))
    attn = mem_fn(xq.reshape(b, t, XA_WIDTH))
    x2, xf, comb = _xo_route(x1, attn.reshape(n, XA_WIDTH), w["w_xo"], w["norm_ffn"],
                             w["w_router"], w["b_router"], tm=_token_tile(n, 1024))
    y = _moe(xf, comb, x2, w["w_expert_gate"], w["w_expert_up"], w["w_expert_down"],
             w["norm_final"], tm=_token_tile(n, 1024))
    k_new = k4.reshape(b, t, MB_HEADS, MB_DIM)
    v_new = v4.reshape(b, t, MB_HEADS, MB_DIM)
    return y.reshape(b, t, D_MODEL), k_new, v_new, s_new


def kernel(x_prompt, x_sample, cache_k, cache_v, state_hgrn, cache_mem_k, cache_mem_v, page_table, mem_prompt, norm_mix, w_in, hg_lb_logits, hg_norm, w_branch_a, w_branch_b, w_mix_out, rel_table, norm_xattn, norm_mem, w_xq, w_xk, w_xv, w_xo, norm_ffn, w_group_router, b_group_router, w_expert_router, b_expert_router, w_expert_gate, w_expert_up, w_expert_down, norm_final):
    assert w_in.shape[0] == DEPTH == 1 and hg_lb_logits.shape[0] == DEPTH + 1
    b = x_prompt.shape[0]
    db = x_sample.shape[0]
    n_pool = cache_k.shape[1]
    pad_lanes = ROUTER_LANES - EXPERT_LANE0 - N_EXPERTS
    w = {
        "norm_mix": norm_mix[0], "w_in": w_in[0].astype(BF16),
        "hg_lb_logits": hg_lb_logits, "hg_norm": hg_norm[0],
        "w_branch_a": w_branch_a[0].astype(BF16), "w_branch_b": w_branch_b[0].astype(BF16),
        "w_mix_out": w_mix_out[0].astype(BF16), "norm_xattn": norm_xattn[0],
        "w_xq": w_xq[0].astype(BF16), "w_xo": w_xo[0].astype(BF16), "norm_ffn": norm_ffn[0],
        "w_router": jnp.pad(jnp.concatenate(
            [w_group_router[0], jnp.zeros((D_MODEL, EXPERT_LANE0 - N_GROUPS), F32), w_expert_router[0]],
            axis=1), ((0, 0), (0, pad_lanes))),
        "b_router": jnp.pad(jnp.concatenate(
            [b_group_router[0], jnp.zeros((EXPERT_LANE0 - N_GROUPS,), F32), b_expert_router[0]]),
            (0, pad_lanes)).reshape(1, ROUTER_LANES),
        "w_expert_gate": w_expert_gate[0].reshape(N_EXPERTS, D_MODEL, EXPERT_HIDDEN).astype(BF16),
        "w_expert_up": w_expert_up[0].reshape(N_EXPERTS, D_MODEL, EXPERT_HIDDEN).astype(BF16),
        "w_expert_down": w_expert_down[0].reshape(N_EXPERTS, EXPERT_HIDDEN, D_MODEL).astype(BF16),
        "norm_final": norm_final,
    }

    w_mem = jnp.concatenate([w_xk[0], w_xv[0]], axis=1).astype(BF16)
    mem_kv, mk_p, mv_p = _norm_matmul(mem_prompt.reshape(b * MEM_LEN, D_MODEL), norm_mem[0], w_mem,
                                      tm=_token_tile(b * MEM_LEN), tn=512, head_cols=(0, XA_WIDTH),
                                      heads=XA_HEADS, head_dim=XA_DIM)
    mem_kv = mem_kv.reshape(b, MEM_LEN, 2 * XA_WIDTH)
    s0 = jnp.zeros((b, HG_HEADS, HG_DIM, HG_DIM), F32)
    y_p, k_p, v_p, s_p = _layer(x_prompt, w, s0,
                                functools.partial(_moba_prompt, rel_table=rel_table),
                                functools.partial(_mem_attn, mem_kv=mem_kv, tq=512))

    ck = cache_k[0].reshape(n_pool, PAGE_SIZE * MB_HEADS, MB_DIM)
    cv = cache_v[0].reshape(n_pool, PAGE_SIZE * MB_HEADS, MB_DIM)
    moba_s = functools.partial(_moba_sample, cache_k=ck, cache_v=cv, page_table=page_table,
                               rel_table=rel_table)
    mem_s = functools.partial(_mem_attn_rows, bb=math.gcd(db, 8),
                              mem_k=cache_mem_k[0].reshape(db, MEM_LEN * XA_HEADS, XA_DIM),
                              mem_v=cache_mem_v[0].reshape(db, MEM_LEN * XA_HEADS, XA_DIM))
    y_s, k_s, v_s, s_s = _layer(x_sample, w, state_hgrn[0], moba_s, mem_s)

    heads = lambda a: a.reshape(b, MEM_LEN, XA_HEADS, XA_DIM)[None]
    return (y_p, y_s, k_p[None], v_p[None], s_p[None], heads(mk_p), heads(mv_p),
            k_s[None], v_s[None], s_s[None])
```

```python
import functools
import math

import numpy as np
import jax
import jax.numpy as jnp
from jax import lax
from jax.experimental import pallas as pl
from jax.experimental.pallas import tpu as pltpu

F32 = jnp.float32
BF16 = jnp.bfloat16
I32 = jnp.int32

D_MODEL = 1024
DEPTH = 1
PAGE_SIZE = 128
HG_HEADS = 4
HG_DIM = 128
HG_WIDTH = HG_HEADS * HG_DIM
MB_HEADS = 4
MB_DIM = 128
MB_WIDTH = MB_HEADS * MB_DIM
MB_BLOCK = 256
MB_TOPK = 3
REL_BUCKETS = 32
REL_MAX_DIST = 128
REL_MAX_EXACT = REL_BUCKETS // 2
MEM_LEN = 256
XA_HEADS = 4
XA_DIM = 128
XA_WIDTH = XA_HEADS * XA_DIM
N_GROUPS = 4
EXPERTS_PER_GROUP = 8
N_EXPERTS = N_GROUPS * EXPERTS_PER_GROUP
EXPERT_TOPK = 2
EXPERT_HIDDEN = 256
NORM_EPS = 1e-6
IN_COLS = 4 * HG_WIDTH + 3 * MB_WIDTH + 2 * D_MODEL
COL_MQ = 4 * HG_WIDTH
COL_MK = COL_MQ + MB_WIDTH
COL_MV = COL_MK + MB_WIDTH
COL_GA = COL_MV + MB_WIDTH
COL_GB = COL_GA + D_MODEL

V7X_LANES = 128
V7X_BF16_SUBLANES = 16
V7X_VMEM_BYTES = 64 * 1024 * 1024
MIB = 1024 * 1024

NEG_INF = float("-inf")
MASK_BIG = -1e30
GATE_PAD = V7X_BF16_SUBLANES
ROUTER_LANES = 128
GRP_LANE = 0
EXPERT_LANE0 = 8
POS_SPLIT = 32.0
MOE_CHUNK_ROWS = 256
MOE_EXPERTS_PER_STEP = EXPERTS_PER_GROUP // 2
MOE_CLASSES = 3
HGRN_PROMPT_CHUNK = 128
HGRN_PROMPT_ROWS = 4
HGRN_MIN_CHUNK = V7X_BF16_SUBLANES
HGRN_CHAINS = 32


def _params(semantics, vmem_mib):
    return pltpu.CompilerParams(dimension_semantics=semantics,
                                vmem_limit_bytes=min(vmem_mib * MIB, V7X_VMEM_BYTES - 8 * MIB))


def _dot(a, b):
    return jnp.dot(a, b, preferred_element_type=F32)


def _dot_nt(a, b):
    return lax.dot_general(a, b, (((1,), (1,)), ((), ())), preferred_element_type=F32)


def _dot_tn(a, b):
    return lax.dot_general(a, b, (((0,), (0,)), ((), ())), preferred_element_type=F32)


def _split2(a):
    hi = a.astype(BF16)
    lo = (a - hi.astype(F32)).astype(BF16)
    return hi, lo


def _dot_nt_f32acc(a, b):
    ah, al = _split2(a)
    bh, bl = _split2(b)
    return _dot_nt(ah, bh) + (_dot_nt(ah, bl) + _dot_nt(al, bh))


def _dot_f32acc(a, b):
    ah, al = _split2(a)
    bh, bl = _split2(b)
    return _dot(ah, bh) + (_dot(ah, bl) + _dot(al, bh))


def _operand_dtype(t):
    return BF16 if t % V7X_BF16_SUBLANES == 0 else F32


def _rmsnorm(x, g):
    return x * lax.rsqrt(jnp.mean(x * x, axis=-1, keepdims=True) + NORM_EPS) * g


def _sigmoid(x):
    return 1.0 / (1.0 + jnp.exp(-x))


def _silu(x):
    return x * _sigmoid(x)


def _norm_matmul_kernel(x_ref, g_ref, w_ref, o_ref, *refs, head_cols, heads, head_dim):
    head_refs, xn_ref = refs[:-1], refs[-1]
    j = pl.program_id(1)
    tm, tn = o_ref.shape

    @pl.when(j == 0)
    def _():
        xn_ref[...] = _rmsnorm(x_ref[...], g_ref[...]).astype(BF16)

    res = _dot(xn_ref[...], w_ref[...])
    o_ref[...] = res
    for col0, href in zip(head_cols, head_refs):
        for hd in range(heads):
            tile, local = divmod(col0 + hd * head_dim, tn)

            @pl.when(j == tile)
            def _(hd=hd, local=local, href=href):
                href[pl.ds(hd, tm, stride=heads), :] = res[:, local:local + head_dim]


def _norm_matmul(x, g, w_bf16, *, tm, tn, head_cols=(), heads=1, head_dim=V7X_LANES):
    m, d = x.shape
    n = w_bf16.shape[1]
    assert m % tm == 0 and n % tn == 0 and tn % head_dim == 0
    vmem = (2 * tm * d * 4 + tm * d * 2 + 2 * d * tn * 2 + 2 * tm * tn * 4
            + len(head_cols) * 2 * tm * heads * head_dim * 4) // MIB + 12
    kern = functools.partial(_norm_matmul_kernel, head_cols=tuple(head_cols), heads=heads,
                             head_dim=head_dim)
    head_spec = pl.BlockSpec((tm * heads, head_dim), lambda i, j: (i, 0))
    head_shape = jax.ShapeDtypeStruct((m * heads, head_dim), F32)
    return pl.pallas_call(
        kern,
        grid=(m // tm, n // tn),
        in_specs=[pl.BlockSpec((tm, d), lambda i, j: (i, 0)),
                  pl.BlockSpec((1, d), lambda i, j: (0, 0)),
                  pl.BlockSpec((d, tn), lambda i, j: (0, j))],
        out_specs=[pl.BlockSpec((tm, tn), lambda i, j: (i, j))] + [head_spec] * len(head_cols),
        out_shape=[jax.ShapeDtypeStruct((m, n), F32)] + [head_shape] * len(head_cols),
        scratch_shapes=[pltpu.VMEM((tm, d), BF16)],
        compiler_params=_params(("parallel", "arbitrary"), vmem),
        name="norm_matmul",
    )(x, g.reshape(1, d), w_bf16)


def _hgrn_levels(c):
    return int(round(math.log2(c)))


def _hgrn_sum_masks(c):
    t = np.arange(c)[:, None]
    j = np.arange(c)[None, :]
    rows = [j <= t, j > t]
    for lv in range(_hgrn_levels(c)):
        half = c >> (lv + 1)
        blk = 2 * half
        mid = (t // blk) * blk + half - 1
        upper = (t % blk) >= half
        rows.append((upper & (j > mid) & (j <= t)) | ((~upper) & (j > t) & (j <= mid)))
    return np.concatenate(rows, axis=0).astype(np.float32)


def _hgrn_kernel(hq_ref, hf_ref, hi_ref, hg_ref, lbl_ref, gn_ref, mask_ref, s0_ref,
                 o_ref, s_ref, st_ref, *, c, t_blk, bb):
    j = pl.program_id(1)
    chains = [(bi, hd) for bi in range(bb) for hd in range(HG_HEADS)]

    @pl.when(j == 0)
    def _():
        for bi, hd in chains:
            st_ref[bi * HG_HEADS + hd] = s0_ref[bi, hd].T

    lbl = lbl_ref[...]
    lmax = jnp.max(lbl, axis=0, keepdims=True)
    lexp = jnp.exp(lbl - lmax)
    lb_all = lexp[0:1, :] / jnp.sum(lexp, axis=0, keepdims=True)

    masks = mask_ref[...]
    pad = c - t_blk
    row = lax.broadcasted_iota(I32, (c, HG_DIM), 0)
    rr = lax.broadcasted_iota(I32, (c, c), 0)
    cc = lax.broadcasted_iota(I32, (c, c), 1)

    def padded(a):
        if pad == 0:
            return a
        return jnp.concatenate([a, jnp.zeros((pad, HG_DIM), F32)], axis=0)

    hsl = lambda hd: slice(hd * HG_DIM, (hd + 1) * HG_DIM)
    qs, ks, vs, logf_parts = [], [], [], []
    for bi, hd in chains:
        xq = hq_ref[bi][:, hsl(hd)]
        lb = lb_all[:, hsl(hd)]
        f = lb + (1.0 - lb) * _sigmoid(hf_ref[bi][:, hsl(hd)])
        qs.append(padded(_silu(xq)))
        ks.append(padded(1.0 - f))
        vs.append(padded(hi_ref[bi][:, hsl(hd)]))
        logf_parts.append(_split2(padded(jnp.log(f))))
    exs = []
    for n in range(0, len(chains), 2):
        (hi_a, lo_a), (hi_b, lo_b) = logf_parts[n], logf_parts[n + 1]
        stacked = jnp.concatenate([jnp.concatenate([hi_a, hi_b], axis=-1),
                                   jnp.concatenate([lo_a, lo_b], axis=-1)], axis=0)
        ex = jnp.exp(_dot(masks, stacked))
        exs += [ex[:, :HG_DIM], ex[:, HG_DIM:]]

    level_dots = []
    for q, kk, ex in zip(qs, ks, exs):
        per_level = []
        for lv in range(_hgrn_levels(c)):
            half = c >> (lv + 1)
            upper = (row & half) != 0
            ex_lv = ex[(2 + lv) * c:(3 + lv) * c]
            a = jnp.where(upper, q * ex_lv, 0.0).astype(BF16)
            bm = jnp.where(upper, 0.0, kk * ex_lv).astype(BF16)
            per_level.append(_dot_nt(a, bm))
        level_dots.append(per_level)

    outs = []
    for n, (bi, hd) in enumerate(chains):
        q, kk, v, ex = qs[n], ks[n], vs[n], exs[n]
        scores = jnp.zeros((c, c), F32)
        for lv, d in enumerate(level_dots[n]):
            scores = scores + jnp.where((rr ^ cc) < 2 * (c >> (lv + 1)), d, 0.0)
        eb = ex[0:c]
        eb_rev = ex[c:2 * c]
        diag = jnp.sum(q * kk, axis=-1, keepdims=True)
        st = st_ref[bi * HG_HEADS + hd]
        vb = v.astype(BF16)
        o = (_dot_nt((q * eb).astype(BF16), st.astype(BF16))
             + _dot(scores.astype(BF16), vb) + diag * v)
        st_ref[bi * HG_HEADS + hd] = st * eb[c - 1:c, :] + _dot_tn(vb, (kk * eb_rev).astype(BF16))
        outs.append(o[0:t_blk])

    for (bi, hd), o in zip(chains, outs):
        xg = hg_ref[bi][:, hsl(hd)]
        o_ref[bi, :, hsl(hd)] = (_rmsnorm(o, gn_ref[:, hsl(hd)]) * _silu(xg)).astype(o_ref.dtype)

    @pl.when(j == pl.num_programs(1) - 1)
    def _():
        for bi, hd in chains:
            s_ref[bi, hd] = st_ref[bi * HG_HEADS + hd].T


def _hgrn(h3, lb_logits, hg_norm, s0):
    b, t, _ = h3.shape
    t_blk = math.gcd(t, HGRN_PROMPT_CHUNK)
    c = max(t_blk, HGRN_MIN_CHUNK)
    bb = math.gcd(b, HGRN_CHAINS // HG_HEADS) if t == t_blk else math.gcd(b, HGRN_PROMPT_ROWS)
    m01 = _hgrn_sum_masks(c)
    masks = jnp.asarray(np.concatenate([m01, m01], axis=1), BF16)
    col = lambda k: pl.BlockSpec((bb, t_blk, HG_WIDTH), lambda i, j: (i, j, k))
    state_spec = pl.BlockSpec((bb, HG_HEADS, HG_DIM, HG_DIM), lambda i, j: (i, 0, 0, 0))
    kern = functools.partial(_hgrn_kernel, c=c, t_blk=t_blk, bb=bb)
    return pl.pallas_call(
        kern,
        grid=(b // bb, t // t_blk),
        in_specs=[col(0), col(1), col(2), col(3),
                  pl.BlockSpec(lb_logits.shape, lambda i, j: (0, 0)),
                  pl.BlockSpec((1, HG_WIDTH), lambda i, j: (0, 0)),
                  pl.BlockSpec(masks.shape, lambda i, j: (0, 0)),
                  state_spec],
        out_specs=[pl.BlockSpec((bb, t_blk, HG_WIDTH), lambda i, j: (i, j, 0)), state_spec],
        out_shape=[jax.ShapeDtypeStruct((b, t, HG_WIDTH), _operand_dtype(t)),
                   jax.ShapeDtypeStruct(s0.shape, F32)],
        scratch_shapes=[pltpu.VMEM((bb * HG_HEADS, HG_DIM, HG_DIM), F32)],
        compiler_params=_params(("parallel", "arbitrary"), 32),
        name="hgrn2",
    )(h3, h3, h3, h3, lb_logits, hg_norm.reshape(1, HG_WIDTH), masks, s0)


def _rel_bucket_edges():
    d = np.arange(2 * REL_MAX_DIST)
    df = np.maximum(d, 1).astype(np.float32)
    large = REL_MAX_EXACT + (np.log(df / np.float32(REL_MAX_EXACT))
                             / np.float32(math.log(REL_MAX_DIST / REL_MAX_EXACT))
                             * np.float32(REL_BUCKETS - REL_MAX_EXACT)).astype(np.int32)
    bucket = np.where(d < REL_MAX_EXACT, d, np.minimum(large, REL_BUCKETS - 1))
    assert np.all(np.diff(bucket) >= 0) and bucket[-1] == REL_BUCKETS - 1
    return [int(np.argmax(bucket >= kb)) for kb in range(REL_BUCKETS)]


def _rel_bias(rel_ref, head, dist):
    edges = _rel_bucket_edges()
    out = jnp.full(dist.shape, rel_ref[0, head], F32)
    for kb in range(1, REL_BUCKETS):
        out = jnp.where(dist >= edges[kb], rel_ref[kb, head], out)
    return out


def _stack_rows(rows):
    ridx = lax.broadcasted_iota(I32, (GATE_PAD, MB_DIM), 0)
    out = jnp.zeros((GATE_PAD, MB_DIM), F32)
    for n, r in enumerate(rows):
        out = jnp.where(ridx == n, r, out)
    return out


def _topk_blocks(gate, n_cand, n_past):
    lane = lax.broadcasted_iota(I32, gate.shape, 1)
    past = lane < n_past
    g = jnp.where(past, gate, NEG_INF)
    rank = jnp.zeros(gate.shape, I32)
    for m in range(n_cand):
        gm = g[:, m:m + 1]
        ahead = jnp.where(gm > g, 1, jnp.where((gm == g) & (lane > m), 1, 0))
        rank = rank + ahead
    return jnp.where(past & (rank < MB_TOPK), 1.0, 0.0)


def _topk_blocks_t(gate_t, n_cand, n_past):
    blk_id = lax.broadcasted_iota(I32, gate_t.shape, 0)
    past = blk_id < n_past
    g = jnp.where(past, gate_t, NEG_INF)
    rank = jnp.zeros(gate_t.shape, I32)
    for m in range(n_cand):
        gm = g[m:m + 1, :]
        rank = rank + jnp.where(gm > g, 1, jnp.where((gm == g) & (blk_id > m), 1, 0))
    return past & (rank < MB_TOPK)


def _moba_prompt_kernel(rel_ref, q_ref, k_ref, v_ref, o_ref,
                        kx_ref, vx_ref, km_ref, bias_ref, eye_ref, s_ref, mx_ref, mb_ref, acc_ref, *, n_blk):
    b = pl.program_id(0)
    i = pl.program_id(1)
    blk = MB_BLOCK
    scale = MB_DIM ** -0.5
    heads = range(MB_HEADS)
    hsl = lambda hd: slice(hd * MB_DIM, (hd + 1) * MB_DIM)

    t = k_ref.shape[1]
    own_slot = n_blk - 1

    @pl.when((b == 0) & (i == 0))
    def _():
        r = lax.broadcasted_iota(I32, (blk, blk), 0)
        c = lax.broadcasted_iota(I32, (blk, blk), 1)
        eye_ref[...] = jnp.where(r == c, 1.0, 0.0).astype(BF16)
        kr = lax.broadcasted_iota(I32, (t, MB_DIM), 0)
        kc = lax.broadcasted_iota(I32, (t, MB_DIM), 1)
        blk_onehot = jnp.where(kr // blk == kc, 1.0, 0.0).astype(BF16)
        for hd in heads:
            bias_ref[hd, 0] = jnp.where(c <= r, _rel_bias(rel_ref, hd, r - c), NEG_INF)
            bias_ref[hd, 1] = _rel_bias(rel_ref, hd, r - c + blk)
            kx_ref[hd, :, MB_DIM:] = blk_onehot
            vx_ref[hd, :, MB_DIM:] = jnp.ones((t, MB_DIM), BF16)

    @pl.when(i == 0)
    def _():
        means = [jnp.sum(k_ref[0, n * blk:(n + 1) * blk, :], axis=0, keepdims=True) * (1.0 / blk)
                 for n in range(n_blk)]
        for hd in heads:
            kx_ref[hd, :, :MB_DIM] = k_ref[0][:, hsl(hd)].astype(BF16)
            vx_ref[hd, :, :MB_DIM] = v_ref[0][:, hsl(hd)].astype(BF16)
            km_ref[hd] = _stack_rows([mn[:, hsl(hd)] for mn in means])

    q = q_ref[0]
    blk_id = lax.broadcasted_iota(I32, (GATE_PAD, blk), 0)
    qh = [q[:, hsl(hd)] for hd in heads]
    gates = [_dot_nt_f32acc(km_ref[hd], qh[hd]) for hd in heads]
    qm_t = []
    for hd in heads:
        sel_t = _topk_blocks_t(gates[hd], n_blk, i)
        m_t = jnp.where(sel_t | (blk_id >= i), 0.0, MASK_BIG).astype(BF16)
        qm_t.append(jnp.concatenate([m_t, jnp.zeros((MB_DIM - GATE_PAD, blk), BF16)], axis=0))
    qmask = [_dot_nt(eye_ref[...], qm_t[hd]) for hd in heads]
    qx = [jnp.concatenate([qh[hd].astype(BF16), qmask[hd].astype(BF16)], axis=-1) for hd in heads]

    own0 = pl.multiple_of(i * blk, blk)
    dots = [_dot_nt(qx[hd], kx_ref[hd, pl.ds(own0, blk), :]) for hd in heads]
    for hd in heads:
        s = dots[hd] * scale + bias_ref[hd, 0]
        s_ref[hd, own_slot] = s
        mx_ref[hd] = jnp.maximum(s[:, :MB_DIM], s[:, MB_DIM:])

    for n in range(n_blk - 1):
        @pl.when(n < i)
        def _(n=n):
            dots = [_dot_nt(qx[hd], kx_ref[hd, n * blk:(n + 1) * blk, :]) for hd in heads]
            for hd in heads:
                far_bias = rel_ref[REL_BUCKETS - 1, hd]
                s = dots[hd] * scale + jnp.where(n == i - 1, bias_ref[hd, 1], far_bias)
                s_ref[hd, n] = s
                mx_ref[hd] = jnp.maximum(mx_ref[hd], jnp.maximum(s[:, :MB_DIM], s[:, MB_DIM:]))

    row_max = [jnp.max(mx_ref[hd], axis=-1, keepdims=True) for hd in heads]
    for hd in heads:
        mb_ref[hd] = jnp.broadcast_to(row_max[hd], (blk, 2 * MB_DIM))
    probs = [jnp.exp(s_ref[hd, own_slot] - mb_ref[hd]).astype(BF16) for hd in heads]
    for hd in heads:
        acc_ref[hd] = _dot(probs[hd], vx_ref[hd, pl.ds(own0, blk), :])

    for n in range(n_blk - 1):
        @pl.when(n < i)
        def _(n=n):
            probs = [jnp.exp(s_ref[hd, n] - mb_ref[hd]).astype(BF16) for hd in heads]
            pv = [_dot(probs[hd], vx_ref[hd, n * blk:(n + 1) * blk, :]) for hd in heads]
            for hd in heads:
                acc_ref[hd] += pv[hd]

    for hd in heads:
        acc = acc_ref[hd]
        o_ref[0, :, hsl(hd)] = (acc[:, :MB_DIM] / acc[:, MB_DIM:]).astype(o_ref.dtype)


def _moba_prompt(h3, rel_table):
    b, t, _ = h3.shape
    assert t % MB_BLOCK == 0
    n_blk = t // MB_BLOCK
    assert n_blk <= GATE_PAD
    qcol, kcol, vcol = COL_MQ // MB_WIDTH, COL_MK // MB_WIDTH, COL_MV // MB_WIDTH
    kern = functools.partial(_moba_prompt_kernel, n_blk=n_blk)
    wide = 2 * MB_DIM
    vmem = (2 * 2 * t * MB_WIDTH * 4 + 2 * MB_HEADS * t * wide * 2
            + MB_HEADS * (n_blk + 4) * MB_BLOCK * wide * 4) // MIB + 12
    return pl.pallas_call(
        kern,
        grid=(b, n_blk),
        in_specs=[pl.BlockSpec(memory_space=pltpu.SMEM),
                  pl.BlockSpec((1, MB_BLOCK, MB_WIDTH), lambda bb, i: (bb, i, qcol)),
                  pl.BlockSpec((1, t, MB_WIDTH), lambda bb, i: (bb, 0, kcol)),
                  pl.BlockSpec((1, t, MB_WIDTH), lambda bb, i: (bb, 0, vcol))],
        out_specs=pl.BlockSpec((1, MB_BLOCK, MB_WIDTH), lambda bb, i: (bb, i, 0)),
        out_shape=jax.ShapeDtypeStruct((b, t, MB_WIDTH), _operand_dtype(t)),
        scratch_shapes=[pltpu.VMEM((MB_HEADS, t, wide), BF16), pltpu.VMEM((MB_HEADS, t, wide), BF16),
                        pltpu.VMEM((MB_HEADS, GATE_PAD, MB_DIM), F32),
                        pltpu.VMEM((MB_HEADS, 2, MB_BLOCK, MB_BLOCK), F32),
                        pltpu.VMEM((MB_BLOCK, MB_BLOCK), BF16),
                        pltpu.VMEM((MB_HEADS, n_blk, MB_BLOCK, MB_BLOCK), F32),
                        pltpu.VMEM((MB_HEADS, MB_BLOCK, MB_DIM), F32),
                        pltpu.VMEM((MB_HEADS, MB_BLOCK, wide), F32),
                        pltpu.VMEM((MB_HEADS, MB_BLOCK, wide), F32)],
        compiler_params=_params(("arbitrary", "arbitrary"), vmem),
        name="moba_prompt",
    )(rel_table, h3, h3, h3)


def _moba_sample_kernel(pt_ref, rel_ref, q_ref, kn_ref, vn_ref, ck_hbm, cv_hbm, o_ref,
                        bias_ref, bias_own_ref, kbuf, vbuf, sem, *, n_pages, past_len):
    b = pl.program_id(0)
    nb = pl.num_programs(0)
    slot = b % 2
    nxt = jnp.minimum(b + 1, nb - 1)

    def page_copy(which, row, p, dst_slot):
        src, buf = (ck_hbm, kbuf) if which == 0 else (cv_hbm, vbuf)
        return pltpu.make_async_copy(src.at[pt_ref[row, p]], buf.at[dst_slot, p], sem.at[dst_slot, which])

    @pl.when(b == 0)
    def _():
        for p in range(n_pages):
            page_copy(0, 0, p, 0).start()
            page_copy(1, 0, p, 0).start()

    for p in range(n_pages):
        page_copy(0, b, p, slot).wait()
        page_copy(1, b, p, slot).wait()

    class _Pages:
        def __init__(self, buf):
            self.buf = buf

        def __getitem__(self, p):
            return self.buf.at[slot, pl.ds(p, 1)]

    kp = _Pages(kbuf)
    vp = _Pages(vbuf)
    t = q_ref.shape[1]
    rows = MB_HEADS * t
    page_rows = PAGE_SIZE * MB_HEADS
    pages_per_blk = MB_BLOCK // PAGE_SIZE
    n_past = past_len // MB_BLOCK
    scale = MB_DIM ** -0.5

    def head_rows(ref):
        x = ref[0]
        return jnp.concatenate([x[:, hd * MB_DIM:(hd + 1) * MB_DIM] for hd in range(MB_HEADS)], axis=0)

    @pl.when(pl.program_id(0) == 0)
    def _():
        tq = lax.broadcasted_iota(I32, (t, page_rows), 0)
        kc = lax.broadcasted_iota(I32, (t, page_rows), 1)
        for p in range(n_pages):
            dist = past_len + tq - (p * PAGE_SIZE + (kc >> 2))
            bias_ref[p] = jnp.concatenate(
                [jnp.where((kc & (MB_HEADS - 1)) == hd, _rel_bias(rel_ref, hd, dist), NEG_INF)
                 for hd in range(MB_HEADS)], axis=0)
        ro = lax.broadcasted_iota(I32, (t, rows), 0)
        co = lax.broadcasted_iota(I32, (t, rows), 1)
        for hd in range(MB_HEADS):
            own = _rel_bias(rel_ref, hd, ro - (co - hd * t))
            keep = (co >= hd * t) & (co <= hd * t + ro)
            bias_own_ref[hd * t:(hd + 1) * t, :] = jnp.where(keep, own, NEG_INF)

    q = head_rows(q_ref)
    qb = q.astype(BF16)
    row_head = lax.broadcasted_iota(I32, (rows, 1), 0) // t

    groups = page_rows // 8
    sums = [jnp.sum(kp[p][0].reshape(groups, 8, MB_DIM), axis=0) for p in range(n_pages)]
    blk_sums = []
    for n in range(n_past):
        acc = sums[n * pages_per_blk]
        for pp in range(1, pages_per_blk):
            acc = acc + sums[n * pages_per_blk + pp]
        blk_sums.append(acc)
    g_full = _dot_nt_f32acc(q, jnp.concatenate(blk_sums, axis=0))
    gc = lax.broadcasted_iota(I32, g_full.shape, 1)
    g_full = jnp.where((gc & (MB_HEADS - 1)) == row_head, g_full, 0.0)
    lane = lax.broadcasted_iota(I32, (rows, GATE_PAD), 1)
    gate = jnp.zeros((rows, GATE_PAD), F32)
    for n in range(n_past):
        g_n = jnp.sum(g_full[:, n * 8:(n + 1) * 8], axis=-1, keepdims=True) * (1.0 / MB_BLOCK)
        gate = jnp.where(lane == n, g_n, gate)
    sel = _topk_blocks(gate, n_past, n_past)

    s_own = _dot_nt(q, head_rows(kn_ref)) * scale + bias_own_ref[...]
    m = jnp.max(s_own, axis=-1, keepdims=True)
    s_past = []
    for p in range(n_pages):
        n = p // pages_per_blk
        page_copy(0, nxt, p, 1 - slot).start()
        page_copy(1, nxt, p, 1 - slot).start()
        s = _dot_nt(qb, kp[p][0].astype(BF16)) * scale + bias_ref[p]
        s = jnp.where(sel[:, n:n + 1] > 0.0, s, NEG_INF)
        s_past.append(s)
        m = jnp.maximum(m, jnp.max(s, axis=-1, keepdims=True))
    p_own = jnp.exp(s_own - m)
    l = jnp.sum(p_own, axis=-1, keepdims=True)
    out = _dot(p_own, head_rows(vn_ref))
    for p in range(n_pages):
        pr = jnp.exp(s_past[p] - m)
        l = l + jnp.sum(pr, axis=-1, keepdims=True)
        out = out + _dot(pr.astype(BF16), vp[p][0].astype(BF16))
    out = out / l
    for hd in range(MB_HEADS):
        o_ref[0, :, hd * MB_DIM:(hd + 1) * MB_DIM] = out[hd * t:(hd + 1) * t]

    @pl.when(b == nb - 1)
    def _():
        for p in range(n_pages):
            page_copy(0, nxt, p, 1 - slot).wait()
            page_copy(1, nxt, p, 1 - slot).wait()


def _moba_sample(h3, cache_k, cache_v, page_table, rel_table):
    db, t, _ = h3.shape
    n_pages = page_table.shape[1]
    past_len = n_pages * PAGE_SIZE
    assert past_len % MB_BLOCK == 0 and t <= MB_BLOCK and past_len // MB_BLOCK < GATE_PAD
    assert MB_HEADS == 4 and t % 8 == 0
    qcol, kcol, vcol = COL_MQ // MB_WIDTH, COL_MK // MB_WIDTH, COL_MV // MB_WIDTH
    page_rows = PAGE_SIZE * MB_HEADS
    new = lambda k: pl.BlockSpec((1, t, MB_WIDTH), lambda i, pt: (i, 0, k))
    kern = functools.partial(_moba_sample_kernel, n_pages=n_pages, past_len=past_len)
    page_slots = pltpu.VMEM((2, n_pages, page_rows, MB_DIM), F32)
    grid_spec = pltpu.PrefetchScalarGridSpec(
        num_scalar_prefetch=1,
        grid=(db,),
        in_specs=[pl.BlockSpec(memory_space=pltpu.SMEM), new(qcol), new(kcol), new(vcol),
                  pl.BlockSpec(memory_space=pl.ANY), pl.BlockSpec(memory_space=pl.ANY)],
        out_specs=pl.BlockSpec((1, t, MB_WIDTH), lambda i, pt: (i, 0, 0)),
        scratch_shapes=[pltpu.VMEM((n_pages, MB_HEADS * t, page_rows), F32),
                        pltpu.VMEM((MB_HEADS * t, MB_HEADS * t), F32),
                        page_slots, page_slots, pltpu.SemaphoreType.DMA((2, 2))],
    )
    return pl.pallas_call(
        kern,
        grid_spec=grid_spec,
        out_shape=jax.ShapeDtypeStruct((db, t, MB_WIDTH), F32),
        compiler_params=_params(("arbitrary",), 40),
        name="moba_sample",
    )(page_table, rel_table, h3, h3, h3, cache_k, cache_v)


def _merge_kernel(oa_ref, ob_ref, ga0_ref, ga1_ref, gb0_ref, gb1_ref, x_ref,
                  wa_ref, wb_ref, wo_ref, gx_ref, wq_ref, x1_ref, xq_ref):
    ga = jnp.concatenate([ga0_ref[...], ga1_ref[...]], axis=-1)
    gb = jnp.concatenate([gb0_ref[...], gb1_ref[...]], axis=-1)
    pa = _dot(oa_ref[...].astype(BF16), wa_ref[...])
    pb = _dot(ob_ref[...].astype(BF16), wb_ref[...])
    merged = _sigmoid(ga) * pa + _sigmoid(gb) * pb
    x1 = x_ref[...] + _dot(merged.astype(BF16), wo_ref[...])
    x1_ref[...] = x1
    xq_ref[...] = _dot(_rmsnorm(x1, gx_ref[...]).astype(BF16), wq_ref[...]).astype(xq_ref.dtype)


def _merge(oa, ob, h2, x, wa, wb, wo, gx, wq, *, tm, xq_dtype):
    n = x.shape[0]
    assert n % tm == 0
    half = D_MODEL // 2
    tok = lambda w, k=0: pl.BlockSpec((tm, w), lambda i: (i, k))
    full = lambda a: pl.BlockSpec(a.shape, lambda i: (0, 0))
    gx2 = gx.reshape(1, D_MODEL)
    return pl.pallas_call(
        _merge_kernel,
        grid=(n // tm,),
        in_specs=[tok(HG_WIDTH), tok(MB_WIDTH),
                  tok(half, COL_GA // half), tok(half, COL_GA // half + 1),
                  tok(half, COL_GB // half), tok(half, COL_GB // half + 1),
                  tok(D_MODEL), full(wa), full(wb), full(wo), full(gx2), full(wq)],
        out_specs=[tok(D_MODEL), tok(XA_WIDTH)],
        out_shape=[jax.ShapeDtypeStruct((n, D_MODEL), F32),
                   jax.ShapeDtypeStruct((n, XA_WIDTH), xq_dtype)],
        compiler_params=_params(("parallel",), 48),
        name="merge_mix",
    )(oa, ob, h2, h2, h2, h2, x, wa, wb, wo, gx2, wq)


def _mem_attn_kernel(q_ref, k_ref, v_ref, o_ref):
    scale = XA_DIM ** -0.5
    heads = range(XA_HEADS)
    hsl = lambda hd: slice(hd * XA_DIM, (hd + 1) * XA_DIM)
    logits = [_dot_nt(q_ref[0][:, hsl(hd)].astype(BF16), k_ref[0][:, hsl(hd)].astype(BF16)) * scale
              for hd in heads]
    row_max = [jnp.max(s, axis=-1, keepdims=True) for s in logits]
    probs = [jnp.exp(s - m) for s, m in zip(logits, row_max)]
    denom = [jnp.sum(p, axis=-1, keepdims=True) for p in probs]
    outs = [_dot(probs[hd].astype(BF16), v_ref[0][:, hsl(hd)].astype(BF16)) for hd in heads]
    for hd in heads:
        o_ref[0, :, hsl(hd)] = (outs[hd] / denom[hd]).astype(o_ref.dtype)


def _mem_attn(xq3, mem_kv, *, tq):
    b, t, _ = xq3.shape
    assert t % tq == 0
    mem = lambda k: pl.BlockSpec((1, mem_kv.shape[1], XA_WIDTH), lambda i, j: (i, 0, k))
    qs = pl.BlockSpec((1, tq, XA_WIDTH), lambda i, j: (i, j, 0))
    return pl.pallas_call(
        _mem_attn_kernel,
        grid=(b, t // tq),
        in_specs=[qs, mem(0), mem(1)],
        out_specs=qs,
        out_shape=jax.ShapeDtypeStruct((b, t, XA_WIDTH), _operand_dtype(t)),
        compiler_params=_params(("parallel", "parallel"), 32),
        name="mem_attn",
    )(xq3, mem_kv, mem_kv)


def _mem_attn_rows_kernel(q_ref, k_ref, v_ref, o_ref):
    bb, t, _ = q_ref.shape
    rows = XA_HEADS * t
    scale = XA_DIM ** -0.5
    row_head = lax.broadcasted_iota(I32, (rows, 1), 0) // t
    kc = lax.broadcasted_iota(I32, (rows, k_ref.shape[1]), 1)
    same_head = (kc & (XA_HEADS - 1)) == row_head
    qs = []
    for i in range(bb):
        x = q_ref[i]
        qs.append(jnp.concatenate([x[:, hd * XA_DIM:(hd + 1) * XA_DIM] for hd in range(XA_HEADS)],
                                  axis=0).astype(BF16))
    logits = [jnp.where(same_head, _dot_nt(qs[i], k_ref[i].astype(BF16)) * scale, NEG_INF)
              for i in range(bb)]
    row_max = [jnp.max(s, axis=-1, keepdims=True) for s in logits]
    probs = [jnp.exp(s - m) for s, m in zip(logits, row_max)]
    denom = [jnp.sum(p, axis=-1, keepdims=True) for p in probs]
    outs = [_dot(probs[i].astype(BF16), v_ref[i].astype(BF16)) / denom[i] for i in range(bb)]
    for i in range(bb):
        for hd in range(XA_HEADS):
            o_ref[i, :, hd * XA_DIM:(hd + 1) * XA_DIM] = outs[i][hd * t:(hd + 1) * t]


def _mem_attn_rows(xq3, mem_k, mem_v, *, bb):
    b, t, _ = xq3.shape
    assert b % bb == 0 and XA_HEADS == 4 and t % 8 == 0
    mem = pl.BlockSpec((bb,) + mem_k.shape[1:], lambda i: (i, 0, 0))
    qs = pl.BlockSpec((bb, t, XA_WIDTH), lambda i: (i, 0, 0))
    return pl.pallas_call(
        _mem_attn_rows_kernel,
        grid=(b // bb,),
        in_specs=[qs, mem, mem],
        out_specs=qs,
        out_shape=jax.ShapeDtypeStruct((b, t, XA_WIDTH), F32),
        compiler_params=_params(("parallel",), 32),
        name="mem_attn_rows",
    )(xq3, mem_k, mem_v)


def _route(logits):
    lane = lax.broadcasted_iota(I32, logits.shape, 1).astype(F32)
    first = lambda hit: jnp.min(jnp.where(hit, lane, float(ROUTER_LANES)), axis=-1, keepdims=True)
    gl = jnp.where(lane < N_GROUPS, logits, NEG_INF)
    gmax = jnp.max(gl, axis=-1, keepdims=True)
    grp = first(gl == gmax)
    g_prob = 1.0 / jnp.sum(jnp.exp(gl - gmax), axis=-1, keepdims=True)
    e_lo = EXPERT_LANE0 + grp * EXPERTS_PER_GROUP
    in_grp = (lane >= e_lo) & (lane < e_lo + EXPERTS_PER_GROUP)
    el = jnp.where(in_grp, logits, NEG_INF)
    top1 = jnp.max(el, axis=-1, keepdims=True)
    idx1 = first(el == top1)
    el2 = jnp.where(lane == idx1, NEG_INF, el)
    top2 = jnp.max(el2, axis=-1, keepdims=True)
    idx2 = first(el2 == top2)
    e2 = jnp.exp(top2 - top1)
    w1 = g_prob / (1.0 + e2)
    w2 = w1 * e2
    comb = jnp.where(lane == idx1, w1, 0.0) + jnp.where(lane == idx2, w2, 0.0)
    return jnp.where(lane == GRP_LANE, grp, comb)


def _xo_route_kernel(x1_ref, at_ref, wxo_ref, gf_ref, wr_ref, br_ref, x2_ref, xf_ref, comb_ref):
    x2 = x1_ref[...] + _dot(at_ref[...].astype(BF16), wxo_ref[...])
    x2_ref[...] = x2
    xf = _rmsnorm(x2, gf_ref[...])
    xf_ref[...] = xf.astype(BF16)
    comb_ref[...] = _route(_dot_f32acc(xf, wr_ref[...]) + br_ref[...])


def _xo_route(x1, attn, wxo, g_ffn, w_router_t, b_router_t, *, tm):
    n = x1.shape[0]
    assert n % tm == 0
    tok = lambda w: pl.BlockSpec((tm, w), lambda i: (i, 0))
    full = lambda a: pl.BlockSpec(a.shape, lambda i: (0, 0))
    gf2 = g_ffn.reshape(1, D_MODEL)
    return pl.pallas_call(
        _xo_route_kernel,
        grid=(n // tm,),
        in_specs=[tok(D_MODEL), tok(XA_WIDTH), full(wxo), full(gf2), full(w_router_t), full(b_router_t)],
        out_specs=[tok(D_MODEL), tok(D_MODEL), tok(ROUTER_LANES)],
        out_shape=[jax.ShapeDtypeStruct((n, D_MODEL), F32), jax.ShapeDtypeStruct((n, D_MODEL), BF16),
                   jax.ShapeDtypeStruct((n, ROUTER_LANES), F32)],
        compiler_params=_params(("parallel",), 40),
        name="xo_route",
    )(x1, attn, wxo, gf2, w_router_t, b_router_t)


def _moe_kernel(xf_ref, comb_ref, x2_ref, wg_ref, wu_ref, wd_ref, gn_ref, y_ref,
                tri_ref, posc_ref, cnt_ref, xs_ref, ws_ref, ys_ref, *, r):
    i = pl.program_id(0)
    step = pl.program_id(1)
    tm = xf_ref.shape[0]
    per_step = wg_ref.shape[0]
    steps_per_group = EXPERTS_PER_GROUP // per_step
    g = step // steps_per_group
    lane = lax.broadcasted_iota(I32, (tm, ROUTER_LANES), 1)

    @pl.when((i == 0) & (step == 0))
    def _():
        rr = lax.broadcasted_iota(I32, (tm, tm), 0)
        cc = lax.broadcasted_iota(I32, (tm, tm), 1)
        tri_ref[...] = jnp.where(cc < rr, 1.0, 0.0).astype(BF16)

    @pl.when(step == 0)
    def _():
        comb = comb_ref[...]
        grp = comb[:, GRP_LANE:GRP_LANE + 1]
        lane_f = lane.astype(F32)
        rel = lane - EXPERT_LANE0
        used = (comb > 0.0) & (rel >= 0) & (rel < N_EXPERTS)
        second = (rel & (EXPERTS_PER_GROUP - 1)) >= EXPERTS_PER_GROUP // 2
        uses_a = jnp.max(jnp.where(used & jnp.logical_not(second), 1.0, 0.0), axis=-1, keepdims=True)
        uses_b = jnp.max(jnp.where(used & second, 1.0, 0.0), axis=-1, keepdims=True)
        key = grp * MOE_CLASSES + jnp.where(uses_b == 0.0, 0.0, jnp.where(uses_a == 0.0, 2.0, 1.0))
        onehot = jnp.where((lane_f == key) & (lane < N_GROUPS * MOE_CLASSES), 1.0, 0.0)
        before = _dot(tri_ref[...], onehot.astype(BF16))
        cnt = jnp.sum(onehot, axis=0, keepdims=True)
        cnt_ref[...] = cnt
        pos = jnp.sum(jnp.where(lane_f < key, cnt, 0.0) + onehot * before, axis=-1, keepdims=True)
        posc_ref[...] = jnp.where(lane == 0, pos, 0.0)
        hi = jnp.floor(pos * (1.0 / POS_SPLIT))
        lo = pos - hi * POS_SPLIT
        cols = jnp.where(lane == 0, hi, jnp.where(lane == 1, lo, 0.0))
        er = lax.broadcasted_iota(I32, (V7X_BF16_SUBLANES, ROUTER_LANES), 0)
        ec = lax.broadcasted_iota(I32, (V7X_BF16_SUBLANES, ROUTER_LANES), 1)
        post = _dot_nt(jnp.where(er == ec, 1.0, 0.0).astype(BF16), cols.astype(BF16))
        pos_row = post[0:1, :] * POS_SPLIT + post[1:2, :]
        slot = lax.broadcasted_iota(I32, (tm, tm), 0).astype(F32)
        perm = jnp.where(pos_row == slot, 1.0, 0.0).astype(BF16)
        comb_hi, comb_lo = _split2(comb)
        xs_ref[0:tm, :] = _dot(perm, xf_ref[...]).astype(BF16)
        ws_ref[0:tm, :] = _dot(perm, comb_hi) + _dot(perm, comb_lo)
        xs_ref[tm:, :] = jnp.zeros((r, D_MODEL), BF16)
        ws_ref[tm:, :] = jnp.zeros((r, ROUTER_LANES), F32)
        ys_ref[...] = jnp.zeros(ys_ref.shape, F32)

    lane_row = lax.broadcasted_iota(I32, (1, ROUTER_LANES), 1)
    key0 = g * MOE_CLASSES + step % steps_per_group
    first = jnp.sum(jnp.where(lane_row < key0, cnt_ref[...], 0.0)).astype(I32)
    count = jnp.sum(jnp.where((lane_row == key0) | (lane_row == key0 + 1), cnt_ref[...], 0.0)).astype(I32)
    start = (first // V7X_BF16_SUBLANES) * V7X_BF16_SUBLANES
    n_windows = (first + count - start + (r - 1)) // r

    def experts(c, carry):
        rows = pl.ds(pl.multiple_of(start + c * r, V7X_BF16_SUBLANES), r)
        xg = xs_ref[rows, :]
        wq = ws_ref[rows, :]
        wl = lax.broadcasted_iota(I32, (r, ROUTER_LANES), 1)
        ups = [(_dot(xg, wg_ref[k]), _dot(xg, wu_ref[k])) for k in range(per_step)]
        hidden = []
        for k, (gate, up) in enumerate(ups):
            w_e = jnp.sum(jnp.where(wl == EXPERT_LANE0 + step * per_step + k, wq, 0.0), axis=-1, keepdims=True)
            hidden.append((_silu(gate) * up * w_e).astype(BF16))
        w_down = wd_ref[...].reshape(per_step * EXPERT_HIDDEN, D_MODEL)
        ys_ref[rows, :] += _dot(jnp.concatenate(hidden, axis=-1), w_down)
        return carry

    lax.fori_loop(0, n_windows, experts, 0)

    @pl.when(step == pl.num_programs(1) - 1)
    def _():
        slot = lax.broadcasted_iota(I32, (tm, tm), 1).astype(F32)
        place = jnp.where(posc_ref[:, 0:1] == slot, 1.0, 0.0).astype(BF16)
        moe = _dot(place, ys_ref[0:tm, :].astype(BF16))
        y_ref[...] = _rmsnorm(x2_ref[...] + moe, gn_ref[...])


def _moe(xf, comb, x2, wg, wu, wd, g_final, *, tm):
    n = xf.shape[0]
    assert n % tm == 0 and tm % V7X_BF16_SUBLANES == 0
    assert 2 * MOE_EXPERTS_PER_STEP == EXPERTS_PER_GROUP and N_GROUPS * MOE_CLASSES <= ROUTER_LANES
    r = min(MOE_CHUNK_ROWS, tm)
    cap = tm + r
    per_step = MOE_EXPERTS_PER_STEP
    tok = lambda w: pl.BlockSpec((tm, w), lambda i, s: (i, 0))
    gn2 = g_final.reshape(1, D_MODEL)
    kern = functools.partial(_moe_kernel, r=r)
    vmem = (2 * tm * D_MODEL * (2 + 4 + 4) + 2 * tm * ROUTER_LANES * 4
            + 2 * 3 * per_step * D_MODEL * EXPERT_HIDDEN * 2 + tm * tm * 2
            + cap * D_MODEL * (2 + 4) + tm * D_MODEL * 4) // MIB + 8
    return pl.pallas_call(
        kern,
        grid=(n // tm, N_EXPERTS // per_step),
        in_specs=[tok(D_MODEL), tok(ROUTER_LANES), tok(D_MODEL),
                  pl.BlockSpec((per_step, D_MODEL, EXPERT_HIDDEN), lambda i, s: (s, 0, 0)),
                  pl.BlockSpec((per_step, D_MODEL, EXPERT_HIDDEN), lambda i, s: (s, 0, 0)),
                  pl.BlockSpec((per_step, EXPERT_HIDDEN, D_MODEL), lambda i, s: (s, 0, 0)),
                  pl.BlockSpec((1, D_MODEL), lambda i, s: (0, 0))],
        out_specs=tok(D_MODEL),
        out_shape=jax.ShapeDtypeStruct((n, D_MODEL), F32),
        scratch_shapes=[pltpu.VMEM((tm, tm), BF16),
                        pltpu.VMEM((tm, ROUTER_LANES), F32), pltpu.VMEM((1, ROUTER_LANES), F32),
                        pltpu.VMEM((cap, D_MODEL), BF16), pltpu.VMEM((cap, ROUTER_LANES), F32),
                        pltpu.VMEM((cap, D_MODEL), F32)],
        compiler_params=_params(("arbitrary", "arbitrary"), vmem),
        name="moe_grouped",
    )(xf, comb, x2, wg, wu, wd, gn2)


def _token_tile(n, cap=512):
    return cap if n % cap == 0 else n


def _col_tile(n, cap=1536):
    return max(c for c in range(V7X_LANES, cap + 1, V7X_LANES) if n % c == 0)


def _layer(x3, w, s0, moba_fn, mem_fn):
    b, t, _ = x3.shape
    n = b * t
    x2d = x3.reshape(n, D_MODEL)
    tm = _token_tile(n)
    h2, k4, v4 = _norm_matmul(x2d, w["norm_mix"], w["w_in"], tm=_token_tile(n, 1024),
                              tn=_col_tile(IN_COLS), head_cols=(COL_MK, COL_MV), heads=MB_HEADS,
                              head_dim=MB_DIM)
    h3 = h2.reshape(b, t, IN_COLS)
    oa, s_new = _hgrn(h3, w["hg_lb_logits"], w["hg_norm"], s0)
    ob = moba_fn(h3)
    x1, xq = _merge(oa.reshape(n, HG_WIDTH), ob.reshape(n, MB_WIDTH), h2, x2d,
                    w["w_branch_a"], w["w_branch_b"], w["w_mix_out"], w["norm_xattn"], w["w_xq"], tm=tm,
                    xq_dtype=_operand_dtype(t))
    attn = mem_fn(xq.reshape(b, t, XA_WIDTH))
    x2, xf, comb = _xo_route(x1, attn.reshape(n, XA_WIDTH), w["w_xo"], w["norm_ffn"],
                             w["w_router"], w["b_router"], tm=_token_tile(n, 1024))
    y = _moe(xf, comb, x2, w["w_expert_gate"], w["w_expert_up"], w["w_expert_down"],
             w["norm_final"], tm=_token_tile(n, 1024))
    k_new = k4.reshape(b, t, MB_HEADS, MB_DIM)
    v_new = v4.reshape(b, t, MB_HEADS, MB_DIM)
    return y.reshape(b, t, D_MODEL), k_new, v_new, s_new


def kernel(x_prompt, x_sample, cache_k, cache_v, state_hgrn, cache_mem_k, cache_mem_v, page_table, mem_prompt, norm_mix, w_in, hg_lb_logits, hg_norm, w_branch_a, w_branch_b, w_mix_out, rel_table, norm_xattn, norm_mem, w_xq, w_xk, w_xv, w_xo, norm_ffn, w_group_router, b_group_router, w_expert_router, b_expert_router, w_expert_gate, w_expert_up, w_expert_down, norm_final):
    assert w_in.shape[0] == DEPTH == 1 and hg_lb_logits.shape[0] == DEPTH + 1
    b = x_prompt.shape[0]
    db = x_sample.shape[0]
    n_pool = cache_k.shape[1]
    pad_lanes = ROUTER_LANES - EXPERT_LANE0 - N_EXPERTS
    w = {
        "norm_mix": norm_mix[0], "w_in": w_in[0].astype(BF16),
        "hg_lb_logits": hg_lb_logits, "hg_norm": hg_norm[0],
        "w_branch_a": w_branch_a[0].astype(BF16), "w_branch_b": w_branch_b[0].astype(BF16),
        "w_mix_out": w_mix_out[0].astype(BF16), "norm_xattn": norm_xattn[0],
        "w_xq": w_xq[0].astype(BF16), "w_xo": w_xo[0].astype(BF16), "norm_ffn": norm_ffn[0],
        "w_router": jnp.pad(jnp.concatenate(
            [w_group_router[0], jnp.zeros((D_MODEL, EXPERT_LANE0 - N_GROUPS), F32), w_expert_router[0]],
            axis=1), ((0, 0), (0, pad_lanes))),
        "b_router": jnp.pad(jnp.concatenate(
            [b_group_router[0], jnp.zeros((EXPERT_LANE0 - N_GROUPS,), F32), b_expert_router[0]]),
            (0, pad_lanes)).reshape(1, ROUTER_LANES),
        "w_expert_gate": w_expert_gate[0].reshape(N_EXPERTS, D_MODEL, EXPERT_HIDDEN).astype(BF16),
        "w_expert_up": w_expert_up[0].reshape(N_EXPERTS, D_MODEL, EXPERT_HIDDEN).astype(BF16),
        "w_expert_down": w_expert_down[0].reshape(N_EXPERTS, EXPERT_HIDDEN, D_MODEL).astype(BF16),
        "norm_final": norm_final,
    }

    w_mem = jnp.concatenate([w_xk[0], w_xv[0]], axis=1).astype(BF16)
    mem_kv, mk_p, mv_p = _norm_matmul(mem_prompt.reshape(b * MEM_LEN, D_MODEL), norm_mem[0], w_mem,
                                      tm=_token_tile(b * MEM_LEN), tn=512, head_cols=(0, XA_WIDTH),
                                      heads=XA_HEADS, head_dim=XA_DIM)
    mem_kv = mem_kv.reshape(b, MEM_LEN, 2 * XA_WIDTH)
    s0 = jnp.zeros((b, HG_HEADS, HG_DIM, HG_DIM), F32)
    y_p, k_p, v_p, s_p = _layer(x_prompt, w, s0,
                                functools.partial(_moba_prompt, rel_table=rel_table),
                                functools.partial(_mem_attn, mem_kv=mem_kv, tq=512))

    ck = cache_k[0].reshape(n_pool, PAGE_SIZE * MB_HEADS, MB_DIM)
    cv = cache_v[0].reshape(n_pool, PAGE_SIZE * MB_HEADS, MB_DIM)
    moba_s = functools.partial(_moba_sample, cache_k=ck, cache_v=cv, page_table=page_table,
                               rel_table=rel_table)
    mem_s = functools.partial(_mem_attn_rows, bb=math.gcd(db, 8),
                              mem_k=cache_mem_k[0].reshape(db, MEM_LEN * XA_HEADS, XA_DIM),
                              mem_v=cache_mem_v[0].reshape(db, MEM_LEN * XA_HEADS, XA_DIM))
    y_s, k_s, v_s, s_s = _layer(x_sample, w, state_hgrn[0], moba_s, mem_s)

    heads = lambda a: a.reshape(b, MEM_LEN, XA_HEADS, XA_DIM)[None]
    return (y_p, y_s, k_p[None], v_p[None], s_p[None], heads(mk_p), heads(mv_p),
            k_s[None], v_s[None], s_s[None])
```

```python
import functools
import math

import numpy as np
import jax
import jax.numpy as jnp
from jax import lax
from jax.experimental import pallas as pl
from jax.experimental.pallas import tpu as pltpu

F32 = jnp.float32
BF16 = jnp.bfloat16
I32 = jnp.int32

D_MODEL = 1024
DEPTH = 1
PAGE_SIZE = 128
HG_HEADS = 4
HG_DIM = 128
HG_WIDTH = HG_HEADS * HG_DIM
MB_HEADS = 4
MB_DIM = 128
MB_WIDTH = MB_HEADS * MB_DIM
MB_BLOCK = 256
MB_TOPK = 3
REL_BUCKETS = 32
REL_MAX_DIST = 128
REL_MAX_EXACT = REL_BUCKETS // 2
MEM_LEN = 256
XA_HEADS = 4
XA_DIM = 128
XA_WIDTH = XA_HEADS * XA_DIM
N_GROUPS = 4
EXPERTS_PER_GROUP = 8
N_EXPERTS = N_GROUPS * EXPERTS_PER_GROUP
EXPERT_TOPK = 2
EXPERT_HIDDEN = 256
NORM_EPS = 1e-6
IN_COLS = 4 * HG_WIDTH + 3 * MB_WIDTH + 2 * D_MODEL
COL_MQ = 4 * HG_WIDTH
COL_MK = COL_MQ + MB_WIDTH
COL_MV = COL_MK + MB_WIDTH
COL_GA = COL_MV + MB_WIDTH
COL_GB = COL_GA + D_MODEL

V7X_LANES = 128
V7X_BF16_SUBLANES = 16
V7X_VMEM_BYTES = 64 * 1024 * 1024
MIB = 1024 * 1024

NEG_INF = float("-inf")
MASK_BIG = -1e30
GATE_PAD = V7X_BF16_SUBLANES
ROUTER_LANES = 128
GRP_LANE = 0
EXPERT_LANE0 = 8
POS_SPLIT = 32.0
MOE_CHUNK_ROWS = 256
MOE_EXPERTS_PER_STEP = EXPERTS_PER_GROUP // 2
MOE_CLASSES = 3
HGRN_PROMPT_CHUNK = 128
HGRN_PROMPT_ROWS = 4
HGRN_MIN_CHUNK = V7X_BF16_SUBLANES
HGRN_CHAINS = 32


def _params(semantics, vmem_mib):
    return pltpu.CompilerParams(dimension_semantics=semantics,
                                vmem_limit_bytes=min(vmem_mib * MIB, V7X_VMEM_BYTES - 8 * MIB))


def _dot(a, b):
    return jnp.dot(a, b, preferred_element_type=F32)


def _dot_nt(a, b):
    return lax.dot_general(a, b, (((1,), (1,)), ((), ())), preferred_element_type=F32)


def _dot_tn(a, b):
    return lax.dot_general(a, b, (((0,), (0,)), ((), ())), preferred_element_type=F32)


def _split2(a):
    hi = a.astype(BF16)
    lo = (a - hi.astype(F32)).astype(BF16)
    return hi, lo


def _dot_nt_f32acc(a, b):
    ah, al = _split2(a)
    bh, bl = _split2(b)
    return _dot_nt(ah, bh) + (_dot_nt(ah, bl) + _dot_nt(al, bh))


def _dot_f32acc(a, b):
    ah, al = _split2(a)
    bh, bl = _split2(b)
    return _dot(ah, bh) + (_dot(ah, bl) + _dot(al, bh))


def _operand_dtype(t):
    return BF16 if t % V7X_BF16_SUBLANES == 0 else F32


def _rmsnorm(x, g):
    return x * lax.rsqrt(jnp.mean(x * x, axis=-1, keepdims=True) + NORM_EPS) * g


def _sigmoid(x):
    return 1.0 / (1.0 + jnp.exp(-x))


def _silu(x):
    return x * _sigmoid(x)


def _norm_matmul_kernel(x_ref, g_ref, w_ref, o_ref, *refs, head_cols, heads, head_dim):
    head_refs, xn_ref = refs[:-1], refs[-1]
    j = pl.program_id(1)
    tm, tn = o_ref.shape

    @pl.when(j == 0)
    def _():
        xn_ref[...] = _rmsnorm(x_ref[...], g_ref[...]).astype(BF16)

    res = _dot(xn_ref[...], w_ref[...])
    o_ref[...] = res
    for col0, href in zip(head_cols, head_refs):
        for hd in range(heads):
            tile, local = divmod(col0 + hd * head_dim, tn)

            @pl.when(j == tile)
            def _(hd=hd, local=local, href=href):
                href[pl.ds(hd, tm, stride=heads), :] = res[:, local:local + head_dim]


def _norm_matmul(x, g, w_bf16, *, tm, tn, head_cols=(), heads=1, head_dim=V7X_LANES):
    m, d = x.shape
    n = w_bf16.shape[1]
    assert m % tm == 0 and n % tn == 0 and tn % head_dim == 0
    vmem = (2 * tm * d * 4 + tm * d * 2 + 2 * d * tn * 2 + 2 * tm * tn * 4
            + len(head_cols) * 2 * tm * heads * head_dim * 4) // MIB + 12
    kern = functools.partial(_norm_matmul_kernel, head_cols=tuple(head_cols), heads=heads,
                             head_dim=head_dim)
    head_spec = pl.BlockSpec((tm * heads, head_dim), lambda i, j: (i, 0))
    head_shape = jax.ShapeDtypeStruct((m * heads, head_dim), F32)
    return pl.pallas_call(
        kern,
        grid=(m // tm, n // tn),
        in_specs=[pl.BlockSpec((tm, d), lambda i, j: (i, 0)),
                  pl.BlockSpec((1, d), lambda i, j: (0, 0)),
                  pl.BlockSpec((d, tn), lambda i, j: (0, j))],
        out_specs=[pl.BlockSpec((tm, tn), lambda i, j: (i, j))] + [head_spec] * len(head_cols),
        out_shape=[jax.ShapeDtypeStruct((m, n), F32)] + [head_shape] * len(head_cols),
        scratch_shapes=[pltpu.VMEM((tm, d), BF16)],
        compiler_params=_params(("parallel", "arbitrary"), vmem),
        name="norm_matmul",
    )(x, g.reshape(1, d), w_bf16)


def _hgrn_levels(c):
    return int(round(math.log2(c)))


def _hgrn_sum_masks(c):
    t = np.arange(c)[:, None]
    j = np.arange(c)[None, :]
    rows = [j <= t, j > t]
    for lv in range(_hgrn_levels(c)):
        half = c >> (lv + 1)
        blk = 2 * half
        mid = (t // blk) * blk + half - 1
        upper = (t % blk) >= half
        rows.append((upper & (j > mid) & (j <= t)) | ((~upper) & (j > t) & (j <= mid)))
    return np.concatenate(rows, axis=0).astype(np.float32)


def _hgrn_kernel(hq_ref, hf_ref, hi_ref, hg_ref, lbl_ref, gn_ref, mask_ref, s0_ref,
                 o_ref, s_ref, st_ref, *, c, t_blk, bb):
    j = pl.program_id(1)
    chains = [(bi, hd) for bi in range(bb) for hd in range(HG_HEADS)]

    @pl.when(j == 0)
    def _():
        for bi, hd in chains:
            st_ref[bi * HG_HEADS + hd] = s0_ref[bi, hd].T

    lbl = lbl_ref[...]
    lmax = jnp.max(lbl, axis=0, keepdims=True)
    lexp = jnp.exp(lbl - lmax)
    lb_all = lexp[0:1, :] / jnp.sum(lexp, axis=0, keepdims=True)

    masks = mask_ref[...]
    pad = c - t_blk
    row = lax.broadcasted_iota(I32, (c, HG_DIM), 0)
    rr = lax.broadcasted_iota(I32, (c, c), 0)
    cc = lax.broadcasted_iota(I32, (c, c), 1)

    def padded(a):
        if pad == 0:
            return a
        return jnp.concatenate([a, jnp.zeros((pad, HG_DIM), F32)], axis=0)

    hsl = lambda hd: slice(hd * HG_DIM, (hd + 1) * HG_DIM)
    qs, ks, vs, logf_parts = [], [], [], []
    for bi, hd in chains:
        xq = hq_ref[bi][:, hsl(hd)]
        lb = lb_all[:, hsl(hd)]
        f = lb + (1.0 - lb) * _sigmoid(hf_ref[bi][:, hsl(hd)])
        qs.append(padded(_silu(xq)))
        ks.append(padded(1.0 - f))
        vs.append(padded(hi_ref[bi][:, hsl(hd)]))
        logf_parts.append(_split2(padded(jnp.log(f))))
    exs = []
    for n in range(0, len(chains), 2):
        (hi_a, lo_a), (hi_b, lo_b) = logf_parts[n], logf_parts[n + 1]
        stacked = jnp.concatenate([jnp.concatenate([hi_a, hi_b], axis=-1),
                                   jnp.concatenate([lo_a, lo_b], axis=-1)], axis=0)
        ex = jnp.exp(_dot(masks, stacked))
        exs += [ex[:, :HG_DIM], ex[:, HG_DIM:]]

    level_dots = []
    for q, kk, ex in zip(qs, ks, exs):
        per_level = []
        for lv in range(_hgrn_levels(c)):
            half = c >> (lv + 1)
            upper = (row & half) != 0
            ex_lv = ex[(2 + lv) * c:(3 + lv) * c]
            a = jnp.where(upper, q * ex_lv, 0.0).astype(BF16)
            bm = jnp.where(upper, 0.0, kk * ex_lv).astype(BF16)
            per_level.append(_dot_nt(a, bm))
        level_dots.append(per_level)

    outs = []
    for n, (bi, hd) in enumerate(chains):
        q, kk, v, ex = qs[n], ks[n], vs[n], exs[n]
        scores = jnp.zeros((c, c), F32)
        for lv, d in enumerate(level_dots[n]):
            scores = scores + jnp.where((rr ^ cc) < 2 * (c >> (lv + 1)), d, 0.0)
        eb = ex[0:c]
        eb_rev = ex[c:2 * c]
        diag = jnp.sum(q * kk, axis=-1, keepdims=True)
        st = st_ref[bi * HG_HEADS + hd]
        vb = v.astype(BF16)
        o = (_dot_nt((q * eb).astype(BF16), st.astype(BF16))
             + _dot(scores.astype(BF16), vb) + diag * v)
        st_ref[bi * HG_HEADS + hd] = st * eb[c - 1:c, :] + _dot_tn(vb, (kk * eb_rev).astype(BF16))
        outs.append(o[0:t_blk])

    for (bi, hd), o in zip(chains, outs):
        xg = hg_ref[bi][:, hsl(hd)]
        o_ref[bi, :, hsl(hd)] = (_rmsnorm(o, gn_ref[:, hsl(hd)]) * _silu(xg)).astype(o_ref.dtype)

    @pl.when(j == pl.num_programs(1) - 1)
    def _():
        for bi, hd in chains:
            s_ref[bi, hd] = st_ref[bi * HG_HEADS + hd].T


def _hgrn(h3, lb_logits, hg_norm, s0):
    b, t, _ = h3.shape
    t_blk = math.gcd(t, HGRN_PROMPT_CHUNK)
    c = max(t_blk, HGRN_MIN_CHUNK)
    bb = math.gcd(b, HGRN_CHAINS // HG_HEADS) if t == t_blk else math.gcd(b, HGRN_PROMPT_ROWS)
    m01 = _hgrn_sum_masks(c)
    masks = jnp.asarray(np.concatenate([m01, m01], axis=1), BF16)
    col = lambda k: pl.BlockSpec((bb, t_blk, HG_WIDTH), lambda i, j: (i, j, k))
    state_spec = pl.BlockSpec((bb, HG_HEADS, HG_DIM, HG_DIM), lambda i, j: (i, 0, 0, 0))
    kern = functools.partial(_hgrn_kernel, c=c, t_blk=t_blk, bb=bb)
    return pl.pallas_call(
        kern,
        grid=(b // bb, t // t_blk),
        in_specs=[col(0), col(1), col(2), col(3),
                  pl.BlockSpec(lb_logits.shape, lambda i, j: (0, 0)),
                  pl.BlockSpec((1, HG_WIDTH), lambda i, j: (0, 0)),
                  pl.BlockSpec(masks.shape, lambda i, j: (0, 0)),
                  state_spec],
        out_specs=[pl.BlockSpec((bb, t_blk, HG_WIDTH), lambda i, j: (i, j, 0)), state_spec],
        out_shape=[jax.ShapeDtypeStruct((b, t, HG_WIDTH), _operand_dtype(t)),
                   jax.ShapeDtypeStruct(s0.shape, F32)],
        scratch_shapes=[pltpu.VMEM((bb * HG_HEADS, HG_DIM, HG_DIM), F32)],
        compiler_params=_params(("parallel", "arbitrary"), 32),
        name="hgrn2",
    )(h3, h3, h3, h3, lb_logits, hg_norm.reshape(1, HG_WIDTH), masks, s0)


def _rel_bucket_edges():
    d = np.arange(2 * REL_MAX_DIST)
    df = np.maximum(d, 1).astype(np.float32)
    large = REL_MAX_EXACT + (np.log(df / np.float32(REL_MAX_EXACT))
                             / np.float32(math.log(REL_MAX_DIST / REL_MAX_EXACT))
                             * np.float32(REL_BUCKETS - REL_MAX_EXACT)).astype(np.int32)
    bucket = np.where(d < REL_MAX_EXACT, d, np.minimum(large, REL_BUCKETS - 1))
    assert np.all(np.diff(bucket) >= 0) and bucket[-1] == REL_BUCKETS - 1
    return [int(np.argmax(bucket >= kb)) for kb in range(REL_BUCKETS)]


def _rel_bias(rel_ref, head, dist):
    edges = _rel_bucket_edges()
    out = jnp.full(dist.shape, rel_ref[0, head], F32)
    for kb in range(1, REL_BUCKETS):
        out = jnp.where(dist >= edges[kb], rel_ref[kb, head], out)
    return out


def _stack_rows(rows):
    ridx = lax.broadcasted_iota(I32, (GATE_PAD, MB_DIM), 0)
    out = jnp.zeros((GATE_PAD, MB_DIM), F32)
    for n, r in enumerate(rows):
        out = jnp.where(ridx == n, r, out)
    return out


def _topk_blocks(gate, n_cand, n_past):
    lane = lax.broadcasted_iota(I32, gate.shape, 1)
    past = lane < n_past
    g = jnp.where(past, gate, NEG_INF)
    rank = jnp.zeros(gate.shape, I32)
    for m in range(n_cand):
        gm = g[:, m:m + 1]
        ahead = jnp.where(gm > g, 1, jnp.where((gm == g) & (lane > m), 1, 0))
        rank = rank + ahead
    return jnp.where(past & (rank < MB_TOPK), 1.0, 0.0)


def _topk_blocks_t(gate_t, n_cand, n_past):
    blk_id = lax.broadcasted_iota(I32, gate_t.shape, 0)
    past = blk_id < n_past
    g = jnp.where(past, gate_t, NEG_INF)
    rank = jnp.zeros(gate_t.shape, I32)
    for m in range(n_cand):
        gm = g[m:m + 1, :]
        rank = rank + jnp.where(gm > g, 1, jnp.where((gm == g) & (blk_id > m), 1, 0))
    return past & (rank < MB_TOPK)


def _moba_prompt_kernel(rel_ref, q_ref, k_ref, v_ref, o_ref,
                        kx_ref, vx_ref, km_ref, bias_ref, eye_ref, s_ref, mx_ref, mb_ref, acc_ref, *, n_blk):
    b = pl.program_id(0)
    i = pl.program_id(1)
    blk = MB_BLOCK
    scale = MB_DIM ** -0.5
    heads = range(MB_HEADS)
    hsl = lambda hd: slice(hd * MB_DIM, (hd + 1) * MB_DIM)

    t = k_ref.shape[1]
    own_slot = n_blk - 1

    @pl.when((b == 0) & (i == 0))
    def _():
        r = lax.broadcasted_iota(I32, (blk, blk), 0)
        c = lax.broadcasted_iota(I32, (blk, blk), 1)
        eye_ref[...] = jnp.where(r == c, 1.0, 0.0).astype(BF16)
        kr = lax.broadcasted_iota(I32, (t, MB_DIM), 0)
        kc = lax.broadcasted_iota(I32, (t, MB_DIM), 1)
        blk_onehot = jnp.where(kr // blk == kc, 1.0, 0.0).astype(BF16)
        for hd in heads:
            bias_ref[hd, 0] = jnp.where(c <= r, _rel_bias(rel_ref, hd, r - c), NEG_INF)
            bias_ref[hd, 1] = _rel_bias(rel_ref, hd, r - c + blk)
            kx_ref[hd, :, MB_DIM:] = blk_onehot
            vx_ref[hd, :, MB_DIM:] = jnp.ones((t, MB_DIM), BF16)

    @pl.when(i == 0)
    def _():
        means = [jnp.sum(k_ref[0, n * blk:(n + 1) * blk, :], axis=0, keepdims=True) * (1.0 / blk)
                 for n in range(n_blk)]
        for hd in heads:
            kx_ref[hd, :, :MB_DIM] = k_ref[0][:, hsl(hd)].astype(BF16)
            vx_ref[hd, :, :MB_DIM] = v_ref[0][:, hsl(hd)].astype(BF16)
            km_ref[hd] = _stack_rows([mn[:, hsl(hd)] for mn in means])

    q = q_ref[0]
    blk_id = lax.broadcasted_iota(I32, (GATE_PAD, blk), 0)
    qh = [q[:, hsl(hd)] for hd in heads]
    gates = [_dot_nt_f32acc(km_ref[hd], qh[hd]) for hd in heads]
    qm_t = []
    for hd in heads:
        sel_t = _topk_blocks_t(gates[hd], n_blk, i)
        m_t = jnp.where(sel_t | (blk_id >= i), 0.0, MASK_BIG).astype(BF16)
        qm_t.append(jnp.concatenate([m_t, jnp.zeros((MB_DIM - GATE_PAD, blk), BF16)], axis=0))
    qmask = [_dot_nt(eye_ref[...], qm_t[hd]) for hd in heads]
    qx = [jnp.concatenate([qh[hd].astype(BF16), qmask[hd].astype(BF16)], axis=-1) for hd in heads]

    own0 = pl.multiple_of(i * blk, blk)
    dots = [_dot_nt(qx[hd], kx_ref[hd, pl.ds(own0, blk), :]) for hd in heads]
    for hd in heads:
        s = dots[hd] * scale + bias_ref[hd, 0]
        s_ref[hd, own_slot] = s
        mx_ref[hd] = jnp.maximum(s[:, :MB_DIM], s[:, MB_DIM:])

    for n in range(n_blk - 1):
        @pl.when(n < i)
        def _(n=n):
            dots = [_dot_nt(qx[hd], kx_ref[hd, n * blk:(n + 1) * blk, :]) for hd in heads]
            for hd in heads:
                far_bias = rel_ref[REL_BUCKETS - 1, hd]
                s = dots[hd] * scale + jnp.where(n == i - 1, bias_ref[hd, 1], far_bias)
                s_ref[hd, n] = s
                mx_ref[hd] = jnp.maximum(mx_ref[hd], jnp.maximum(s[:, :MB_DIM], s[:, MB_DIM:]))

    row_max = [jnp.max(mx_ref[hd], axis=-1, keepdims=True) for hd in heads]
    for hd in heads:
        mb_ref[hd] = jnp.broadcast_to(row_max[hd], (blk, 2 * MB_DIM))
    probs = [jnp.exp(s_ref[hd, own_slot] - mb_ref[hd]).astype(BF16) for hd in heads]
    for hd in heads:
        acc_ref[hd] = _dot(probs[hd], vx_ref[hd, pl.ds(own0, blk), :])

    for n in range(n_blk - 1):
        @pl.when(n < i)
        def _(n=n):
            probs = [jnp.exp(s_ref[hd, n] - mb_ref[hd]).astype(BF16) for hd in heads]
            pv = [_dot(probs[hd], vx_ref[hd, n * blk:(n + 1) * blk, :]) for hd in heads]
            for hd in heads:
                acc_ref[hd] += pv[hd]

    for hd in heads:
        acc = acc_ref[hd]
        o_ref[0, :, hsl(hd)] = (acc[:, :MB_DIM] / acc[:, MB_DIM:]).astype(o_ref.dtype)


def _moba_prompt(h3, rel_table):
    b, t, _ = h3.shape
    assert t % MB_BLOCK == 0
    n_blk = t // MB_BLOCK
    assert n_blk <= GATE_PAD
    qcol, kcol, vcol = COL_MQ // MB_WIDTH, COL_MK // MB_WIDTH, COL_MV // MB_WIDTH
    kern = functools.partial(_moba_prompt_kernel, n_blk=n_blk)
    wide = 2 * MB_DIM
    vmem = (2 * 2 * t * MB_WIDTH * 4 + 2 * MB_HEADS * t * wide * 2
            + MB_HEADS * (n_blk + 4) * MB_BLOCK * wide * 4) // MIB + 12
    return pl.pallas_call(
        kern,
        grid=(b, n_blk),
        in_specs=[pl.BlockSpec(memory_space=pltpu.SMEM),
                  pl.BlockSpec((1, MB_BLOCK, MB_WIDTH), lambda bb, i: (bb, i, qcol)),
                  pl.BlockSpec((1, t, MB_WIDTH), lambda bb, i: (bb, 0, kcol)),
                  pl.BlockSpec((1, t, MB_WIDTH), lambda bb, i: (bb, 0, vcol))],
        out_specs=pl.BlockSpec((1, MB_BLOCK, MB_WIDTH), lambda bb, i: (bb, i, 0)),
        out_shape=jax.ShapeDtypeStruct((b, t, MB_WIDTH), _operand_dtype(t)),
        scratch_shapes=[pltpu.VMEM((MB_HEADS, t, wide), BF16), pltpu.VMEM((MB_HEADS, t, wide), BF16),
                        pltpu.VMEM((MB_HEADS, GATE_PAD, MB_DIM), F32),
                        pltpu.VMEM((MB_HEADS, 2, MB_BLOCK, MB_BLOCK), F32),
                        pltpu.VMEM((MB_BLOCK, MB_BLOCK), BF16),
                        pltpu.VMEM((MB_HEADS, n_blk, MB_BLOCK, MB_BLOCK), F32),
                        pltpu.VMEM((MB_HEADS, MB_BLOCK, MB_DIM), F32),
                        pltpu.VMEM((MB_HEADS, MB_BLOCK, wide), F32),
                        pltpu.VMEM((MB_HEADS, MB_BLOCK, wide), F32)],
        compiler_params=_params(("arbitrary", "arbitrary"), vmem),
        name="moba_prompt",
    )(rel_table, h3, h3, h3)


def _moba_sample_kernel(pt_ref, rel_ref, q_ref, kn_ref, vn_ref, ck_hbm, cv_hbm, o_ref,
                        bias_ref, bias_own_ref, kbuf, vbuf, sem, *, n_pages, past_len):
    b = pl.program_id(0)
    nb = pl.num_programs(0)
    slot = b % 2
    nxt = jnp.minimum(b + 1, nb - 1)

    def page_copy(which, row, p, dst_slot):
        src, buf = (ck_hbm, kbuf) if which == 0 else (cv_hbm, vbuf)
        return pltpu.make_async_copy(src.at[pt_ref[row, p]], buf.at[dst_slot, p], sem.at[dst_slot, which])

    @pl.when(b == 0)
    def _():
        for p in range(n_pages):
            page_copy(0, 0, p, 0).start()
            page_copy(1, 0, p, 0).start()

    for p in range(n_pages):
        page_copy(0, b, p, slot).wait()
        page_copy(1, b, p, slot).wait()

    for p in range(n_pages):
        page_copy(0, nxt, p, 1 - slot).start()
        page_copy(1, nxt, p, 1 - slot).start()

    class _Pages:
        def __init__(self, buf):
            self.buf = buf

        def __getitem__(self, p):
            return self.buf.at[slot, pl.ds(p, 1)]

    kp = _Pages(kbuf)
    vp = _Pages(vbuf)
    t = q_ref.shape[1]
    rows = MB_HEADS * t
    page_rows = PAGE_SIZE * MB_HEADS
    pages_per_blk = MB_BLOCK // PAGE_SIZE
    n_past = past_len // MB_BLOCK
    scale = MB_DIM ** -0.5

    def head_rows(ref):
        x = ref[0]
        return jnp.concatenate([x[:, hd * MB_DIM:(hd + 1) * MB_DIM] for hd in range(MB_HEADS)], axis=0)

    @pl.when(pl.program_id(0) == 0)
    def _():
        tq = lax.broadcasted_iota(I32, (t, page_rows), 0)
        kc = lax.broadcasted_iota(I32, (t, page_rows), 1)
        for p in range(n_pages):
            dist = past_len + tq - (p * PAGE_SIZE + (kc >> 2))
            bias_ref[p] = jnp.concatenate(
                [jnp.where((kc & (MB_HEADS - 1)) == hd, _rel_bias(rel_ref, hd, dist), NEG_INF)
                 for hd in range(MB_HEADS)], axis=0)
        ro = lax.broadcasted_iota(I32, (t, rows), 0)
        co = lax.broadcasted_iota(I32, (t, rows), 1)
        for hd in range(MB_HEADS):
            own = _rel_bias(rel_ref, hd, ro - (co - hd * t))
            keep = (co >= hd * t) & (co <= hd * t + ro)
            bias_own_ref[hd * t:(hd + 1) * t, :] = jnp.where(keep, own, NEG_INF)

    q = head_rows(q_ref)
    qb = q.astype(BF16)
    row_head = lax.broadcasted_iota(I32, (rows, 1), 0) // t

    groups = page_rows // 8
    sums = [jnp.sum(kp[p][0].reshape(groups, 8, MB_DIM), axis=0) for p in range(n_pages)]
    blk_sums = []
    for n in range(n_past):
        acc = sums[n * pages_per_blk]
        for pp in range(1, pages_per_blk):
            acc = acc + sums[n * pages_per_blk + pp]
        blk_sums.append(acc)
    g_full = _dot_nt_f32acc(q, jnp.concatenate(blk_sums, axis=0))
    gc = lax.broadcasted_iota(I32, g_full.shape, 1)
    g_full = jnp.where((gc & (MB_HEADS - 1)) == row_head, g_full, 0.0)
    lane = lax.broadcasted_iota(I32, (rows, GATE_PAD), 1)
    gate = jnp.zeros((rows, GATE_PAD), F32)
    for n in range(n_past):
        g_n = jnp.sum(g_full[:, n * 8:(n + 1) * 8], axis=-1, keepdims=True) * (1.0 / MB_BLOCK)
        gate = jnp.where(lane == n, g_n, gate)
    sel = _topk_blocks(gate, n_past, n_past)

    s_own = _dot_nt(q, head_rows(kn_ref)) * scale + bias_own_ref[...]
    m = jnp.max(s_own, axis=-1, keepdims=True)
    s_past = []
    for p in range(n_pages):
        n = p // pages_per_blk
        s = _dot_nt(qb, kp[p][0].astype(BF16)) * scale + bias_ref[p]
        s = jnp.where(sel[:, n:n + 1] > 0.0, s, NEG_INF)
        s_past.append(s)
        m = jnp.maximum(m, jnp.max(s, axis=-1, keepdims=True))
    p_own = jnp.exp(s_own - m)
    l = jnp.sum(p_own, axis=-1, keepdims=True)
    out = _dot(p_own, head_rows(vn_ref))
    for p in range(n_pages):
        pr = jnp.exp(s_past[p] - m)
        l = l + jnp.sum(pr, axis=-1, keepdims=True)
        out = out + _dot(pr.astype(BF16), vp[p][0].astype(BF16))
    out = out / l
    for hd in range(MB_HEADS):
        o_ref[0, :, hd * MB_DIM:(hd + 1) * MB_DIM] = out[hd * t:(hd + 1) * t]

    @pl.when(b == nb - 1)
    def _():
        for p in range(n_pages):
            page_copy(0, nxt, p, 1 - slot).wait()
            page_copy(1, nxt, p, 1 - slot).wait()


def _moba_sample(h3, cache_k, cache_v, page_table, rel_table):
    db, t, _ = h3.shape
    n_pages = page_table.shape[1]
    past_len = n_pages * PAGE_SIZE
    assert past_len % MB_BLOCK == 0 and t <= MB_BLOCK and past_len // MB_BLOCK < GATE_PAD
    assert MB_HEADS == 4 and t % 8 == 0
    qcol, kcol, vcol = COL_MQ // MB_WIDTH, COL_MK // MB_WIDTH, COL_MV // MB_WIDTH
    page_rows = PAGE_SIZE * MB_HEADS
    new = lambda k: pl.BlockSpec((1, t, MB_WIDTH), lambda i, pt: (i, 0, k))
    kern = functools.partial(_moba_sample_kernel, n_pages=n_pages, past_len=past_len)
    page_slots = pltpu.VMEM((2, n_pages, page_rows, MB_DIM), F32)
    grid_spec = pltpu.PrefetchScalarGridSpec(
        num_scalar_prefetch=1,
        grid=(db,),
        in_specs=[pl.BlockSpec(memory_space=pltpu.SMEM), new(qcol), new(kcol), new(vcol),
                  pl.BlockSpec(memory_space=pl.ANY), pl.BlockSpec(memory_space=pl.ANY)],
        out_specs=pl.BlockSpec((1, t, MB_WIDTH), lambda i, pt: (i, 0, 0)),
        scratch_shapes=[pltpu.VMEM((n_pages, MB_HEADS * t, page_rows), F32),
                        pltpu.VMEM((MB_HEADS * t, MB_HEADS * t), F32),
                        page_slots, page_slots, pltpu.SemaphoreType.DMA((2, 2))],
    )
    return pl.pallas_call(
        kern,
        grid_spec=grid_spec,
        out_shape=jax.ShapeDtypeStruct((db, t, MB_WIDTH), F32),
        compiler_params=_params(("arbitrary",), 40),
        name="moba_sample",
    )(page_table, rel_table, h3, h3, h3, cache_k, cache_v)


def _merge_kernel(oa_ref, ob_ref, ga0_ref, ga1_ref, gb0_ref, gb1_ref, x_ref,
                  wa_ref, wb_ref, wo_ref, gx_ref, wq_ref, x1_ref, xq_ref):
    ga = jnp.concatenate([ga0_ref[...], ga1_ref[...]], axis=-1)
    gb = jnp.concatenate([gb0_ref[...], gb1_ref[...]], axis=-1)
    pa = _dot(oa_ref[...].astype(BF16), wa_ref[...])
    pb = _dot(ob_ref[...].astype(BF16), wb_ref[...])
    merged = _sigmoid(ga) * pa + _sigmoid(gb) * pb
    x1 = x_ref[...] + _dot(merged.astype(BF16), wo_ref[...])
    x1_ref[...] = x1
    xq_ref[...] = _dot(_rmsnorm(x1, gx_ref[...]).astype(BF16), wq_ref[...]).astype(xq_ref.dtype)


def _merge(oa, ob, h2, x, wa, wb, wo, gx, wq, *, tm, xq_dtype):
    n = x.shape[0]
    assert n % tm == 0
    half = D_MODEL // 2
    tok = lambda w, k=0: pl.BlockSpec((tm, w), lambda i: (i, k))
    full = lambda a: pl.BlockSpec(a.shape, lambda i: (0, 0))
    gx2 = gx.reshape(1, D_MODEL)
    return pl.pallas_call(
        _merge_kernel,
        grid=(n // tm,),
        in_specs=[tok(HG_WIDTH), tok(MB_WIDTH),
                  tok(half, COL_GA // half), tok(half, COL_GA // half + 1),
                  tok(half, COL_GB // half), tok(half, COL_GB // half + 1),
                  tok(D_MODEL), full(wa), full(wb), full(wo), full(gx2), full(wq)],
        out_specs=[tok(D_MODEL), tok(XA_WIDTH)],
        out_shape=[jax.ShapeDtypeStruct((n, D_MODEL), F32),
                   jax.ShapeDtypeStruct((n, XA_WIDTH), xq_dtype)],
        compiler_params=_params(("parallel",), 48),
        name="merge_mix",
    )(oa, ob, h2, h2, h2, h2, x, wa, wb, wo, gx2, wq)


def _mem_attn_kernel(q_ref, k_ref, v_ref, o_ref):
    scale = XA_DIM ** -0.5
    heads = range(XA_HEADS)
    hsl = lambda hd: slice(hd * XA_DIM, (hd + 1) * XA_DIM)
    logits = [_dot_nt(q_ref[0][:, hsl(hd)].astype(BF16), k_ref[0][:, hsl(hd)].astype(BF16)) * scale
              for hd in heads]
    row_max = [jnp.max(s, axis=-1, keepdims=True) for s in logits]
    probs = [jnp.exp(s - m) for s, m in zip(logits, row_max)]
    denom = [jnp.sum(p, axis=-1, keepdims=True) for p in probs]
    outs = [_dot(probs[hd].astype(BF16), v_ref[0][:, hsl(hd)].astype(BF16)) for hd in heads]
    for hd in heads:
        o_ref[0, :, hsl(hd)] = (outs[hd] / denom[hd]).astype(o_ref.dtype)


def _mem_attn(xq3, mem_kv, *, tq):
    b, t, _ = xq3.shape
    assert t % tq == 0
    mem = lambda k: pl.BlockSpec((1, mem_kv.shape[1], XA_WIDTH), lambda i, j: (i, 0, k))
    qs = pl.BlockSpec((1, tq, XA_WIDTH), lambda i, j: (i, j, 0))
    return pl.pallas_call(
        _mem_attn_kernel,
        grid=(b, t // tq),
        in_specs=[qs, mem(0), mem(1)],
        out_specs=qs,
        out_shape=jax.ShapeDtypeStruct((b, t, XA_WIDTH), _operand_dtype(t)),
        compiler_params=_params(("parallel", "parallel"), 32),
        name="mem_attn",
    )(xq3, mem_kv, mem_kv)


def _mem_attn_rows_kernel(q_ref, k_ref, v_ref, o_ref):
    bb, t, _ = q_ref.shape
    rows = XA_HEADS * t
    scale = XA_DIM ** -0.5
    row_head = lax.broadcasted_iota(I32, (rows, 1), 0) // t
    kc = lax.broadcasted_iota(I32, (rows, k_ref.shape[1]), 1)
    same_head = (kc & (XA_HEADS - 1)) == row_head
    qs = []
    for i in range(bb):
        x = q_ref[i]
        qs.append(jnp.concatenate([x[:, hd * XA_DIM:(hd + 1) * XA_DIM] for hd in range(XA_HEADS)],
                                  axis=0).astype(BF16))
    logits = [jnp.where(same_head, _dot_nt(qs[i], k_ref[i].astype(BF16)) * scale, NEG_INF)
              for i in range(bb)]
    row_max = [jnp.max(s, axis=-1, keepdims=True) for s in logits]
    probs = [jnp.exp(s - m) for s, m in zip(logits, row_max)]
    denom = [jnp.sum(p, axis=-1, keepdims=True) for p in probs]
    outs = [_dot(probs[i].astype(BF16), v_ref[i].astype(BF16)) / denom[i] for i in range(bb)]
    for i in range(bb):
        for hd in range(XA_HEADS):
            o_ref[i, :, hd * XA_DIM:(hd + 1) * XA_DIM] = outs[i][hd * t:(hd + 1) * t]


def _mem_attn_rows(xq3, mem_k, mem_v, *, bb):
    b, t, _ = xq3.shape
    assert b % bb == 0 and XA_HEADS == 4 and t % 8 == 0
    mem = pl.BlockSpec((bb,) + mem_k.shape[1:], lambda i: (i, 0, 0))
    qs = pl.BlockSpec((bb, t, XA_WIDTH), lambda i: (i, 0, 0))
    return pl.pallas_call(
        _mem_attn_rows_kernel,
        grid=(b // bb,),
        in_specs=[qs, mem, mem],
        out_specs=qs,
        out_shape=jax.ShapeDtypeStruct((b, t, XA_WIDTH), F32),
        compiler_params=_params(("parallel",), 32),
        name="mem_attn_rows",
    )(xq3, mem_k, mem_v)


def _route(logits):
    lane = lax.broadcasted_iota(I32, logits.shape, 1).astype(F32)
    first = lambda hit: jnp.min(jnp.where(hit, lane, float(ROUTER_LANES)), axis=-1, keepdims=True)
    gl = jnp.where(lane < N_GROUPS, logits, NEG_INF)
    gmax = jnp.max(gl, axis=-1, keepdims=True)
    grp = first(gl == gmax)
    g_prob = 1.0 / jnp.sum(jnp.exp(gl - gmax), axis=-1, keepdims=True)
    e_lo = EXPERT_LANE0 + grp * EXPERTS_PER_GROUP
    in_grp = (lane >= e_lo) & (lane < e_lo + EXPERTS_PER_GROUP)
    el = jnp.where(in_grp, logits, NEG_INF)
    top1 = jnp.max(el, axis=-1, keepdims=True)
    idx1 = first(el == top1)
    el2 = jnp.where(lane == idx1, NEG_INF, el)
    top2 = jnp.max(el2, axis=-1, keepdims=True)
    idx2 = first(el2 == top2)
    e2 = jnp.exp(top2 - top1)
    w1 = g_prob / (1.0 + e2)
    w2 = w1 * e2
    comb = jnp.where(lane == idx1, w1, 0.0) + jnp.where(lane == idx2, w2, 0.0)
    return jnp.where(lane == GRP_LANE, grp, comb)


def _xo_route_kernel(x1_ref, at_ref, wxo_ref, gf_ref, wr_ref, br_ref, x2_ref, xf_ref, comb_ref):
    x2 = x1_ref[...] + _dot(at_ref[...].astype(BF16), wxo_ref[...])
    x2_ref[...] = x2
    xf = _rmsnorm(x2, gf_ref[...])
    xf_ref[...] = xf.astype(BF16)
    comb_ref[...] = _route(_dot_f32acc(xf, wr_ref[...]) + br_ref[...])


def _xo_route(x1, attn, wxo, g_ffn, w_router_t, b_router_t, *, tm):
    n = x1.shape[0]
    assert n % tm == 0
    tok = lambda w: pl.BlockSpec((tm, w), lambda i: (i, 0))
    full = lambda a: pl.BlockSpec(a.shape, lambda i: (0, 0))
    gf2 = g_ffn.reshape(1, D_MODEL)
    return pl.pallas_call(
        _xo_route_kernel,
        grid=(n // tm,),
        in_specs=[tok(D_MODEL), tok(XA_WIDTH), full(wxo), full(gf2), full(w_router_t), full(b_router_t)],
        out_specs=[tok(D_MODEL), tok(D_MODEL), tok(ROUTER_LANES)],
        out_shape=[jax.ShapeDtypeStruct((n, D_MODEL), F32), jax.ShapeDtypeStruct((n, D_MODEL), BF16),
                   jax.ShapeDtypeStruct((n, ROUTER_LANES), F32)],
        compiler_params=_params(("parallel",), 40),
        name="xo_route",
    )(x1, attn, wxo, gf2, w_router_t, b_router_t)


def _moe_kernel(xf_ref, comb_ref, x2_ref, wg_ref, wu_ref, wd_ref, gn_ref, y_ref,
                tri_ref, posc_ref, cnt_ref, xs_ref, ws_ref, ys_ref, *, r):
    i = pl.program_id(0)
    step = pl.program_id(1)
    tm = xf_ref.shape[0]
    per_step = wg_ref.shape[0]
    steps_per_group = EXPERTS_PER_GROUP // per_step
    g = step // steps_per_group
    lane = lax.broadcasted_iota(I32, (tm, ROUTER_LANES), 1)

    @pl.when((i == 0) & (step == 0))
    def _():
        rr = lax.broadcasted_iota(I32, (tm, tm), 0)
        cc = lax.broadcasted_iota(I32, (tm, tm), 1)
        tri_ref[...] = jnp.where(cc < rr, 1.0, 0.0).astype(BF16)

    @pl.when(step == 0)
    def _():
        comb = comb_ref[...]
        grp = comb[:, GRP_LANE:GRP_LANE + 1]
        lane_f = lane.astype(F32)
        rel = lane - EXPERT_LANE0
        used = (comb > 0.0) & (rel >= 0) & (rel < N_EXPERTS)
        second = (rel & (EXPERTS_PER_GROUP - 1)) >= EXPERTS_PER_GROUP // 2
        uses_a = jnp.max(jnp.where(used & jnp.logical_not(second), 1.0, 0.0), axis=-1, keepdims=True)
        uses_b = jnp.max(jnp.where(used & second, 1.0, 0.0), axis=-1, keepdims=True)
        key = grp * MOE_CLASSES + jnp.where(uses_b == 0.0, 0.0, jnp.where(uses_a == 0.0, 2.0, 1.0))
        onehot = jnp.where((lane_f == key) & (lane < N_GROUPS * MOE_CLASSES), 1.0, 0.0)
        before = _dot(tri_ref[...], onehot.astype(BF16))
        cnt = jnp.sum(onehot, axis=0, keepdims=True)
        cnt_ref[...] = cnt
        pos = jnp.sum(jnp.where(lane_f < key, cnt, 0.0) + onehot * before, axis=-1, keepdims=True)
        posc_ref[...] = jnp.where(lane == 0, pos, 0.0)
        hi = jnp.floor(pos * (1.0 / POS_SPLIT))
        lo = pos - hi * POS_SPLIT
        cols = jnp.where(lane == 0, hi, jnp.where(lane == 1, lo, 0.0))
        er = lax.broadcasted_iota(I32, (V7X_BF16_SUBLANES, ROUTER_LANES), 0)
        ec = lax.broadcasted_iota(I32, (V7X_BF16_SUBLANES, ROUTER_LANES), 1)
        post = _dot_nt(jnp.where(er == ec, 1.0, 0.0).astype(BF16), cols.astype(BF16))
        pos_row = post[0:1, :] * POS_SPLIT + post[1:2, :]
        slot = lax.broadcasted_iota(I32, (tm, tm), 0).astype(F32)
        perm = jnp.where(pos_row == slot, 1.0, 0.0).astype(BF16)
        comb_hi, comb_lo = _split2(comb)
        xs_ref[0:tm, :] = _dot(perm, xf_ref[...]).astype(BF16)
        ws_ref[0:tm, :] = _dot(perm, comb_hi) + _dot(perm, comb_lo)
        xs_ref[tm:, :] = jnp.zeros((r, D_MODEL), BF16)
        ws_ref[tm:, :] = jnp.zeros((r, ROUTER_LANES), F32)
        ys_ref[...] = jnp.zeros(ys_ref.shape, F32)

    lane_row = lax.broadcasted_iota(I32, (1, ROUTER_LANES), 1)
    key0 = g * MOE_CLASSES + step % steps_per_group
    first = jnp.sum(jnp.where(lane_row < key0, cnt_ref[...], 0.0)).astype(I32)
    count = jnp.sum(jnp.where((lane_row == key0) | (lane_row == key0 + 1), cnt_ref[...], 0.0)).astype(I32)
    start = (first // V7X_BF16_SUBLANES) * V7X_BF16_SUBLANES
    n_windows = (first + count - start + (r - 1)) // r

    def experts(c, carry):
        rows = pl.ds(pl.multiple_of(start + c * r, V7X_BF16_SUBLANES), r)
        xg = xs_ref[rows, :]
        wq = ws_ref[rows, :]
        wl = lax.broadcasted_iota(I32, (r, ROUTER_LANES), 1)
        ups = [(_dot(xg, wg_ref[k]), _dot(xg, wu_ref[k])) for k in range(per_step)]
        hidden = []
        for k, (gate, up) in enumerate(ups):
            w_e = jnp.sum(jnp.where(wl == EXPERT_LANE0 + step * per_step + k, wq, 0.0), axis=-1, keepdims=True)
            hidden.append((_silu(gate) * up * w_e).astype(BF16))
        w_down = wd_ref[...].reshape(per_step * EXPERT_HIDDEN, D_MODEL)
        ys_ref[rows, :] += _dot(jnp.concatenate(hidden, axis=-1), w_down)
        return carry

    lax.fori_loop(0, n_windows, experts, 0)

    @pl.when(step == pl.num_programs(1) - 1)
    def _():
        slot = lax.broadcasted_iota(I32, (tm, tm), 1).astype(F32)
        place = jnp.where(posc_ref[:, 0:1] == slot, 1.0, 0.0).astype(BF16)
        moe = _dot(place, ys_ref[0:tm, :].astype(BF16))
        y_ref[...] = _rmsnorm(x2_ref[...] + moe, gn_ref[...])


def _moe(xf, comb, x2, wg, wu, wd, g_final, *, tm):
    n = xf.shape[0]
    assert n % tm == 0 and tm % V7X_BF16_SUBLANES == 0
    assert 2 * MOE_EXPERTS_PER_STEP == EXPERTS_PER_GROUP and N_GROUPS * MOE_CLASSES <= ROUTER_LANES
    r = min(MOE_CHUNK_ROWS, tm)
    cap = tm + r
    per_step = MOE_EXPERTS_PER_STEP
    tok = lambda w: pl.BlockSpec((tm, w), lambda i, s: (i, 0))
    gn2 = g_final.reshape(1, D_MODEL)
    kern = functools.partial(_moe_kernel, r=r)
    vmem = (2 * tm * D_MODEL * (2 + 4 + 4) + 2 * tm * ROUTER_LANES * 4
            + 2 * 3 * per_step * D_MODEL * EXPERT_HIDDEN * 2 + tm * tm * 2
            + cap * D_MODEL * (2 + 4) + tm * D_MODEL * 4) // MIB + 8
    return pl.pallas_call(
        kern,
        grid=(n // tm, N_EXPERTS // per_step),
        in_specs=[tok(D_MODEL), tok(ROUTER_LANES), tok(D_MODEL),
                  pl.BlockSpec((per_step, D_MODEL, EXPERT_HIDDEN), lambda i, s: (s, 0, 0)),
                  pl.BlockSpec((per_step, D_MODEL, EXPERT_HIDDEN), lambda i, s: (s, 0, 0)),
                  pl.BlockSpec((per_step, EXPERT_HIDDEN, D_MODEL), lambda i, s: (s, 0, 0)),
                  pl.BlockSpec((1, D_MODEL), lambda i, s: (0, 0))],
        out_specs=tok(D_MODEL),
        out_shape=jax.ShapeDtypeStruct((n, D_MODEL), F32),
        scratch_shapes=[pltpu.VMEM((tm, tm), BF16),
                        pltpu.VMEM((tm, ROUTER_LANES), F32), pltpu.VMEM((1, ROUTER_LANES), F32),
                        pltpu.VMEM((cap, D_MODEL), BF16), pltpu.VMEM((cap, ROUTER_LANES), F32),
                        pltpu.VMEM((cap, D_MODEL), F32)],
        compiler_params=_params(("arbitrary", "arbitrary"), vmem),
        name="moe_grouped",
    )(xf, comb, x2, wg, wu, wd, gn2)


def _token_tile(n, cap=512):
    return cap if n % cap == 0 else n


def _col_tile(n, cap=1536):
    return max(c for c in range(V7X_LANES, cap + 1, V7X_LANES) if n % c == 0)


def _layer(x3, w, s0, moba_fn, mem_fn):
    b, t, _ = x3.shape
    n = b * t
    x2d = x3.reshape(n, D_MODEL)
    tm = _token_tile(n)
    h2, k4, v4 = _norm_matmul(x2d, w["norm_mix"], w["w_in"], tm=_token_tile(n, 1024),
                              tn=_col_tile(IN_COLS), head_cols=(COL_MK, COL_MV), heads=MB_HEADS,
                              head_dim=MB_DIM)
    h3 = h2.reshape(b, t, IN_COLS)
    oa, s_new = _hgrn(h3, w["hg_lb_logits"], w["hg_norm"], s0)
    ob = moba_fn(h3)
    x1, xq = _merge(oa.reshape(n, HG_WIDTH), ob.reshape(n, MB_WIDTH), h2, x2d,
                    w["w_branch_a"], w["w_branch_b"], w["w_mix_out"], w["norm_xattn"], w["w_xq"], tm=tm,
                    xq_dtype=_operand_dtype(t))
    attn = mem_fn(xq.reshape(b, t, XA_WIDTH))
    x2, xf, comb = _xo_route(x1, attn.reshape(n, XA_WIDTH), w["w_xo"], w["norm_ffn"],
                             w["w_router"], w["b_router"], tm=_token_tile(n, 1024))
    y = _moe(xf, comb, x2, w["w_expert_gate"], w["w_expert_up"], w["w_expert_down"],
             w["norm_final"], tm=_token_tile(n, 1024))
    k_new = k4.reshape(b, t, MB_HEADS, MB_DIM)
    v_new = v4.reshape(b, t, MB_HEADS, MB_DIM)
    return y.reshape(b, t, D_MODEL), k_new, v_new, s_new


def kernel(x_prompt, x_sample, cache_k, cache_v, state_hgrn, cache_mem_k, cache_mem_v, page_table, mem_prompt, norm_mix, w_in, hg_lb_logits, hg_norm, w_branch_a, w_branch_b, w_mix_out, rel_table, norm_xattn, norm_mem, w_xq, w_xk, w_xv, w_xo, norm_ffn, w_group_router, b_group_router, w_expert_router, b_expert_router, w_expert_gate, w_expert_up, w_expert_down, norm_final):
    assert w_in.shape[0] == DEPTH == 1 and hg_lb_logits.shape[0] == DEPTH + 1
    b = x_prompt.shape[0]
    db = x_sample.shape[0]
    n_pool = cache_k.shape[1]
    pad_lanes = ROUTER_LANES - EXPERT_LANE0 - N_EXPERTS
    w = {
        "norm_mix": norm_mix[0], "w_in": w_in[0].astype(BF16),
        "hg_lb_logits": hg_lb_logits, "hg_norm": hg_norm[0],
        "w_branch_a": w_branch_a[0].astype(BF16), "w_branch_b": w_branch_b[0].astype(BF16),
        "w_mix_out": w_mix_out[0].astype(BF16), "norm_xattn": norm_xattn[0],
        "w_xq": w_xq[0].astype(BF16), "w_xo": w_xo[0].astype(BF16), "norm_ffn": norm_ffn[0],
        "w_router": jnp.pad(jnp.concatenate(
            [w_group_router[0], jnp.zeros((D_MODEL, EXPERT_LANE0 - N_GROUPS), F32), w_expert_router[0]],
            axis=1), ((0, 0), (0, pad_lanes))),
        "b_router": jnp.pad(jnp.concatenate(
            [b_group_router[0], jnp.zeros((EXPERT_LANE0 - N_GROUPS,), F32), b_expert_router[0]]),
            (0, pad_lanes)).reshape(1, ROUTER_LANES),
        "w_expert_gate": w_expert_gate[0].reshape(N_EXPERTS, D_MODEL, EXPERT_HIDDEN).astype(BF16),
        "w_expert_up": w_expert_up[0].reshape(N_EXPERTS, D_MODEL, EXPERT_HIDDEN).astype(BF16),
        "w_expert_down": w_expert_down[0].reshape(N_EXPERTS, EXPERT_HIDDEN, D_MODEL).astype(BF16),
        "norm_final": norm_final,
    }

    w_mem = jnp.concatenate([w_xk[0], w_xv[0]], axis=1).astype(BF16)
    mem_kv, mk_p, mv_p = _norm_matmul(mem_prompt.reshape(b * MEM_LEN, D_MODEL), norm_mem[0], w_mem,
                                      tm=_token_tile(b * MEM_LEN), tn=512, head_cols=(0, XA_WIDTH),
                                      heads=XA_HEADS, head_dim=XA_DIM)
    mem_kv = mem_kv.reshape(b, MEM_LEN, 2 * XA_WIDTH)
    s0 = jnp.zeros((b, HG_HEADS, HG_DIM, HG_DIM), F32)
    y_p, k_p, v_p, s_p = _layer(x_prompt, w, s0,
                                functools.partial(_moba_prompt, rel_table=rel_table),
                                functools.partial(_mem_attn, mem_kv=mem_kv, tq=512))

    ck = cache_k[0].reshape(n_pool, PAGE_SIZE * MB_HEADS, MB_DIM)
    cv = cache_v[0].reshape(n_pool, PAGE_SIZE * MB_HEADS, MB_DIM)
    moba_s = functools.partial(_moba_sample, cache_k=ck, cache_v=cv, page_table=page_table,
                               rel_table=rel_table)
    mem_s = functools.partial(_mem_attn_rows, bb=math.gcd(db, 8),
                              mem_k=cache_mem_k[0].reshape(db, MEM_LEN * XA_HEADS, XA_DIM),
                              mem_v=cache_mem_v[0].reshape(db, MEM_LEN * XA_HEADS, XA_DIM))
    y_s, k_s, v_s, s_s = _layer(x_sample, w, state_hgrn[0], moba_s, mem_s)

    heads = lambda a: a.reshape(b, MEM_LEN, XA_HEADS, XA_DIM)[None]
    return (y_p, y_s, k_p[None], v_p[None], s_p[None], heads(mk_p), heads(mv_p),
            k_s[None], v_s[None], s_s[None])
```
